```python
import jax, jax.numpy as jnp
from jax import lax
import numpy as np

D_MODEL = 1024
BATCH = 16
SEQ = 256
DEPTH = 2
DEC_BATCH = 2
DEC_SEQ = 2048
PAST_LEN = 256

GRID_W = 64
HEAD_DIM = 64
D_CONV = 256
CONV_W = 3
N_HEADS = 8
N_KV = 2
GROUP = N_HEADS // N_KV
WINDOW = 128
Q_BLOCK = 128
ROPE_BASE = 10000.0
RET_HEADS = 4
RET_DK = 64
RET_DV = 64
RET_CHUNK = 128
N_EXPERTS = 32
TOP_K = 4
D_EXPERT = D_MODEL
SWIGLU_LIMIT = 7.0
SWIGLU_ALPHA = 1.702
MOE_BLOCK = 128
EPS = 1e-6
NEG_INF = -1e30
F32 = jnp.float32
IN_SPLITS = (D_CONV, D_CONV, D_CONV,
             N_HEADS * HEAD_DIM, N_KV * HEAD_DIM, N_KV * HEAD_DIM,
             RET_HEADS * RET_DK, RET_HEADS * RET_DK, RET_HEADS * RET_DV, RET_HEADS * RET_DV,
             D_MODEL, D_MODEL, D_MODEL)
IN_COLS = 3 * D_CONV + (N_HEADS + 2 * N_KV) * HEAD_DIM + RET_HEADS * (2 * RET_DK + 2 * RET_DV) + 3 * D_MODEL

kernel_name = 'hybrid_diffusion_prefix_trunk_step'


def rms_norm(x, g):
    xf = x.astype(F32)
    y = xf * lax.rsqrt(jnp.mean(xf * xf, axis=-1, keepdims=True) + EPS) * g.astype(F32)
    return y.astype(x.dtype)


def short_conv(u, w):
    up = jnp.pad(u, ((0, 0), (1, 1), (0, 0)))
    return up[:, :-2] * w[0] + up[:, 1:-1] * w[1] + up[:, 2:] * w[2]


def axial_rope(x):
    n = x.shape[1]
    rows = n // GRID_W
    row = jnp.repeat(jnp.arange(rows), GRID_W).astype(F32)
    col = jnp.tile(jnp.arange(GRID_W), rows).astype(F32)
    half = HEAD_DIM // 2
    inv = ROPE_BASE ** (-jnp.arange(0, half, 2, dtype=F32) / half)

    def rot(xa, pos):
        ang = pos[:, None] * inv[None, :]
        cos = jnp.cos(ang)[None, :, None, :]
        sin = jnp.sin(ang)[None, :, None, :]
        x1, x2 = jnp.split(xa, 2, axis=-1)
        return jnp.concatenate([x1 * cos - x2 * sin, x1 * sin + x2 * cos], axis=-1)

    xf = x.astype(F32)
    return jnp.concatenate([rot(xf[..., :half], row), rot(xf[..., half:], col)], axis=-1).astype(x.dtype)


def sink_softmax_av(s, sink, v, mask):
    if mask is not None:
        s = jnp.where(mask, s, NEG_INF)
    sk = sink.astype(F32)[None, :, :, None]
    m = jnp.maximum(jnp.max(s, axis=-1), sk)
    p = jnp.exp(s - m[..., None])
    p = p / (jnp.sum(p, axis=-1) + jnp.exp(sk - m))[..., None]
    return jnp.einsum('bkgqs,bskd->bqkgd', p, v.astype(F32)).astype(v.dtype)


def context_attention(q, k, v, sink):
    b, n = q.shape[:2]
    nb = n // Q_BLOCK
    qb = jnp.moveaxis(q.reshape(b, nb, Q_BLOCK, N_KV, GROUP, HEAD_DIM), 1, 0)
    sk = sink.reshape(N_KV, GROUP)

    def one_block(qj):
        s = jnp.einsum('bqkgd,bskd->bkgqs', qj, k, preferred_element_type=F32) * HEAD_DIM ** -0.5
        return sink_softmax_av(s, sk, v, None)

    o = lax.map(one_block, qb)
    return jnp.moveaxis(o, 0, 1).reshape(b, n, N_HEADS * HEAD_DIM)


def latent_window_attention(q, k, v, k_ctx, v_ctx, sink):
    b, n = q.shape[:2]
    nb = n // Q_BLOCK
    pad = ((0, 0), (Q_BLOCK, Q_BLOCK), (0, 0), (0, 0))
    kp = jnp.pad(k, pad)
    vp = jnp.pad(v, pad)
    qb = jnp.moveaxis(q.reshape(b, nb, Q_BLOCK, N_KV, GROUP, HEAD_DIM), 1, 0)
    sk = sink.reshape(N_KV, GROUP)
    ctx_ok = jnp.ones((Q_BLOCK, k_ctx.shape[1]), dtype=bool)

    def one_block(args):
        j, qj = args
        start = j * Q_BLOCK
        kj = jnp.concatenate([lax.dynamic_slice_in_dim(kp, start, 3 * Q_BLOCK, axis=1), k_ctx], axis=1)
        vj = jnp.concatenate([lax.dynamic_slice_in_dim(vp, start, 3 * Q_BLOCK, axis=1), v_ctx], axis=1)
        qpos = start + jnp.arange(Q_BLOCK)
        kpos = start - Q_BLOCK + jnp.arange(3 * Q_BLOCK)
        win = (jnp.abs(qpos[:, None] - kpos[None, :]) <= WINDOW) & (kpos >= 0)[None, :] & (kpos < n)[None, :]
        mask = jnp.concatenate([win, ctx_ok], axis=1)
        s = jnp.einsum('bqkgd,bskd->bkgqs', qj, kj, preferred_element_type=F32) * HEAD_DIM ** -0.5
        return sink_softmax_av(s, sk, vj, mask)

    o = lax.map(one_block, (jnp.arange(nb), qb))
    return jnp.moveaxis(o, 0, 1).reshape(b, n, N_HEADS * HEAD_DIM)


def retention_scan(q, k, v, log_g, r0):
    b, n, h, _ = q.shape
    dv = v.shape[-1]
    nc = n // RET_CHUNK

    def chunks(t):
        return t.reshape(b, nc, RET_CHUNK, h, t.shape[-1]).transpose(1, 0, 3, 2, 4)

    idx = jnp.arange(RET_CHUNK, dtype=F32)
    diff = idx[:, None] - idx[None, :]
    decay = jnp.where(diff >= 0, jnp.exp(jnp.maximum(diff, 0.0) * log_g[:, None, None]), 0.0)
    q_decay = jnp.exp((idx + 1.0) * log_g[:, None])[..., None]
    k_decay = jnp.exp((RET_CHUNK - 1.0 - idx) * log_g[:, None])[..., None]
    chunk_decay = jnp.exp(RET_CHUNK * log_g)[:, None, None]

    def step(r, qkv):
        qc, kc, vc = qkv
        inner = jnp.einsum('bhid,bhjd->bhij', qc, kc) * decay
        o = jnp.einsum('bhij,bhje->bhie', inner, vc) + jnp.einsum('bhid,bhde->bhie', qc, r) * q_decay
        r = r * chunk_decay + jnp.einsum('bhjd,bhje->bhde', kc * k_decay, vc)
        return r, o

    r_fin, o = lax.scan(step, r0, (chunks(q), chunks(k), chunks(v)))
    return o.transpose(1, 0, 3, 2, 4).reshape(b, n, h, dv), r_fin


def moe_ffn(h, w_router, b_router, w_gu, b_gu, w_down, b_down):
    xt = h.reshape(-1, D_MODEL)
    t = xt.shape[0]
    tk = t * TOP_K
    logits = xt.astype(F32) @ w_router.astype(F32) + b_router.astype(F32)
    top_v, top_e = lax.top_k(logits, TOP_K)
    top_w = jax.nn.softmax(top_v, axis=-1)
    flat_e = top_e.reshape(tk)
    order = jnp.argsort(flat_e)
    sorted_e = flat_e[order]
    sorted_tok = order // TOP_K
    sorted_w = top_w.reshape(tk)[order]
    counts = jnp.bincount(flat_e, length=N_EXPERTS)
    padded = (counts + MOE_BLOCK - 1) // MOE_BLOCK * MOE_BLOCK
    pad_end = jnp.cumsum(padded)
    pad_start = pad_end - padded
    start = jnp.cumsum(counts) - counts
    dest = pad_start[sorted_e] + jnp.arange(tk) - start[sorted_e]
    n_blocks = (tk + N_EXPERTS * (MOE_BLOCK - 1) + MOE_BLOCK - 1) // MOE_BLOCK
    n_rows = n_blocks * MOE_BLOCK
    row_tok = jnp.zeros((n_rows,), jnp.int32).at[dest].set(sorted_tok.astype(jnp.int32))
    row_w = jnp.zeros((n_rows,), F32).at[dest].set(sorted_w)
    blk_e = jnp.minimum(jnp.searchsorted(pad_end, jnp.arange(n_blocks) * MOE_BLOCK, side='right'), N_EXPERTS - 1)
    xr = xt[row_tok].reshape(n_blocks, MOE_BLOCK, D_MODEL)

    def expert_block(args):
        xb, e = args
        gu = xb @ w_gu[e] + b_gu[e]
        gate, up = jnp.split(gu, 2, axis=-1)
        gate = jnp.minimum(gate, SWIGLU_LIMIT)
        up = jnp.clip(up, -SWIGLU_LIMIT, SWIGLU_LIMIT)
        glu = gate * jax.nn.sigmoid(SWIGLU_ALPHA * gate)
        return ((up + 1.0) * glu) @ w_down[e] + b_down[e]

    yr = lax.map(expert_block, (xr, blk_e)).reshape(n_rows, D_MODEL)
    out = jax.ops.segment_sum(yr.astype(F32) * row_w[:, None], row_tok, num_segments=t)
    return out.astype(h.dtype).reshape(h.shape)


def trunk_layer(x, cond, p, ctx):
    is_latent = ctx is not None
    b, n, _ = x.shape
    mod = (jax.nn.silu(cond) @ p['w_mod'] + p['b_mod'])[:, None, :]
    shift1, scale1, gate1, shift2, scale2, gate2 = jnp.split(mod, 6, axis=-1)
    h = rms_norm(x, p['norm1_g']) * (1.0 + scale1) + shift1
    points = np.cumsum(IN_SPLITS)[:-1].tolist()
    (cb, cc, cu, q, k, v, rq, rk, rv, rg, ga, gb, gc) = jnp.split(h @ p['w_in'], points, axis=-1)

    y_conv = cb * short_conv(cc * cu, p['conv_w'])

    q = q.reshape(b, n, N_HEADS, HEAD_DIM)
    k = k.reshape(b, n, N_KV, HEAD_DIM)
    v = v.reshape(b, n, N_KV, HEAD_DIM)
    if is_latent:
        k_ctx, v_ctx, r_ctx = ctx
        y_attn = latent_window_attention(axial_rope(q), axial_rope(k), v, k_ctx, v_ctx, p['attn_sink'])
        r0 = r_ctx.astype(F32)
    else:
        y_attn = context_attention(q, k, v, p['attn_sink'])
        r0 = jnp.zeros((b, 2, RET_HEADS, RET_DK, RET_DV), F32)

    rq = rq.reshape(b, n, RET_HEADS, RET_DK).astype(F32)
    rk = rk.reshape(b, n, RET_HEADS, RET_DK).astype(F32) * RET_DK ** -0.5
    rv = rv.reshape(b, n, RET_HEADS, RET_DV).astype(F32)
    log_g = jax.nn.log_sigmoid(p['ret_decay'].astype(F32))
    o_f, r_f = retention_scan(rq, rk, rv, log_g[0], r0[:, 0])
    o_b, r_b = retention_scan(rq[:, ::-1], rk[:, ::-1], rv[:, ::-1], log_g[1], r0[:, 1])
    o = o_f + o_b[:, ::-1]
    o = o * lax.rsqrt(jnp.mean(o * o, axis=-1, keepdims=True) + EPS)
    y_ret = jax.nn.silu(rg) * o.reshape(b, n, RET_HEADS * RET_DV).astype(x.dtype)

    merged = (jax.nn.sigmoid(ga) * (y_conv @ p['w_a'])
              + jax.nn.sigmoid(gb) * (y_attn @ p['w_b'])
              + jax.nn.sigmoid(gc) * (y_ret @ p['w_c']))
    x = x + gate1 * (merged @ p['w_o'])

    h2 = rms_norm(x, p['norm2_g']) * (1.0 + scale2) + shift2
    x = x + gate2 * moe_ffn(h2, p['w_router'], p['b_router'], p['w_gu'], p['b_gu'], p['w_down'], p['b_down'])
    if is_latent:
        return x, None
    return x, (k, v, jnp.stack([r_f, r_b], axis=1).astype(x.dtype))


def setup_inputs(seed: int = 0) -> dict:
    key = jax.random.key(seed)
    ks = jax.random.split(key, 26)

    def nrm(kk, shape, scale=1.0):
        return jax.random.normal(kk, shape, F32) * scale

    gam = 1.0 - 2.0 ** (-5.0 - np.arange(RET_HEADS, dtype=np.float32))
    decay_logit = jnp.asarray(np.log(gam / (1.0 - gam)), dtype=F32)
    return {
        'x_prompt': nrm(ks[0], (BATCH, SEQ, D_MODEL)),
        'x_sample': nrm(ks[1], (DEC_BATCH, DEC_SEQ, D_MODEL)),
        'cache_k': nrm(ks[2], (DEC_BATCH, DEPTH, PAST_LEN, N_KV, HEAD_DIM)),
        'cache_v': nrm(ks[3], (DEC_BATCH, DEPTH, PAST_LEN, N_KV, HEAD_DIM)),
        'state_ret': nrm(ks[4], (DEC_BATCH, DEPTH, 2, RET_HEADS, RET_DK, RET_DV)),
        'c': nrm(ks[5], (DEC_BATCH, D_MODEL)),
        'c_ctx': nrm(ks[6], (D_MODEL,)),
        'norm1_g': 1.0 + nrm(ks[7], (DEPTH, D_MODEL), 0.02),
        'norm2_g': 1.0 + nrm(ks[8], (DEPTH, D_MODEL), 0.02),
        'w_mod': nrm(ks[9], (DEPTH, D_MODEL, 6 * D_MODEL), 0.5 * D_MODEL ** -0.5),
        'b_mod': nrm(ks[10], (DEPTH, 6 * D_MODEL), 0.02),
        'w_in': nrm(ks[11], (DEPTH, D_MODEL, IN_COLS), D_MODEL ** -0.5),
        'conv_w': nrm(ks[12], (DEPTH, CONV_W, D_CONV), CONV_W ** -0.5),
        'attn_sink': nrm(ks[13], (DEPTH, N_HEADS), 0.5),
        'ret_decay': decay_logit + nrm(ks[14], (DEPTH, 2, RET_HEADS), 0.1),
        'w_a': nrm(ks[15], (DEPTH, D_CONV, D_MODEL), D_CONV ** -0.5),
        'w_b': nrm(ks[16], (DEPTH, N_HEADS * HEAD_DIM, D_MODEL), (N_HEADS * HEAD_DIM) ** -0.5),
        'w_c': nrm(ks[17], (DEPTH, RET_HEADS * RET_DV, D_MODEL), (RET_HEADS * RET_DV) ** -0.5),
        'w_o': nrm(ks[18], (DEPTH, D_MODEL, D_MODEL), D_MODEL ** -0.5),
        'w_router': nrm(ks[19], (DEPTH, D_MODEL, N_EXPERTS), D_MODEL ** -0.5),
        'b_router': nrm(ks[20], (DEPTH, N_EXPERTS), 0.01),
        'w_gu': nrm(ks[21], (DEPTH, N_EXPERTS, D_MODEL, 2 * D_EXPERT), D_MODEL ** -0.5),
        'b_gu': nrm(ks[22], (DEPTH, N_EXPERTS, 2 * D_EXPERT), 0.01),
        'w_down': nrm(ks[23], (DEPTH, N_EXPERTS, D_EXPERT, D_MODEL), D_EXPERT ** -0.5),
        'b_down': nrm(ks[24], (DEPTH, N_EXPERTS, D_MODEL), 0.01),
        'final_g': 1.0 + nrm(ks[25], (D_MODEL,), 0.02),
    }


def reference(x_prompt, x_sample, cache_k, cache_v, state_ret, c, c_ctx, norm1_g, norm2_g, w_mod, b_mod,
              w_in, conv_w, attn_sink, ret_decay, w_a, w_b, w_c, w_o, w_router, b_router, w_gu, b_gu,
              w_down, b_down, final_g):
    def params(l):
        return {'norm1_g': norm1_g[l], 'norm2_g': norm2_g[l], 'w_mod': w_mod[l], 'b_mod': b_mod[l],
                'w_in': w_in[l], 'conv_w': conv_w[l], 'attn_sink': attn_sink[l], 'ret_decay': ret_decay[l],
                'w_a': w_a[l], 'w_b': w_b[l], 'w_c': w_c[l], 'w_o': w_o[l],
                'w_router': w_router[l], 'b_router': b_router[l], 'w_gu': w_gu[l], 'b_gu': b_gu[l],
                'w_down': w_down[l], 'b_down': b_down[l]}

    xp = x_prompt
    ks, vs, rs = [], [], []
    for l in range(DEPTH):
        xp, (k_l, v_l, r_l) = trunk_layer(xp, c_ctx[None, :], params(l), None)
        ks.append(k_l)
        vs.append(v_l)
        rs.append(r_l)
    y_prompt = rms_norm(xp, final_g)
    new_cache_k = jnp.stack(ks, axis=1)
    new_cache_v = jnp.stack(vs, axis=1)
    new_state_ret = jnp.stack(rs, axis=1)

    xs = x_sample
    for l in range(DEPTH):
        xs, _ = trunk_layer(xs, c, params(l), (cache_k[:, l], cache_v[:, l], state_ret[:, l]))
    y_sample = rms_norm(xs, final_g)
    return (y_prompt, y_sample, new_cache_k, new_cache_v, new_state_ret)
```

```python
import functools

import numpy as np
import jax
import jax.numpy as jnp
from jax import lax
from jax.experimental import pallas as pl
from jax.experimental.pallas import tpu as pltpu

F32 = jnp.float32
BF16 = jnp.bfloat16

D_MODEL = 1024
BATCH = 16
SEQ = 256
DEPTH = 2
DEC_BATCH = 2
DEC_SEQ = 2048
PAST_LEN = 256
GRID_W = 64
HEAD_DIM = 64
D_CONV = 256
N_HEADS = 8
N_KV = 2
GROUP = N_HEADS // N_KV
WINDOW = 128
ROPE_BASE = 10000.0
RET_HEADS = 4
RET_DK = 64
RET_DV = 64
CHUNK = 128
N_EXPERTS = 32
TOP_K = 4
D_EXPERT = D_MODEL
SWIGLU_LIMIT = 7.0
SWIGLU_ALPHA = 1.702
EPS = 1e-6
NEG_INF = -1e30

T_CTX = BATCH * SEQ
T_LAT = DEC_BATCH * DEC_SEQ
T = T_CTX + T_LAT
D_ATTN = N_HEADS * HEAD_DIM
D_KV = N_KV * HEAD_DIM
D_RET = RET_HEADS * RET_DK
C_CONV = 0
C_Q = 3 * D_CONV
C_K = C_Q + D_ATTN
C_V = C_K + D_KV
C_RET = C_V + D_KV
C_GATE = C_RET + 4 * D_RET
IN_COLS = C_GATE + 3 * D_MODEL

TM = 512
MOE_M = 128
N_ASSIGN = T * TOP_K
N_BLOCKS = (N_ASSIGN + N_EXPERTS * (MOE_M - 1) + MOE_M - 1) // MOE_M
N_ROWS = N_BLOCKS * MOE_M
LANES = 128
SUBLANES = 8
ROW_VREGS = D_MODEL // LANES
TILE_STRIDE = MOE_M + SUBLANES
MIB = 1024 * 1024


def _sigmoid(x):
    return 1.0 / (1.0 + jnp.exp(-x))


def _mod_group(i):
    n_ctx = T_CTX // TM
    per_lat = DEC_SEQ // TM
    g = jnp.zeros_like(i)
    for b in range(DEC_BATCH):
        g = g + (i >= n_ctx + b * per_lat).astype(jnp.int32)
    return g


def _mod_kernel(cond_ref, w_ref, b_ref, o_ref):
    c = cond_ref[...]
    s = c * _sigmoid(c)
    o_ref[0] = jnp.dot(s.astype(BF16), w_ref[0].astype(BF16), preferred_element_type=F32) + b_ref[0]


def _modulation(cond8, w_mod, b_mod):
    n_col = 4
    cw = 6 * D_MODEL // n_col
    return pl.pallas_call(
        _mod_kernel,
        grid=(DEPTH, n_col),
        in_specs=[pl.BlockSpec((8, D_MODEL), lambda l, j: (0, 0)),
                  pl.BlockSpec((1, D_MODEL, cw), lambda l, j: (l, 0, j)),
                  pl.BlockSpec((1, 1, cw), lambda l, j: (l, 0, j))],
        out_specs=pl.BlockSpec((1, 8, cw), lambda l, j: (l, 0, j)),
        out_shape=jax.ShapeDtypeStruct((DEPTH, 8, 6 * D_MODEL), F32),
        compiler_params=pltpu.CompilerParams(vmem_limit_bytes=32 * MIB),
        name="modulation",
    )(cond8, w_mod, b_mod.reshape(DEPTH, 1, 6 * D_MODEL))


def _inproj_kernel(x_ref, mod_ref, g_ref, w_ref, conv_o, q_o, k_o, v_o, ret_o, gate_o):
    x = x_ref[...]
    ms = jnp.mean(x * x, axis=-1, keepdims=True)
    h = x * lax.rsqrt(ms + EPS) * g_ref[...]
    h = h * (1.0 + mod_ref[0, 1:2, :]) + mod_ref[0, 0:1, :]
    hb = h.astype(BF16)

    def proj(c0, c1):
        return jnp.dot(hb, w_ref[:, c0:c1], preferred_element_type=F32)

    conv_o[...] = proj(C_CONV, C_Q).astype(BF16)
    q_o[...] = (proj(C_Q, C_K) * HEAD_DIM ** -0.5).astype(BF16)
    k_o[...] = proj(C_K, C_V)
    v_o[...] = proj(C_V, C_RET)
    ret_o[:, 0:D_RET] = proj(C_RET, C_RET + D_RET).astype(BF16)
    ret_o[:, D_RET:2 * D_RET] = (proj(C_RET + D_RET, C_RET + 2 * D_RET) * RET_DK ** -0.5).astype(BF16)
    ret_o[:, 2 * D_RET:4 * D_RET] = proj(C_RET + 2 * D_RET, C_GATE).astype(BF16)
    for b in range(3):
        g = proj(C_GATE + b * D_MODEL, C_GATE + (b + 1) * D_MODEL)
        gate_o[:, b * D_MODEL:(b + 1) * D_MODEL] = _sigmoid(g).astype(BF16)


def _inproj(x, mod_l, g1, w_in_bf):
    row = lambda i: (i, 0)
    return pl.pallas_call(
        _inproj_kernel,
        grid=(T // TM,),
        in_specs=[pl.BlockSpec((TM, D_MODEL), row),
                  pl.BlockSpec((1, 6, D_MODEL), lambda i: (_mod_group(i), 0, 0)),
                  pl.BlockSpec((1, D_MODEL), lambda i: (0, 0)),
                  pl.BlockSpec((D_MODEL, IN_COLS), lambda i: (0, 0), pipeline_mode=pl.Buffered(1))],
        out_specs=[pl.BlockSpec((TM, 3 * D_CONV), row),
                   pl.BlockSpec((TM, D_ATTN), row),
                   pl.BlockSpec((TM, D_KV), row),
                   pl.BlockSpec((TM, D_KV), row),
                   pl.BlockSpec((TM, 4 * D_RET), row),
                   pl.BlockSpec((TM, 3 * D_MODEL), row)],
        out_shape=[jax.ShapeDtypeStruct((T, 3 * D_CONV), BF16),
                   jax.ShapeDtypeStruct((T, D_ATTN), BF16),
                   jax.ShapeDtypeStruct((T, D_KV), F32),
                   jax.ShapeDtypeStruct((T, D_KV), F32),
                   jax.ShapeDtypeStruct((T, 4 * D_RET), BF16),
                   jax.ShapeDtypeStruct((T, 3 * D_MODEL), BF16)],
        compiler_params=pltpu.CompilerParams(vmem_limit_bytes=48 * MIB),
        name="inproj",
    )(x, mod_l, g1, w_in_bf)


def _rope(x, cos, sin_signed):
    lane = lax.broadcasted_iota(jnp.int32, x.shape, 1)
    first = (lane % 32) < 16
    partner = jnp.where(first, pltpu.roll(x, x.shape[1] - 16, 1), pltpu.roll(x, 16, 1))
    return x * cos + partner * sin_signed


def _dot_nt(a, b):
    return lax.dot_general(a, b, (((1,), (1,)), ((), ())), preferred_element_type=F32)


def _dot_tn(a, b):
    return lax.dot_general(a, b, (((0,), (0,)), ((), ())), preferred_element_type=F32)


def _mixer_kernel(*refs, n, latent):
    if latent:
        (sink_ref, conv_ref, q_ref, k_ref, v_ref, ret_ref, cw_ref, rd_ref, cos_ref, sin_ref, ck_ref, cv_ref,
         r0_ref, yc_ref, ya_ref, yr_ref, kb, vb, o_f, o_b, rst, dmat, qdec, kdec) = refs
    else:
        (sink_ref, conv_ref, q_ref, k_ref, v_ref, ret_ref, cw_ref, rd_ref,
         yc_ref, ya_ref, yr_ref, rfin_ref, kb, vb, o_f, o_b, rst, dmat, qdec, kdec) = refs
    nb = n // CHUNK

    cv = conv_ref[...].astype(F32)
    cb, cc, cu = cv[:, 0:D_CONV], cv[:, D_CONV:2 * D_CONV], cv[:, 2 * D_CONV:3 * D_CONV]
    p = cc * cu
    row = lax.broadcasted_iota(jnp.int32, p.shape, 0)
    prev = jnp.where(row == 0, 0.0, pltpu.roll(p, 1, 0))
    nxt = jnp.where(row == n - 1, 0.0, pltpu.roll(p, n - 1, 0))
    cw = cw_ref[...]
    yc_ref[...] = (cb * (prev * cw[0:1, :] + p * cw[1:2, :] + nxt * cw[2:3, :])).astype(BF16)

    if latent:
        kr = _rope(k_ref[...], cos_ref[...], sin_ref[...])
        zpad = jnp.zeros((CHUNK, D_KV), BF16)
        kb[0:CHUNK, :] = zpad
        vb[0:CHUNK, :] = zpad
        kb[CHUNK + n:2 * CHUNK + n, :] = zpad
        vb[CHUNK + n:2 * CHUNK + n, :] = zpad
        kb[CHUNK:CHUNK + n, :] = kr.astype(BF16)
        vb[CHUNK:CHUNK + n, :] = v_ref[...].astype(BF16)
        ckb = ck_ref[0, 0].astype(BF16)
        cvb = cv_ref[0, 0].astype(BF16)
    else:
        kb[...] = k_ref[...].astype(BF16)
        vb[...] = v_ref[...].astype(BF16)

    def attn_block(j):
        r0 = j * CHUNK if isinstance(j, int) else pl.multiple_of(j * CHUNK, CHUNK)
        qj = q_ref[pl.ds(r0, CHUNK), :]
        if latent:
            cosj = cos_ref[pl.ds(r0, CHUNK), :]
            sinj = sin_ref[pl.ds(r0, CHUNK), :]
            qf = qj.astype(F32)
            qj = jnp.concatenate(
                [_rope(qf[:, c * LANES:(c + 1) * LANES], cosj, sinj) for c in range(D_ATTN // LANES)],
                axis=1).astype(BF16)
            kw = kb[pl.ds(r0, 3 * CHUNK), :]
            vw = vb[pl.ds(r0, 3 * CHUNK), :]
            qpos = r0 + lax.broadcasted_iota(jnp.int32, (CHUNK, 3 * CHUNK), 0)
            kpos = r0 - CHUNK + lax.broadcasted_iota(jnp.int32, (CHUNK, 3 * CHUNK), 1)
            ok = (jnp.abs(qpos - kpos) <= WINDOW) & (kpos >= 0) & (kpos < n)
            ok = jnp.concatenate([ok] * GROUP, axis=0)
        else:
            kw = kb[...]
            vw = vb[...]
        for g in range(N_KV):
            lo = g * HEAD_DIM
            qg = jnp.concatenate(
                [qj[:, (GROUP * g + i) * HEAD_DIM:(GROUP * g + i + 1) * HEAD_DIM] for i in range(GROUP)], axis=0)
            sk = jnp.concatenate(
                [jnp.full((CHUNK, 1), sink_ref[GROUP * g + i], F32) for i in range(GROUP)], axis=0)
            s = _dot_nt(qg, kw[:, lo:lo + HEAD_DIM])
            if latent:
                s = jnp.where(ok, s, NEG_INF)
                s2 = _dot_nt(qg, ckb[:, lo:lo + HEAD_DIM])
            m = jnp.maximum(jnp.max(s, axis=-1, keepdims=True), sk)
            if latent:
                m = jnp.maximum(m, jnp.max(s2, axis=-1, keepdims=True))
            pw = jnp.exp(s - m)
            den = jnp.sum(pw, axis=-1, keepdims=True) + jnp.exp(sk - m)
            o = jnp.dot(pw.astype(BF16), vw[:, lo:lo + HEAD_DIM], preferred_element_type=F32)
            if latent:
                p2 = jnp.exp(s2 - m)
                den = den + jnp.sum(p2, axis=-1, keepdims=True)
                o = o + jnp.dot(p2.astype(BF16), cvb[:, lo:lo + HEAD_DIM], preferred_element_type=F32)
            o = o / den
            ya_ref[pl.ds(r0, CHUNK), g * GROUP * HEAD_DIM:(g + 1) * GROUP * HEAD_DIM] = jnp.concatenate(
                [o[i * CHUNK:(i + 1) * CHUNK, :] for i in range(GROUP)], axis=1).astype(BF16)

    if latent:
        def attn_body(j, carry):
            attn_block(j)
            return carry
        lax.fori_loop(0, nb, attn_body, 0)
    else:
        for j in range(nb):
            attn_block(j)

    rd = rd_ref[...]
    log_g = jnp.minimum(rd, 0.0) - jnp.log(1.0 + jnp.exp(-jnp.abs(rd)))
    ii = lax.broadcasted_iota(jnp.int32, (CHUNK, CHUNK), 0).astype(F32)
    jj = lax.broadcasted_iota(jnp.int32, (CHUNK, CHUNK), 1).astype(F32)
    for d in range(2):
        for h in range(RET_HEADS):
            r = d * RET_HEADS + h
            lg = log_g[r:r + 1, :]
            diff = (ii - jj) if d == 0 else (jj - ii)
            dmat[r] = jnp.where(diff >= 0, jnp.exp(jnp.maximum(diff, 0.0) * lg), 0.0)
            if d == 0:
                qdec[r] = jnp.exp((ii + 1.0) * lg)
                kdec[r] = jnp.exp((CHUNK - 1.0 - ii) * lg)
            else:
                qdec[r] = jnp.exp((CHUNK - ii) * lg)
                kdec[r] = jnp.exp(ii * lg)
            if latent:
                rst[r] = r0_ref[0, 0, d, h]
            else:
                rst[r] = jnp.zeros((RET_DK, RET_DV), F32)
    chunk_decay = jnp.exp(float(CHUNK) * log_g)

    def ret_chunk(c0, d, out_ref):
        blk = ret_ref[pl.ds(c0, CHUNK), :]
        outs = []
        for h in range(RET_HEADS):
            r = d * RET_HEADS + h
            qc = blk[:, h * RET_DK:(h + 1) * RET_DK]
            kc = blk[:, D_RET + h * RET_DK:D_RET + (h + 1) * RET_DK]
            vc = blk[:, 2 * D_RET + h * RET_DV:2 * D_RET + (h + 1) * RET_DV]
            inner = (_dot_nt(qc, kc) * dmat[r]).astype(BF16)
            state = rst[r]
            o = jnp.dot(inner, vc, preferred_element_type=F32)
            o = o + jnp.dot(qc, state.astype(BF16), preferred_element_type=F32) * qdec[r][:, 0:RET_DV]
            kd = (kc.astype(F32) * kdec[r][:, 0:RET_DK]).astype(BF16)
            rst[r] = state * chunk_decay[r:r + 1, 0:RET_DV] + _dot_tn(kd, vc)
            outs.append(o)
        out_ref[pl.ds(c0, CHUNK), :] = jnp.concatenate(outs, axis=1)

    def ret_body(j, carry):
        ret_chunk(pl.multiple_of(j * CHUNK, CHUNK), 0, o_f)
        ret_chunk(pl.multiple_of((nb - 1 - j) * CHUNK, CHUNK), 1, o_b)
        return carry

    lax.fori_loop(0, nb, ret_body, 0)

    def norm_body(j, carry):
        c0 = pl.multiple_of(j * CHUNK, CHUNK)
        o = o_f[pl.ds(c0, CHUNK), :] + o_b[pl.ds(c0, CHUNK), :]
        parts = []
        for h in range(RET_HEADS):
            oh = o[:, h * RET_DV:(h + 1) * RET_DV]
            ms = jnp.mean(oh * oh, axis=-1, keepdims=True)
            parts.append(oh * lax.rsqrt(ms + EPS))
        rg = ret_ref[pl.ds(c0, CHUNK), 3 * D_RET:4 * D_RET].astype(F32)
        yr_ref[pl.ds(c0, CHUNK), :] = (rg * _sigmoid(rg) * jnp.concatenate(parts, axis=1)).astype(BF16)
        return carry

    lax.fori_loop(0, nb, norm_body, 0)

    if not latent:
        for d in range(2):
            for h in range(RET_HEADS):
                rfin_ref[0, d, h] = rst[d * RET_HEADS + h]


def _mixers(conv_in, q, k, v, ret, conv_w_l, sink_l, rd8, *, latent, cos=None, sin=None, cache_k=None,
            cache_v=None, state=None, layer=0):
    n = DEC_SEQ if latent else SEQ
    nseq = DEC_BATCH if latent else BATCH
    off = T_CTX // n if latent else 0
    seq = lambda s: (s + off, 0)
    const = lambda s: (0, 0)
    in_specs = [pl.BlockSpec(memory_space=pltpu.SMEM),
                pl.BlockSpec((n, 3 * D_CONV), seq),
                pl.BlockSpec((n, D_ATTN), seq),
                pl.BlockSpec((n, D_KV), seq),
                pl.BlockSpec((n, D_KV), seq),
                pl.BlockSpec((n, 4 * D_RET), seq),
                pl.BlockSpec((3, D_CONV), const),
                pl.BlockSpec((8, LANES), const)]
    args = [sink_l, conv_in, q, k, v, ret, conv_w_l, rd8]
    out_specs = [pl.BlockSpec((n, D_CONV), lambda s: (s, 0)),
                 pl.BlockSpec((n, D_ATTN), lambda s: (s, 0)),
                 pl.BlockSpec((n, D_RET), lambda s: (s, 0))]
    out_shape = [jax.ShapeDtypeStruct((nseq * n, D_CONV), BF16),
                 jax.ShapeDtypeStruct((nseq * n, D_ATTN), BF16),
                 jax.ShapeDtypeStruct((nseq * n, D_RET), BF16)]
    if latent:
        in_specs += [pl.BlockSpec((n, LANES), const),
                     pl.BlockSpec((n, LANES), const),
                     pl.BlockSpec((1, 1, PAST_LEN, D_KV), lambda s: (s, layer, 0, 0)),
                     pl.BlockSpec((1, 1, PAST_LEN, D_KV), lambda s: (s, layer, 0, 0)),
                     pl.BlockSpec((1, 1, 2, RET_HEADS, RET_DK, RET_DV), lambda s: (s, layer, 0, 0, 0, 0))]
        args += [cos, sin, cache_k, cache_v, state]
        kv_rows = n + 2 * CHUNK
    else:
        out_specs.append(pl.BlockSpec((1, 2, RET_HEADS, RET_DK, RET_DV), lambda s: (s, 0, 0, 0, 0)))
        out_shape.append(jax.ShapeDtypeStruct((nseq, 2, RET_HEADS, RET_DK, RET_DV), F32))
        kv_rows = n
    scratch = [pltpu.VMEM((kv_rows, D_KV), BF16),
               pltpu.VMEM((kv_rows, D_KV), BF16),
               pltpu.VMEM((n, D_RET), F32),
               pltpu.VMEM((n, D_RET), F32),
               pltpu.VMEM((2 * RET_HEADS, RET_DK, RET_DV), F32),
               pltpu.VMEM((2 * RET_HEADS, CHUNK, CHUNK), F32),
               pltpu.VMEM((2 * RET_HEADS, CHUNK, CHUNK), F32),
               pltpu.VMEM((2 * RET_HEADS, CHUNK, CHUNK), F32)]
    return pl.pallas_call(
        functools.partial(_mixer_kernel, n=n, latent=latent),
        grid=(nseq,),
        in_specs=in_specs,
        out_specs=out_specs,
        out_shape=out_shape,
        scratch_shapes=scratch,
        compiler_params=pltpu.CompilerParams(vmem_limit_bytes=48 * MIB),
        name="mixers_latent" if latent else "mixers_context",
    )(*args)


def _merge_kernel(yc_ref, ya_ref, yr_ref, gate_ref, x_ref, mod_ref, g2_ref, wa_ref, wb_ref, wc_ref, wo_ref,
                  wrh_ref, wrl_ref, br_ref, x1_o, h2_o, lg_o):
    merged = (gate_ref[:, 0:D_MODEL].astype(F32)
              * jnp.dot(yc_ref[...], wa_ref[...], preferred_element_type=F32)
              + gate_ref[:, D_MODEL:2 * D_MODEL].astype(F32)
              * jnp.dot(ya_ref[...], wb_ref[...], preferred_element_type=F32)
              + gate_ref[:, 2 * D_MODEL:3 * D_MODEL].astype(F32)
              * jnp.dot(yr_ref[...], wc_ref[...], preferred_element_type=F32))
    x1 = x_ref[...] + mod_ref[0, 2:3, :] * jnp.dot(merged.astype(BF16), wo_ref[...], preferred_element_type=F32)
    x1_o[...] = x1
    ms = jnp.mean(x1 * x1, axis=-1, keepdims=True)
    h2 = x1 * lax.rsqrt(ms + EPS) * g2_ref[...]
    h2 = h2 * (1.0 + mod_ref[0, 4:5, :]) + mod_ref[0, 3:4, :]
    h2_o[...] = h2
    hh = h2.astype(BF16)
    hl = (h2 - hh.astype(F32)).astype(BF16)
    lg_o[...] = (jnp.dot(hh, wrh_ref[...], preferred_element_type=F32)
                 + jnp.dot(hl, wrh_ref[...], preferred_element_type=F32)
                 + jnp.dot(hh, wrl_ref[...], preferred_element_type=F32)
                 + br_ref[...])


def _merge(yc, ya, yr, gates, x, mod_l, g2, wa, wb, wc, wo, wrh, wrl, br):
    row = lambda i: (i, 0)
    const = lambda i: (0, 0)
    return pl.pallas_call(
        _merge_kernel,
        grid=(T // TM,),
        in_specs=[pl.BlockSpec((TM, D_CONV), row),
                  pl.BlockSpec((TM, D_ATTN), row),
                  pl.BlockSpec((TM, D_RET), row),
                  pl.BlockSpec((TM, 3 * D_MODEL), row),
                  pl.BlockSpec((TM, D_MODEL), row),
                  pl.BlockSpec((1, 6, D_MODEL), lambda i: (_mod_group(i), 0, 0)),
                  pl.BlockSpec((1, D_MODEL), const),
                  pl.BlockSpec((D_CONV, D_MODEL), const),
                  pl.BlockSpec((D_ATTN, D_MODEL), const),
                  pl.BlockSpec((D_RET, D_MODEL), const),
                  pl.BlockSpec((D_MODEL, D_MODEL), const),
                  pl.BlockSpec((D_MODEL, LANES), const),
                  pl.BlockSpec((D_MODEL, LANES), const),
                  pl.BlockSpec((1, LANES), const)],
        out_specs=[pl.BlockSpec((TM, D_MODEL), row),
                   pl.BlockSpec((TM, D_MODEL), row),
                   pl.BlockSpec((TM, LANES), row)],
        out_shape=[jax.ShapeDtypeStruct((T, D_MODEL), F32),
                   jax.ShapeDtypeStruct((T, D_MODEL), F32),
                   jax.ShapeDtypeStruct((T, LANES), F32)],
        compiler_params=pltpu.CompilerParams(vmem_limit_bytes=48 * MIB),
        name="merge_router",
    )(yc, ya, yr, gates, x, mod_l, g2, wa, wb, wc, wo, wrh, wrl, br)


def _gather_kernel(tok_ref, x_ref, o_ref, tile):
    for mi in range(MOE_M):
        t = jnp.minimum(tok_ref[0, 0, mi], T - 1)
        slab = x_ref[pl.ds(pl.multiple_of(t * ROW_VREGS, ROW_VREGS), ROW_VREGS), :]
        tile[pl.ds(mi, ROW_VREGS, stride=TILE_STRIDE), :] = slab
    o_ref[...] = jnp.concatenate(
        [tile[c * TILE_STRIDE:c * TILE_STRIDE + MOE_M, :] for c in range(ROW_VREGS)], axis=1).astype(BF16)


def _gather_rows(row_tok3, h2_tiles):
    return pl.pallas_call(
        _gather_kernel,
        grid=(N_BLOCKS,),
        in_specs=[pl.BlockSpec((1, 1, MOE_M), lambda b: (b, 0, 0), memory_space=pltpu.SMEM),
                  pl.BlockSpec((T * ROW_VREGS, LANES), lambda b: (0, 0), pipeline_mode=pl.Buffered(1))],
        out_specs=pl.BlockSpec((MOE_M, D_MODEL), lambda b: (b, 0)),
        out_shape=jax.ShapeDtypeStruct((N_ROWS, D_MODEL), BF16),
        scratch_shapes=[pltpu.VMEM((ROW_VREGS * TILE_STRIDE, LANES), F32)],
        compiler_params=pltpu.CompilerParams(vmem_limit_bytes=48 * MIB),
        name="moe_gather",
    )(row_tok3, h2_tiles)


def _expert_kernel(blk_e_ref, x_ref, wgu_ref, bgu_ref, wd_ref, bd_ref, y_ref):
    del blk_e_ref
    gu = jnp.dot(x_ref[...], wgu_ref[0].astype(BF16), preferred_element_type=F32) + bgu_ref[0]
    gate = jnp.minimum(gu[:, 0:D_EXPERT], SWIGLU_LIMIT)
    up = jnp.clip(gu[:, D_EXPERT:2 * D_EXPERT], -SWIGLU_LIMIT, SWIGLU_LIMIT)
    glu = gate * _sigmoid(SWIGLU_ALPHA * gate)
    mid = ((up + 1.0) * glu).astype(BF16)
    y = jnp.dot(mid, wd_ref[0].astype(BF16), preferred_element_type=F32) + bd_ref[0]
    y_ref[...] = y.astype(BF16)


def _experts(blk_e, xr, w_gu_l, b_gu_l, w_down_l, b_down_l):
    grid_spec = pltpu.PrefetchScalarGridSpec(
        num_scalar_prefetch=1,
        grid=(N_BLOCKS,),
        in_specs=[pl.BlockSpec((MOE_M, D_MODEL), lambda b, e: (b, 0)),
                  pl.BlockSpec((1, D_MODEL, 2 * D_EXPERT), lambda b, e: (e[b], 0, 0)),
                  pl.BlockSpec((1, 1, 2 * D_EXPERT), lambda b, e: (e[b], 0, 0)),
                  pl.BlockSpec((1, D_EXPERT, D_MODEL), lambda b, e: (e[b], 0, 0)),
                  pl.BlockSpec((1, 1, D_MODEL), lambda b, e: (e[b], 0, 0))],
        out_specs=pl.BlockSpec((MOE_M, D_MODEL), lambda b, e: (b, 0)),
    )
    return pl.pallas_call(
        _expert_kernel,
        grid_spec=grid_spec,
        out_shape=jax.ShapeDtypeStruct((N_ROWS, D_MODEL), BF16),
        compiler_params=pltpu.CompilerParams(vmem_limit_bytes=48 * MIB),
        name="moe_experts",
    )(blk_e, xr, w_gu_l, b_gu_l.reshape(N_EXPERTS, 1, 2 * D_EXPERT), w_down_l,
      b_down_l.reshape(N_EXPERTS, 1, D_MODEL))


SCATTER_UNROLL = 8


def _combine_kernel(tok_ref, w_ref, y_ref, o_ref, acc, tile, sem):
    b = pl.program_id(0)

    @pl.when(b == 0)
    def _():
        acc[...] = jnp.zeros_like(acc)

    y = y_ref[...].astype(F32)
    for c in range(ROW_VREGS):
        tile[c * TILE_STRIDE:c * TILE_STRIDE + MOE_M, :] = y[:, c * LANES:(c + 1) * LANES]
    for m0 in range(0, MOE_M, SCATTER_UNROLL):
        addrs, vals = [], []
        for u in range(SCATTER_UNROLL):
            mi = m0 + u
            a = pl.multiple_of(tok_ref[0, 0, mi] * ROW_VREGS, ROW_VREGS)
            yv = tile[pl.ds(mi, ROW_VREGS, stride=TILE_STRIDE), :]
            addrs.append(a)
            vals.append(acc[pl.ds(a, ROW_VREGS), :] + w_ref[0, 0, mi] * yv)
        for u in range(SCATTER_UNROLL):
            acc[pl.ds(addrs[u], ROW_VREGS), :] = vals[u]

    @pl.when(b == pl.num_programs(0) - 1)
    def _():
        cp = pltpu.make_async_copy(acc.at[pl.ds(0, T * ROW_VREGS)], o_ref, sem)
        cp.start()
        cp.wait()


def _combine(row_tok3, row_w3, yr):
    return pl.pallas_call(
        _combine_kernel,
        grid=(N_BLOCKS,),
        in_specs=[pl.BlockSpec((1, 1, MOE_M), lambda b: (b, 0, 0), memory_space=pltpu.SMEM),
                  pl.BlockSpec((1, 1, MOE_M), lambda b: (b, 0, 0), memory_space=pltpu.SMEM),
                  pl.BlockSpec((MOE_M, D_MODEL), lambda b: (b, 0))],
        out_specs=pl.BlockSpec(memory_space=pl.ANY),
        out_shape=jax.ShapeDtypeStruct((T * ROW_VREGS, LANES), F32),
        scratch_shapes=[pltpu.VMEM(((T + 1) * ROW_VREGS, LANES), F32),
                        pltpu.VMEM((ROW_VREGS * TILE_STRIDE, LANES), F32),
                        pltpu.SemaphoreType.DMA(())],
        compiler_params=pltpu.CompilerParams(vmem_limit_bytes=48 * MIB),
        name="moe_combine",
    )(row_tok3, row_w3, yr)


def _route(logits):
    top_v, top_e = lax.top_k(logits, TOP_K)
    top_w = jax.nn.softmax(top_v, axis=-1)
    flat_e = top_e.reshape(N_ASSIGN)
    onehot = (flat_e[:, None] == jnp.arange(N_EXPERTS, dtype=flat_e.dtype)[None, :]).astype(jnp.int32)
    csum = jnp.cumsum(onehot, axis=0)
    counts = csum[-1]
    rank = jnp.take_along_axis(csum, flat_e[:, None], axis=1)[:, 0] - 1
    padded = (counts + MOE_M - 1) // MOE_M * MOE_M
    pad_end = jnp.cumsum(padded)
    pad_start = pad_end - padded
    dest = pad_start[flat_e] + rank
    tok = (jnp.arange(N_ASSIGN, dtype=jnp.int32) // TOP_K)
    row_tok = jnp.full((N_ROWS,), T, jnp.int32).at[dest].set(tok)
    row_w = jnp.zeros((N_ROWS,), F32).at[dest].set(top_w.reshape(N_ASSIGN))
    blk_e = jnp.minimum(jnp.searchsorted(pad_end, jnp.arange(N_BLOCKS, dtype=jnp.int32) * MOE_M, side='right'),
                        N_EXPERTS - 1).astype(jnp.int32)
    return row_tok.reshape(N_BLOCKS, 1, MOE_M), row_w.reshape(N_BLOCKS, 1, MOE_M), blk_e


def _residual_kernel(x_ref, moe_ref, mod_ref, g_ref, o_ref, *, final):
    x = x_ref[...] + mod_ref[0, 5:6, :] * moe_ref[...]
    if final:
        ms = jnp.mean(x * x, axis=-1, keepdims=True)
        x = x * lax.rsqrt(ms + EPS) * g_ref[...]
    o_ref[...] = x


def _residual(x1, moe, mod_l, g, final):
    row = lambda i: (i, 0)
    return pl.pallas_call(
        functools.partial(_residual_kernel, final=final),
        grid=(T // TM,),
        in_specs=[pl.BlockSpec((TM, D_MODEL), row),
                  pl.BlockSpec((TM, D_MODEL), row),
                  pl.BlockSpec((1, 6, D_MODEL), lambda i: (_mod_group(i), 0, 0)),
                  pl.BlockSpec((1, D_MODEL), lambda i: (0, 0))],
        out_specs=pl.BlockSpec((TM, D_MODEL), row),
        out_shape=jax.ShapeDtypeStruct((T, D_MODEL), F32),
        compiler_params=pltpu.CompilerParams(vmem_limit_bytes=32 * MIB),
        name="residual_final" if final else "residual",
    )(x1, moe, mod_l, g)


def _rope_tables():
    t = np.arange(DEC_SEQ)
    pos = np.stack([t // GRID_W, t % GRID_W], axis=1).astype(np.float32)
    half = HEAD_DIM // 2
    inv = jnp.asarray(ROPE_BASE, F32) ** (-jnp.arange(0, half, 2, dtype=F32) / half)
    d = np.arange(HEAD_DIM)
    which = d // half
    freq = d % (half // 2)
    sign = np.where((d % half) < half // 2, -1.0, 1.0).astype(np.float32)
    ang = jnp.asarray(pos)[:, which] * inv[freq][None, :]
    cos = jnp.cos(ang)
    sin = jnp.sin(ang) * jnp.asarray(sign)[None, :]
    reps = LANES // HEAD_DIM
    return jnp.tile(cos, (1, reps)), jnp.tile(sin, (1, reps))


def kernel(x_prompt, x_sample, cache_k, cache_v, state_ret, c, c_ctx, norm1_g, norm2_g, w_mod, b_mod, w_in, conv_w, attn_sink, ret_decay, w_a, w_b, w_c, w_o, w_router, b_router, w_gu, b_gu, w_down, b_down, final_g):
    x = jnp.concatenate([x_prompt.reshape(T_CTX, D_MODEL), x_sample.reshape(T_LAT, D_MODEL)], axis=0)
    cond8 = jnp.zeros((8, D_MODEL), F32).at[0].set(c_ctx).at[1:1 + DEC_BATCH].set(c)
    mod = _modulation(cond8, w_mod, b_mod)
    cos, sin = _rope_tables()
    ck = cache_k.reshape(DEC_BATCH, DEPTH, PAST_LEN, D_KV)
    cv = cache_v.reshape(DEC_BATCH, DEPTH, PAST_LEN, D_KV)

    ks, vs, rs = [], [], []
    for l in range(DEPTH):
        mod_l = mod[l, 0:1 + DEC_BATCH].reshape(1 + DEC_BATCH, 6, D_MODEL)
        conv_in, q, k, v, ret, gates = _inproj(x, mod_l, norm1_g[l][None, :], w_in[l].astype(BF16))
        rd8 = jnp.broadcast_to(ret_decay[l].reshape(2 * RET_HEADS, 1), (2 * RET_HEADS, LANES))
        yc_c, ya_c, yr_c, rfin = _mixers(conv_in, q, k, v, ret, conv_w[l], attn_sink[l], rd8, latent=False)
        yc_s, ya_s, yr_s = _mixers(conv_in, q, k, v, ret, conv_w[l], attn_sink[l], rd8, latent=True,
                                   cos=cos, sin=sin, cache_k=ck, cache_v=cv, state=state_ret, layer=l)
        yc = jnp.concatenate([yc_c, yc_s], axis=0)
        ya = jnp.concatenate([ya_c, ya_s], axis=0)
        yr = jnp.concatenate([yr_c, yr_s], axis=0)
        wr = jnp.pad(w_router[l], ((0, 0), (0, LANES - N_EXPERTS)))
        wrh = wr.astype(BF16)
        wrl = (wr - wrh.astype(F32)).astype(BF16)
        br = jnp.pad(b_router[l], (0, LANES - N_EXPERTS))[None, :]
        x1, h2, logits = _merge(yc, ya, yr, gates, x, mod_l, norm2_g[l][None, :], w_a[l].astype(BF16),
                                w_b[l].astype(BF16), w_c[l].astype(BF16), w_o[l].astype(BF16), wrh, wrl, br)
        row_tok, row_w, blk_e = _route(logits[:, 0:N_EXPERTS])
        xr = _gather_rows(row_tok, h2.reshape(T * ROW_VREGS, LANES))
        yrows = _experts(blk_e, xr, w_gu[l], b_gu[l], w_down[l], b_down[l])
        moe = _combine(row_tok, row_w, yrows).reshape(T, D_MODEL)
        final = l == DEPTH - 1
        x = _residual(x1, moe, mod_l, final_g[None, :], final)
        ks.append(k[0:T_CTX].reshape(BATCH, SEQ, N_KV, HEAD_DIM))
        vs.append(v[0:T_CTX].reshape(BATCH, SEQ, N_KV, HEAD_DIM))
        rs.append(rfin)

    y_prompt = x[0:T_CTX].reshape(BATCH, SEQ, D_MODEL)
    y_sample = x[T_CTX:T].reshape(DEC_BATCH, DEC_SEQ, D_MODEL)
    return (y_prompt, y_sample, jnp.stack(ks, axis=1), jnp.stack(vs, axis=1), jnp.stack(rs, axis=1))
```

```python
import functools

import numpy as np
import jax
import jax.numpy as jnp
from jax import lax
from jax.experimental import pallas as pl
from jax.experimental.pallas import tpu as pltpu

F32 = jnp.float32
BF16 = jnp.bfloat16

D_MODEL = 1024
BATCH = 16
SEQ = 256
DEPTH = 2
DEC_BATCH = 2
DEC_SEQ = 2048
PAST_LEN = 256
GRID_W = 64
HEAD_DIM = 64
D_CONV = 256
N_HEADS = 8
N_KV = 2
GROUP = N_HEADS // N_KV
WINDOW = 128
ROPE_BASE = 10000.0
RET_HEADS = 4
RET_DK = 64
RET_DV = 64
CHUNK = 128
N_EXPERTS = 32
TOP_K = 4
D_EXPERT = D_MODEL
SWIGLU_LIMIT = 7.0
SWIGLU_ALPHA = 1.702
EPS = 1e-6
NEG_INF = -1e30

T_CTX = BATCH * SEQ
T_LAT = DEC_BATCH * DEC_SEQ
T = T_CTX + T_LAT
D_ATTN = N_HEADS * HEAD_DIM
D_KV = N_KV * HEAD_DIM
D_RET = RET_HEADS * RET_DK
C_CONV = 0
C_Q = 3 * D_CONV
C_K = C_Q + D_ATTN
C_V = C_K + D_KV
C_RET = C_V + D_KV
C_GATE = C_RET + 4 * D_RET
IN_COLS = C_GATE + 3 * D_MODEL

TM = 512
MOE_M = 128
N_ASSIGN = T * TOP_K
N_BLOCKS = (N_ASSIGN + N_EXPERTS * (MOE_M - 1) + MOE_M - 1) // MOE_M
N_ROWS = N_BLOCKS * MOE_M
LANES = 128
SUBLANES = 8
ROW_VREGS = D_MODEL // LANES
TILE_STRIDE = MOE_M + SUBLANES
MIB = 1024 * 1024


def _sigmoid(x):
    return 1.0 / (1.0 + jnp.exp(-x))


def _mod_group(i):
    n_ctx = T_CTX // TM
    per_lat = DEC_SEQ // TM
    g = jnp.zeros_like(i)
    for b in range(DEC_BATCH):
        g = g + (i >= n_ctx + b * per_lat).astype(jnp.int32)
    return g


def _mod_kernel(cond_ref, w_ref, b_ref, o_ref):
    c = cond_ref[...]
    s = c * _sigmoid(c)
    o_ref[0] = jnp.dot(s.astype(BF16), w_ref[0].astype(BF16), preferred_element_type=F32) + b_ref[0]


def _modulation(cond8, w_mod, b_mod):
    n_col = 4
    cw = 6 * D_MODEL // n_col
    return pl.pallas_call(
        _mod_kernel,
        grid=(DEPTH, n_col),
        in_specs=[pl.BlockSpec((8, D_MODEL), lambda l, j: (0, 0)),
                  pl.BlockSpec((1, D_MODEL, cw), lambda l, j: (l, 0, j)),
                  pl.BlockSpec((1, 1, cw), lambda l, j: (l, 0, j))],
        out_specs=pl.BlockSpec((1, 8, cw), lambda l, j: (l, 0, j)),
        out_shape=jax.ShapeDtypeStruct((DEPTH, 8, 6 * D_MODEL), F32),
        compiler_params=pltpu.CompilerParams(vmem_limit_bytes=32 * MIB),
        name="modulation",
    )(cond8, w_mod, b_mod.reshape(DEPTH, 1, 6 * D_MODEL))


def _inproj_kernel(x_ref, mod_ref, g_ref, w_ref, conv_o, q_o, k_o, v_o, ret_o, gate_o):
    x = x_ref[...]
    ms = jnp.mean(x * x, axis=-1, keepdims=True)
    h = x * lax.rsqrt(ms + EPS) * g_ref[...]
    h = h * (1.0 + mod_ref[0, 1:2, :]) + mod_ref[0, 0:1, :]
    hb = h.astype(BF16)

    def proj(c0, c1):
        return jnp.dot(hb, w_ref[:, c0:c1], preferred_element_type=F32)

    conv_o[...] = proj(C_CONV, C_Q).astype(BF16)
    q_o[...] = (proj(C_Q, C_K) * HEAD_DIM ** -0.5).astype(BF16)
    k_o[...] = proj(C_K, C_V)
    v_o[...] = proj(C_V, C_RET)
    ret_o[:, 0:D_RET] = proj(C_RET, C_RET + D_RET).astype(BF16)
    ret_o[:, D_RET:2 * D_RET] = (proj(C_RET + D_RET, C_RET + 2 * D_RET) * RET_DK ** -0.5).astype(BF16)
    ret_o[:, 2 * D_RET:4 * D_RET] = proj(C_RET + 2 * D_RET, C_GATE).astype(BF16)
    for b in range(3):
        g = proj(C_GATE + b * D_MODEL, C_GATE + (b + 1) * D_MODEL)
        gate_o[:, b * D_MODEL:(b + 1) * D_MODEL] = _sigmoid(g).astype(BF16)


def _inproj(x, mod_l, g1, w_in_bf):
    row = lambda i: (i, 0)
    return pl.pallas_call(
        _inproj_kernel,
        grid=(T // TM,),
        in_specs=[pl.BlockSpec((TM, D_MODEL), row),
                  pl.BlockSpec((1, 6, D_MODEL), lambda i: (_mod_group(i), 0, 0)),
                  pl.BlockSpec((1, D_MODEL), lambda i: (0, 0)),
                  pl.BlockSpec((D_MODEL, IN_COLS), lambda i: (0, 0), pipeline_mode=pl.Buffered(1))],
        out_specs=[pl.BlockSpec((TM, 3 * D_CONV), row),
                   pl.BlockSpec((TM, D_ATTN), row),
                   pl.BlockSpec((TM, D_KV), row),
                   pl.BlockSpec((TM, D_KV), row),
                   pl.BlockSpec((TM, 4 * D_RET), row),
                   pl.BlockSpec((TM, 3 * D_MODEL), row)],
        out_shape=[jax.ShapeDtypeStruct((T, 3 * D_CONV), BF16),
                   jax.ShapeDtypeStruct((T, D_ATTN), BF16),
                   jax.ShapeDtypeStruct((T, D_KV), F32),
                   jax.ShapeDtypeStruct((T, D_KV), F32),
                   jax.ShapeDtypeStruct((T, 4 * D_RET), BF16),
                   jax.ShapeDtypeStruct((T, 3 * D_MODEL), BF16)],
        compiler_params=pltpu.CompilerParams(vmem_limit_bytes=48 * MIB),
        name="inproj",
    )(x, mod_l, g1, w_in_bf)


def _rope(x, cos, sin_signed):
    lane = lax.broadcasted_iota(jnp.int32, x.shape, 1)
    first = (lane % 32) < 16
    partner = jnp.where(first, pltpu.roll(x, x.shape[1] - 16, 1), pltpu.roll(x, 16, 1))
    return x * cos + partner * sin_signed


def _dot_nt(a, b):
    return lax.dot_general(a, b, (((1,), (1,)), ((), ())), preferred_element_type=F32)


def _dot_tn(a, b):
    return lax.dot_general(a, b, (((0,), (0,)), ((), ())), preferred_element_type=F32)


def _mixer_kernel(*refs, n, latent):
    if latent:
        (sink_ref, conv_ref, q_ref, k_ref, v_ref, ret_ref, cw_ref, rd_ref, cos_ref, sin_ref, ck_ref, cv_ref,
         r0_ref, yc_ref, ya_ref, yr_ref, kb, vb, o_f, o_b, rst, dmat, qdec, kdec) = refs
    else:
        (sink_ref, conv_ref, q_ref, k_ref, v_ref, ret_ref, cw_ref, rd_ref,
         yc_ref, ya_ref, yr_ref, rfin_ref, kb, vb, o_f, o_b, rst, dmat, qdec, kdec) = refs
    nb = n // CHUNK

    cv = conv_ref[...].astype(F32)
    cb, cc, cu = cv[:, 0:D_CONV], cv[:, D_CONV:2 * D_CONV], cv[:, 2 * D_CONV:3 * D_CONV]
    p = cc * cu
    row = lax.broadcasted_iota(jnp.int32, p.shape, 0)
    prev = jnp.where(row == 0, 0.0, pltpu.roll(p, 1, 0))
    nxt = jnp.where(row == n - 1, 0.0, pltpu.roll(p, n - 1, 0))
    cw = cw_ref[...]
    yc_ref[...] = (cb * (prev * cw[0:1, :] + p * cw[1:2, :] + nxt * cw[2:3, :])).astype(BF16)

    if latent:
        kr = _rope(k_ref[...], cos_ref[...], sin_ref[...])
        zpad = jnp.zeros((CHUNK, D_KV), BF16)
        kb[0:CHUNK, :] = zpad
        vb[0:CHUNK, :] = zpad
        kb[CHUNK + n:2 * CHUNK + n, :] = zpad
        vb[CHUNK + n:2 * CHUNK + n, :] = zpad
        kb[CHUNK:CHUNK + n, :] = kr.astype(BF16)
        vb[CHUNK:CHUNK + n, :] = v_ref[...].astype(BF16)
        ckb = ck_ref[0, 0].astype(BF16)
        cvb = cv_ref[0, 0].astype(BF16)
    else:
        kb[...] = k_ref[...].astype(BF16)
        vb[...] = v_ref[...].astype(BF16)

    def attn_block(j):
        r0 = j * CHUNK if isinstance(j, int) else pl.multiple_of(j * CHUNK, CHUNK)
        qj = q_ref[pl.ds(r0, CHUNK), :]
        if latent:
            cosj = cos_ref[pl.ds(r0, CHUNK), :]
            sinj = sin_ref[pl.ds(r0, CHUNK), :]
            qf = qj.astype(F32)
            qj = jnp.concatenate(
                [_rope(qf[:, c * LANES:(c + 1) * LANES], cosj, sinj) for c in range(D_ATTN // LANES)],
                axis=1).astype(BF16)
            kw = kb[pl.ds(r0, 3 * CHUNK), :]
            vw = vb[pl.ds(r0, 3 * CHUNK), :]
            qpos = r0 + lax.broadcasted_iota(jnp.int32, (CHUNK, 3 * CHUNK), 0)
            kpos = r0 - CHUNK + lax.broadcasted_iota(jnp.int32, (CHUNK, 3 * CHUNK), 1)
            ok = (jnp.abs(qpos - kpos) <= WINDOW) & (kpos >= 0) & (kpos < n)
            ok = jnp.concatenate([ok] * GROUP, axis=0)
        else:
            kw = kb[...]
            vw = vb[...]
        for g in range(N_KV):
            lo = g * HEAD_DIM
            qg = jnp.concatenate(
                [qj[:, (GROUP * g + i) * HEAD_DIM:(GROUP * g + i + 1) * HEAD_DIM] for i in range(GROUP)], axis=0)
            sk = jnp.concatenate(
                [jnp.full((CHUNK, 1), sink_ref[GROUP * g + i], F32) for i in range(GROUP)], axis=0)
            s = _dot_nt(qg, kw[:, lo:lo + HEAD_DIM])
            if latent:
                s = jnp.where(ok, s, NEG_INF)
                s2 = _dot_nt(qg, ckb[:, lo:lo + HEAD_DIM])
            m = jnp.maximum(jnp.max(s, axis=-1, keepdims=True), sk)
            if latent:
                m = jnp.maximum(m, jnp.max(s2, axis=-1, keepdims=True))
            pw = jnp.exp(s - m)
            den = jnp.sum(pw, axis=-1, keepdims=True) + jnp.exp(sk - m)
            o = jnp.dot(pw.astype(BF16), vw[:, lo:lo + HEAD_DIM], preferred_element_type=F32)
            if latent:
                p2 = jnp.exp(s2 - m)
                den = den + jnp.sum(p2, axis=-1, keepdims=True)
                o = o + jnp.dot(p2.astype(BF16), cvb[:, lo:lo + HEAD_DIM], preferred_element_type=F32)
            o = o / den
            ya_ref[pl.ds(r0, CHUNK), g * GROUP * HEAD_DIM:(g + 1) * GROUP * HEAD_DIM] = jnp.concatenate(
                [o[i * CHUNK:(i + 1) * CHUNK, :] for i in range(GROUP)], axis=1).astype(BF16)

    if latent:
        def attn_body(j, carry):
            attn_block(j)
            return carry
        lax.fori_loop(0, nb, attn_body, 0)
    else:
        for j in range(nb):
            attn_block(j)

    rd = rd_ref[...]
    log_g = jnp.minimum(rd, 0.0) - jnp.log(1.0 + jnp.exp(-jnp.abs(rd)))
    ii = lax.broadcasted_iota(jnp.int32, (CHUNK, CHUNK), 0).astype(F32)
    jj = lax.broadcasted_iota(jnp.int32, (CHUNK, CHUNK), 1).astype(F32)
    for d in range(2):
        for h in range(RET_HEADS):
            r = d * RET_HEADS + h
            lg = log_g[r:r + 1, :]
            diff = (ii - jj) if d == 0 else (jj - ii)
            dmat[r] = jnp.where(diff >= 0, jnp.exp(jnp.maximum(diff, 0.0) * lg), 0.0)
            if d == 0:
                qdec[r] = jnp.exp((ii + 1.0) * lg)
                kdec[r] = jnp.exp((CHUNK - 1.0 - ii) * lg)
            else:
                qdec[r] = jnp.exp((CHUNK - ii) * lg)
                kdec[r] = jnp.exp(ii * lg)
            if latent:
                rst[r] = r0_ref[0, 0, d, h]
            else:
                rst[r] = jnp.zeros((RET_DK, RET_DV), F32)
    chunk_decay = jnp.exp(float(CHUNK) * log_g)

    def ret_chunk(c0, d, out_ref):
        blk = ret_ref[pl.ds(c0, CHUNK), :]
        outs = []
        for h in range(RET_HEADS):
            r = d * RET_HEADS + h
            qc = blk[:, h * RET_DK:(h + 1) * RET_DK]
            kc = blk[:, D_RET + h * RET_DK:D_RET + (h + 1) * RET_DK]
            vc = blk[:, 2 * D_RET + h * RET_DV:2 * D_RET + (h + 1) * RET_DV]
            inner = (_dot_nt(qc, kc) * dmat[r]).astype(BF16)
            state = rst[r]
            o = jnp.dot(inner, vc, preferred_element_type=F32)
            o = o + jnp.dot(qc, state.astype(BF16), preferred_element_type=F32) * qdec[r][:, 0:RET_DV]
            kd = (kc.astype(F32) * kdec[r][:, 0:RET_DK]).astype(BF16)
            rst[r] = state * chunk_decay[r:r + 1, 0:RET_DV] + _dot_tn(kd, vc)
            outs.append(o)
        out_ref[pl.ds(c0, CHUNK), :] = jnp.concatenate(outs, axis=1)

    def ret_body(j, carry):
        ret_chunk(pl.multiple_of(j * CHUNK, CHUNK), 0, o_f)
        ret_chunk(pl.multiple_of((nb - 1 - j) * CHUNK, CHUNK), 1, o_b)
        return carry

    lax.fori_loop(0, nb, ret_body, 0)

    def norm_body(j, carry):
        c0 = pl.multiple_of(j * CHUNK, CHUNK)
        o = o_f[pl.ds(c0, CHUNK), :] + o_b[pl.ds(c0, CHUNK), :]
        parts = []
        for h in range(RET_HEADS):
            oh = o[:, h * RET_DV:(h + 1) * RET_DV]
            ms = jnp.mean(oh * oh, axis=-1, keepdims=True)
            parts.append(oh * lax.rsqrt(ms + EPS))
        rg = ret_ref[pl.ds(c0, CHUNK), 3 * D_RET:4 * D_RET].astype(F32)
        yr_ref[pl.ds(c0, CHUNK), :] = (rg * _sigmoid(rg) * jnp.concatenate(parts, axis=1)).astype(BF16)
        return carry

    lax.fori_loop(0, nb, norm_body, 0)

    if not latent:
        for d in range(2):
            for h in range(RET_HEADS):
                rfin_ref[0, d, h] = rst[d * RET_HEADS + h]


def _mixers(conv_in, q, k, v, ret, conv_w_l, sink_l, rd8, *, latent, cos=None, sin=None, cache_k=None,
            cache_v=None, state=None, layer=0):
    n = DEC_SEQ if latent else SEQ
    nseq = DEC_BATCH if latent else BATCH
    off = T_CTX // n if latent else 0
    seq = lambda s: (s + off, 0)
    const = lambda s: (0, 0)
    in_specs = [pl.BlockSpec(memory_space=pltpu.SMEM),
                pl.BlockSpec((n, 3 * D_CONV), seq),
                pl.BlockSpec((n, D_ATTN), seq),
                pl.BlockSpec((n, D_KV), seq),
                pl.BlockSpec((n, D_KV), seq),
                pl.BlockSpec((n, 4 * D_RET), seq),
                pl.BlockSpec((3, D_CONV), const),
                pl.BlockSpec((8, LANES), const)]
    args = [sink_l, conv_in, q, k, v, ret, conv_w_l, rd8]
    out_specs = [pl.BlockSpec((n, D_CONV), lambda s: (s, 0)),
                 pl.BlockSpec((n, D_ATTN), lambda s: (s, 0)),
                 pl.BlockSpec((n, D_RET), lambda s: (s, 0))]
    out_shape = [jax.ShapeDtypeStruct((nseq * n, D_CONV), BF16),
                 jax.ShapeDtypeStruct((nseq * n, D_ATTN), BF16),
                 jax.ShapeDtypeStruct((nseq * n, D_RET), BF16)]
    if latent:
        in_specs += [pl.BlockSpec((n, LANES), const),
                     pl.BlockSpec((n, LANES), const),
                     pl.BlockSpec((1, 1, PAST_LEN, D_KV), lambda s: (s, layer, 0, 0)),
                     pl.BlockSpec((1, 1, PAST_LEN, D_KV), lambda s: (s, layer, 0, 0)),
                     pl.BlockSpec((1, 1, 2, RET_HEADS, RET_DK, RET_DV), lambda s: (s, layer, 0, 0, 0, 0))]
        args += [cos, sin, cache_k, cache_v, state]
        kv_rows = n + 2 * CHUNK
    else:
        out_specs.append(pl.BlockSpec((1, 2, RET_HEADS, RET_DK, RET_DV), lambda s: (s, 0, 0, 0, 0)))
        out_shape.append(jax.ShapeDtypeStruct((nseq, 2, RET_HEADS, RET_DK, RET_DV), F32))
        kv_rows = n
    scratch = [pltpu.VMEM((kv_rows, D_KV), BF16),
               pltpu.VMEM((kv_rows, D_KV), BF16),
               pltpu.VMEM((n, D_RET), F32),
               pltpu.VMEM((n, D_RET), F32),
               pltpu.VMEM((2 * RET_HEADS, RET_DK, RET_DV), F32),
               pltpu.VMEM((2 * RET_HEADS, CHUNK, CHUNK), F32),
               pltpu.VMEM((2 * RET_HEADS, CHUNK, CHUNK), F32),
               pltpu.VMEM((2 * RET_HEADS, CHUNK, CHUNK), F32)]
    return pl.pallas_call(
        functools.partial(_mixer_kernel, n=n, latent=latent),
        grid=(nseq,),
        in_specs=in_specs,
        out_specs=out_specs,
        out_shape=out_shape,
        scratch_shapes=scratch,
        compiler_params=pltpu.CompilerParams(vmem_limit_bytes=48 * MIB),
        name="mixers_latent" if latent else "mixers_context",
    )(*args)


def _merge_kernel(yc_ref, ya_ref, yr_ref, gate_ref, x_ref, mod_ref, g2_ref, wa_ref, wb_ref, wc_ref, wo_ref,
                  wrh_ref, wrl_ref, br_ref, x1_o, h2_o, lg_o):
    merged = (gate_ref[:, 0:D_MODEL].astype(F32)
              * jnp.dot(yc_ref[...], wa_ref[...], preferred_element_type=F32)
              + gate_ref[:, D_MODEL:2 * D_MODEL].astype(F32)
              * jnp.dot(ya_ref[...], wb_ref[...], preferred_element_type=F32)
              + gate_ref[:, 2 * D_MODEL:3 * D_MODEL].astype(F32)
              * jnp.dot(yr_ref[...], wc_ref[...], preferred_element_type=F32))
    x1 = x_ref[...] + mod_ref[0, 2:3, :] * jnp.dot(merged.astype(BF16), wo_ref[...], preferred_element_type=F32)
    x1_o[...] = x1
    ms = jnp.mean(x1 * x1, axis=-1, keepdims=True)
    h2 = x1 * lax.rsqrt(ms + EPS) * g2_ref[...]
    h2 = h2 * (1.0 + mod_ref[0, 4:5, :]) + mod_ref[0, 3:4, :]
    h2_o[...] = h2
    hh = h2.astype(BF16)
    hl = (h2 - hh.astype(F32)).astype(BF16)
    lg_o[...] = (jnp.dot(hh, wrh_ref[...], preferred_element_type=F32)
                 + jnp.dot(hl, wrh_ref[...], preferred_element_type=F32)
                 + jnp.dot(hh, wrl_ref[...], preferred_element_type=F32)
                 + br_ref[...])


def _merge(yc, ya, yr, gates, x, mod_l, g2, wa, wb, wc, wo, wrh, wrl, br):
    row = lambda i: (i, 0)
    const = lambda i: (0, 0)
    return pl.pallas_call(
        _merge_kernel,
        grid=(T // TM,),
        in_specs=[pl.BlockSpec((TM, D_CONV), row),
                  pl.BlockSpec((TM, D_ATTN), row),
                  pl.BlockSpec((TM, D_RET), row),
                  pl.BlockSpec((TM, 3 * D_MODEL), row),
                  pl.BlockSpec((TM, D_MODEL), row),
                  pl.BlockSpec((1, 6, D_MODEL), lambda i: (_mod_group(i), 0, 0)),
                  pl.BlockSpec((1, D_MODEL), const),
                  pl.BlockSpec((D_CONV, D_MODEL), const),
                  pl.BlockSpec((D_ATTN, D_MODEL), const),
                  pl.BlockSpec((D_RET, D_MODEL), const),
                  pl.BlockSpec((D_MODEL, D_MODEL), const),
                  pl.BlockSpec((D_MODEL, LANES), const),
                  pl.BlockSpec((D_MODEL, LANES), const),
                  pl.BlockSpec((1, LANES), const)],
        out_specs=[pl.BlockSpec((TM, D_MODEL), row),
                   pl.BlockSpec((TM, D_MODEL), row),
                   pl.BlockSpec((TM, LANES), row)],
        out_shape=[jax.ShapeDtypeStruct((T, D_MODEL), F32),
                   jax.ShapeDtypeStruct((T, D_MODEL), F32),
                   jax.ShapeDtypeStruct((T, LANES), F32)],
        compiler_params=pltpu.CompilerParams(vmem_limit_bytes=48 * MIB),
        name="merge_router",
    )(yc, ya, yr, gates, x, mod_l, g2, wa, wb, wc, wo, wrh, wrl, br)


def _gather_kernel(tok_ref, x_ref, o_ref, tile):
    for mi in range(MOE_M):
        t = jnp.minimum(tok_ref[0, 0, mi], T - 1)
        slab = x_ref[pl.ds(pl.multiple_of(t * ROW_VREGS, ROW_VREGS), ROW_VREGS), :]
        tile[pl.ds(mi, ROW_VREGS, stride=TILE_STRIDE), :] = slab
    o_ref[...] = jnp.concatenate(
        [tile[c * TILE_STRIDE:c * TILE_STRIDE + MOE_M, :] for c in range(ROW_VREGS)], axis=1).astype(BF16)


def _gather_rows(row_tok3, h2_tiles):
    return pl.pallas_call(
        _gather_kernel,
        grid=(N_BLOCKS,),
        in_specs=[pl.BlockSpec((1, 1, MOE_M), lambda b: (b, 0, 0), memory_space=pltpu.SMEM),
                  pl.BlockSpec((T * ROW_VREGS, LANES), lambda b: (0, 0), pipeline_mode=pl.Buffered(1))],
        out_specs=pl.BlockSpec((MOE_M, D_MODEL), lambda b: (b, 0)),
        out_shape=jax.ShapeDtypeStruct((N_ROWS, D_MODEL), BF16),
        scratch_shapes=[pltpu.VMEM((ROW_VREGS * TILE_STRIDE, LANES), F32)],
        compiler_params=pltpu.CompilerParams(vmem_limit_bytes=48 * MIB),
        name="moe_gather",
    )(row_tok3, h2_tiles)


def _expert_kernel(blk_e_ref, nu_ref, x_ref, wgu_ref, bgu_ref, wd_ref, bd_ref, y_ref):
    del blk_e_ref

    @pl.when(pl.program_id(0) < nu_ref[0])
    def _():
        gu = jnp.dot(x_ref[...], wgu_ref[0, 0].astype(BF16), preferred_element_type=F32) + bgu_ref[0, 0]
        gate = jnp.minimum(gu[:, 0:D_EXPERT], SWIGLU_LIMIT)
        up = jnp.clip(gu[:, D_EXPERT:2 * D_EXPERT], -SWIGLU_LIMIT, SWIGLU_LIMIT)
        glu = gate * _sigmoid(SWIGLU_ALPHA * gate)
        mid = ((up + 1.0) * glu).astype(BF16)
        y = jnp.dot(mid, wd_ref[0, 0].astype(BF16), preferred_element_type=F32) + bd_ref[0, 0]
        y_ref[...] = y.astype(BF16)

    @pl.when(pl.program_id(0) >= nu_ref[0])
    def _():
        y_ref[...] = jnp.zeros_like(y_ref)


def _experts(blk_e, n_used, xr, w_gu, b_gu, w_down, b_down, layer):
    grid_spec = pltpu.PrefetchScalarGridSpec(
        num_scalar_prefetch=2,
        grid=(N_BLOCKS,),
        in_specs=[pl.BlockSpec((MOE_M, D_MODEL), lambda b, e, nu: (b, 0)),
                  pl.BlockSpec((1, 1, D_MODEL, 2 * D_EXPERT), lambda b, e, nu: (layer, e[b], 0, 0)),
                  pl.BlockSpec((1, 1, 1, 2 * D_EXPERT), lambda b, e, nu: (layer, e[b], 0, 0)),
                  pl.BlockSpec((1, 1, D_EXPERT, D_MODEL), lambda b, e, nu: (layer, e[b], 0, 0)),
                  pl.BlockSpec((1, 1, 1, D_MODEL), lambda b, e, nu: (layer, e[b], 0, 0))],
        out_specs=pl.BlockSpec((MOE_M, D_MODEL), lambda b, e, nu: (b, 0)),
    )
    return pl.pallas_call(
        _expert_kernel,
        grid_spec=grid_spec,
        out_shape=jax.ShapeDtypeStruct((N_ROWS, D_MODEL), BF16),
        compiler_params=pltpu.CompilerParams(vmem_limit_bytes=48 * MIB),
        name="moe_experts",
    )(blk_e, n_used, xr, w_gu, b_gu.reshape(DEPTH, N_EXPERTS, 1, 2 * D_EXPERT), w_down,
      b_down.reshape(DEPTH, N_EXPERTS, 1, D_MODEL))


SCATTER_UNROLL = 8


def _combine_kernel(tok_ref, w_ref, y_ref, o_ref, acc, tile, sem):
    b = pl.program_id(0)

    @pl.when(b == 0)
    def _():
        acc[...] = jnp.zeros_like(acc)

    y = y_ref[...].astype(F32)
    for c in range(ROW_VREGS):
        tile[c * TILE_STRIDE:c * TILE_STRIDE + MOE_M, :] = y[:, c * LANES:(c + 1) * LANES]
    for m0 in range(0, MOE_M, SCATTER_UNROLL):
        addrs, vals = [], []
        for u in range(SCATTER_UNROLL):
            mi = m0 + u
            a = pl.multiple_of(tok_ref[0, 0, mi] * ROW_VREGS, ROW_VREGS)
            yv = tile[pl.ds(mi, ROW_VREGS, stride=TILE_STRIDE), :]
            addrs.append(a)
            vals.append(acc[pl.ds(a, ROW_VREGS), :] + w_ref[0, 0, mi] * yv)
        for u in range(SCATTER_UNROLL):
            acc[pl.ds(addrs[u], ROW_VREGS), :] = vals[u]

    @pl.when(b == pl.num_programs(0) - 1)
    def _():
        cp = pltpu.make_async_copy(acc.at[pl.ds(0, T * ROW_VREGS)], o_ref, sem)
        cp.start()
        cp.wait()


def _combine(row_tok3, row_w3, yr):
    return pl.pallas_call(
        _combine_kernel,
        grid=(N_BLOCKS,),
        in_specs=[pl.BlockSpec((1, 1, MOE_M), lambda b: (b, 0, 0), memory_space=pltpu.SMEM),
                  pl.BlockSpec((1, 1, MOE_M), lambda b: (b, 0, 0), memory_space=pltpu.SMEM),
                  pl.BlockSpec((MOE_M, D_MODEL), lambda b: (b, 0))],
        out_specs=pl.BlockSpec(memory_space=pl.ANY),
        out_shape=jax.ShapeDtypeStruct((T * ROW_VREGS, LANES), F32),
        scratch_shapes=[pltpu.VMEM(((T + 1) * ROW_VREGS, LANES), F32),
                        pltpu.VMEM((ROW_VREGS * TILE_STRIDE, LANES), F32),
                        pltpu.SemaphoreType.DMA(())],
        compiler_params=pltpu.CompilerParams(vmem_limit_bytes=48 * MIB),
        name="moe_combine",
    )(row_tok3, row_w3, yr)


def _route(logits):
    top_v, top_e = lax.top_k(logits, TOP_K)
    top_w = jax.nn.softmax(top_v, axis=-1)
    flat_e = top_e.reshape(N_ASSIGN).astype(jnp.int32)
    idx = jnp.arange(N_ASSIGN, dtype=jnp.int32)
    sorted_e, sorted_idx, sorted_w = lax.sort((flat_e, idx, top_w.reshape(N_ASSIGN)), num_keys=1, is_stable=True)
    bounds = jnp.searchsorted(sorted_e, jnp.arange(N_EXPERTS + 1, dtype=jnp.int32), side='left').astype(jnp.int32)
    start = bounds[:-1]
    counts = bounds[1:] - bounds[:-1]
    padded = (counts + MOE_M - 1) // MOE_M * MOE_M
    pad_end = jnp.cumsum(padded)
    pad_start = pad_end - padded
    n_used = (pad_end[-1] // MOE_M).astype(jnp.int32)
    blk0 = jnp.arange(N_BLOCKS, dtype=jnp.int32) * MOE_M
    blk_e = jnp.minimum(jnp.searchsorted(pad_end, blk0, side='right'), N_EXPERTS - 1).astype(jnp.int32)
    q = (blk0 - pad_start[blk_e])[:, None] + jnp.arange(MOE_M, dtype=jnp.int32)[None, :]
    valid = q < counts[blk_e][:, None]
    pos = jnp.clip(start[blk_e][:, None] + q, 0, N_ASSIGN - 1)
    row_tok = jnp.where(valid, sorted_idx[pos] // TOP_K, T).astype(jnp.int32)
    row_w = jnp.where(valid, sorted_w[pos], 0.0)
    blk_e = jnp.where(blk0 < pad_end[-1], blk_e, blk_e[jnp.maximum(n_used - 1, 0)])
    return (row_tok.reshape(N_BLOCKS, 1, MOE_M), row_w.reshape(N_BLOCKS, 1, MOE_M), blk_e,
            n_used.reshape(1))


def _residual_kernel(x_ref, moe_ref, mod_ref, g_ref, o_ref, *, final):
    x = x_ref[...] + mod_ref[0, 5:6, :] * moe_ref[...]
    if final:
        ms = jnp.mean(x * x, axis=-1, keepdims=True)
        x = x * lax.rsqrt(ms + EPS) * g_ref[...]
    o_ref[...] = x


def _residual(x1, moe, mod_l, g, final):
    row = lambda i: (i, 0)
    return pl.pallas_call(
        functools.partial(_residual_kernel, final=final),
        grid=(T // TM,),
        in_specs=[pl.BlockSpec((TM, D_MODEL), row),
                  pl.BlockSpec((TM, D_MODEL), row),
                  pl.BlockSpec((1, 6, D_MODEL), lambda i: (_mod_group(i), 0, 0)),
                  pl.BlockSpec((1, D_MODEL), lambda i: (0, 0))],
        out_specs=pl.BlockSpec((TM, D_MODEL), row),
        out_shape=jax.ShapeDtypeStruct((T, D_MODEL), F32),
        compiler_params=pltpu.CompilerParams(vmem_limit_bytes=32 * MIB),
        name="residual_final" if final else "residual",
    )(x1, moe, mod_l, g)


def _rope_tables():
    t = np.arange(DEC_SEQ)
    pos = np.stack([t // GRID_W, t % GRID_W], axis=1).astype(np.float32)
    half = HEAD_DIM // 2
    inv = jnp.asarray(ROPE_BASE, F32) ** (-jnp.arange(0, half, 2, dtype=F32) / half)
    d = np.arange(HEAD_DIM)
    which = d // half
    freq = d % (half // 2)
    sign = np.where((d % half) < half // 2, -1.0, 1.0).astype(np.float32)
    ang = jnp.asarray(pos)[:, which] * inv[freq][None, :]
    cos = jnp.cos(ang)
    sin = jnp.sin(ang) * jnp.asarray(sign)[None, :]
    reps = LANES // HEAD_DIM
    return jnp.tile(cos, (1, reps)), jnp.tile(sin, (1, reps))


def kernel(x_prompt, x_sample, cache_k, cache_v, state_ret, c, c_ctx, norm1_g, norm2_g, w_mod, b_mod, w_in, conv_w, attn_sink, ret_decay, w_a, w_b, w_c, w_o, w_router, b_router, w_gu, b_gu, w_down, b_down, final_g):
    x = jnp.concatenate([x_prompt.reshape(T_CTX, D_MODEL), x_sample.reshape(T_LAT, D_MODEL)], axis=0)
    cond8 = jnp.zeros((8, D_MODEL), F32).at[0].set(c_ctx).at[1:1 + DEC_BATCH].set(c)
    mod = _modulation(cond8, w_mod, b_mod)
    cos, sin = _rope_tables()
    ck = cache_k.reshape(DEC_BATCH, DEPTH, PAST_LEN, D_KV)
    cv = cache_v.reshape(DEC_BATCH, DEPTH, PAST_LEN, D_KV)

    ks, vs, rs = [], [], []
    for l in range(DEPTH):
        mod_l = mod[l, 0:1 + DEC_BATCH].reshape(1 + DEC_BATCH, 6, D_MODEL)
        conv_in, q, k, v, ret, gates = _inproj(x, mod_l, norm1_g[l][None, :], w_in[l].astype(BF16))
        rd8 = jnp.broadcast_to(ret_decay[l].reshape(2 * RET_HEADS, 1), (2 * RET_HEADS, LANES))
        yc_c, ya_c, yr_c, rfin = _mixers(conv_in, q, k, v, ret, conv_w[l], attn_sink[l], rd8, latent=False)
        yc_s, ya_s, yr_s = _mixers(conv_in, q, k, v, ret, conv_w[l], attn_sink[l], rd8, latent=True,
                                   cos=cos, sin=sin, cache_k=ck, cache_v=cv, state=state_ret, layer=l)
        yc = jnp.concatenate([yc_c, yc_s], axis=0)
        ya = jnp.concatenate([ya_c, ya_s], axis=0)
        yr = jnp.concatenate([yr_c, yr_s], axis=0)
        wr = jnp.pad(w_router[l], ((0, 0), (0, LANES - N_EXPERTS)))
        wrh = wr.astype(BF16)
        wrl = (wr - wrh.astype(F32)).astype(BF16)
        br = jnp.pad(b_router[l], (0, LANES - N_EXPERTS))[None, :]
        x1, h2, logits = _merge(yc, ya, yr, gates, x, mod_l, norm2_g[l][None, :], w_a[l].astype(BF16),
                                w_b[l].astype(BF16), w_c[l].astype(BF16), w_o[l].astype(BF16), wrh, wrl, br)
        row_tok, row_w, blk_e, n_used = _route(logits[:, 0:N_EXPERTS])
        xr = _gather_rows(row_tok, h2.reshape(T * ROW_VREGS, LANES))
        yrows = _experts(blk_e, n_used, xr, w_gu, b_gu, w_down, b_down, l)
        moe = _combine(row_tok, row_w, yrows).reshape(T, D_MODEL)
        final = l == DEPTH - 1
        x = _residual(x1, moe, mod_l, final_g[None, :], final)
        ks.append(k[0:T_CTX].reshape(BATCH, SEQ, N_KV, HEAD_DIM))
        vs.append(v[0:T_CTX].reshape(BATCH, SEQ, N_KV, HEAD_DIM))
        rs.append(rfin)

    y_prompt = x[0:T_CTX].reshape(BATCH, SEQ, D_MODEL)
    y_sample = x[T_CTX:T].reshape(DEC_BATCH, DEC_SEQ, D_MODEL)
    return (y_prompt, y_sample, jnp.stack(ks, axis=1), jnp.stack(vs, axis=1), jnp.stack(rs, axis=1))
```

```python
import functools

import numpy as np
import jax
import jax.numpy as jnp
from jax import lax
from jax.experimental import pallas as pl
from jax.experimental.pallas import tpu as pltpu

F32 = jnp.float32
BF16 = jnp.bfloat16

D_MODEL = 1024
BATCH = 16
SEQ = 256
DEPTH = 2
DEC_BATCH = 2
DEC_SEQ = 2048
PAST_LEN = 256
GRID_W = 64
HEAD_DIM = 64
D_CONV = 256
N_HEADS = 8
N_KV = 2
GROUP = N_HEADS // N_KV
WINDOW = 128
ROPE_BASE = 10000.0
RET_HEADS = 4
RET_DK = 64
RET_DV = 64
CHUNK = 128
N_EXPERTS = 32
TOP_K = 4
D_EXPERT = D_MODEL
SWIGLU_LIMIT = 7.0
SWIGLU_ALPHA = 1.702
EPS = 1e-6
NEG_INF = -1e30

T_CTX = BATCH * SEQ
T_LAT = DEC_BATCH * DEC_SEQ
T = T_CTX + T_LAT
D_ATTN = N_HEADS * HEAD_DIM
D_KV = N_KV * HEAD_DIM
D_RET = RET_HEADS * RET_DK
C_CONV = 0
C_Q = 3 * D_CONV
C_K = C_Q + D_ATTN
C_V = C_K + D_KV
C_RET = C_V + D_KV
C_GATE = C_RET + 4 * D_RET
IN_COLS = C_GATE + 3 * D_MODEL

TM = 512
MOE_M = 256
N_ASSIGN = T * TOP_K
N_BLOCKS = (N_ASSIGN + N_EXPERTS * (MOE_M - 1) + MOE_M - 1) // MOE_M
N_ROWS = N_BLOCKS * MOE_M
LANES = 128
SUBLANES = 8
ROW_VREGS = D_MODEL // LANES
PACK_ROWS = ROW_VREGS // 2
GATHER_STRIDE = MOE_M + SUBLANES
SCATTER_M = 128
SCATTER_STRIDE = SCATTER_M + SUBLANES
COMBINE_BLOCKS = 2
MIB = 1024 * 1024


def _sigmoid(x):
    return 1.0 / (1.0 + jnp.exp(-x))


def _mod_group(i):
    n_ctx = T_CTX // TM
    per_lat = DEC_SEQ // TM
    g = jnp.zeros_like(i)
    for b in range(DEC_BATCH):
        g = g + (i >= n_ctx + b * per_lat).astype(jnp.int32)
    return g


def _mod_kernel(cond_ref, w_ref, b_ref, o_ref):
    c = cond_ref[...]
    s = c * _sigmoid(c)
    o_ref[0] = jnp.dot(s.astype(BF16), w_ref[0].astype(BF16), preferred_element_type=F32) + b_ref[0]


def _modulation(cond8, w_mod, b_mod):
    n_col = 4
    cw = 6 * D_MODEL // n_col
    return pl.pallas_call(
        _mod_kernel,
        grid=(DEPTH, n_col),
        in_specs=[pl.BlockSpec((8, D_MODEL), lambda l, j: (0, 0)),
                  pl.BlockSpec((1, D_MODEL, cw), lambda l, j: (l, 0, j)),
                  pl.BlockSpec((1, 1, cw), lambda l, j: (l, 0, j))],
        out_specs=pl.BlockSpec((1, 8, cw), lambda l, j: (l, 0, j)),
        out_shape=jax.ShapeDtypeStruct((DEPTH, 8, 6 * D_MODEL), F32),
        compiler_params=pltpu.CompilerParams(vmem_limit_bytes=32 * MIB),
        name="modulation",
    )(cond8, w_mod, b_mod.reshape(DEPTH, 1, 6 * D_MODEL))


def _inproj_kernel(x_ref, mod_ref, g_ref, w_ref, conv_o, q_o, k_o, v_o, ret_o, gate_o):
    x = x_ref[...]
    ms = jnp.mean(x * x, axis=-1, keepdims=True)
    h = x * lax.rsqrt(ms + EPS) * g_ref[...]
    h = h * (1.0 + mod_ref[0, 1:2, :]) + mod_ref[0, 0:1, :]
    hb = h.astype(BF16)

    def proj(c0, c1):
        return jnp.dot(hb, w_ref[:, c0:c1], preferred_element_type=F32)

    conv_o[...] = proj(C_CONV, C_Q).astype(BF16)
    q_o[...] = (proj(C_Q, C_K) * HEAD_DIM ** -0.5).astype(BF16)
    k_o[...] = proj(C_K, C_V)
    v_o[...] = proj(C_V, C_RET)
    ret_o[:, 0:D_RET] = proj(C_RET, C_RET + D_RET).astype(BF16)
    ret_o[:, D_RET:2 * D_RET] = (proj(C_RET + D_RET, C_RET + 2 * D_RET) * RET_DK ** -0.5).astype(BF16)
    ret_o[:, 2 * D_RET:4 * D_RET] = proj(C_RET + 2 * D_RET, C_GATE).astype(BF16)
    for b in range(3):
        g = proj(C_GATE + b * D_MODEL, C_GATE + (b + 1) * D_MODEL)
        gate_o[:, b * D_MODEL:(b + 1) * D_MODEL] = _sigmoid(g).astype(BF16)


def _inproj(x, mod_l, g1, w_in_bf):
    row = lambda i: (i, 0)
    return pl.pallas_call(
        _inproj_kernel,
        grid=(T // TM,),
        in_specs=[pl.BlockSpec((TM, D_MODEL), row),
                  pl.BlockSpec((1, 6, D_MODEL), lambda i: (_mod_group(i), 0, 0)),
                  pl.BlockSpec((1, D_MODEL), lambda i: (0, 0)),
                  pl.BlockSpec((D_MODEL, IN_COLS), lambda i: (0, 0), pipeline_mode=pl.Buffered(1))],
        out_specs=[pl.BlockSpec((TM, 3 * D_CONV), row),
                   pl.BlockSpec((TM, D_ATTN), row),
                   pl.BlockSpec((TM, D_KV), row),
                   pl.BlockSpec((TM, D_KV), row),
                   pl.BlockSpec((TM, 4 * D_RET), row),
                   pl.BlockSpec((TM, 3 * D_MODEL), row)],
        out_shape=[jax.ShapeDtypeStruct((T, 3 * D_CONV), BF16),
                   jax.ShapeDtypeStruct((T, D_ATTN), BF16),
                   jax.ShapeDtypeStruct((T, D_KV), F32),
                   jax.ShapeDtypeStruct((T, D_KV), F32),
                   jax.ShapeDtypeStruct((T, 4 * D_RET), BF16),
                   jax.ShapeDtypeStruct((T, 3 * D_MODEL), BF16)],
        compiler_params=pltpu.CompilerParams(vmem_limit_bytes=48 * MIB),
        name="inproj",
    )(x, mod_l, g1, w_in_bf)


def _rope(x, cos, sin_signed):
    lane = lax.broadcasted_iota(jnp.int32, x.shape, 1)
    first = (lane % 32) < 16
    partner = jnp.where(first, pltpu.roll(x, x.shape[1] - 16, 1), pltpu.roll(x, 16, 1))
    return x * cos + partner * sin_signed


def _dot_nt(a, b):
    return lax.dot_general(a, b, (((1,), (1,)), ((), ())), preferred_element_type=F32)


def _dot_tn(a, b):
    return lax.dot_general(a, b, (((0,), (0,)), ((), ())), preferred_element_type=F32)


def _mixer_kernel(*refs, n, latent):
    if latent:
        (sink_ref, conv_ref, q_ref, k_ref, v_ref, ret_ref, cw_ref, rd_ref, cos_ref, sin_ref, ck_ref, cv_ref,
         r0_ref, yc_ref, ya_ref, yr_ref, kb, vb, o_f, o_b, rst, dmat, qdec, kdec) = refs
    else:
        (sink_ref, conv_ref, q_ref, k_ref, v_ref, ret_ref, cw_ref, rd_ref,
         yc_ref, ya_ref, yr_ref, rfin_ref, kb, vb, o_f, o_b, rst, dmat, qdec, kdec) = refs
    nb = n // CHUNK

    cv = conv_ref[...].astype(F32)
    cb, cc, cu = cv[:, 0:D_CONV], cv[:, D_CONV:2 * D_CONV], cv[:, 2 * D_CONV:3 * D_CONV]
    p = cc * cu
    row = lax.broadcasted_iota(jnp.int32, p.shape, 0)
    prev = jnp.where(row == 0, 0.0, pltpu.roll(p, 1, 0))
    nxt = jnp.where(row == n - 1, 0.0, pltpu.roll(p, n - 1, 0))
    cw = cw_ref[...]
    yc_ref[...] = (cb * (prev * cw[0:1, :] + p * cw[1:2, :] + nxt * cw[2:3, :])).astype(BF16)

    if latent:
        kr = _rope(k_ref[...], cos_ref[...], sin_ref[...])
        zpad = jnp.zeros((CHUNK, D_KV), BF16)
        kb[0:CHUNK, :] = zpad
        vb[0:CHUNK, :] = zpad
        kb[CHUNK + n:2 * CHUNK + n, :] = zpad
        vb[CHUNK + n:2 * CHUNK + n, :] = zpad
        kb[CHUNK:CHUNK + n, :] = kr.astype(BF16)
        vb[CHUNK:CHUNK + n, :] = v_ref[...].astype(BF16)
        ckb = ck_ref[0, 0].astype(BF16)
        cvb = cv_ref[0, 0].astype(BF16)
    else:
        kb[...] = k_ref[...].astype(BF16)
        vb[...] = v_ref[...].astype(BF16)

    def attn_block(j):
        r0 = j * CHUNK if isinstance(j, int) else pl.multiple_of(j * CHUNK, CHUNK)
        qj = q_ref[pl.ds(r0, CHUNK), :]
        if latent:
            cosj = cos_ref[pl.ds(r0, CHUNK), :]
            sinj = sin_ref[pl.ds(r0, CHUNK), :]
            qf = qj.astype(F32)
            qj = jnp.concatenate(
                [_rope(qf[:, c * LANES:(c + 1) * LANES], cosj, sinj) for c in range(D_ATTN // LANES)],
                axis=1).astype(BF16)
            kw = kb[pl.ds(r0, 3 * CHUNK), :]
            vw = vb[pl.ds(r0, 3 * CHUNK), :]
            qpos = r0 + lax.broadcasted_iota(jnp.int32, (CHUNK, 3 * CHUNK), 0)
            kpos = r0 - CHUNK + lax.broadcasted_iota(jnp.int32, (CHUNK, 3 * CHUNK), 1)
            ok = (jnp.abs(qpos - kpos) <= WINDOW) & (kpos >= 0) & (kpos < n)
            ok = jnp.concatenate([ok] * GROUP, axis=0)
        else:
            kw = kb[...]
            vw = vb[...]
        for g in range(N_KV):
            lo = g * HEAD_DIM
            qg = jnp.concatenate(
                [qj[:, (GROUP * g + i) * HEAD_DIM:(GROUP * g + i + 1) * HEAD_DIM] for i in range(GROUP)], axis=0)
            sk = jnp.concatenate(
                [jnp.full((CHUNK, 1), sink_ref[GROUP * g + i], F32) for i in range(GROUP)], axis=0)
            s = _dot_nt(qg, kw[:, lo:lo + HEAD_DIM])
            if latent:
                s = jnp.where(ok, s, NEG_INF)
                s2 = _dot_nt(qg, ckb[:, lo:lo + HEAD_DIM])
            m = jnp.maximum(jnp.max(s, axis=-1, keepdims=True), sk)
            if latent:
                m = jnp.maximum(m, jnp.max(s2, axis=-1, keepdims=True))
            pw = jnp.exp(s - m)
            den = jnp.sum(pw, axis=-1, keepdims=True) + jnp.exp(sk - m)
            o = jnp.dot(pw.astype(BF16), vw[:, lo:lo + HEAD_DIM], preferred_element_type=F32)
            if latent:
                p2 = jnp.exp(s2 - m)
                den = den + jnp.sum(p2, axis=-1, keepdims=True)
                o = o + jnp.dot(p2.astype(BF16), cvb[:, lo:lo + HEAD_DIM], preferred_element_type=F32)
            o = o / den
            ya_ref[pl.ds(r0, CHUNK), g * GROUP * HEAD_DIM:(g + 1) * GROUP * HEAD_DIM] = jnp.concatenate(
                [o[i * CHUNK:(i + 1) * CHUNK, :] for i in range(GROUP)], axis=1).astype(BF16)

    if latent:
        def attn_body(j, carry):
            attn_block(j)
            return carry
        lax.fori_loop(0, nb, attn_body, 0)
    else:
        for j in range(nb):
            attn_block(j)

    rd = rd_ref[...]
    log_g = jnp.minimum(rd, 0.0) - jnp.log(1.0 + jnp.exp(-jnp.abs(rd)))
    ii = lax.broadcasted_iota(jnp.int32, (CHUNK, CHUNK), 0).astype(F32)
    jj = lax.broadcasted_iota(jnp.int32, (CHUNK, CHUNK), 1).astype(F32)
    for d in range(2):
        for h in range(RET_HEADS):
            r = d * RET_HEADS + h
            lg = log_g[r:r + 1, :]
            diff = (ii - jj) if d == 0 else (jj - ii)
            dmat[r] = jnp.where(diff >= 0, jnp.exp(jnp.maximum(diff, 0.0) * lg), 0.0)
            if d == 0:
                qdec[r] = jnp.exp((ii + 1.0) * lg)
                kdec[r] = jnp.exp((CHUNK - 1.0 - ii) * lg)
            else:
                qdec[r] = jnp.exp((CHUNK - ii) * lg)
                kdec[r] = jnp.exp(ii * lg)
            if latent:
                rst[r] = r0_ref[0, 0, d, h]
            else:
                rst[r] = jnp.zeros((RET_DK, RET_DV), F32)
    chunk_decay = jnp.exp(float(CHUNK) * log_g)

    def ret_chunk(c0, d, out_ref):
        blk = ret_ref[pl.ds(c0, CHUNK), :]
        outs = []
        for h in range(RET_HEADS):
            r = d * RET_HEADS + h
            qc = blk[:, h * RET_DK:(h + 1) * RET_DK]
            kc = blk[:, D_RET + h * RET_DK:D_RET + (h + 1) * RET_DK]
            vc = blk[:, 2 * D_RET + h * RET_DV:2 * D_RET + (h + 1) * RET_DV]
            inner = (_dot_nt(qc, kc) * dmat[r]).astype(BF16)
            state = rst[r]
            o = jnp.dot(inner, vc, preferred_element_type=F32)
            o = o + jnp.dot(qc, state.astype(BF16), preferred_element_type=F32) * qdec[r][:, 0:RET_DV]
            kd = (kc.astype(F32) * kdec[r][:, 0:RET_DK]).astype(BF16)
            rst[r] = state * chunk_decay[r:r + 1, 0:RET_DV] + _dot_tn(kd, vc)
            outs.append(o)
        out_ref[pl.ds(c0, CHUNK), :] = jnp.concatenate(outs, axis=1)

    def ret_body(j, carry):
        ret_chunk(pl.multiple_of(j * CHUNK, CHUNK), 0, o_f)
        ret_chunk(pl.multiple_of((nb - 1 - j) * CHUNK, CHUNK), 1, o_b)
        return carry

    lax.fori_loop(0, nb, ret_body, 0)

    def norm_body(j, carry):
        c0 = pl.multiple_of(j * CHUNK, CHUNK)
        o = o_f[pl.ds(c0, CHUNK), :] + o_b[pl.ds(c0, CHUNK), :]
        parts = []
        for h in range(RET_HEADS):
            oh = o[:, h * RET_DV:(h + 1) * RET_DV]
            ms = jnp.mean(oh * oh, axis=-1, keepdims=True)
            parts.append(oh * lax.rsqrt(ms + EPS))
        rg = ret_ref[pl.ds(c0, CHUNK), 3 * D_RET:4 * D_RET].astype(F32)
        yr_ref[pl.ds(c0, CHUNK), :] = (rg * _sigmoid(rg) * jnp.concatenate(parts, axis=1)).astype(BF16)
        return carry

    lax.fori_loop(0, nb, norm_body, 0)

    if not latent:
        for d in range(2):
            for h in range(RET_HEADS):
                rfin_ref[0, d, h] = rst[d * RET_HEADS + h]


def _mixers(conv_in, q, k, v, ret, conv_w_l, sink_l, rd8, *, latent, cos=None, sin=None, cache_k=None,
            cache_v=None, state=None, layer=0):
    n = DEC_SEQ if latent else SEQ
    nseq = DEC_BATCH if latent else BATCH
    off = T_CTX // n if latent else 0
    seq = lambda s: (s + off, 0)
    const = lambda s: (0, 0)
    in_specs = [pl.BlockSpec(memory_space=pltpu.SMEM),
                pl.BlockSpec((n, 3 * D_CONV), seq),
                pl.BlockSpec((n, D_ATTN), seq),
                pl.BlockSpec((n, D_KV), seq),
                pl.BlockSpec((n, D_KV), seq),
                pl.BlockSpec((n, 4 * D_RET), seq),
                pl.BlockSpec((3, D_CONV), const),
                pl.BlockSpec((8, LANES), const)]
    args = [sink_l, conv_in, q, k, v, ret, conv_w_l, rd8]
    out_specs = [pl.BlockSpec((n, D_CONV), lambda s: (s, 0)),
                 pl.BlockSpec((n, D_ATTN), lambda s: (s, 0)),
                 pl.BlockSpec((n, D_RET), lambda s: (s, 0))]
    out_shape = [jax.ShapeDtypeStruct((nseq * n, D_CONV), BF16),
                 jax.ShapeDtypeStruct((nseq * n, D_ATTN), BF16),
                 jax.ShapeDtypeStruct((nseq * n, D_RET), BF16)]
    if latent:
        in_specs += [pl.BlockSpec((n, LANES), const),
                     pl.BlockSpec((n, LANES), const),
                     pl.BlockSpec((1, 1, PAST_LEN, D_KV), lambda s: (s, layer, 0, 0)),
                     pl.BlockSpec((1, 1, PAST_LEN, D_KV), lambda s: (s, layer, 0, 0)),
                     pl.BlockSpec((1, 1, 2, RET_HEADS, RET_DK, RET_DV), lambda s: (s, layer, 0, 0, 0, 0))]
        args += [cos, sin, cache_k, cache_v, state]
        kv_rows = n + 2 * CHUNK
    else:
        out_specs.append(pl.BlockSpec((1, 2, RET_HEADS, RET_DK, RET_DV), lambda s: (s, 0, 0, 0, 0)))
        out_shape.append(jax.ShapeDtypeStruct((nseq, 2, RET_HEADS, RET_DK, RET_DV), F32))
        kv_rows = n
    scratch = [pltpu.VMEM((kv_rows, D_KV), BF16),
               pltpu.VMEM((kv_rows, D_KV), BF16),
               pltpu.VMEM((n, D_RET), F32),
               pltpu.VMEM((n, D_RET), F32),
               pltpu.VMEM((2 * RET_HEADS, RET_DK, RET_DV), F32),
               pltpu.VMEM((2 * RET_HEADS, CHUNK, CHUNK), F32),
               pltpu.VMEM((2 * RET_HEADS, CHUNK, CHUNK), F32),
               pltpu.VMEM((2 * RET_HEADS, CHUNK, CHUNK), F32)]
    return pl.pallas_call(
        functools.partial(_mixer_kernel, n=n, latent=latent),
        grid=(nseq,),
        in_specs=in_specs,
        out_specs=out_specs,
        out_shape=out_shape,
        scratch_shapes=scratch,
        compiler_params=pltpu.CompilerParams(vmem_limit_bytes=48 * MIB),
        name="mixers_latent" if latent else "mixers_context",
    )(*args)


def _merge_kernel(yc_ref, ya_ref, yr_ref, gate_ref, x_ref, mod_ref, g2_ref, wa_ref, wb_ref, wc_ref, wo_ref,
                  wrh_ref, wrl_ref, br_ref, x1_o, xp_o, te_o, tw_o):
    merged = (gate_ref[:, 0:D_MODEL].astype(F32)
              * jnp.dot(yc_ref[...], wa_ref[...], preferred_element_type=F32)
              + gate_ref[:, D_MODEL:2 * D_MODEL].astype(F32)
              * jnp.dot(ya_ref[...], wb_ref[...], preferred_element_type=F32)
              + gate_ref[:, 2 * D_MODEL:3 * D_MODEL].astype(F32)
              * jnp.dot(yr_ref[...], wc_ref[...], preferred_element_type=F32))
    x1 = x_ref[...] + mod_ref[0, 2:3, :] * jnp.dot(merged.astype(BF16), wo_ref[...], preferred_element_type=F32)
    x1_o[...] = x1
    ms = jnp.mean(x1 * x1, axis=-1, keepdims=True)
    h2 = x1 * lax.rsqrt(ms + EPS) * g2_ref[...]
    h2 = h2 * (1.0 + mod_ref[0, 4:5, :]) + mod_ref[0, 3:4, :]
    hh = h2.astype(BF16)
    hf = hh.astype(F32)
    bits = lax.bitcast_convert_type(hf, jnp.uint32)
    for c in range(PACK_ROWS):
        lo = bits[:, c * LANES:(c + 1) * LANES] >> 16
        hi = bits[:, (c + PACK_ROWS) * LANES:(c + PACK_ROWS + 1) * LANES] & jnp.uint32(0xFFFF0000)
        xp_o[pl.ds(c, TM, stride=PACK_ROWS), :] = lax.bitcast_convert_type(lo | hi, jnp.int32)
    hl = (h2 - hf).astype(BF16)
    logits = (jnp.dot(hh, wrh_ref[...], preferred_element_type=F32)
              + jnp.dot(hl, wrh_ref[...], preferred_element_type=F32)
              + jnp.dot(hh, wrl_ref[...], preferred_element_type=F32)
              + br_ref[...])
    lane = lax.broadcasted_iota(jnp.int32, logits.shape, 1)
    work = jnp.where(lane < N_EXPERTS, logits, -jnp.inf)
    vals, idxs = [], []
    for _ in range(TOP_K):
        m = jnp.max(work, axis=-1, keepdims=True)
        am = jnp.min(jnp.where(work == m, lane, LANES), axis=-1, keepdims=True)
        vals.append(m)
        idxs.append(am)
        work = jnp.where(lane == am, -jnp.inf, work)
    es = [jnp.exp(v - vals[0]) for v in vals]
    den = es[0] + es[1] + es[2] + es[3]
    te = jnp.zeros(logits.shape, jnp.int32)
    tw = jnp.zeros(logits.shape, F32)
    for k in range(TOP_K):
        te = jnp.where(lane == k, idxs[k], te)
        tw = jnp.where(lane == k, es[k] / den, tw)
    te_o[...] = te
    tw_o[...] = tw


def _merge(yc, ya, yr, gates, x, mod_l, g2, wa, wb, wc, wo, wrh, wrl, br):
    row = lambda i: (i, 0)
    const = lambda i: (0, 0)
    return pl.pallas_call(
        _merge_kernel,
        grid=(T // TM,),
        in_specs=[pl.BlockSpec((TM, D_CONV), row),
                  pl.BlockSpec((TM, D_ATTN), row),
                  pl.BlockSpec((TM, D_RET), row),
                  pl.BlockSpec((TM, 3 * D_MODEL), row),
                  pl.BlockSpec((TM, D_MODEL), row),
                  pl.BlockSpec((1, 6, D_MODEL), lambda i: (_mod_group(i), 0, 0)),
                  pl.BlockSpec((1, D_MODEL), const),
                  pl.BlockSpec((D_CONV, D_MODEL), const),
                  pl.BlockSpec((D_ATTN, D_MODEL), const),
                  pl.BlockSpec((D_RET, D_MODEL), const),
                  pl.BlockSpec((D_MODEL, D_MODEL), const),
                  pl.BlockSpec((D_MODEL, LANES), const),
                  pl.BlockSpec((D_MODEL, LANES), const),
                  pl.BlockSpec((1, LANES), const)],
        out_specs=[pl.BlockSpec((TM, D_MODEL), row),
                   pl.BlockSpec((TM * PACK_ROWS, LANES), row),
                   pl.BlockSpec((TM, LANES), row),
                   pl.BlockSpec((TM, LANES), row)],
        out_shape=[jax.ShapeDtypeStruct((T, D_MODEL), F32),
                   jax.ShapeDtypeStruct((T * PACK_ROWS, LANES), jnp.int32),
                   jax.ShapeDtypeStruct((T, LANES), jnp.int32),
                   jax.ShapeDtypeStruct((T, LANES), F32)],
        compiler_params=pltpu.CompilerParams(vmem_limit_bytes=48 * MIB),
        name="merge_router",
    )(yc, ya, yr, gates, x, mod_l, g2, wa, wb, wc, wo, wrh, wrl, br)


def _moe_kernel(blk_e_ref, first_ref, nu_ref, tok_ref, xp_ref, wgu_ref, bgu_ref, wd_ref, bd_ref, y_ref,
                tile, wgu_bf, wd_bf):
    del blk_e_ref
    b = pl.program_id(0)

    @pl.when(b < nu_ref[0])
    def _():
        @pl.when(first_ref[b] == 1)
        def _():
            wgu_bf[...] = wgu_ref[0, 0].astype(BF16)
            wd_bf[...] = wd_ref[0, 0].astype(BF16)

        for mi in range(MOE_M):
            t = jnp.minimum(tok_ref[0, 0, mi], T - 1)
            slab = xp_ref[pl.ds(pl.multiple_of(t * PACK_ROWS, PACK_ROWS), PACK_ROWS), :]
            tile[pl.ds(mi, PACK_ROWS, stride=GATHER_STRIDE), :] = slab
        lo, hi = [], []
        for c in range(PACK_ROWS):
            bits = lax.bitcast_convert_type(tile[c * GATHER_STRIDE:c * GATHER_STRIDE + MOE_M, :], jnp.uint32)
            lo.append(lax.bitcast_convert_type(bits << 16, F32).astype(BF16))
            hi.append(lax.bitcast_convert_type(bits & jnp.uint32(0xFFFF0000), F32).astype(BF16))
        x = jnp.concatenate(lo + hi, axis=1)

        gu = jnp.dot(x, wgu_bf[...], preferred_element_type=F32) + bgu_ref[0, 0]
        gate = jnp.minimum(gu[:, 0:D_EXPERT], SWIGLU_LIMIT)
        up = jnp.clip(gu[:, D_EXPERT:2 * D_EXPERT], -SWIGLU_LIMIT, SWIGLU_LIMIT)
        glu = gate * _sigmoid(SWIGLU_ALPHA * gate)
        mid = ((up + 1.0) * glu).astype(BF16)
        y = jnp.dot(mid, wd_bf[...], preferred_element_type=F32) + bd_ref[0, 0]
        y_ref[...] = y.astype(BF16)

    @pl.when(b >= nu_ref[0])
    def _():
        y_ref[...] = jnp.zeros_like(y_ref)


def _moe_experts(blk_e, first, n_used, row_tok3, xp, w_gu, b_gu, w_down, b_down, layer):
    grid_spec = pltpu.PrefetchScalarGridSpec(
        num_scalar_prefetch=3,
        grid=(N_BLOCKS,),
        in_specs=[pl.BlockSpec((1, 1, MOE_M), lambda b, e, f, nu: (b, 0, 0), memory_space=pltpu.SMEM),
                  pl.BlockSpec((T * PACK_ROWS, LANES), lambda b, e, f, nu: (0, 0), pipeline_mode=pl.Buffered(1)),
                  pl.BlockSpec((1, 1, D_MODEL, 2 * D_EXPERT), lambda b, e, f, nu: (layer, e[b], 0, 0)),
                  pl.BlockSpec((1, 1, 1, 2 * D_EXPERT), lambda b, e, f, nu: (layer, e[b], 0, 0)),
                  pl.BlockSpec((1, 1, D_EXPERT, D_MODEL), lambda b, e, f, nu: (layer, e[b], 0, 0)),
                  pl.BlockSpec((1, 1, 1, D_MODEL), lambda b, e, f, nu: (layer, e[b], 0, 0))],
        out_specs=pl.BlockSpec((MOE_M, D_MODEL), lambda b, e, f, nu: (b, 0)),
        scratch_shapes=[pltpu.VMEM((PACK_ROWS * GATHER_STRIDE, LANES), jnp.int32),
                        pltpu.VMEM((D_MODEL, 2 * D_EXPERT), BF16),
                        pltpu.VMEM((D_EXPERT, D_MODEL), BF16)],
    )
    return pl.pallas_call(
        _moe_kernel,
        grid_spec=grid_spec,
        out_shape=jax.ShapeDtypeStruct((N_ROWS, D_MODEL), BF16),
        compiler_params=pltpu.CompilerParams(vmem_limit_bytes=58 * MIB),
        name="moe_experts",
    )(blk_e, first, n_used, row_tok3, xp, w_gu, b_gu.reshape(DEPTH, N_EXPERTS, 1, 2 * D_EXPERT), w_down,
      b_down.reshape(DEPTH, N_EXPERTS, 1, D_MODEL))


SCATTER_UNROLL = 8


def _combine_kernel(nu_ref, tok_ref, w_ref, y_ref, o_ref, acc, tile, sem):
    s = pl.program_id(0)

    @pl.when(s == 0)
    def _():
        acc[...] = jnp.zeros_like(acc)

    def sub_block(sb, carry):
        r0 = pl.multiple_of(sb * SCATTER_M, SCATTER_M)
        y = y_ref[pl.ds(r0, SCATTER_M), :].astype(F32)
        for c in range(ROW_VREGS):
            tile[c * SCATTER_STRIDE:c * SCATTER_STRIDE + SCATTER_M, :] = y[:, c * LANES:(c + 1) * LANES]
        for m0 in range(0, SCATTER_M, SCATTER_UNROLL):
            addrs, vals = [], []
            for u in range(SCATTER_UNROLL):
                mi = m0 + u
                a = pl.multiple_of(tok_ref[0, 0, r0 + mi] * ROW_VREGS, ROW_VREGS)
                yv = tile[pl.ds(mi, ROW_VREGS, stride=SCATTER_STRIDE), :]
                addrs.append(a)
                vals.append(acc[pl.ds(a, ROW_VREGS), :] + w_ref[0, 0, r0 + mi] * yv)
            for u in range(SCATTER_UNROLL):
                acc[pl.ds(addrs[u], ROW_VREGS), :] = vals[u]
        return carry

    @pl.when(s * COMBINE_BLOCKS < nu_ref[0])
    def _():
        lax.fori_loop(0, COMBINE_BLOCKS * MOE_M // SCATTER_M, sub_block, 0)

    @pl.when(s == pl.num_programs(0) - 1)
    def _():
        cp = pltpu.make_async_copy(acc.at[pl.ds(0, T * ROW_VREGS)], o_ref, sem)
        cp.start()
        cp.wait()


def _combine(n_used, row_tok3, row_w3, yr):
    rows = COMBINE_BLOCKS * MOE_M
    steps = N_BLOCKS // COMBINE_BLOCKS
    grid_spec = pltpu.PrefetchScalarGridSpec(
        num_scalar_prefetch=1,
        grid=(steps,),
        in_specs=[pl.BlockSpec((1, 1, rows), lambda s, nu: (s, 0, 0), memory_space=pltpu.SMEM),
                  pl.BlockSpec((1, 1, rows), lambda s, nu: (s, 0, 0), memory_space=pltpu.SMEM),
                  pl.BlockSpec((rows, D_MODEL), lambda s, nu: (s, 0))],
        out_specs=pl.BlockSpec(memory_space=pl.ANY),
        scratch_shapes=[pltpu.VMEM(((T + 1) * ROW_VREGS, LANES), F32),
                        pltpu.VMEM((ROW_VREGS * SCATTER_STRIDE, LANES), F32),
                        pltpu.SemaphoreType.DMA(())],
    )
    return pl.pallas_call(
        _combine_kernel,
        grid_spec=grid_spec,
        out_shape=jax.ShapeDtypeStruct((T * ROW_VREGS, LANES), F32),
        compiler_params=pltpu.CompilerParams(vmem_limit_bytes=48 * MIB),
        name="moe_combine",
    )(n_used, row_tok3.reshape(steps, 1, rows), row_w3.reshape(steps, 1, rows), yr)


def _route(top_e, top_w):
    flat_e = top_e.reshape(N_ASSIGN)
    idx = jnp.arange(N_ASSIGN, dtype=jnp.int32)
    _, sorted_idx, sorted_w = lax.sort((flat_e, idx, top_w.reshape(N_ASSIGN)), num_keys=1, is_stable=True)
    experts = jnp.arange(N_EXPERTS, dtype=jnp.int32)
    counts = jnp.sum((flat_e[:, None] == experts[None, :]).astype(jnp.int32), axis=0)
    start = jnp.cumsum(counts) - counts
    padded = (counts + MOE_M - 1) // MOE_M * MOE_M
    pad_end = jnp.cumsum(padded)
    pad_start = pad_end - padded
    n_used = (pad_end[-1] // MOE_M).astype(jnp.int32)
    blk0 = jnp.arange(N_BLOCKS, dtype=jnp.int32) * MOE_M
    blk_e = jnp.minimum(jnp.sum((blk0[:, None] >= pad_end[None, :]).astype(jnp.int32), axis=1), N_EXPERTS - 1)
    q = (blk0 - pad_start[blk_e])[:, None] + jnp.arange(MOE_M, dtype=jnp.int32)[None, :]
    valid = q < counts[blk_e][:, None]
    pos = jnp.clip(start[blk_e][:, None] + q, 0, N_ASSIGN - 1)
    row_tok = jnp.where(valid, sorted_idx[pos] // TOP_K, T).astype(jnp.int32)
    row_w = jnp.where(valid, sorted_w[pos], 0.0)
    first = (blk0 == pad_start[blk_e]).astype(jnp.int32)
    blk_e = jnp.where(blk0 < pad_end[-1], blk_e, blk_e[jnp.maximum(n_used - 1, 0)])
    return (row_tok.reshape(N_BLOCKS, 1, MOE_M), row_w.reshape(N_BLOCKS, 1, MOE_M), blk_e, first,
            n_used.reshape(1))


def _residual_kernel(x_ref, moe_ref, mod_ref, g_ref, o_ref, *, final):
    moe = jnp.concatenate([moe_ref[pl.ds(c, TM, stride=ROW_VREGS), :] for c in range(ROW_VREGS)], axis=1)
    x = x_ref[...] + mod_ref[0, 5:6, :] * moe
    if final:
        ms = jnp.mean(x * x, axis=-1, keepdims=True)
        x = x * lax.rsqrt(ms + EPS) * g_ref[...]
    o_ref[...] = x


def _residual(x1, moe_tiles, mod_l, g, final):
    row = lambda i: (i, 0)
    return pl.pallas_call(
        functools.partial(_residual_kernel, final=final),
        grid=(T // TM,),
        in_specs=[pl.BlockSpec((TM, D_MODEL), row),
                  pl.BlockSpec((TM * ROW_VREGS, LANES), row),
                  pl.BlockSpec((1, 6, D_MODEL), lambda i: (_mod_group(i), 0, 0)),
                  pl.BlockSpec((1, D_MODEL), lambda i: (0, 0))],
        out_specs=pl.BlockSpec((TM, D_MODEL), row),
        out_shape=jax.ShapeDtypeStruct((T, D_MODEL), F32),
        compiler_params=pltpu.CompilerParams(vmem_limit_bytes=32 * MIB),
        name="residual_final" if final else "residual",
    )(x1, moe_tiles, mod_l, g)


def _rope_tables():
    t = np.arange(DEC_SEQ)
    pos = np.stack([t // GRID_W, t % GRID_W], axis=1).astype(np.float32)
    half = HEAD_DIM // 2
    inv = jnp.asarray(ROPE_BASE, F32) ** (-jnp.arange(0, half, 2, dtype=F32) / half)
    d = np.arange(HEAD_DIM)
    which = d // half
    freq = d % (half // 2)
    sign = np.where((d % half) < half // 2, -1.0, 1.0).astype(np.float32)
    ang = jnp.asarray(pos)[:, which] * inv[freq][None, :]
    cos = jnp.cos(ang)
    sin = jnp.sin(ang) * jnp.asarray(sign)[None, :]
    reps = LANES // HEAD_DIM
    return jnp.tile(cos, (1, reps)), jnp.tile(sin, (1, reps))


def kernel(x_prompt, x_sample, cache_k, cache_v, state_ret, c, c_ctx, norm1_g, norm2_g, w_mod, b_mod, w_in, conv_w, attn_sink, ret_decay, w_a, w_b, w_c, w_o, w_router, b_router, w_gu, b_gu, w_down, b_down, final_g):
    x = jnp.concatenate([x_prompt.reshape(T_CTX, D_MODEL), x_sample.reshape(T_LAT, D_MODEL)], axis=0)
    cond8 = jnp.zeros((8, D_MODEL), F32).at[0].set(c_ctx).at[1:1 + DEC_BATCH].set(c)
    mod = _modulation(cond8, w_mod, b_mod)
    cos, sin = _rope_tables()
    ck = cache_k.reshape(DEC_BATCH, DEPTH, PAST_LEN, D_KV)
    cv = cache_v.reshape(DEC_BATCH, DEPTH, PAST_LEN, D_KV)

    ks, vs, rs = [], [], []
    for l in range(DEPTH):
        mod_l = mod[l, 0:1 + DEC_BATCH].reshape(1 + DEC_BATCH, 6, D_MODEL)
        conv_in, q, k, v, ret, gates = _inproj(x, mod_l, norm1_g[l][None, :], w_in[l].astype(BF16))
        rd8 = jnp.broadcast_to(ret_decay[l].reshape(2 * RET_HEADS, 1), (2 * RET_HEADS, LANES))
        yc_c, ya_c, yr_c, rfin = _mixers(conv_in, q, k, v, ret, conv_w[l], attn_sink[l], rd8, latent=False)
        yc_s, ya_s, yr_s = _mixers(conv_in, q, k, v, ret, conv_w[l], attn_sink[l], rd8, latent=True,
                                   cos=cos, sin=sin, cache_k=ck, cache_v=cv, state=state_ret, layer=l)
        yc = jnp.concatenate([yc_c, yc_s], axis=0)
        ya = jnp.concatenate([ya_c, ya_s], axis=0)
        yr = jnp.concatenate([yr_c, yr_s], axis=0)
        wr = jnp.pad(w_router[l], ((0, 0), (0, LANES - N_EXPERTS)))
        wrh = wr.astype(BF16)
        wrl = (wr - wrh.astype(F32)).astype(BF16)
        br = jnp.pad(b_router[l], (0, LANES - N_EXPERTS))[None, :]
        x1, xp, top_e, top_w = _merge(yc, ya, yr, gates, x, mod_l, norm2_g[l][None, :], w_a[l].astype(BF16),
                                      w_b[l].astype(BF16), w_c[l].astype(BF16), w_o[l].astype(BF16), wrh, wrl, br)
        row_tok, row_w, blk_e, first, n_used = _route(top_e[:, 0:TOP_K], top_w[:, 0:TOP_K])
        yrows = _moe_experts(blk_e, first, n_used, row_tok, xp, w_gu, b_gu, w_down, b_down, l)
        moe = _combine(n_used, row_tok, row_w, yrows)
        final = l == DEPTH - 1
        x = _residual(x1, moe, mod_l, final_g[None, :], final)
        ks.append(k[0:T_CTX].reshape(BATCH, SEQ, N_KV, HEAD_DIM))
        vs.append(v[0:T_CTX].reshape(BATCH, SEQ, N_KV, HEAD_DIM))
        rs.append(rfin)

    y_prompt = x[0:T_CTX].reshape(BATCH, SEQ, D_MODEL)
    y_sample = x[T_CTX:T].reshape(DEC_BATCH, DEC_SEQ, D_MODEL)
    return (y_prompt, y_sample, jnp.stack(ks, axis=1), jnp.stack(vs, axis=1), jnp.stack(rs, axis=1))
```

```python
import functools

import numpy as np
import jax
import jax.numpy as jnp
from jax import lax
from jax.experimental import pallas as pl
from jax.experimental.pallas import tpu as pltpu

F32 = jnp.float32
BF16 = jnp.bfloat16

D_MODEL = 1024
BATCH = 16
SEQ = 256
DEPTH = 2
DEC_BATCH = 2
DEC_SEQ = 2048
PAST_LEN = 256
GRID_W = 64
HEAD_DIM = 64
D_CONV = 256
N_HEADS = 8
N_KV = 2
GROUP = N_HEADS // N_KV
WINDOW = 128
ROPE_BASE = 10000.0
RET_HEADS = 4
RET_DK = 64
RET_DV = 64
CHUNK = 128
N_EXPERTS = 32
TOP_K = 4
D_EXPERT = D_MODEL
SWIGLU_LIMIT = 7.0
SWIGLU_ALPHA = 1.702
EPS = 1e-6
NEG_INF = -1e30

T_CTX = BATCH * SEQ
T_LAT = DEC_BATCH * DEC_SEQ
T = T_CTX + T_LAT
D_ATTN = N_HEADS * HEAD_DIM
D_KV = N_KV * HEAD_DIM
D_RET = RET_HEADS * RET_DK
C_CONV = 0
C_Q = 3 * D_CONV
C_K = C_Q + D_ATTN
C_V = C_K + D_KV
C_RET = C_V + D_KV
C_GATE = C_RET + 4 * D_RET
IN_COLS = C_GATE + 3 * D_MODEL

TM = 512
MOE_M = 256
N_ASSIGN = T * TOP_K
N_BLOCKS = (N_ASSIGN + N_EXPERTS * (MOE_M - 1) + MOE_M - 1) // MOE_M
N_ROWS = N_BLOCKS * MOE_M
LANES = 128
SUBLANES = 8
ROW_VREGS = D_MODEL // LANES
PACK_ROWS = ROW_VREGS // 2
GATHER_STRIDE = MOE_M + SUBLANES
SCATTER_M = 128
SCATTER_STRIDE = SCATTER_M + SUBLANES
COMBINE_BLOCKS = 2
MIB = 1024 * 1024


def _sigmoid(x):
    return 1.0 / (1.0 + jnp.exp(-x))


def _mod_group(i):
    n_ctx = T_CTX // TM
    per_lat = DEC_SEQ // TM
    g = jnp.zeros_like(i)
    for b in range(DEC_BATCH):
        g = g + (i >= n_ctx + b * per_lat).astype(jnp.int32)
    return g


def _mod_kernel(cond_ref, w_ref, b_ref, o_ref):
    c = cond_ref[...]
    s = c * _sigmoid(c)
    o_ref[0] = jnp.dot(s.astype(BF16), w_ref[0].astype(BF16), preferred_element_type=F32) + b_ref[0]


def _modulation(cond8, w_mod, b_mod):
    n_col = 4
    cw = 6 * D_MODEL // n_col
    return pl.pallas_call(
        _mod_kernel,
        grid=(DEPTH, n_col),
        in_specs=[pl.BlockSpec((8, D_MODEL), lambda l, j: (0, 0)),
                  pl.BlockSpec((1, D_MODEL, cw), lambda l, j: (l, 0, j)),
                  pl.BlockSpec((1, 1, cw), lambda l, j: (l, 0, j))],
        out_specs=pl.BlockSpec((1, 8, cw), lambda l, j: (l, 0, j)),
        out_shape=jax.ShapeDtypeStruct((DEPTH, 8, 6 * D_MODEL), F32),
        compiler_params=pltpu.CompilerParams(vmem_limit_bytes=32 * MIB),
        name="modulation",
    )(cond8, w_mod, b_mod.reshape(DEPTH, 1, 6 * D_MODEL))


def _inproj_kernel(x_ref, mod_ref, g_ref, w_ref, conv_o, q_o, k_o, v_o, ret_o, gate_o):
    x = x_ref[...]
    ms = jnp.mean(x * x, axis=-1, keepdims=True)
    h = x * lax.rsqrt(ms + EPS) * g_ref[...]
    h = h * (1.0 + mod_ref[0, 1:2, :]) + mod_ref[0, 0:1, :]
    hb = h.astype(BF16)

    def proj(c0, c1):
        return jnp.dot(hb, w_ref[:, c0:c1], preferred_element_type=F32)

    conv_o[...] = proj(C_CONV, C_Q).astype(BF16)
    q_o[...] = (proj(C_Q, C_K) * HEAD_DIM ** -0.5).astype(BF16)
    k_o[...] = proj(C_K, C_V)
    v_o[...] = proj(C_V, C_RET)
    ret_o[:, 0:D_RET] = proj(C_RET, C_RET + D_RET).astype(BF16)
    ret_o[:, D_RET:2 * D_RET] = (proj(C_RET + D_RET, C_RET + 2 * D_RET) * RET_DK ** -0.5).astype(BF16)
    ret_o[:, 2 * D_RET:4 * D_RET] = proj(C_RET + 2 * D_RET, C_GATE).astype(BF16)
    for b in range(3):
        g = proj(C_GATE + b * D_MODEL, C_GATE + (b + 1) * D_MODEL)
        gate_o[:, b * D_MODEL:(b + 1) * D_MODEL] = _sigmoid(g).astype(BF16)


def _inproj(x, mod_l, g1, w_in_bf):
    row = lambda i: (i, 0)
    return pl.pallas_call(
        _inproj_kernel,
        grid=(T // TM,),
        in_specs=[pl.BlockSpec((TM, D_MODEL), row),
                  pl.BlockSpec((1, 6, D_MODEL), lambda i: (_mod_group(i), 0, 0)),
                  pl.BlockSpec((1, D_MODEL), lambda i: (0, 0)),
                  pl.BlockSpec((D_MODEL, IN_COLS), lambda i: (0, 0), pipeline_mode=pl.Buffered(1))],
        out_specs=[pl.BlockSpec((TM, 3 * D_CONV), row),
                   pl.BlockSpec((TM, D_ATTN), row),
                   pl.BlockSpec((TM, D_KV), row),
                   pl.BlockSpec((TM, D_KV), row),
                   pl.BlockSpec((TM, 4 * D_RET), row),
                   pl.BlockSpec((TM, 3 * D_MODEL), row)],
        out_shape=[jax.ShapeDtypeStruct((T, 3 * D_CONV), BF16),
                   jax.ShapeDtypeStruct((T, D_ATTN), BF16),
                   jax.ShapeDtypeStruct((T, D_KV), F32),
                   jax.ShapeDtypeStruct((T, D_KV), F32),
                   jax.ShapeDtypeStruct((T, 4 * D_RET), BF16),
                   jax.ShapeDtypeStruct((T, 3 * D_MODEL), BF16)],
        compiler_params=pltpu.CompilerParams(vmem_limit_bytes=48 * MIB),
        name="inproj",
    )(x, mod_l, g1, w_in_bf)


def _rope(x, cos, sin_signed):
    lane = lax.broadcasted_iota(jnp.int32, x.shape, 1)
    first = (lane % 32) < 16
    partner = jnp.where(first, pltpu.roll(x, x.shape[1] - 16, 1), pltpu.roll(x, 16, 1))
    return x * cos + partner * sin_signed


def _dot_nt(a, b):
    return lax.dot_general(a, b, (((1,), (1,)), ((), ())), preferred_element_type=F32)


def _dot_tn(a, b):
    return lax.dot_general(a, b, (((0,), (0,)), ((), ())), preferred_element_type=F32)


def _mixer_kernel(*refs, n, latent):
    if latent:
        (sink_ref, conv_ref, q_ref, k_ref, v_ref, ret_ref, cw_ref, rd_ref, cos_ref, sin_ref, ck_ref, cv_ref,
         r0_ref, yc_ref, ya_ref, yr_ref, kb, vb, o_f, o_b, rst, dmat, qdec, kdec) = refs
    else:
        (sink_ref, conv_ref, q_ref, k_ref, v_ref, ret_ref, cw_ref, rd_ref,
         yc_ref, ya_ref, yr_ref, rfin_ref, kb, vb, o_f, o_b, rst, dmat, qdec, kdec) = refs
    nb = n // CHUNK

    cv = conv_ref[...].astype(F32)
    cb, cc, cu = cv[:, 0:D_CONV], cv[:, D_CONV:2 * D_CONV], cv[:, 2 * D_CONV:3 * D_CONV]
    p = cc * cu
    row = lax.broadcasted_iota(jnp.int32, p.shape, 0)
    prev = jnp.where(row == 0, 0.0, pltpu.roll(p, 1, 0))
    nxt = jnp.where(row == n - 1, 0.0, pltpu.roll(p, n - 1, 0))
    cw = cw_ref[...]
    yc_ref[...] = (cb * (prev * cw[0:1, :] + p * cw[1:2, :] + nxt * cw[2:3, :])).astype(BF16)

    if latent:
        kr = _rope(k_ref[...], cos_ref[...], sin_ref[...])
        zpad = jnp.zeros((CHUNK, D_KV), BF16)
        kb[0:CHUNK, :] = zpad
        vb[0:CHUNK, :] = zpad
        kb[CHUNK + n:2 * CHUNK + n, :] = zpad
        vb[CHUNK + n:2 * CHUNK + n, :] = zpad
        kb[CHUNK:CHUNK + n, :] = kr.astype(BF16)
        vb[CHUNK:CHUNK + n, :] = v_ref[...].astype(BF16)
        ckb = ck_ref[0, 0].astype(BF16)
        cvb = cv_ref[0, 0].astype(BF16)
    else:
        kb[...] = k_ref[...].astype(BF16)
        vb[...] = v_ref[...].astype(BF16)

    def attn_block(j):
        r0 = j * CHUNK if isinstance(j, int) else pl.multiple_of(j * CHUNK, CHUNK)
        qj = q_ref[pl.ds(r0, CHUNK), :]
        if latent:
            cosj = cos_ref[pl.ds(r0, CHUNK), :]
            sinj = sin_ref[pl.ds(r0, CHUNK), :]
            qf = qj.astype(F32)
            qj = jnp.concatenate(
                [_rope(qf[:, c * LANES:(c + 1) * LANES], cosj, sinj) for c in range(D_ATTN // LANES)],
                axis=1).astype(BF16)
            kw = kb[pl.ds(r0, 3 * CHUNK), :]
            vw = vb[pl.ds(r0, 3 * CHUNK), :]
            qpos = r0 + lax.broadcasted_iota(jnp.int32, (CHUNK, 3 * CHUNK), 0)
            kpos = r0 - CHUNK + lax.broadcasted_iota(jnp.int32, (CHUNK, 3 * CHUNK), 1)
            ok = (jnp.abs(qpos - kpos) <= WINDOW) & (kpos >= 0) & (kpos < n)
            ok = jnp.concatenate([ok] * GROUP, axis=0)
        else:
            kw = kb[...]
            vw = vb[...]
        for g in range(N_KV):
            lo = g * HEAD_DIM
            qg = jnp.concatenate(
                [qj[:, (GROUP * g + i) * HEAD_DIM:(GROUP * g + i + 1) * HEAD_DIM] for i in range(GROUP)], axis=0)
            sk = jnp.concatenate(
                [jnp.full((CHUNK, 1), sink_ref[GROUP * g + i], F32) for i in range(GROUP)], axis=0)
            s = _dot_nt(qg, kw[:, lo:lo + HEAD_DIM])
            if latent:
                s = jnp.where(ok, s, NEG_INF)
                s2 = _dot_nt(qg, ckb[:, lo:lo + HEAD_DIM])
            m = jnp.maximum(jnp.max(s, axis=-1, keepdims=True), sk)
            if latent:
                m = jnp.maximum(m, jnp.max(s2, axis=-1, keepdims=True))
            pw = jnp.exp(s - m)
            den = jnp.sum(pw, axis=-1, keepdims=True) + jnp.exp(sk - m)
            o = jnp.dot(pw.astype(BF16), vw[:, lo:lo + HEAD_DIM], preferred_element_type=F32)
            if latent:
                p2 = jnp.exp(s2 - m)
                den = den + jnp.sum(p2, axis=-1, keepdims=True)
                o = o + jnp.dot(p2.astype(BF16), cvb[:, lo:lo + HEAD_DIM], preferred_element_type=F32)
            o = o / den
            ya_ref[pl.ds(r0, CHUNK), g * GROUP * HEAD_DIM:(g + 1) * GROUP * HEAD_DIM] = jnp.concatenate(
                [o[i * CHUNK:(i + 1) * CHUNK, :] for i in range(GROUP)], axis=1).astype(BF16)

    if latent:
        def attn_body(j, carry):
            attn_block(j)
            return carry
        lax.fori_loop(0, nb, attn_body, 0)
    else:
        for j in range(nb):
            attn_block(j)

    rd = rd_ref[...]
    log_g = jnp.minimum(rd, 0.0) - jnp.log(1.0 + jnp.exp(-jnp.abs(rd)))
    ii = lax.broadcasted_iota(jnp.int32, (CHUNK, CHUNK), 0).astype(F32)
    jj = lax.broadcasted_iota(jnp.int32, (CHUNK, CHUNK), 1).astype(F32)
    for d in range(2):
        for h in range(RET_HEADS):
            r = d * RET_HEADS + h
            lg = log_g[r:r + 1, :]
            diff = (ii - jj) if d == 0 else (jj - ii)
            dmat[r] = jnp.where(diff >= 0, jnp.exp(jnp.maximum(diff, 0.0) * lg), 0.0)
            if d == 0:
                qdec[r] = jnp.exp((ii + 1.0) * lg)
                kdec[r] = jnp.exp((CHUNK - 1.0 - ii) * lg)
            else:
                qdec[r] = jnp.exp((CHUNK - ii) * lg)
                kdec[r] = jnp.exp(ii * lg)
            if latent:
                rst[r] = r0_ref[0, 0, d, h]
            else:
                rst[r] = jnp.zeros((RET_DK, RET_DV), F32)
    chunk_decay = jnp.exp(float(CHUNK) * log_g)

    def ret_chunk(c0, d, out_ref):
        blk = ret_ref[pl.ds(c0, CHUNK), :]
        outs = []
        for h in range(RET_HEADS):
            r = d * RET_HEADS + h
            qc = blk[:, h * RET_DK:(h + 1) * RET_DK]
            kc = blk[:, D_RET + h * RET_DK:D_RET + (h + 1) * RET_DK]
            vc = blk[:, 2 * D_RET + h * RET_DV:2 * D_RET + (h + 1) * RET_DV]
            inner = (_dot_nt(qc, kc) * dmat[r]).astype(BF16)
            state = rst[r]
            o = jnp.dot(inner, vc, preferred_element_type=F32)
            o = o + jnp.dot(qc, state.astype(BF16), preferred_element_type=F32) * qdec[r][:, 0:RET_DV]
            kd = (kc.astype(F32) * kdec[r][:, 0:RET_DK]).astype(BF16)
            rst[r] = state * chunk_decay[r:r + 1, 0:RET_DV] + _dot_tn(kd, vc)
            outs.append(o)
        out_ref[pl.ds(c0, CHUNK), :] = jnp.concatenate(outs, axis=1)

    def ret_body(j, carry):
        ret_chunk(pl.multiple_of(j * CHUNK, CHUNK), 0, o_f)
        ret_chunk(pl.multiple_of((nb - 1 - j) * CHUNK, CHUNK), 1, o_b)
        return carry

    lax.fori_loop(0, nb, ret_body, 0)

    def norm_body(j, carry):
        c0 = pl.multiple_of(j * CHUNK, CHUNK)
        o = o_f[pl.ds(c0, CHUNK), :] + o_b[pl.ds(c0, CHUNK), :]
        parts = []
        for h in range(RET_HEADS):
            oh = o[:, h * RET_DV:(h + 1) * RET_DV]
            ms = jnp.mean(oh * oh, axis=-1, keepdims=True)
            parts.append(oh * lax.rsqrt(ms + EPS))
        rg = ret_ref[pl.ds(c0, CHUNK), 3 * D_RET:4 * D_RET].astype(F32)
        yr_ref[pl.ds(c0, CHUNK), :] = (rg * _sigmoid(rg) * jnp.concatenate(parts, axis=1)).astype(BF16)
        return carry

    lax.fori_loop(0, nb, norm_body, 0)

    if not latent:
        for d in range(2):
            for h in range(RET_HEADS):
                rfin_ref[0, d, h] = rst[d * RET_HEADS + h]


def _mixers(conv_in, q, k, v, ret, conv_w_l, sink_l, rd8, *, latent, cos=None, sin=None, cache_k=None,
            cache_v=None, state=None, layer=0):
    n = DEC_SEQ if latent else SEQ
    nseq = DEC_BATCH if latent else BATCH
    off = T_CTX // n if latent else 0
    seq = lambda s: (s + off, 0)
    const = lambda s: (0, 0)
    in_specs = [pl.BlockSpec(memory_space=pltpu.SMEM),
                pl.BlockSpec((n, 3 * D_CONV), seq),
                pl.BlockSpec((n, D_ATTN), seq),
                pl.BlockSpec((n, D_KV), seq),
                pl.BlockSpec((n, D_KV), seq),
                pl.BlockSpec((n, 4 * D_RET), seq),
                pl.BlockSpec((3, D_CONV), const),
                pl.BlockSpec((8, LANES), const)]
    args = [sink_l, conv_in, q, k, v, ret, conv_w_l, rd8]
    out_specs = [pl.BlockSpec((n, D_CONV), lambda s: (s, 0)),
                 pl.BlockSpec((n, D_ATTN), lambda s: (s, 0)),
                 pl.BlockSpec((n, D_RET), lambda s: (s, 0))]
    out_shape = [jax.ShapeDtypeStruct((nseq * n, D_CONV), BF16),
                 jax.ShapeDtypeStruct((nseq * n, D_ATTN), BF16),
                 jax.ShapeDtypeStruct((nseq * n, D_RET), BF16)]
    if latent:
        in_specs += [pl.BlockSpec((n, LANES), const),
                     pl.BlockSpec((n, LANES), const),
                     pl.BlockSpec((1, 1, PAST_LEN, D_KV), lambda s: (s, layer, 0, 0)),
                     pl.BlockSpec((1, 1, PAST_LEN, D_KV), lambda s: (s, layer, 0, 0)),
                     pl.BlockSpec((1, 1, 2, RET_HEADS, RET_DK, RET_DV), lambda s: (s, layer, 0, 0, 0, 0))]
        args += [cos, sin, cache_k, cache_v, state]
        kv_rows = n + 2 * CHUNK
    else:
        out_specs.append(pl.BlockSpec((1, 2, RET_HEADS, RET_DK, RET_DV), lambda s: (s, 0, 0, 0, 0)))
        out_shape.append(jax.ShapeDtypeStruct((nseq, 2, RET_HEADS, RET_DK, RET_DV), F32))
        kv_rows = n
    scratch = [pltpu.VMEM((kv_rows, D_KV), BF16),
               pltpu.VMEM((kv_rows, D_KV), BF16),
               pltpu.VMEM((n, D_RET), F32),
               pltpu.VMEM((n, D_RET), F32),
               pltpu.VMEM((2 * RET_HEADS, RET_DK, RET_DV), F32),
               pltpu.VMEM((2 * RET_HEADS, CHUNK, CHUNK), F32),
               pltpu.VMEM((2 * RET_HEADS, CHUNK, CHUNK), F32),
               pltpu.VMEM((2 * RET_HEADS, CHUNK, CHUNK), F32)]
    return pl.pallas_call(
        functools.partial(_mixer_kernel, n=n, latent=latent),
        grid=(nseq,),
        in_specs=in_specs,
        out_specs=out_specs,
        out_shape=out_shape,
        scratch_shapes=scratch,
        compiler_params=pltpu.CompilerParams(vmem_limit_bytes=48 * MIB),
        name="mixers_latent" if latent else "mixers_context",
    )(*args)


def _merge_kernel(yc_ref, ya_ref, yr_ref, gate_ref, x_ref, mod_ref, g2_ref, wa_ref, wb_ref, wc_ref, wo_ref,
                  wrh_ref, wrl_ref, br_ref, x1_o, xp_o, te_o, tw_o):
    merged = (gate_ref[:, 0:D_MODEL].astype(F32)
              * jnp.dot(yc_ref[...], wa_ref[...], preferred_element_type=F32)
              + gate_ref[:, D_MODEL:2 * D_MODEL].astype(F32)
              * jnp.dot(ya_ref[...], wb_ref[...], preferred_element_type=F32)
              + gate_ref[:, 2 * D_MODEL:3 * D_MODEL].astype(F32)
              * jnp.dot(yr_ref[...], wc_ref[...], preferred_element_type=F32))
    x1 = x_ref[...] + mod_ref[0, 2:3, :] * jnp.dot(merged.astype(BF16), wo_ref[...], preferred_element_type=F32)
    x1_o[...] = x1
    ms = jnp.mean(x1 * x1, axis=-1, keepdims=True)
    h2 = x1 * lax.rsqrt(ms + EPS) * g2_ref[...]
    h2 = h2 * (1.0 + mod_ref[0, 4:5, :]) + mod_ref[0, 3:4, :]
    hh = h2.astype(BF16)
    hf = hh.astype(F32)
    bits = lax.bitcast_convert_type(hf, jnp.uint32)
    for c in range(PACK_ROWS):
        lo = bits[:, c * LANES:(c + 1) * LANES] >> 16
        hi = bits[:, (c + PACK_ROWS) * LANES:(c + PACK_ROWS + 1) * LANES] & jnp.uint32(0xFFFF0000)
        xp_o[pl.ds(c, TM, stride=PACK_ROWS), :] = lax.bitcast_convert_type(lo | hi, jnp.int32)
    hl = (h2 - hf).astype(BF16)
    logits = (jnp.dot(hh, wrh_ref[...], preferred_element_type=F32)
              + jnp.dot(hl, wrh_ref[...], preferred_element_type=F32)
              + jnp.dot(hh, wrl_ref[...], preferred_element_type=F32)
              + br_ref[...])
    lane = lax.broadcasted_iota(jnp.int32, logits.shape, 1)
    work = jnp.where(lane < N_EXPERTS, logits, -jnp.inf)
    vals, idxs = [], []
    for _ in range(TOP_K):
        m = jnp.max(work, axis=-1, keepdims=True)
        am = jnp.min(jnp.where(work == m, lane, LANES), axis=-1, keepdims=True)
        vals.append(m)
        idxs.append(am)
        work = jnp.where(lane == am, -jnp.inf, work)
    es = [jnp.exp(v - vals[0]) for v in vals]
    den = es[0] + es[1] + es[2] + es[3]
    te = jnp.zeros(logits.shape, jnp.int32)
    tw = jnp.zeros(logits.shape, F32)
    for k in range(TOP_K):
        te = jnp.where(lane == k, idxs[k], te)
        tw = jnp.where(lane == k, es[k] / den, tw)
    te_o[...] = te
    tw_o[...] = tw


def _merge(yc, ya, yr, gates, x, mod_l, g2, wa, wb, wc, wo, wrh, wrl, br):
    row = lambda i: (i, 0)
    const = lambda i: (0, 0)
    return pl.pallas_call(
        _merge_kernel,
        grid=(T // TM,),
        in_specs=[pl.BlockSpec((TM, D_CONV), row),
                  pl.BlockSpec((TM, D_ATTN), row),
                  pl.BlockSpec((TM, D_RET), row),
                  pl.BlockSpec((TM, 3 * D_MODEL), row),
                  pl.BlockSpec((TM, D_MODEL), row),
                  pl.BlockSpec((1, 6, D_MODEL), lambda i: (_mod_group(i), 0, 0)),
                  pl.BlockSpec((1, D_MODEL), const),
                  pl.BlockSpec((D_CONV, D_MODEL), const),
                  pl.BlockSpec((D_ATTN, D_MODEL), const),
                  pl.BlockSpec((D_RET, D_MODEL), const),
                  pl.BlockSpec((D_MODEL, D_MODEL), const),
                  pl.BlockSpec((D_MODEL, LANES), const),
                  pl.BlockSpec((D_MODEL, LANES), const),
                  pl.BlockSpec((1, LANES), const)],
        out_specs=[pl.BlockSpec((TM, D_MODEL), row),
                   pl.BlockSpec((TM * PACK_ROWS, LANES), row),
                   pl.BlockSpec((TM, LANES), row),
                   pl.BlockSpec((TM, LANES), row)],
        out_shape=[jax.ShapeDtypeStruct((T, D_MODEL), F32),
                   jax.ShapeDtypeStruct((T * PACK_ROWS, LANES), jnp.int32),
                   jax.ShapeDtypeStruct((T, LANES), jnp.int32),
                   jax.ShapeDtypeStruct((T, LANES), F32)],
        compiler_params=pltpu.CompilerParams(vmem_limit_bytes=48 * MIB),
        name="merge_router",
    )(yc, ya, yr, gates, x, mod_l, g2, wa, wb, wc, wo, wrh, wrl, br)


def _moe_kernel(blk_e_ref, first_ref, next_e_ref, nu_ref, tok_ref, tok_next_ref, xp_ref, wgu_hbm, bgu_ref, wd_hbm,
                bd_ref, y_ref, tile, wgu_st, wd_st, wgu_bf, wd_bf, sems, *, layer):
    b = pl.program_id(0)
    slot = b % 2

    def weight_copies(e):
        return (pltpu.make_async_copy(wgu_hbm.at[layer, e], wgu_st, sems.at[0]),
                pltpu.make_async_copy(wd_hbm.at[layer, e], wd_st, sems.at[1]))

    def gather(tok, dst_slot):
        for mi in range(MOE_M):
            t = jnp.minimum(tok[0, 0, mi], T - 1)
            slab = xp_ref[pl.ds(pl.multiple_of(t * PACK_ROWS, PACK_ROWS), PACK_ROWS), :]
            tile[dst_slot, pl.ds(mi, PACK_ROWS, stride=GATHER_STRIDE), :] = slab

    @pl.when(b == 0)
    def _():
        for cp in weight_copies(blk_e_ref[0]):
            cp.start()
        gather(tok_ref, 0)

    @pl.when(b < nu_ref[0])
    def _():
        @pl.when(first_ref[b] == 1)
        def _():
            for cp in weight_copies(blk_e_ref[b]):
                cp.wait()
            wgu_bf[...] = wgu_st[...].astype(BF16)
            wd_bf[...] = wd_st[...].astype(BF16)

            @pl.when(next_e_ref[b] >= 0)
            def _():
                for cp in weight_copies(next_e_ref[b]):
                    cp.start()

        lo, hi = [], []
        for c in range(PACK_ROWS):
            bits = lax.bitcast_convert_type(
                tile[slot, c * GATHER_STRIDE:c * GATHER_STRIDE + MOE_M, :], jnp.uint32)
            lo.append(lax.bitcast_convert_type(bits << 16, F32).astype(BF16))
            hi.append(lax.bitcast_convert_type(bits & jnp.uint32(0xFFFF0000), F32).astype(BF16))
        x = jnp.concatenate(lo + hi, axis=1)
        gather(tok_next_ref, 1 - slot)

        gu = jnp.dot(x, wgu_bf[...], preferred_element_type=F32) + bgu_ref[0, 0]
        gate = jnp.minimum(gu[:, 0:D_EXPERT], SWIGLU_LIMIT)
        up = jnp.clip(gu[:, D_EXPERT:2 * D_EXPERT], -SWIGLU_LIMIT, SWIGLU_LIMIT)
        glu = gate * _sigmoid(SWIGLU_ALPHA * gate)
        mid = ((up + 1.0) * glu).astype(BF16)
        y = jnp.dot(mid, wd_bf[...], preferred_element_type=F32) + bd_ref[0, 0]
        y_ref[...] = y.astype(BF16)

    @pl.when(b >= nu_ref[0])
    def _():
        y_ref[...] = jnp.zeros_like(y_ref)


def _moe_experts(blk_e, first, next_e, n_used, row_tok3, xp, w_gu, b_gu, w_down, b_down, layer):
    bias = lambda b, e, f, ne, nu: (layer, e[b], 0, 0)
    grid_spec = pltpu.PrefetchScalarGridSpec(
        num_scalar_prefetch=4,
        grid=(N_BLOCKS,),
        in_specs=[pl.BlockSpec((1, 1, MOE_M), lambda b, e, f, ne, nu: (b, 0, 0), memory_space=pltpu.SMEM),
                  pl.BlockSpec((1, 1, MOE_M), lambda b, e, f, ne, nu: (jnp.minimum(b + 1, N_BLOCKS - 1), 0, 0),
                               memory_space=pltpu.SMEM),
                  pl.BlockSpec((T * PACK_ROWS, LANES), lambda b, e, f, ne, nu: (0, 0), pipeline_mode=pl.Buffered(1)),
                  pl.BlockSpec(memory_space=pl.ANY),
                  pl.BlockSpec((1, 1, 1, 2 * D_EXPERT), bias),
                  pl.BlockSpec(memory_space=pl.ANY),
                  pl.BlockSpec((1, 1, 1, D_MODEL), bias)],
        out_specs=pl.BlockSpec((MOE_M, D_MODEL), lambda b, e, f, ne, nu: (b, 0)),
        scratch_shapes=[pltpu.VMEM((2, PACK_ROWS * GATHER_STRIDE, LANES), jnp.int32),
                        pltpu.VMEM((D_MODEL, 2 * D_EXPERT), F32),
                        pltpu.VMEM((D_EXPERT, D_MODEL), F32),
                        pltpu.VMEM((D_MODEL, 2 * D_EXPERT), BF16),
                        pltpu.VMEM((D_EXPERT, D_MODEL), BF16),
                        pltpu.SemaphoreType.DMA((2,))],
    )
    return pl.pallas_call(
        functools.partial(_moe_kernel, layer=layer),
        grid_spec=grid_spec,
        out_shape=jax.ShapeDtypeStruct((N_ROWS, D_MODEL), BF16),
        compiler_params=pltpu.CompilerParams(vmem_limit_bytes=48 * MIB),
        name="moe_experts",
    )(blk_e, first, next_e, n_used, row_tok3, row_tok3, xp, w_gu, b_gu.reshape(DEPTH, N_EXPERTS, 1, 2 * D_EXPERT),
      w_down, b_down.reshape(DEPTH, N_EXPERTS, 1, D_MODEL))


SCATTER_UNROLL = 8


def _combine_kernel(nu_ref, tok_ref, w_ref, y_ref, o_ref, acc, tile, sem):
    s = pl.program_id(0)

    @pl.when(s == 0)
    def _():
        acc[...] = jnp.zeros_like(acc)

    def sub_block(sb, carry):
        r0 = pl.multiple_of(sb * SCATTER_M, SCATTER_M)
        y = y_ref[pl.ds(r0, SCATTER_M), :].astype(F32)
        for c in range(ROW_VREGS):
            tile[c * SCATTER_STRIDE:c * SCATTER_STRIDE + SCATTER_M, :] = y[:, c * LANES:(c + 1) * LANES]
        for m0 in range(0, SCATTER_M, SCATTER_UNROLL):
            addrs, vals = [], []
            for u in range(SCATTER_UNROLL):
                mi = m0 + u
                a = pl.multiple_of(tok_ref[0, 0, r0 + mi] * ROW_VREGS, ROW_VREGS)
                yv = tile[pl.ds(mi, ROW_VREGS, stride=SCATTER_STRIDE), :]
                addrs.append(a)
                vals.append(acc[pl.ds(a, ROW_VREGS), :] + w_ref[0, 0, r0 + mi] * yv)
            for u in range(SCATTER_UNROLL):
                acc[pl.ds(addrs[u], ROW_VREGS), :] = vals[u]
        return carry

    @pl.when(s * COMBINE_BLOCKS < nu_ref[0])
    def _():
        lax.fori_loop(0, COMBINE_BLOCKS * MOE_M // SCATTER_M, sub_block, 0)

    @pl.when(s == pl.num_programs(0) - 1)
    def _():
        cp = pltpu.make_async_copy(acc.at[pl.ds(0, T * ROW_VREGS)], o_ref, sem)
        cp.start()
        cp.wait()


def _combine(n_used, row_tok3, row_w3, yr):
    rows = COMBINE_BLOCKS * MOE_M
    steps = N_BLOCKS // COMBINE_BLOCKS
    grid_spec = pltpu.PrefetchScalarGridSpec(
        num_scalar_prefetch=1,
        grid=(steps,),
        in_specs=[pl.BlockSpec((1, 1, rows), lambda s, nu: (s, 0, 0), memory_space=pltpu.SMEM),
                  pl.BlockSpec((1, 1, rows), lambda s, nu: (s, 0, 0), memory_space=pltpu.SMEM),
                  pl.BlockSpec((rows, D_MODEL), lambda s, nu: (s, 0))],
        out_specs=pl.BlockSpec(memory_space=pl.ANY),
        scratch_shapes=[pltpu.VMEM(((T + 1) * ROW_VREGS, LANES), F32),
                        pltpu.VMEM((ROW_VREGS * SCATTER_STRIDE, LANES), F32),
                        pltpu.SemaphoreType.DMA(())],
    )
    return pl.pallas_call(
        _combine_kernel,
        grid_spec=grid_spec,
        out_shape=jax.ShapeDtypeStruct((T * ROW_VREGS, LANES), F32),
        compiler_params=pltpu.CompilerParams(vmem_limit_bytes=48 * MIB),
        name="moe_combine",
    )(n_used, row_tok3.reshape(steps, 1, rows), row_w3.reshape(steps, 1, rows), yr)


def _route(top_e, top_w):
    flat_e = top_e.reshape(N_ASSIGN)
    idx = jnp.arange(N_ASSIGN, dtype=jnp.int32)
    _, sorted_idx, sorted_w = lax.sort((flat_e, idx, top_w.reshape(N_ASSIGN)), num_keys=1, is_stable=True)
    experts = jnp.arange(N_EXPERTS, dtype=jnp.int32)
    counts = jnp.sum((flat_e[:, None] == experts[None, :]).astype(jnp.int32), axis=0)
    start = jnp.cumsum(counts) - counts
    padded = (counts + MOE_M - 1) // MOE_M * MOE_M
    pad_end = jnp.cumsum(padded)
    pad_start = pad_end - padded
    n_used = (pad_end[-1] // MOE_M).astype(jnp.int32)
    blk0 = jnp.arange(N_BLOCKS, dtype=jnp.int32) * MOE_M
    blk_e = jnp.minimum(jnp.sum((blk0[:, None] >= pad_end[None, :]).astype(jnp.int32), axis=1), N_EXPERTS - 1)
    q = (blk0 - pad_start[blk_e])[:, None] + jnp.arange(MOE_M, dtype=jnp.int32)[None, :]
    valid = q < counts[blk_e][:, None]
    pos = jnp.clip(start[blk_e][:, None] + q, 0, N_ASSIGN - 1)
    row_tok = jnp.where(valid, sorted_idx[pos] // TOP_K, T).astype(jnp.int32)
    row_w = jnp.where(valid, sorted_w[pos], 0.0)
    first = (blk0 == pad_start[blk_e]).astype(jnp.int32)
    later = (experts[None, :] > experts[:, None]) & (counts[None, :] > 0)
    nxt = jnp.min(jnp.where(later, experts[None, :], N_EXPERTS), axis=1)
    next_e = jnp.where(nxt == N_EXPERTS, -1, nxt)[blk_e].astype(jnp.int32)
    return (row_tok.reshape(N_BLOCKS, 1, MOE_M), row_w.reshape(N_BLOCKS, 1, MOE_M), blk_e.astype(jnp.int32),
            first, next_e, n_used.reshape(1))


def _residual_kernel(x_ref, moe_ref, mod_ref, g_ref, o_ref, *, final):
    moe = jnp.concatenate([moe_ref[pl.ds(c, TM, stride=ROW_VREGS), :] for c in range(ROW_VREGS)], axis=1)
    x = x_ref[...] + mod_ref[0, 5:6, :] * moe
    if final:
        ms = jnp.mean(x * x, axis=-1, keepdims=True)
        x = x * lax.rsqrt(ms + EPS) * g_ref[...]
    o_ref[...] = x


def _residual(x1, moe_tiles, mod_l, g, final):
    row = lambda i: (i, 0)
    return pl.pallas_call(
        functools.partial(_residual_kernel, final=final),
        grid=(T // TM,),
        in_specs=[pl.BlockSpec((TM, D_MODEL), row),
                  pl.BlockSpec((TM * ROW_VREGS, LANES), row),
                  pl.BlockSpec((1, 6, D_MODEL), lambda i: (_mod_group(i), 0, 0)),
                  pl.BlockSpec((1, D_MODEL), lambda i: (0, 0))],
        out_specs=pl.BlockSpec((TM, D_MODEL), row),
        out_shape=jax.ShapeDtypeStruct((T, D_MODEL), F32),
        compiler_params=pltpu.CompilerParams(vmem_limit_bytes=32 * MIB),
        name="residual_final" if final else "residual",
    )(x1, moe_tiles, mod_l, g)


def _rope_tables():
    t = np.arange(DEC_SEQ)
    pos = np.stack([t // GRID_W, t % GRID_W], axis=1).astype(np.float32)
    half = HEAD_DIM // 2
    inv = jnp.asarray(ROPE_BASE, F32) ** (-jnp.arange(0, half, 2, dtype=F32) / half)
    d = np.arange(HEAD_DIM)
    which = d // half
    freq = d % (half // 2)
    sign = np.where((d % half) < half // 2, -1.0, 1.0).astype(np.float32)
    ang = jnp.asarray(pos)[:, which] * inv[freq][None, :]
    cos = jnp.cos(ang)
    sin = jnp.sin(ang) * jnp.asarray(sign)[None, :]
    reps = LANES // HEAD_DIM
    return jnp.tile(cos, (1, reps)), jnp.tile(sin, (1, reps))


def kernel(x_prompt, x_sample, cache_k, cache_v, state_ret, c, c_ctx, norm1_g, norm2_g, w_mod, b_mod, w_in, conv_w, attn_sink, ret_decay, w_a, w_b, w_c, w_o, w_router, b_router, w_gu, b_gu, w_down, b_down, final_g):
    x = jnp.concatenate([x_prompt.reshape(T_CTX, D_MODEL), x_sample.reshape(T_LAT, D_MODEL)], axis=0)
    cond8 = jnp.zeros((8, D_MODEL), F32).at[0].set(c_ctx).at[1:1 + DEC_BATCH].set(c)
    mod = _modulation(cond8, w_mod, b_mod)
    cos, sin = _rope_tables()
    ck = cache_k.reshape(DEC_BATCH, DEPTH, PAST_LEN, D_KV)
    cv = cache_v.reshape(DEC_BATCH, DEPTH, PAST_LEN, D_KV)

    ks, vs, rs = [], [], []
    for l in range(DEPTH):
        mod_l = mod[l, 0:1 + DEC_BATCH].reshape(1 + DEC_BATCH, 6, D_MODEL)
        conv_in, q, k, v, ret, gates = _inproj(x, mod_l, norm1_g[l][None, :], w_in[l].astype(BF16))
        rd8 = jnp.broadcast_to(ret_decay[l].reshape(2 * RET_HEADS, 1), (2 * RET_HEADS, LANES))
        yc_c, ya_c, yr_c, rfin = _mixers(conv_in, q, k, v, ret, conv_w[l], attn_sink[l], rd8, latent=False)
        yc_s, ya_s, yr_s = _mixers(conv_in, q, k, v, ret, conv_w[l], attn_sink[l], rd8, latent=True,
                                   cos=cos, sin=sin, cache_k=ck, cache_v=cv, state=state_ret, layer=l)
        yc = jnp.concatenate([yc_c, yc_s], axis=0)
        ya = jnp.concatenate([ya_c, ya_s], axis=0)
        yr = jnp.concatenate([yr_c, yr_s], axis=0)
        wr = jnp.pad(w_router[l], ((0, 0), (0, LANES - N_EXPERTS)))
        wrh = wr.astype(BF16)
        wrl = (wr - wrh.astype(F32)).astype(BF16)
        br = jnp.pad(b_router[l], (0, LANES - N_EXPERTS))[None, :]
        x1, xp, top_e, top_w = _merge(yc, ya, yr, gates, x, mod_l, norm2_g[l][None, :], w_a[l].astype(BF16),
                                      w_b[l].astype(BF16), w_c[l].astype(BF16), w_o[l].astype(BF16), wrh, wrl, br)
        row_tok, row_w, blk_e, first, next_e, n_used = _route(top_e[:, 0:TOP_K], top_w[:, 0:TOP_K])
        yrows = _moe_experts(blk_e, first, next_e, n_used, row_tok, xp, w_gu, b_gu, w_down, b_down, l)
        moe = _combine(n_used, row_tok, row_w, yrows)
        final = l == DEPTH - 1
        x = _residual(x1, moe, mod_l, final_g[None, :], final)
        ks.append(k[0:T_CTX].reshape(BATCH, SEQ, N_KV, HEAD_DIM))
        vs.append(v[0:T_CTX].reshape(BATCH, SEQ, N_KV, HEAD_DIM))
        rs.append(rfin)

    y_prompt = x[0:T_CTX].reshape(BATCH, SEQ, D_MODEL)
    y_sample = x[T_CTX:T].reshape(DEC_BATCH, DEC_SEQ, D_MODEL)
    return (y_prompt, y_sample, jnp.stack(ks, axis=1), jnp.stack(vs, axis=1), jnp.stack(rs, axis=1))
```

```python
import functools

import numpy as np
import jax
import jax.numpy as jnp
from jax import lax
from jax.experimental import pallas as pl
from jax.experimental.pallas import tpu as pltpu

F32 = jnp.float32
BF16 = jnp.bfloat16

D_MODEL = 1024
BATCH = 16
SEQ = 256
DEPTH = 2
DEC_BATCH = 2
DEC_SEQ = 2048
PAST_LEN = 256
GRID_W = 64
HEAD_DIM = 64
D_CONV = 256
N_HEADS = 8
N_KV = 2
GROUP = N_HEADS // N_KV
WINDOW = 128
ROPE_BASE = 10000.0
RET_HEADS = 4
RET_DK = 64
RET_DV = 64
CHUNK = 128
N_EXPERTS = 32
TOP_K = 4
D_EXPERT = D_MODEL
SWIGLU_LIMIT = 7.0
SWIGLU_ALPHA = 1.702
EPS = 1e-6
NEG_INF = -1e30

T_CTX = BATCH * SEQ
T_LAT = DEC_BATCH * DEC_SEQ
T = T_CTX + T_LAT
D_ATTN = N_HEADS * HEAD_DIM
D_KV = N_KV * HEAD_DIM
D_RET = RET_HEADS * RET_DK
C_CONV = 0
C_Q = 3 * D_CONV
C_K = C_Q + D_ATTN
C_V = C_K + D_KV
C_RET = C_V + D_KV
C_GATE = C_RET + 4 * D_RET
IN_COLS = C_GATE + 3 * D_MODEL

TM = 512
MOE_M = 256
N_ASSIGN = T * TOP_K
N_BLOCKS = (N_ASSIGN + N_EXPERTS * (MOE_M - 1) + MOE_M - 1) // MOE_M
N_ROWS = N_BLOCKS * MOE_M
LANES = 128
SUBLANES = 8
ROW_VREGS = D_MODEL // LANES
PACK_ROWS = ROW_VREGS // 2
GATHER_STRIDE = MOE_M + SUBLANES
SCATTER_M = 128
SCATTER_STRIDE = SCATTER_M + SUBLANES
COMBINE_BLOCKS = 2
MIB = 1024 * 1024


def _sigmoid(x):
    return 1.0 / (1.0 + jnp.exp(-x))


def _mod_group(i):
    n_ctx = T_CTX // TM
    per_lat = DEC_SEQ // TM
    g = jnp.zeros_like(i)
    for b in range(DEC_BATCH):
        g = g + (i >= n_ctx + b * per_lat).astype(jnp.int32)
    return g


def _mod_kernel(cond_ref, w_ref, b_ref, o_ref):
    c = cond_ref[...]
    s = c * _sigmoid(c)
    o_ref[0] = jnp.dot(s.astype(BF16), w_ref[0].astype(BF16), preferred_element_type=F32) + b_ref[0]


def _modulation(cond8, w_mod, b_mod):
    n_col = 4
    cw = 6 * D_MODEL // n_col
    return pl.pallas_call(
        _mod_kernel,
        grid=(DEPTH, n_col),
        in_specs=[pl.BlockSpec((8, D_MODEL), lambda l, j: (0, 0)),
                  pl.BlockSpec((1, D_MODEL, cw), lambda l, j: (l, 0, j)),
                  pl.BlockSpec((1, 1, cw), lambda l, j: (l, 0, j))],
        out_specs=pl.BlockSpec((1, 8, cw), lambda l, j: (l, 0, j)),
        out_shape=jax.ShapeDtypeStruct((DEPTH, 8, 6 * D_MODEL), F32),
        compiler_params=pltpu.CompilerParams(vmem_limit_bytes=32 * MIB),
        name="modulation",
    )(cond8, w_mod, b_mod.reshape(DEPTH, 1, 6 * D_MODEL))


def _inproj_kernel(x_ref, mod_ref, g_ref, w_ref, conv_o, q_o, k_o, v_o, ret_o, gate_o):
    x = x_ref[...]
    ms = jnp.mean(x * x, axis=-1, keepdims=True)
    h = x * lax.rsqrt(ms + EPS) * g_ref[...]
    h = h * (1.0 + mod_ref[0, 1:2, :]) + mod_ref[0, 0:1, :]
    hb = h.astype(BF16)

    def proj(c0, c1):
        return jnp.dot(hb, w_ref[0, :, c0:c1].astype(BF16), preferred_element_type=F32)

    conv_o[...] = proj(C_CONV, C_Q).astype(BF16)
    q_o[...] = (proj(C_Q, C_K) * HEAD_DIM ** -0.5).astype(BF16)
    k_o[...] = proj(C_K, C_V)
    v_o[...] = proj(C_V, C_RET)
    ret_o[:, 0:D_RET] = proj(C_RET, C_RET + D_RET).astype(BF16)
    ret_o[:, D_RET:2 * D_RET] = (proj(C_RET + D_RET, C_RET + 2 * D_RET) * RET_DK ** -0.5).astype(BF16)
    ret_o[:, 2 * D_RET:4 * D_RET] = proj(C_RET + 2 * D_RET, C_GATE).astype(BF16)
    for b in range(3):
        g = proj(C_GATE + b * D_MODEL, C_GATE + (b + 1) * D_MODEL)
        gate_o[:, b * D_MODEL:(b + 1) * D_MODEL] = _sigmoid(g).astype(BF16)


def _inproj(x, mod_l, g1, w_in, layer):
    row = lambda i: (i, 0)
    return pl.pallas_call(
        _inproj_kernel,
        grid=(T // TM,),
        in_specs=[pl.BlockSpec((TM, D_MODEL), row),
                  pl.BlockSpec((1, 6, D_MODEL), lambda i: (_mod_group(i), 0, 0)),
                  pl.BlockSpec((1, D_MODEL), lambda i: (0, 0)),
                  pl.BlockSpec((1, D_MODEL, IN_COLS), lambda i: (layer, 0, 0), pipeline_mode=pl.Buffered(1))],
        out_specs=[pl.BlockSpec((TM, 3 * D_CONV), row),
                   pl.BlockSpec((TM, D_ATTN), row),
                   pl.BlockSpec((TM, D_KV), row),
                   pl.BlockSpec((TM, D_KV), row),
                   pl.BlockSpec((TM, 4 * D_RET), row),
                   pl.BlockSpec((TM, 3 * D_MODEL), row)],
        out_shape=[jax.ShapeDtypeStruct((T, 3 * D_CONV), BF16),
                   jax.ShapeDtypeStruct((T, D_ATTN), BF16),
                   jax.ShapeDtypeStruct((T, D_KV), F32),
                   jax.ShapeDtypeStruct((T, D_KV), F32),
                   jax.ShapeDtypeStruct((T, 4 * D_RET), BF16),
                   jax.ShapeDtypeStruct((T, 3 * D_MODEL), BF16)],
        compiler_params=pltpu.CompilerParams(vmem_limit_bytes=56 * MIB),
        name="inproj",
    )(x, mod_l, g1, w_in)


def _rope(x, cos, sin_signed):
    lane = lax.broadcasted_iota(jnp.int32, x.shape, 1)
    first = (lane % 32) < 16
    partner = jnp.where(first, pltpu.roll(x, x.shape[1] - 16, 1), pltpu.roll(x, 16, 1))
    return x * cos + partner * sin_signed


def _dot_nt(a, b):
    return lax.dot_general(a, b, (((1,), (1,)), ((), ())), preferred_element_type=F32)


def _dot_tn(a, b):
    return lax.dot_general(a, b, (((0,), (0,)), ((), ())), preferred_element_type=F32)


def _mixer_kernel(*refs, n, latent):
    if latent:
        (sink_ref, conv_ref, q_ref, k_ref, v_ref, ret_ref, cw_ref, rd_ref, cos_ref, sin_ref, ck_ref, cv_ref,
         r0_ref, yc_ref, ya_ref, yr_ref, kb, vb, o_f, o_b, rst, dmat, qdec, kdec) = refs
    else:
        (sink_ref, conv_ref, q_ref, k_ref, v_ref, ret_ref, cw_ref, rd_ref,
         yc_ref, ya_ref, yr_ref, rfin_ref, kb, vb, o_f, o_b, rst, dmat, qdec, kdec) = refs
    nb = n // CHUNK

    cv = conv_ref[...].astype(F32)
    cb, cc, cu = cv[:, 0:D_CONV], cv[:, D_CONV:2 * D_CONV], cv[:, 2 * D_CONV:3 * D_CONV]
    p = cc * cu
    row = lax.broadcasted_iota(jnp.int32, p.shape, 0)
    prev = jnp.where(row == 0, 0.0, pltpu.roll(p, 1, 0))
    nxt = jnp.where(row == n - 1, 0.0, pltpu.roll(p, n - 1, 0))
    cw = cw_ref[...]
    yc_ref[...] = (cb * (prev * cw[0:1, :] + p * cw[1:2, :] + nxt * cw[2:3, :])).astype(BF16)

    if latent:
        kr = _rope(k_ref[...], cos_ref[...], sin_ref[...])
        zpad = jnp.zeros((CHUNK, D_KV), BF16)
        kb[0:CHUNK, :] = zpad
        vb[0:CHUNK, :] = zpad
        kb[CHUNK + n:2 * CHUNK + n, :] = zpad
        vb[CHUNK + n:2 * CHUNK + n, :] = zpad
        kb[CHUNK:CHUNK + n, :] = kr.astype(BF16)
        vb[CHUNK:CHUNK + n, :] = v_ref[...].astype(BF16)
        ckb = ck_ref[0, 0].astype(BF16)
        cvb = cv_ref[0, 0].astype(BF16)
    else:
        kb[...] = k_ref[...].astype(BF16)
        vb[...] = v_ref[...].astype(BF16)

    def attn_block(j):
        r0 = j * CHUNK if isinstance(j, int) else pl.multiple_of(j * CHUNK, CHUNK)
        qj = q_ref[pl.ds(r0, CHUNK), :]
        if latent:
            cosj = cos_ref[pl.ds(r0, CHUNK), :]
            sinj = sin_ref[pl.ds(r0, CHUNK), :]
            qf = qj.astype(F32)
            qj = jnp.concatenate(
                [_rope(qf[:, c * LANES:(c + 1) * LANES], cosj, sinj) for c in range(D_ATTN // LANES)],
                axis=1).astype(BF16)
            kw = kb[pl.ds(r0, 3 * CHUNK), :]
            vw = vb[pl.ds(r0, 3 * CHUNK), :]
            qpos = r0 + lax.broadcasted_iota(jnp.int32, (CHUNK, 3 * CHUNK), 0)
            kpos = r0 - CHUNK + lax.broadcasted_iota(jnp.int32, (CHUNK, 3 * CHUNK), 1)
            ok = (jnp.abs(qpos - kpos) <= WINDOW) & (kpos >= 0) & (kpos < n)
            ok = jnp.concatenate([ok] * GROUP, axis=0)
        else:
            kw = kb[...]
            vw = vb[...]
        for g in range(N_KV):
            lo = g * HEAD_DIM
            qg = jnp.concatenate(
                [qj[:, (GROUP * g + i) * HEAD_DIM:(GROUP * g + i + 1) * HEAD_DIM] for i in range(GROUP)], axis=0)
            sk = jnp.concatenate(
                [jnp.full((CHUNK, 1), sink_ref[GROUP * g + i], F32) for i in range(GROUP)], axis=0)
            s = _dot_nt(qg, kw[:, lo:lo + HEAD_DIM])
            if latent:
                s = jnp.where(ok, s, NEG_INF)
                s2 = _dot_nt(qg, ckb[:, lo:lo + HEAD_DIM])
            m = jnp.maximum(jnp.max(s, axis=-1, keepdims=True), sk)
            if latent:
                m = jnp.maximum(m, jnp.max(s2, axis=-1, keepdims=True))
            pw = jnp.exp(s - m)
            den = jnp.sum(pw, axis=-1, keepdims=True) + jnp.exp(sk - m)
            o = jnp.dot(pw.astype(BF16), vw[:, lo:lo + HEAD_DIM], preferred_element_type=F32)
            if latent:
                p2 = jnp.exp(s2 - m)
                den = den + jnp.sum(p2, axis=-1, keepdims=True)
                o = o + jnp.dot(p2.astype(BF16), cvb[:, lo:lo + HEAD_DIM], preferred_element_type=F32)
            o = o / den
            ya_ref[pl.ds(r0, CHUNK), g * GROUP * HEAD_DIM:(g + 1) * GROUP * HEAD_DIM] = jnp.concatenate(
                [o[i * CHUNK:(i + 1) * CHUNK, :] for i in range(GROUP)], axis=1).astype(BF16)

    if latent:
        def attn_body(j, carry):
            attn_block(j)
            return carry
        lax.fori_loop(0, nb, attn_body, 0)
    else:
        for j in range(nb):
            attn_block(j)

    rd = rd_ref[...]
    log_g = jnp.minimum(rd, 0.0) - jnp.log(1.0 + jnp.exp(-jnp.abs(rd)))
    ii = lax.broadcasted_iota(jnp.int32, (CHUNK, CHUNK), 0).astype(F32)
    jj = lax.broadcasted_iota(jnp.int32, (CHUNK, CHUNK), 1).astype(F32)
    for d in range(2):
        for h in range(RET_HEADS):
            r = d * RET_HEADS + h
            lg = log_g[r:r + 1, :]
            diff = (ii - jj) if d == 0 else (jj - ii)
            dmat[r] = jnp.where(diff >= 0, jnp.exp(jnp.maximum(diff, 0.0) * lg), 0.0)
            if d == 0:
                qdec[r] = jnp.exp((ii + 1.0) * lg)
                kdec[r] = jnp.exp((CHUNK - 1.0 - ii) * lg)
            else:
                qdec[r] = jnp.exp((CHUNK - ii) * lg)
                kdec[r] = jnp.exp(ii * lg)
            if latent:
                rst[r] = r0_ref[0, 0, d, h]
            else:
                rst[r] = jnp.zeros((RET_DK, RET_DV), F32)
    chunk_decay = jnp.exp(float(CHUNK) * log_g)

    def ret_chunk(c0, d, out_ref):
        blk = ret_ref[pl.ds(c0, CHUNK), :]
        outs = []
        for h in range(RET_HEADS):
            r = d * RET_HEADS + h
            qc = blk[:, h * RET_DK:(h + 1) * RET_DK]
            kc = blk[:, D_RET + h * RET_DK:D_RET + (h + 1) * RET_DK]
            vc = blk[:, 2 * D_RET + h * RET_DV:2 * D_RET + (h + 1) * RET_DV]
            inner = (_dot_nt(qc, kc) * dmat[r]).astype(BF16)
            state = rst[r]
            o = jnp.dot(inner, vc, preferred_element_type=F32)
            o = o + jnp.dot(qc, state.astype(BF16), preferred_element_type=F32) * qdec[r][:, 0:RET_DV]
            kd = (kc.astype(F32) * kdec[r][:, 0:RET_DK]).astype(BF16)
            rst[r] = state * chunk_decay[r:r + 1, 0:RET_DV] + _dot_tn(kd, vc)
            outs.append(o)
        out_ref[pl.ds(c0, CHUNK), :] = jnp.concatenate(outs, axis=1)

    def ret_body(j, carry):
        ret_chunk(pl.multiple_of(j * CHUNK, CHUNK), 0, o_f)
        ret_chunk(pl.multiple_of((nb - 1 - j) * CHUNK, CHUNK), 1, o_b)
        return carry

    lax.fori_loop(0, nb, ret_body, 0)

    def norm_body(j, carry):
        c0 = pl.multiple_of(j * CHUNK, CHUNK)
        o = o_f[pl.ds(c0, CHUNK), :] + o_b[pl.ds(c0, CHUNK), :]
        parts = []
        for h in range(RET_HEADS):
            oh = o[:, h * RET_DV:(h + 1) * RET_DV]
            ms = jnp.mean(oh * oh, axis=-1, keepdims=True)
            parts.append(oh * lax.rsqrt(ms + EPS))
        rg = ret_ref[pl.ds(c0, CHUNK), 3 * D_RET:4 * D_RET].astype(F32)
        yr_ref[pl.ds(c0, CHUNK), :] = (rg * _sigmoid(rg) * jnp.concatenate(parts, axis=1)).astype(BF16)
        return carry

    lax.fori_loop(0, nb, norm_body, 0)

    if not latent:
        for d in range(2):
            for h in range(RET_HEADS):
                rfin_ref[0, d, h] = rst[d * RET_HEADS + h]


def _mixers(conv_in, q, k, v, ret, conv_w_l, sink_l, rd8, *, latent, cos=None, sin=None, cache_k=None,
            cache_v=None, state=None, layer=0):
    n = DEC_SEQ if latent else SEQ
    nseq = DEC_BATCH if latent else BATCH
    off = T_CTX // n if latent else 0
    seq = lambda s: (s + off, 0)
    const = lambda s: (0, 0)
    in_specs = [pl.BlockSpec(memory_space=pltpu.SMEM),
                pl.BlockSpec((n, 3 * D_CONV), seq),
                pl.BlockSpec((n, D_ATTN), seq),
                pl.BlockSpec((n, D_KV), seq),
                pl.BlockSpec((n, D_KV), seq),
                pl.BlockSpec((n, 4 * D_RET), seq),
                pl.BlockSpec((3, D_CONV), const),
                pl.BlockSpec((8, LANES), const)]
    args = [sink_l, conv_in, q, k, v, ret, conv_w_l, rd8]
    out_specs = [pl.BlockSpec((n, D_CONV), lambda s: (s, 0)),
                 pl.BlockSpec((n, D_ATTN), lambda s: (s, 0)),
                 pl.BlockSpec((n, D_RET), lambda s: (s, 0))]
    out_shape = [jax.ShapeDtypeStruct((nseq * n, D_CONV), BF16),
                 jax.ShapeDtypeStruct((nseq * n, D_ATTN), BF16),
                 jax.ShapeDtypeStruct((nseq * n, D_RET), BF16)]
    if latent:
        in_specs += [pl.BlockSpec((n, LANES), const),
                     pl.BlockSpec((n, LANES), const),
                     pl.BlockSpec((1, 1, PAST_LEN, D_KV), lambda s: (s, layer, 0, 0)),
                     pl.BlockSpec((1, 1, PAST_LEN, D_KV), lambda s: (s, layer, 0, 0)),
                     pl.BlockSpec((1, 1, 2, RET_HEADS, RET_DK, RET_DV), lambda s: (s, layer, 0, 0, 0, 0))]
        args += [cos, sin, cache_k, cache_v, state]
        kv_rows = n + 2 * CHUNK
    else:
        out_specs.append(pl.BlockSpec((1, 2, RET_HEADS, RET_DK, RET_DV), lambda s: (s, 0, 0, 0, 0)))
        out_shape.append(jax.ShapeDtypeStruct((nseq, 2, RET_HEADS, RET_DK, RET_DV), F32))
        kv_rows = n
    scratch = [pltpu.VMEM((kv_rows, D_KV), BF16),
               pltpu.VMEM((kv_rows, D_KV), BF16),
               pltpu.VMEM((n, D_RET), F32),
               pltpu.VMEM((n, D_RET), F32),
               pltpu.VMEM((2 * RET_HEADS, RET_DK, RET_DV), F32),
               pltpu.VMEM((2 * RET_HEADS, CHUNK, CHUNK), F32),
               pltpu.VMEM((2 * RET_HEADS, CHUNK, CHUNK), F32),
               pltpu.VMEM((2 * RET_HEADS, CHUNK, CHUNK), F32)]
    return pl.pallas_call(
        functools.partial(_mixer_kernel, n=n, latent=latent),
        grid=(nseq,),
        in_specs=in_specs,
        out_specs=out_specs,
        out_shape=out_shape,
        scratch_shapes=scratch,
        compiler_params=pltpu.CompilerParams(vmem_limit_bytes=48 * MIB),
        name="mixers_latent" if latent else "mixers_context",
    )(*args)


def _merge_kernel(yc_ref, ya_ref, yr_ref, gate_ref, x_ref, mod_ref, g2_ref, wa_ref, wb_ref, wc_ref, wo_ref,
                  wrh_ref, wrl_ref, br_ref, x1_o, xp_o, te_o, tw_o, cnt_o):
    merged = (gate_ref[:, 0:D_MODEL].astype(F32)
              * jnp.dot(yc_ref[...], wa_ref[0].astype(BF16), preferred_element_type=F32)
              + gate_ref[:, D_MODEL:2 * D_MODEL].astype(F32)
              * jnp.dot(ya_ref[...], wb_ref[0].astype(BF16), preferred_element_type=F32)
              + gate_ref[:, 2 * D_MODEL:3 * D_MODEL].astype(F32)
              * jnp.dot(yr_ref[...], wc_ref[0].astype(BF16), preferred_element_type=F32))
    x1 = x_ref[...] + mod_ref[0, 2:3, :] * jnp.dot(merged.astype(BF16), wo_ref[0].astype(BF16),
                                                   preferred_element_type=F32)
    x1_o[...] = x1
    ms = jnp.mean(x1 * x1, axis=-1, keepdims=True)
    h2 = x1 * lax.rsqrt(ms + EPS) * g2_ref[...]
    h2 = h2 * (1.0 + mod_ref[0, 4:5, :]) + mod_ref[0, 3:4, :]
    hh = h2.astype(BF16)
    hf = hh.astype(F32)
    bits = lax.bitcast_convert_type(hf, jnp.uint32)
    for c in range(PACK_ROWS):
        lo = bits[:, c * LANES:(c + 1) * LANES] >> 16
        hi = bits[:, (c + PACK_ROWS) * LANES:(c + PACK_ROWS + 1) * LANES] & jnp.uint32(0xFFFF0000)
        xp_o[pl.ds(c, TM, stride=PACK_ROWS), :] = lax.bitcast_convert_type(lo | hi, jnp.int32)
    hl = (h2 - hf).astype(BF16)
    logits = (jnp.dot(hh, wrh_ref[...], preferred_element_type=F32)
              + jnp.dot(hl, wrh_ref[...], preferred_element_type=F32)
              + jnp.dot(hh, wrl_ref[...], preferred_element_type=F32)
              + br_ref[...])
    lane = lax.broadcasted_iota(jnp.int32, logits.shape, 1)
    work = jnp.where(lane < N_EXPERTS, logits, -jnp.inf)
    vals, idxs = [], []
    for _ in range(TOP_K):
        m = jnp.max(work, axis=-1, keepdims=True)
        am = jnp.min(jnp.where(work == m, lane, LANES), axis=-1, keepdims=True)
        vals.append(m)
        idxs.append(am)
        work = jnp.where(lane == am, -jnp.inf, work)
    es = [jnp.exp(v - vals[0]) for v in vals]
    den = es[0] + es[1] + es[2] + es[3]
    te = jnp.zeros(logits.shape, jnp.int32)
    tw = jnp.zeros(logits.shape, F32)
    for k in range(TOP_K):
        te = jnp.where(lane == k, idxs[k], te)
        tw = jnp.where(lane == k, es[k] / den, tw)
    te_o[...] = te
    tw_o[...] = tw
    sel = (lane == idxs[0]) | (lane == idxs[1]) | (lane == idxs[2]) | (lane == idxs[3])
    part = jnp.sum(sel.astype(jnp.int32), axis=0, keepdims=True)

    @pl.when(pl.program_id(0) == 0)
    def _():
        cnt_o[...] = jnp.zeros_like(cnt_o)

    cnt_o[...] += jnp.broadcast_to(part, cnt_o.shape)


def _merge(yc, ya, yr, gates, x, mod_l, g2, wa, wb, wc, wo, wrh, wrl, br, layer):
    row = lambda i: (i, 0)
    const = lambda i: (0, 0)
    wl = lambda i: (layer, 0, 0)
    return pl.pallas_call(
        _merge_kernel,
        grid=(T // TM,),
        in_specs=[pl.BlockSpec((TM, D_CONV), row),
                  pl.BlockSpec((TM, D_ATTN), row),
                  pl.BlockSpec((TM, D_RET), row),
                  pl.BlockSpec((TM, 3 * D_MODEL), row),
                  pl.BlockSpec((TM, D_MODEL), row),
                  pl.BlockSpec((1, 6, D_MODEL), lambda i: (_mod_group(i), 0, 0)),
                  pl.BlockSpec((1, D_MODEL), const),
                  pl.BlockSpec((1, D_CONV, D_MODEL), wl),
                  pl.BlockSpec((1, D_ATTN, D_MODEL), wl),
                  pl.BlockSpec((1, D_RET, D_MODEL), wl),
                  pl.BlockSpec((1, D_MODEL, D_MODEL), wl),
                  pl.BlockSpec((D_MODEL, LANES), const),
                  pl.BlockSpec((D_MODEL, LANES), const),
                  pl.BlockSpec((1, LANES), const)],
        out_specs=[pl.BlockSpec((TM, D_MODEL), row),
                   pl.BlockSpec((TM * PACK_ROWS, LANES), row),
                   pl.BlockSpec((TM, LANES), row),
                   pl.BlockSpec((TM, LANES), row),
                   pl.BlockSpec((SUBLANES, LANES), const)],
        out_shape=[jax.ShapeDtypeStruct((T, D_MODEL), F32),
                   jax.ShapeDtypeStruct((T * PACK_ROWS, LANES), jnp.int32),
                   jax.ShapeDtypeStruct((T, LANES), jnp.int32),
                   jax.ShapeDtypeStruct((T, LANES), F32),
                   jax.ShapeDtypeStruct((SUBLANES, LANES), jnp.int32)],
        compiler_params=pltpu.CompilerParams(vmem_limit_bytes=48 * MIB),
        name="merge_router",
    )(yc, ya, yr, gates, x, mod_l, g2, wa, wb, wc, wo, wrh, wrl, br)


def _moe_kernel(blk_e_ref, first_ref, next_e_ref, nu_ref, tok_ref, tok_next_ref, xp_ref, wgu_hbm, bgu_ref, wd_hbm,
                bd_ref, y_ref, tile, wgu_st, wd_st, wgu_bf, wd_bf, sems, *, layer):
    b = pl.program_id(0)
    slot = b % 2

    def weight_copies(e):
        return (pltpu.make_async_copy(wgu_hbm.at[layer, e], wgu_st, sems.at[0]),
                pltpu.make_async_copy(wd_hbm.at[layer, e], wd_st, sems.at[1]))

    def gather(tok, dst_slot):
        for mi in range(MOE_M):
            t = jnp.minimum(tok[0, 0, mi], T - 1)
            slab = xp_ref[pl.ds(pl.multiple_of(t * PACK_ROWS, PACK_ROWS), PACK_ROWS), :]
            tile[dst_slot, pl.ds(mi, PACK_ROWS, stride=GATHER_STRIDE), :] = slab

    @pl.when(b == 0)
    def _():
        for cp in weight_copies(blk_e_ref[0]):
            cp.start()
        gather(tok_ref, 0)

    @pl.when(b < nu_ref[0])
    def _():
        @pl.when(first_ref[b] == 1)
        def _():
            for cp in weight_copies(blk_e_ref[b]):
                cp.wait()
            wgu_bf[...] = wgu_st[...].astype(BF16)
            wd_bf[...] = wd_st[...].astype(BF16)

            @pl.when(next_e_ref[b] >= 0)
            def _():
                for cp in weight_copies(next_e_ref[b]):
                    cp.start()

        lo, hi = [], []
        for c in range(PACK_ROWS):
            bits = lax.bitcast_convert_type(
                tile[slot, c * GATHER_STRIDE:c * GATHER_STRIDE + MOE_M, :], jnp.uint32)
            lo.append(lax.bitcast_convert_type(bits << 16, F32).astype(BF16))
            hi.append(lax.bitcast_convert_type(bits & jnp.uint32(0xFFFF0000), F32).astype(BF16))
        x = jnp.concatenate(lo + hi, axis=1)
        gather(tok_next_ref, 1 - slot)

        gu = jnp.dot(x, wgu_bf[...], preferred_element_type=F32) + bgu_ref[0, 0]
        gate = jnp.minimum(gu[:, 0:D_EXPERT], SWIGLU_LIMIT)
        up = jnp.clip(gu[:, D_EXPERT:2 * D_EXPERT], -SWIGLU_LIMIT, SWIGLU_LIMIT)
        glu = gate * _sigmoid(SWIGLU_ALPHA * gate)
        mid = ((up + 1.0) * glu).astype(BF16)
        y = jnp.dot(mid, wd_bf[...], preferred_element_type=F32) + bd_ref[0, 0]
        y_ref[...] = y.astype(BF16)

    @pl.when(b >= nu_ref[0])
    def _():
        y_ref[...] = jnp.zeros_like(y_ref)


def _moe_experts(blk_e, first, next_e, n_used, row_tok3, xp, w_gu, b_gu, w_down, b_down, layer):
    bias = lambda b, e, f, ne, nu: (layer, e[b], 0, 0)
    grid_spec = pltpu.PrefetchScalarGridSpec(
        num_scalar_prefetch=4,
        grid=(N_BLOCKS,),
        in_specs=[pl.BlockSpec((1, 1, MOE_M), lambda b, e, f, ne, nu: (b, 0, 0), memory_space=pltpu.SMEM),
                  pl.BlockSpec((1, 1, MOE_M), lambda b, e, f, ne, nu: (jnp.minimum(b + 1, N_BLOCKS - 1), 0, 0),
                               memory_space=pltpu.SMEM),
                  pl.BlockSpec((T * PACK_ROWS, LANES), lambda b, e, f, ne, nu: (0, 0), pipeline_mode=pl.Buffered(1)),
                  pl.BlockSpec(memory_space=pl.ANY),
                  pl.BlockSpec((1, 1, 1, 2 * D_EXPERT), bias),
                  pl.BlockSpec(memory_space=pl.ANY),
                  pl.BlockSpec((1, 1, 1, D_MODEL), bias)],
        out_specs=pl.BlockSpec((MOE_M, D_MODEL), lambda b, e, f, ne, nu: (b, 0)),
        scratch_shapes=[pltpu.VMEM((2, PACK_ROWS * GATHER_STRIDE, LANES), jnp.int32),
                        pltpu.VMEM((D_MODEL, 2 * D_EXPERT), F32),
                        pltpu.VMEM((D_EXPERT, D_MODEL), F32),
                        pltpu.VMEM((D_MODEL, 2 * D_EXPERT), BF16),
                        pltpu.VMEM((D_EXPERT, D_MODEL), BF16),
                        pltpu.SemaphoreType.DMA((2,))],
    )
    return pl.pallas_call(
        functools.partial(_moe_kernel, layer=layer),
        grid_spec=grid_spec,
        out_shape=jax.ShapeDtypeStruct((N_ROWS, D_MODEL), BF16),
        compiler_params=pltpu.CompilerParams(vmem_limit_bytes=48 * MIB),
        name="moe_experts",
    )(blk_e, first, next_e, n_used, row_tok3, row_tok3, xp, w_gu, b_gu.reshape(DEPTH, N_EXPERTS, 1, 2 * D_EXPERT),
      w_down, b_down.reshape(DEPTH, N_EXPERTS, 1, D_MODEL))


SCATTER_UNROLL = 8


def _combine_kernel(nu_ref, tok_ref, w_ref, y_ref, o_ref, acc, tile, sem):
    s = pl.program_id(0)

    @pl.when(s == 0)
    def _():
        acc[...] = jnp.zeros_like(acc)

    def sub_block(sb, carry):
        r0 = pl.multiple_of(sb * SCATTER_M, SCATTER_M)
        y = y_ref[pl.ds(r0, SCATTER_M), :].astype(F32)
        for c in range(ROW_VREGS):
            tile[c * SCATTER_STRIDE:c * SCATTER_STRIDE + SCATTER_M, :] = y[:, c * LANES:(c + 1) * LANES]
        for m0 in range(0, SCATTER_M, SCATTER_UNROLL):
            addrs, vals = [], []
            for u in range(SCATTER_UNROLL):
                mi = m0 + u
                a = pl.multiple_of(tok_ref[0, 0, r0 + mi] * ROW_VREGS, ROW_VREGS)
                yv = tile[pl.ds(mi, ROW_VREGS, stride=SCATTER_STRIDE), :]
                addrs.append(a)
                vals.append(acc[pl.ds(a, ROW_VREGS), :] + w_ref[0, 0, r0 + mi] * yv)
            for u in range(SCATTER_UNROLL):
                acc[pl.ds(addrs[u], ROW_VREGS), :] = vals[u]
        return carry

    @pl.when(s * COMBINE_BLOCKS < nu_ref[0])
    def _():
        lax.fori_loop(0, COMBINE_BLOCKS * MOE_M // SCATTER_M, sub_block, 0)

    @pl.when(s == pl.num_programs(0) - 1)
    def _():
        cp = pltpu.make_async_copy(acc.at[pl.ds(0, T * ROW_VREGS)], o_ref, sem)
        cp.start()
        cp.wait()


def _combine(n_used, row_tok3, row_w3, yr):
    rows = COMBINE_BLOCKS * MOE_M
    steps = N_BLOCKS // COMBINE_BLOCKS
    grid_spec = pltpu.PrefetchScalarGridSpec(
        num_scalar_prefetch=1,
        grid=(steps,),
        in_specs=[pl.BlockSpec((1, 1, rows), lambda s, nu: (s, 0, 0), memory_space=pltpu.SMEM),
                  pl.BlockSpec((1, 1, rows), lambda s, nu: (s, 0, 0), memory_space=pltpu.SMEM),
                  pl.BlockSpec((rows, D_MODEL), lambda s, nu: (s, 0))],
        out_specs=pl.BlockSpec(memory_space=pl.ANY),
        scratch_shapes=[pltpu.VMEM(((T + 1) * ROW_VREGS, LANES), F32),
                        pltpu.VMEM((ROW_VREGS * SCATTER_STRIDE, LANES), F32),
                        pltpu.SemaphoreType.DMA(())],
    )
    return pl.pallas_call(
        _combine_kernel,
        grid_spec=grid_spec,
        out_shape=jax.ShapeDtypeStruct((T * ROW_VREGS, LANES), F32),
        compiler_params=pltpu.CompilerParams(vmem_limit_bytes=48 * MIB),
        name="moe_combine",
    )(n_used, row_tok3.reshape(steps, 1, rows), row_w3.reshape(steps, 1, rows), yr)


def _route(top_e, top_w, counts):
    experts = jnp.arange(N_EXPERTS, dtype=jnp.int32)
    padded = (counts + MOE_M - 1) // MOE_M * MOE_M
    pad_end = jnp.cumsum(padded)
    pad_start = pad_end - padded
    n_used = (pad_end[-1] // MOE_M).astype(jnp.int32)
    n_pad = N_ROWS - N_ASSIGN
    pad_cum = jnp.cumsum(padded - counts)
    pad_expert = jnp.sum((jnp.arange(n_pad, dtype=jnp.int32)[:, None] >= pad_cum[None, :]).astype(jnp.int32), axis=1)
    keys = jnp.concatenate([2 * top_e.reshape(N_ASSIGN), 2 * pad_expert + 1])
    toks = jnp.concatenate([jnp.arange(N_ASSIGN, dtype=jnp.int32) // TOP_K, jnp.full((n_pad,), T, jnp.int32)])
    wts = jnp.concatenate([top_w.reshape(N_ASSIGN), jnp.zeros((n_pad,), F32)])
    _, row_tok, row_w = lax.sort((keys, toks, wts), num_keys=1, is_stable=True)
    blk0 = jnp.arange(N_BLOCKS, dtype=jnp.int32) * MOE_M
    blk_e = jnp.minimum(jnp.sum((blk0[:, None] >= pad_end[None, :]).astype(jnp.int32), axis=1), N_EXPERTS - 1)
    first = (blk0 == pad_start[blk_e]).astype(jnp.int32)
    later = (experts[None, :] > experts[:, None]) & (counts[None, :] > 0)
    nxt = jnp.min(jnp.where(later, experts[None, :], N_EXPERTS), axis=1)
    next_e = jnp.where(nxt == N_EXPERTS, -1, nxt)[blk_e].astype(jnp.int32)
    return (row_tok.reshape(N_BLOCKS, 1, MOE_M), row_w.reshape(N_BLOCKS, 1, MOE_M), blk_e.astype(jnp.int32),
            first, next_e, n_used.reshape(1))


def _residual_kernel(x_ref, moe_ref, mod_ref, g_ref, o_ref, *, final):
    moe = jnp.concatenate([moe_ref[pl.ds(c, TM, stride=ROW_VREGS), :] for c in range(ROW_VREGS)], axis=1)
    x = x_ref[...] + mod_ref[0, 5:6, :] * moe
    if final:
        ms = jnp.mean(x * x, axis=-1, keepdims=True)
        x = x * lax.rsqrt(ms + EPS) * g_ref[...]
    o_ref[...] = x


def _residual(x1, moe_tiles, mod_l, g, final):
    row = lambda i: (i, 0)
    return pl.pallas_call(
        functools.partial(_residual_kernel, final=final),
        grid=(T // TM,),
        in_specs=[pl.BlockSpec((TM, D_MODEL), row),
                  pl.BlockSpec((TM * ROW_VREGS, LANES), row),
                  pl.BlockSpec((1, 6, D_MODEL), lambda i: (_mod_group(i), 0, 0)),
                  pl.BlockSpec((1, D_MODEL), lambda i: (0, 0))],
        out_specs=pl.BlockSpec((TM, D_MODEL), row),
        out_shape=jax.ShapeDtypeStruct((T, D_MODEL), F32),
        compiler_params=pltpu.CompilerParams(vmem_limit_bytes=32 * MIB),
        name="residual_final" if final else "residual",
    )(x1, moe_tiles, mod_l, g)


def _rope_tables():
    t = np.arange(DEC_SEQ)
    pos = np.stack([t // GRID_W, t % GRID_W], axis=1).astype(np.float32)
    half = HEAD_DIM // 2
    inv = jnp.asarray(ROPE_BASE, F32) ** (-jnp.arange(0, half, 2, dtype=F32) / half)
    d = np.arange(HEAD_DIM)
    which = d // half
    freq = d % (half // 2)
    sign = np.where((d % half) < half // 2, -1.0, 1.0).astype(np.float32)
    ang = jnp.asarray(pos)[:, which] * inv[freq][None, :]
    cos = jnp.cos(ang)
    sin = jnp.sin(ang) * jnp.asarray(sign)[None, :]
    reps = LANES // HEAD_DIM
    return jnp.tile(cos, (1, reps)), jnp.tile(sin, (1, reps))


def kernel(x_prompt, x_sample, cache_k, cache_v, state_ret, c, c_ctx, norm1_g, norm2_g, w_mod, b_mod, w_in, conv_w, attn_sink, ret_decay, w_a, w_b, w_c, w_o, w_router, b_router, w_gu, b_gu, w_down, b_down, final_g):
    x = jnp.concatenate([x_prompt.reshape(T_CTX, D_MODEL), x_sample.reshape(T_LAT, D_MODEL)], axis=0)
    cond8 = jnp.zeros((8, D_MODEL), F32).at[0].set(c_ctx).at[1:1 + DEC_BATCH].set(c)
    mod = _modulation(cond8, w_mod, b_mod)
    cos, sin = _rope_tables()
    ck = cache_k.reshape(DEC_BATCH, DEPTH, PAST_LEN, D_KV)
    cv = cache_v.reshape(DEC_BATCH, DEPTH, PAST_LEN, D_KV)

    ks, vs, rs = [], [], []
    for l in range(DEPTH):
        mod_l = mod[l, 0:1 + DEC_BATCH].reshape(1 + DEC_BATCH, 6, D_MODEL)
        conv_in, q, k, v, ret, gates = _inproj(x, mod_l, norm1_g[l][None, :], w_in, l)
        rd8 = jnp.broadcast_to(ret_decay[l].reshape(2 * RET_HEADS, 1), (2 * RET_HEADS, LANES))
        yc_c, ya_c, yr_c, rfin = _mixers(conv_in, q, k, v, ret, conv_w[l], attn_sink[l], rd8, latent=False)
        yc_s, ya_s, yr_s = _mixers(conv_in, q, k, v, ret, conv_w[l], attn_sink[l], rd8, latent=True,
                                   cos=cos, sin=sin, cache_k=ck, cache_v=cv, state=state_ret, layer=l)
        yc = jnp.concatenate([yc_c, yc_s], axis=0)
        ya = jnp.concatenate([ya_c, ya_s], axis=0)
        yr = jnp.concatenate([yr_c, yr_s], axis=0)
        wr = jnp.pad(w_router[l], ((0, 0), (0, LANES - N_EXPERTS)))
        wrh = wr.astype(BF16)
        wrl = (wr - wrh.astype(F32)).astype(BF16)
        br = jnp.pad(b_router[l], (0, LANES - N_EXPERTS))[None, :]
        x1, xp, top_e, top_w, cnt = _merge(yc, ya, yr, gates, x, mod_l, norm2_g[l][None, :], w_a, w_b, w_c, w_o,
                                           wrh, wrl, br, l)
        row_tok, row_w, blk_e, first, next_e, n_used = _route(top_e[:, 0:TOP_K], top_w[:, 0:TOP_K],
                                                              cnt[0, 0:N_EXPERTS])
        yrows = _moe_experts(blk_e, first, next_e, n_used, row_tok, xp, w_gu, b_gu, w_down, b_down, l)
        moe = _combine(n_used, row_tok, row_w, yrows)
        final = l == DEPTH - 1
        x = _residual(x1, moe, mod_l, final_g[None, :], final)
        ks.append(k[0:T_CTX].reshape(BATCH, SEQ, N_KV, HEAD_DIM))
        vs.append(v[0:T_CTX].reshape(BATCH, SEQ, N_KV, HEAD_DIM))
        rs.append(rfin)

    y_prompt = x[0:T_CTX].reshape(BATCH, SEQ, D_MODEL)
    y_sample = x[T_CTX:T].reshape(DEC_BATCH, DEC_SEQ, D_MODEL)
    return (y_prompt, y_sample, jnp.stack(ks, axis=1), jnp.stack(vs, axis=1), jnp.stack(rs, axis=1))
```

```python
import functools

import numpy as np
import jax
import jax.numpy as jnp
from jax import lax
from jax.experimental import pallas as pl
from jax.experimental.pallas import tpu as pltpu

F32 = jnp.float32
BF16 = jnp.bfloat16

D_MODEL = 1024
BATCH = 16
SEQ = 256
DEPTH = 2
DEC_BATCH = 2
DEC_SEQ = 2048
PAST_LEN = 256
GRID_W = 64
HEAD_DIM = 64
D_CONV = 256
N_HEADS = 8
N_KV = 2
GROUP = N_HEADS // N_KV
WINDOW = 128
ROPE_BASE = 10000.0
RET_HEADS = 4
RET_DK = 64
RET_DV = 64
CHUNK = 128
N_EXPERTS = 32
TOP_K = 4
D_EXPERT = D_MODEL
SWIGLU_LIMIT = 7.0
SWIGLU_ALPHA = 1.702
EPS = 1e-6
NEG_INF = -1e30

T_CTX = BATCH * SEQ
T_LAT = DEC_BATCH * DEC_SEQ
T = T_CTX + T_LAT
D_ATTN = N_HEADS * HEAD_DIM
D_KV = N_KV * HEAD_DIM
D_RET = RET_HEADS * RET_DK
C_CONV = 0
C_Q = 3 * D_CONV
C_K = C_Q + D_ATTN
C_V = C_K + D_KV
C_RET = C_V + D_KV
C_GATE = C_RET + 4 * D_RET
IN_COLS = C_GATE + 3 * D_MODEL

TM = 512
MOE_M = 256
N_ASSIGN = T * TOP_K
N_BLOCKS = (N_ASSIGN + N_EXPERTS * (MOE_M - 1) + MOE_M - 1) // MOE_M
N_ROWS = N_BLOCKS * MOE_M
LANES = 128
SUBLANES = 8
ROW_VREGS = D_MODEL // LANES
PACK_ROWS = ROW_VREGS // 2
GATHER_STRIDE = MOE_M + SUBLANES
SCATTER_M = 128
SCATTER_STRIDE = SCATTER_M + SUBLANES
COMBINE_BLOCKS = 2
TOK_BITS = 14
assert T < (1 << TOK_BITS)
MIB = 1024 * 1024


def _sigmoid(x):
    return 1.0 / (1.0 + jnp.exp(-x))


def _mod_group(i):
    n_ctx = T_CTX // TM
    per_lat = DEC_SEQ // TM
    g = jnp.zeros_like(i)
    for b in range(DEC_BATCH):
        g = g + (i >= n_ctx + b * per_lat).astype(jnp.int32)
    return g


def _mod_kernel(cond_ref, w_ref, b_ref, o_ref):
    c = cond_ref[...]
    s = c * _sigmoid(c)
    o_ref[0] = jnp.dot(s.astype(BF16), w_ref[0].astype(BF16), preferred_element_type=F32) + b_ref[0]


def _modulation(cond8, w_mod, b_mod):
    n_col = 4
    cw = 6 * D_MODEL // n_col
    return pl.pallas_call(
        _mod_kernel,
        grid=(DEPTH, n_col),
        in_specs=[pl.BlockSpec((8, D_MODEL), lambda l, j: (0, 0)),
                  pl.BlockSpec((1, D_MODEL, cw), lambda l, j: (l, 0, j)),
                  pl.BlockSpec((1, 1, cw), lambda l, j: (l, 0, j))],
        out_specs=pl.BlockSpec((1, 8, cw), lambda l, j: (l, 0, j)),
        out_shape=jax.ShapeDtypeStruct((DEPTH, 8, 6 * D_MODEL), F32),
        compiler_params=pltpu.CompilerParams(vmem_limit_bytes=32 * MIB),
        name="modulation",
    )(cond8, w_mod, b_mod.reshape(DEPTH, 1, 6 * D_MODEL))


N_CTX_TILES = T_CTX // TM


def _ctx_tile(i):
    return (jnp.minimum(i, N_CTX_TILES - 1), 0)


def _lat_tile(i):
    return (jnp.maximum(i - N_CTX_TILES, 0), 0)


def _pick(ctx_ref, lat_ref):
    return jnp.where(pl.program_id(0) < N_CTX_TILES, ctx_ref[...], lat_ref[...])


def _inproj_kernel(*refs, split):
    if split:
        xc_ref, xl_ref, mod_ref, g_ref, w_ref, conv_o, q_o, k_o, v_o, ret_o, gate_o = refs
        x = _pick(xc_ref, xl_ref)
    else:
        x_ref, mod_ref, g_ref, w_ref, conv_o, q_o, k_o, v_o, ret_o, gate_o = refs
        x = x_ref[...]
    ms = jnp.mean(x * x, axis=-1, keepdims=True)
    h = x * lax.rsqrt(ms + EPS) * g_ref[...]
    h = h * (1.0 + mod_ref[0, 1:2, :]) + mod_ref[0, 0:1, :]
    hb = h.astype(BF16)

    def proj(c0, c1):
        return jnp.dot(hb, w_ref[0, :, c0:c1].astype(BF16), preferred_element_type=F32)

    conv_o[...] = proj(C_CONV, C_Q).astype(BF16)
    q_o[...] = (proj(C_Q, C_K) * HEAD_DIM ** -0.5).astype(BF16)
    k_o[...] = proj(C_K, C_V)
    v_o[...] = proj(C_V, C_RET)
    ret_o[:, 0:D_RET] = proj(C_RET, C_RET + D_RET).astype(BF16)
    ret_o[:, D_RET:2 * D_RET] = (proj(C_RET + D_RET, C_RET + 2 * D_RET) * RET_DK ** -0.5).astype(BF16)
    ret_o[:, 2 * D_RET:4 * D_RET] = proj(C_RET + 2 * D_RET, C_GATE).astype(BF16)
    for b in range(3):
        g = proj(C_GATE + b * D_MODEL, C_GATE + (b + 1) * D_MODEL)
        gate_o[:, b * D_MODEL:(b + 1) * D_MODEL] = _sigmoid(g).astype(BF16)


def _x_specs(xs):
    if len(xs) == 2:
        return [pl.BlockSpec((TM, D_MODEL), _ctx_tile), pl.BlockSpec((TM, D_MODEL), _lat_tile)]
    return [pl.BlockSpec((TM, D_MODEL), lambda i: (i, 0))]


def _inproj(xs, mod_l, g1, w_in, layer):
    row = lambda i: (i, 0)
    return pl.pallas_call(
        functools.partial(_inproj_kernel, split=len(xs) == 2),
        grid=(T // TM,),
        in_specs=_x_specs(xs) + [
                  pl.BlockSpec((1, 6, D_MODEL), lambda i: (_mod_group(i), 0, 0)),
                  pl.BlockSpec((1, D_MODEL), lambda i: (0, 0)),
                  pl.BlockSpec((1, D_MODEL, IN_COLS), lambda i: (layer, 0, 0), pipeline_mode=pl.Buffered(1))],
        out_specs=[pl.BlockSpec((TM, 3 * D_CONV), row),
                   pl.BlockSpec((TM, D_ATTN), row),
                   pl.BlockSpec((TM, D_KV), row),
                   pl.BlockSpec((TM, D_KV), row),
                   pl.BlockSpec((TM, 4 * D_RET), row),
                   pl.BlockSpec((TM, 3 * D_MODEL), row)],
        out_shape=[jax.ShapeDtypeStruct((T, 3 * D_CONV), BF16),
                   jax.ShapeDtypeStruct((T, D_ATTN), BF16),
                   jax.ShapeDtypeStruct((T, D_KV), F32),
                   jax.ShapeDtypeStruct((T, D_KV), F32),
                   jax.ShapeDtypeStruct((T, 4 * D_RET), BF16),
                   jax.ShapeDtypeStruct((T, 3 * D_MODEL), BF16)],
        compiler_params=pltpu.CompilerParams(vmem_limit_bytes=56 * MIB),
        name="inproj",
    )(*xs, mod_l, g1, w_in)


def _rope(x, cos, sin_signed):
    lane = lax.broadcasted_iota(jnp.int32, x.shape, 1)
    first = (lane % 32) < 16
    partner = jnp.where(first, pltpu.roll(x, x.shape[1] - 16, 1), pltpu.roll(x, 16, 1))
    return x * cos + partner * sin_signed


def _dot_nt(a, b):
    return lax.dot_general(a, b, (((1,), (1,)), ((), ())), preferred_element_type=F32)


def _dot_tn(a, b):
    return lax.dot_general(a, b, (((0,), (0,)), ((), ())), preferred_element_type=F32)


def _mixer_kernel(*refs, n, latent):
    if latent:
        (sink_ref, conv_ref, q_ref, k_ref, v_ref, ret_ref, cw_ref, rd_ref, cos_ref, sin_ref, ck_ref, cv_ref,
         r0_ref, yc_ref, ya_ref, yr_ref, kb, vb, o_f, o_b, rst, dmat, qdec, kdec) = refs
    else:
        (sink_ref, conv_ref, q_ref, k_ref, v_ref, ret_ref, cw_ref, rd_ref,
         yc_ref, ya_ref, yr_ref, rfin_ref, kb, vb, o_f, o_b, rst, dmat, qdec, kdec) = refs
    nb = n // CHUNK

    cv = conv_ref[...].astype(F32)
    cb, cc, cu = cv[:, 0:D_CONV], cv[:, D_CONV:2 * D_CONV], cv[:, 2 * D_CONV:3 * D_CONV]
    p = cc * cu
    row = lax.broadcasted_iota(jnp.int32, p.shape, 0)
    prev = jnp.where(row == 0, 0.0, pltpu.roll(p, 1, 0))
    nxt = jnp.where(row == n - 1, 0.0, pltpu.roll(p, n - 1, 0))
    cw = cw_ref[...]
    yc_ref[...] = (cb * (prev * cw[0:1, :] + p * cw[1:2, :] + nxt * cw[2:3, :])).astype(BF16)

    if latent:
        kr = _rope(k_ref[...], cos_ref[...], sin_ref[...])
        zpad = jnp.zeros((CHUNK, D_KV), BF16)
        kb[0:CHUNK, :] = zpad
        vb[0:CHUNK, :] = zpad
        kb[CHUNK + n:2 * CHUNK + n, :] = zpad
        vb[CHUNK + n:2 * CHUNK + n, :] = zpad
        kb[CHUNK:CHUNK + n, :] = kr.astype(BF16)
        vb[CHUNK:CHUNK + n, :] = v_ref[...].astype(BF16)
        ckb = ck_ref[0, 0].astype(BF16)
        cvb = cv_ref[0, 0].astype(BF16)
    else:
        kb[...] = k_ref[...].astype(BF16)
        vb[...] = v_ref[...].astype(BF16)

    def attn_block(j):
        r0 = j * CHUNK if isinstance(j, int) else pl.multiple_of(j * CHUNK, CHUNK)
        qj = q_ref[pl.ds(r0, CHUNK), :]
        if latent:
            cosj = cos_ref[pl.ds(r0, CHUNK), :]
            sinj = sin_ref[pl.ds(r0, CHUNK), :]
            qf = qj.astype(F32)
            qj = jnp.concatenate(
                [_rope(qf[:, c * LANES:(c + 1) * LANES], cosj, sinj) for c in range(D_ATTN // LANES)],
                axis=1).astype(BF16)
            kw = kb[pl.ds(r0, 3 * CHUNK), :]
            vw = vb[pl.ds(r0, 3 * CHUNK), :]
            qpos = r0 + lax.broadcasted_iota(jnp.int32, (CHUNK, 3 * CHUNK), 0)
            kpos = r0 - CHUNK + lax.broadcasted_iota(jnp.int32, (CHUNK, 3 * CHUNK), 1)
            ok = (jnp.abs(qpos - kpos) <= WINDOW) & (kpos >= 0) & (kpos < n)
            ok = jnp.concatenate([ok] * GROUP, axis=0)
        else:
            kw = kb[...]
            vw = vb[...]
        for g in range(N_KV):
            lo = g * HEAD_DIM
            qg = jnp.concatenate(
                [qj[:, (GROUP * g + i) * HEAD_DIM:(GROUP * g + i + 1) * HEAD_DIM] for i in range(GROUP)], axis=0)
            sk = jnp.concatenate(
                [jnp.full((CHUNK, 1), sink_ref[GROUP * g + i], F32) for i in range(GROUP)], axis=0)
            s = _dot_nt(qg, kw[:, lo:lo + HEAD_DIM])
            if latent:
                s = jnp.where(ok, s, NEG_INF)
                s2 = _dot_nt(qg, ckb[:, lo:lo + HEAD_DIM])
            m = jnp.maximum(jnp.max(s, axis=-1, keepdims=True), sk)
            if latent:
                m = jnp.maximum(m, jnp.max(s2, axis=-1, keepdims=True))
            pw = jnp.exp(s - m)
            den = jnp.sum(pw, axis=-1, keepdims=True) + jnp.exp(sk - m)
            o = jnp.dot(pw.astype(BF16), vw[:, lo:lo + HEAD_DIM], preferred_element_type=F32)
            if latent:
                p2 = jnp.exp(s2 - m)
                den = den + jnp.sum(p2, axis=-1, keepdims=True)
                o = o + jnp.dot(p2.astype(BF16), cvb[:, lo:lo + HEAD_DIM], preferred_element_type=F32)
            o = o / den
            ya_ref[pl.ds(r0, CHUNK), g * GROUP * HEAD_DIM:(g + 1) * GROUP * HEAD_DIM] = jnp.concatenate(
                [o[i * CHUNK:(i + 1) * CHUNK, :] for i in range(GROUP)], axis=1).astype(BF16)

    if latent:
        def attn_body(j, carry):
            attn_block(j)
            return carry
        lax.fori_loop(0, nb, attn_body, 0)
    else:
        for j in range(nb):
            attn_block(j)

    rd = rd_ref[...]
    log_g = jnp.minimum(rd, 0.0) - jnp.log(1.0 + jnp.exp(-jnp.abs(rd)))
    ii = lax.broadcasted_iota(jnp.int32, (CHUNK, CHUNK), 0).astype(F32)
    jj = lax.broadcasted_iota(jnp.int32, (CHUNK, CHUNK), 1).astype(F32)
    for d in range(2):
        for h in range(RET_HEADS):
            r = d * RET_HEADS + h
            lg = log_g[r:r + 1, :]
            diff = (ii - jj) if d == 0 else (jj - ii)
            dmat[r] = jnp.where(diff >= 0, jnp.exp(jnp.maximum(diff, 0.0) * lg), 0.0)
            if d == 0:
                qdec[r] = jnp.exp((ii + 1.0) * lg)
                kdec[r] = jnp.exp((CHUNK - 1.0 - ii) * lg)
            else:
                qdec[r] = jnp.exp((CHUNK - ii) * lg)
                kdec[r] = jnp.exp(ii * lg)
            if latent:
                rst[r] = r0_ref[0, 0, d, h]
            else:
                rst[r] = jnp.zeros((RET_DK, RET_DV), F32)
    chunk_decay = jnp.exp(float(CHUNK) * log_g)

    def ret_chunk(c0, d, out_ref):
        blk = ret_ref[pl.ds(c0, CHUNK), :]
        outs = []
        for h in range(RET_HEADS):
            r = d * RET_HEADS + h
            qc = blk[:, h * RET_DK:(h + 1) * RET_DK]
            kc = blk[:, D_RET + h * RET_DK:D_RET + (h + 1) * RET_DK]
            vc = blk[:, 2 * D_RET + h * RET_DV:2 * D_RET + (h + 1) * RET_DV]
            inner = (_dot_nt(qc, kc) * dmat[r]).astype(BF16)
            state = rst[r]
            o = jnp.dot(inner, vc, preferred_element_type=F32)
            o = o + jnp.dot(qc, state.astype(BF16), preferred_element_type=F32) * qdec[r][:, 0:RET_DV]
            kd = (kc.astype(F32) * kdec[r][:, 0:RET_DK]).astype(BF16)
            rst[r] = state * chunk_decay[r:r + 1, 0:RET_DV] + _dot_tn(kd, vc)
            outs.append(o)
        out_ref[pl.ds(c0, CHUNK), :] = jnp.concatenate(outs, axis=1)

    def ret_body(j, carry):
        ret_chunk(pl.multiple_of(j * CHUNK, CHUNK), 0, o_f)
        ret_chunk(pl.multiple_of((nb - 1 - j) * CHUNK, CHUNK), 1, o_b)
        return carry

    lax.fori_loop(0, nb, ret_body, 0)

    def norm_body(j, carry):
        c0 = pl.multiple_of(j * CHUNK, CHUNK)
        o = o_f[pl.ds(c0, CHUNK), :] + o_b[pl.ds(c0, CHUNK), :]
        parts = []
        for h in range(RET_HEADS):
            oh = o[:, h * RET_DV:(h + 1) * RET_DV]
            ms = jnp.mean(oh * oh, axis=-1, keepdims=True)
            parts.append(oh * lax.rsqrt(ms + EPS))
        rg = ret_ref[pl.ds(c0, CHUNK), 3 * D_RET:4 * D_RET].astype(F32)
        yr_ref[pl.ds(c0, CHUNK), :] = (rg * _sigmoid(rg) * jnp.concatenate(parts, axis=1)).astype(BF16)
        return carry

    lax.fori_loop(0, nb, norm_body, 0)

    if not latent:
        for d in range(2):
            for h in range(RET_HEADS):
                rfin_ref[0, d, h] = rst[d * RET_HEADS + h]


def _mixers(conv_in, q, k, v, ret, conv_w_l, sink_l, rd8, *, latent, cos=None, sin=None, cache_k=None,
            cache_v=None, state=None, layer=0):
    n = DEC_SEQ if latent else SEQ
    nseq = DEC_BATCH if latent else BATCH
    off = T_CTX // n if latent else 0
    seq = lambda s: (s + off, 0)
    const = lambda s: (0, 0)
    in_specs = [pl.BlockSpec(memory_space=pltpu.SMEM),
                pl.BlockSpec((n, 3 * D_CONV), seq),
                pl.BlockSpec((n, D_ATTN), seq),
                pl.BlockSpec((n, D_KV), seq),
                pl.BlockSpec((n, D_KV), seq),
                pl.BlockSpec((n, 4 * D_RET), seq),
                pl.BlockSpec((3, D_CONV), const),
                pl.BlockSpec((8, LANES), const)]
    args = [sink_l, conv_in, q, k, v, ret, conv_w_l, rd8]
    out_specs = [pl.BlockSpec((n, D_CONV), lambda s: (s, 0)),
                 pl.BlockSpec((n, D_ATTN), lambda s: (s, 0)),
                 pl.BlockSpec((n, D_RET), lambda s: (s, 0))]
    out_shape = [jax.ShapeDtypeStruct((nseq * n, D_CONV), BF16),
                 jax.ShapeDtypeStruct((nseq * n, D_ATTN), BF16),
                 jax.ShapeDtypeStruct((nseq * n, D_RET), BF16)]
    if latent:
        in_specs += [pl.BlockSpec((n, LANES), const),
                     pl.BlockSpec((n, LANES), const),
                     pl.BlockSpec((1, 1, PAST_LEN, D_KV), lambda s: (s, layer, 0, 0)),
                     pl.BlockSpec((1, 1, PAST_LEN, D_KV), lambda s: (s, layer, 0, 0)),
                     pl.BlockSpec((1, 1, 2, RET_HEADS, RET_DK, RET_DV), lambda s: (s, layer, 0, 0, 0, 0))]
        args += [cos, sin, cache_k, cache_v, state]
        kv_rows = n + 2 * CHUNK
    else:
        out_specs.append(pl.BlockSpec((1, 2, RET_HEADS, RET_DK, RET_DV), lambda s: (s, 0, 0, 0, 0)))
        out_shape.append(jax.ShapeDtypeStruct((nseq, 2, RET_HEADS, RET_DK, RET_DV), F32))
        kv_rows = n
    scratch = [pltpu.VMEM((kv_rows, D_KV), BF16),
               pltpu.VMEM((kv_rows, D_KV), BF16),
               pltpu.VMEM((n, D_RET), F32),
               pltpu.VMEM((n, D_RET), F32),
               pltpu.VMEM((2 * RET_HEADS, RET_DK, RET_DV), F32),
               pltpu.VMEM((2 * RET_HEADS, CHUNK, CHUNK), F32),
               pltpu.VMEM((2 * RET_HEADS, CHUNK, CHUNK), F32),
               pltpu.VMEM((2 * RET_HEADS, CHUNK, CHUNK), F32)]
    return pl.pallas_call(
        functools.partial(_mixer_kernel, n=n, latent=latent),
        grid=(nseq,),
        in_specs=in_specs,
        out_specs=out_specs,
        out_shape=out_shape,
        scratch_shapes=scratch,
        compiler_params=pltpu.CompilerParams(vmem_limit_bytes=48 * MIB),
        name="mixers_latent" if latent else "mixers_context",
    )(*args)


def _merge_kernel(*refs, split):
    if split:
        xc_ref, xl_ref = refs[0:2]
        x = _pick(xc_ref, xl_ref)
        refs = refs[2:]
    else:
        x = refs[0][...]
        refs = refs[1:]
    (ycc_ref, ycl_ref, yac_ref, yal_ref, yrc_ref, yrl_ref, gate_ref, mod_ref, g2_ref, wa_ref, wb_ref, wc_ref,
     wo_ref, wrh_ref, wrl_ref, br_ref, x1_o, xp_o, te_o, tw_o, cnt_o) = refs
    merged = (gate_ref[:, 0:D_MODEL].astype(F32)
              * jnp.dot(_pick(ycc_ref, ycl_ref), wa_ref[0].astype(BF16), preferred_element_type=F32)
              + gate_ref[:, D_MODEL:2 * D_MODEL].astype(F32)
              * jnp.dot(_pick(yac_ref, yal_ref), wb_ref[0].astype(BF16), preferred_element_type=F32)
              + gate_ref[:, 2 * D_MODEL:3 * D_MODEL].astype(F32)
              * jnp.dot(_pick(yrc_ref, yrl_ref), wc_ref[0].astype(BF16), preferred_element_type=F32))
    x1 = x + mod_ref[0, 2:3, :] * jnp.dot(merged.astype(BF16), wo_ref[0].astype(BF16),
                                          preferred_element_type=F32)
    x1_o[...] = x1
    ms = jnp.mean(x1 * x1, axis=-1, keepdims=True)
    h2 = x1 * lax.rsqrt(ms + EPS) * g2_ref[...]
    h2 = h2 * (1.0 + mod_ref[0, 4:5, :]) + mod_ref[0, 3:4, :]
    hh = h2.astype(BF16)
    hf = hh.astype(F32)
    bits = lax.bitcast_convert_type(hf, jnp.uint32)
    for c in range(PACK_ROWS):
        lo = bits[:, c * LANES:(c + 1) * LANES] >> 16
        hi = bits[:, (c + PACK_ROWS) * LANES:(c + PACK_ROWS + 1) * LANES] & jnp.uint32(0xFFFF0000)
        xp_o[pl.ds(c, TM, stride=PACK_ROWS), :] = lax.bitcast_convert_type(lo | hi, jnp.int32)
    hl = (h2 - hf).astype(BF16)
    logits = (jnp.dot(hh, wrh_ref[...], preferred_element_type=F32)
              + jnp.dot(hl, wrh_ref[...], preferred_element_type=F32)
              + jnp.dot(hh, wrl_ref[...], preferred_element_type=F32)
              + br_ref[...])
    lane = lax.broadcasted_iota(jnp.int32, logits.shape, 1)
    work = jnp.where(lane < N_EXPERTS, logits, -jnp.inf)
    vals, idxs = [], []
    for _ in range(TOP_K):
        m = jnp.max(work, axis=-1, keepdims=True)
        am = jnp.min(jnp.where(work == m, lane, LANES), axis=-1, keepdims=True)
        vals.append(m)
        idxs.append(am)
        work = jnp.where(lane == am, -jnp.inf, work)
    es = [jnp.exp(v - vals[0]) for v in vals]
    den = es[0] + es[1] + es[2] + es[3]
    te = jnp.zeros(logits.shape, jnp.int32)
    tw = jnp.zeros(logits.shape, F32)
    for k in range(TOP_K):
        te = jnp.where(lane == k, idxs[k], te)
        tw = jnp.where(lane == k, es[k] / den, tw)
    te_o[...] = te
    tw_o[...] = tw
    sel = (lane == idxs[0]) | (lane == idxs[1]) | (lane == idxs[2]) | (lane == idxs[3])
    part = jnp.sum(sel.astype(jnp.int32), axis=0, keepdims=True)

    @pl.when(pl.program_id(0) == 0)
    def _():
        cnt_o[...] = jnp.zeros_like(cnt_o)

    cnt_o[...] += jnp.broadcast_to(part, cnt_o.shape)


def _merge(xs, ys_ctx, ys_lat, gates, mod_l, g2, wa, wb, wc, wo, wrh, wrl, br, layer):
    row = lambda i: (i, 0)
    const = lambda i: (0, 0)
    wl = lambda i: (layer, 0, 0)
    y_specs, y_args = [], []
    for width, yc, yl in zip((D_CONV, D_ATTN, D_RET), ys_ctx, ys_lat):
        y_specs += [pl.BlockSpec((TM, width), _ctx_tile), pl.BlockSpec((TM, width), _lat_tile)]
        y_args += [yc, yl]
    return pl.pallas_call(
        functools.partial(_merge_kernel, split=len(xs) == 2),
        grid=(T // TM,),
        in_specs=_x_specs(xs) + y_specs + [
                  pl.BlockSpec((TM, 3 * D_MODEL), row),
                  pl.BlockSpec((1, 6, D_MODEL), lambda i: (_mod_group(i), 0, 0)),
                  pl.BlockSpec((1, D_MODEL), const),
                  pl.BlockSpec((1, D_CONV, D_MODEL), wl),
                  pl.BlockSpec((1, D_ATTN, D_MODEL), wl),
                  pl.BlockSpec((1, D_RET, D_MODEL), wl),
                  pl.BlockSpec((1, D_MODEL, D_MODEL), wl),
                  pl.BlockSpec((D_MODEL, LANES), const),
                  pl.BlockSpec((D_MODEL, LANES), const),
                  pl.BlockSpec((1, LANES), const)],
        out_specs=[pl.BlockSpec((TM, D_MODEL), row),
                   pl.BlockSpec((TM * PACK_ROWS, LANES), row),
                   pl.BlockSpec((TM, LANES), row),
                   pl.BlockSpec((TM, LANES), row),
                   pl.BlockSpec((SUBLANES, LANES), const)],
        out_shape=[jax.ShapeDtypeStruct((T, D_MODEL), F32),
                   jax.ShapeDtypeStruct((T * PACK_ROWS, LANES), jnp.int32),
                   jax.ShapeDtypeStruct((T, LANES), jnp.int32),
                   jax.ShapeDtypeStruct((T, LANES), F32),
                   jax.ShapeDtypeStruct((SUBLANES, LANES), jnp.int32)],
        compiler_params=pltpu.CompilerParams(vmem_limit_bytes=48 * MIB),
        name="merge_router",
    )(*xs, *y_args, gates, mod_l, g2, wa, wb, wc, wo, wrh, wrl, br)


def _moe_kernel(blk_e_ref, first_ref, next_e_ref, nu_ref, tok_ref, tok_next_ref, xp_ref, wgu_hbm, bgu_ref, wd_hbm,
                bd_ref, y_ref, tile, wgu_st, wd_st, wgu_bf, wd_bf, sems, *, layer):
    b = pl.program_id(0)
    slot = b % 2

    def weight_copies(e):
        return (pltpu.make_async_copy(wgu_hbm.at[layer, e], wgu_st, sems.at[0]),
                pltpu.make_async_copy(wd_hbm.at[layer, e], wd_st, sems.at[1]))

    def gather(tok, dst_slot):
        for mi in range(MOE_M):
            t = jnp.minimum(tok[0, 0, mi], T - 1)
            slab = xp_ref[pl.ds(pl.multiple_of(t * PACK_ROWS, PACK_ROWS), PACK_ROWS), :]
            tile[dst_slot, pl.ds(mi, PACK_ROWS, stride=GATHER_STRIDE), :] = slab

    @pl.when(b == 0)
    def _():
        for cp in weight_copies(blk_e_ref[0]):
            cp.start()
        gather(tok_ref, 0)

    @pl.when(b < nu_ref[0])
    def _():
        @pl.when(first_ref[b] == 1)
        def _():
            for cp in weight_copies(blk_e_ref[b]):
                cp.wait()
            wgu_bf[...] = wgu_st[...].astype(BF16)
            wd_bf[...] = wd_st[...].astype(BF16)

            @pl.when(next_e_ref[b] >= 0)
            def _():
                for cp in weight_copies(next_e_ref[b]):
                    cp.start()

        lo, hi = [], []
        for c in range(PACK_ROWS):
            bits = lax.bitcast_convert_type(
                tile[slot, c * GATHER_STRIDE:c * GATHER_STRIDE + MOE_M, :], jnp.uint32)
            lo.append(lax.bitcast_convert_type(bits << 16, F32).astype(BF16))
            hi.append(lax.bitcast_convert_type(bits & jnp.uint32(0xFFFF0000), F32).astype(BF16))
        x = jnp.concatenate(lo + hi, axis=1)
        gather(tok_next_ref, 1 - slot)

        gu = jnp.dot(x, wgu_bf[...], preferred_element_type=F32) + bgu_ref[0, 0]
        gate = jnp.minimum(gu[:, 0:D_EXPERT], SWIGLU_LIMIT)
        up = jnp.clip(gu[:, D_EXPERT:2 * D_EXPERT], -SWIGLU_LIMIT, SWIGLU_LIMIT)
        glu = gate * _sigmoid(SWIGLU_ALPHA * gate)
        mid = ((up + 1.0) * glu).astype(BF16)
        y = jnp.dot(mid, wd_bf[...], preferred_element_type=F32) + bd_ref[0, 0]
        y_ref[...] = y.astype(BF16)

    @pl.when(b >= nu_ref[0])
    def _():
        y_ref[...] = jnp.zeros_like(y_ref)


def _moe_experts(blk_e, first, next_e, n_used, row_tok3, xp, w_gu, b_gu, w_down, b_down, layer):
    bias = lambda b, e, f, ne, nu: (layer, e[b], 0, 0)
    grid_spec = pltpu.PrefetchScalarGridSpec(
        num_scalar_prefetch=4,
        grid=(N_BLOCKS,),
        in_specs=[pl.BlockSpec((1, 1, MOE_M), lambda b, e, f, ne, nu: (b, 0, 0), memory_space=pltpu.SMEM),
                  pl.BlockSpec((1, 1, MOE_M), lambda b, e, f, ne, nu: (jnp.minimum(b + 1, N_BLOCKS - 1), 0, 0),
                               memory_space=pltpu.SMEM),
                  pl.BlockSpec((T * PACK_ROWS, LANES), lambda b, e, f, ne, nu: (0, 0), pipeline_mode=pl.Buffered(1)),
                  pl.BlockSpec(memory_space=pl.ANY),
                  pl.BlockSpec((1, 1, 1, 2 * D_EXPERT), bias),
                  pl.BlockSpec(memory_space=pl.ANY),
                  pl.BlockSpec((1, 1, 1, D_MODEL), bias)],
        out_specs=pl.BlockSpec((MOE_M, D_MODEL), lambda b, e, f, ne, nu: (b, 0)),
        scratch_shapes=[pltpu.VMEM((2, PACK_ROWS * GATHER_STRIDE, LANES), jnp.int32),
                        pltpu.VMEM((D_MODEL, 2 * D_EXPERT), F32),
                        pltpu.VMEM((D_EXPERT, D_MODEL), F32),
                        pltpu.VMEM((D_MODEL, 2 * D_EXPERT), BF16),
                        pltpu.VMEM((D_EXPERT, D_MODEL), BF16),
                        pltpu.SemaphoreType.DMA((2,))],
    )
    return pl.pallas_call(
        functools.partial(_moe_kernel, layer=layer),
        grid_spec=grid_spec,
        out_shape=jax.ShapeDtypeStruct((N_ROWS, D_MODEL), BF16),
        compiler_params=pltpu.CompilerParams(vmem_limit_bytes=48 * MIB),
        name="moe_experts",
    )(blk_e, first, next_e, n_used, row_tok3, row_tok3, xp, w_gu, b_gu.reshape(DEPTH, N_EXPERTS, 1, 2 * D_EXPERT),
      w_down, b_down.reshape(DEPTH, N_EXPERTS, 1, D_MODEL))


SCATTER_UNROLL = 8


def _combine_kernel(nu_ref, tok_ref, w_ref, y_ref, o_ref, acc, tile, sem):
    s = pl.program_id(0)

    @pl.when(s == 0)
    def _():
        acc[...] = jnp.zeros_like(acc)

    def sub_block(sb, carry):
        r0 = pl.multiple_of(sb * SCATTER_M, SCATTER_M)
        y = y_ref[pl.ds(r0, SCATTER_M), :].astype(F32)
        for c in range(ROW_VREGS):
            tile[c * SCATTER_STRIDE:c * SCATTER_STRIDE + SCATTER_M, :] = y[:, c * LANES:(c + 1) * LANES]
        for m0 in range(0, SCATTER_M, SCATTER_UNROLL):
            addrs, vals = [], []
            for u in range(SCATTER_UNROLL):
                mi = m0 + u
                a = pl.multiple_of(tok_ref[0, 0, r0 + mi] * ROW_VREGS, ROW_VREGS)
                yv = tile[pl.ds(mi, ROW_VREGS, stride=SCATTER_STRIDE), :]
                addrs.append(a)
                vals.append(acc[pl.ds(a, ROW_VREGS), :] + w_ref[0, 0, r0 + mi] * yv)
            for u in range(SCATTER_UNROLL):
                acc[pl.ds(addrs[u], ROW_VREGS), :] = vals[u]
        return carry

    @pl.when(s * COMBINE_BLOCKS < nu_ref[0])
    def _():
        lax.fori_loop(0, COMBINE_BLOCKS * MOE_M // SCATTER_M, sub_block, 0)

    @pl.when(s == pl.num_programs(0) - 1)
    def _():
        cp = pltpu.make_async_copy(acc.at[pl.ds(0, T * ROW_VREGS)], o_ref, sem)
        cp.start()
        cp.wait()


def _combine(n_used, row_tok3, row_w3, yr):
    rows = COMBINE_BLOCKS * MOE_M
    steps = N_BLOCKS // COMBINE_BLOCKS
    grid_spec = pltpu.PrefetchScalarGridSpec(
        num_scalar_prefetch=1,
        grid=(steps,),
        in_specs=[pl.BlockSpec((1, 1, rows), lambda s, nu: (s, 0, 0), memory_space=pltpu.SMEM),
                  pl.BlockSpec((1, 1, rows), lambda s, nu: (s, 0, 0), memory_space=pltpu.SMEM),
                  pl.BlockSpec((rows, D_MODEL), lambda s, nu: (s, 0))],
        out_specs=pl.BlockSpec(memory_space=pl.ANY),
        scratch_shapes=[pltpu.VMEM(((T + 1) * ROW_VREGS, LANES), F32),
                        pltpu.VMEM((ROW_VREGS * SCATTER_STRIDE, LANES), F32),
                        pltpu.SemaphoreType.DMA(())],
    )
    return pl.pallas_call(
        _combine_kernel,
        grid_spec=grid_spec,
        out_shape=jax.ShapeDtypeStruct((T * ROW_VREGS, LANES), F32),
        compiler_params=pltpu.CompilerParams(vmem_limit_bytes=48 * MIB),
        name="moe_combine",
    )(n_used, row_tok3.reshape(steps, 1, rows), row_w3.reshape(steps, 1, rows), yr)


def _route(top_e, top_w, counts):
    experts = jnp.arange(N_EXPERTS, dtype=jnp.int32)
    padded = (counts + MOE_M - 1) // MOE_M * MOE_M
    pad_end = jnp.cumsum(padded)
    pad_start = pad_end - padded
    n_used = (pad_end[-1] // MOE_M).astype(jnp.int32)
    n_pad = N_ROWS - N_ASSIGN
    pad_cum = jnp.cumsum(padded - counts)
    pad_expert = jnp.sum((pad_cum[:, None] <= jnp.arange(n_pad, dtype=jnp.int32)[None, :]).astype(jnp.int32), axis=0)
    tok = jnp.broadcast_to(jnp.arange(T, dtype=jnp.int32)[:, None], (T, TOP_K))
    keys = jnp.concatenate([((2 * top_e) << TOK_BITS | tok).reshape(N_ASSIGN),
                            (2 * pad_expert + 1) << TOK_BITS | T])
    wts = jnp.concatenate([top_w.reshape(N_ASSIGN), jnp.zeros((n_pad,), F32)])
    keys, row_w = lax.sort((keys, wts), num_keys=1)
    row_tok = keys & ((1 << TOK_BITS) - 1)
    blk0 = jnp.arange(N_BLOCKS, dtype=jnp.int32) * MOE_M
    blk_e = jnp.minimum(jnp.sum((pad_end[:, None] <= blk0[None, :]).astype(jnp.int32), axis=0), N_EXPERTS - 1)
    first = (blk0 == pad_start[blk_e]).astype(jnp.int32)
    later = (experts[None, :] > experts[:, None]) & (counts[None, :] > 0)
    nxt = jnp.min(jnp.where(later, experts[None, :], N_EXPERTS), axis=1)
    next_e = jnp.where(nxt == N_EXPERTS, -1, nxt)[blk_e].astype(jnp.int32)
    return (row_tok.reshape(N_BLOCKS, 1, MOE_M), row_w.reshape(N_BLOCKS, 1, MOE_M), blk_e.astype(jnp.int32),
            first, next_e, n_used.reshape(1))


def _residual_kernel(*refs, final):
    x_ref, moe_ref, mod_ref, g_ref = refs[0:4]
    moe = jnp.concatenate([moe_ref[pl.ds(c, TM, stride=ROW_VREGS), :] for c in range(ROW_VREGS)], axis=1)
    x = x_ref[...] + mod_ref[0, 5:6, :] * moe
    if not final:
        refs[4][...] = x
        return
    ms = jnp.mean(x * x, axis=-1, keepdims=True)
    y = x * lax.rsqrt(ms + EPS) * g_ref[...]
    yc_o, yl_o = refs[4:6]
    i = pl.program_id(0)

    @pl.when(i < N_CTX_TILES)
    def _():
        yc_o[...] = y

    @pl.when(i >= N_CTX_TILES)
    def _():
        yl_o[...] = y


def _residual(x1, moe_tiles, mod_l, g, final):
    row = lambda i: (i, 0)
    if final:
        out_specs = [pl.BlockSpec((TM, D_MODEL), _ctx_tile), pl.BlockSpec((TM, D_MODEL), _lat_tile)]
        out_shape = [jax.ShapeDtypeStruct((T_CTX, D_MODEL), F32), jax.ShapeDtypeStruct((T_LAT, D_MODEL), F32)]
    else:
        out_specs = pl.BlockSpec((TM, D_MODEL), row)
        out_shape = jax.ShapeDtypeStruct((T, D_MODEL), F32)
    return pl.pallas_call(
        functools.partial(_residual_kernel, final=final),
        grid=(T // TM,),
        in_specs=[pl.BlockSpec((TM, D_MODEL), row),
                  pl.BlockSpec((TM * ROW_VREGS, LANES), row),
                  pl.BlockSpec((1, 6, D_MODEL), lambda i: (_mod_group(i), 0, 0)),
                  pl.BlockSpec((1, D_MODEL), lambda i: (0, 0))],
        out_specs=out_specs,
        out_shape=out_shape,
        compiler_params=pltpu.CompilerParams(vmem_limit_bytes=32 * MIB),
        name="residual_final" if final else "residual",
    )(x1, moe_tiles, mod_l, g)


def _rope_tables():
    t = np.arange(DEC_SEQ)
    pos = np.stack([t // GRID_W, t % GRID_W], axis=1).astype(np.float32)
    half = HEAD_DIM // 2
    inv = jnp.asarray(ROPE_BASE, F32) ** (-jnp.arange(0, half, 2, dtype=F32) / half)
    d = np.arange(HEAD_DIM)
    which = d // half
    freq = d % (half // 2)
    sign = np.where((d % half) < half // 2, -1.0, 1.0).astype(np.float32)
    ang = jnp.asarray(pos)[:, which] * inv[freq][None, :]
    cos = jnp.cos(ang)
    sin = jnp.sin(ang) * jnp.asarray(sign)[None, :]
    reps = LANES // HEAD_DIM
    return jnp.tile(cos, (1, reps)), jnp.tile(sin, (1, reps))


def kernel(x_prompt, x_sample, cache_k, cache_v, state_ret, c, c_ctx, norm1_g, norm2_g, w_mod, b_mod, w_in, conv_w, attn_sink, ret_decay, w_a, w_b, w_c, w_o, w_router, b_router, w_gu, b_gu, w_down, b_down, final_g):
    xs = (x_prompt.reshape(T_CTX, D_MODEL), x_sample.reshape(T_LAT, D_MODEL))
    cond8 = jnp.zeros((8, D_MODEL), F32).at[0].set(c_ctx).at[1:1 + DEC_BATCH].set(c)
    mod = _modulation(cond8, w_mod, b_mod)
    cos, sin = _rope_tables()
    ck = cache_k.reshape(DEC_BATCH, DEPTH, PAST_LEN, D_KV)
    cv = cache_v.reshape(DEC_BATCH, DEPTH, PAST_LEN, D_KV)

    ks, vs, rs = [], [], []
    for l in range(DEPTH):
        mod_l = mod[l, 0:1 + DEC_BATCH].reshape(1 + DEC_BATCH, 6, D_MODEL)
        conv_in, q, k, v, ret, gates = _inproj(xs, mod_l, norm1_g[l][None, :], w_in, l)
        rd8 = jnp.broadcast_to(ret_decay[l].reshape(2 * RET_HEADS, 1), (2 * RET_HEADS, LANES))
        *ys_ctx, rfin = _mixers(conv_in, q, k, v, ret, conv_w[l], attn_sink[l], rd8, latent=False)
        ys_lat = _mixers(conv_in, q, k, v, ret, conv_w[l], attn_sink[l], rd8, latent=True,
                         cos=cos, sin=sin, cache_k=ck, cache_v=cv, state=state_ret, layer=l)
        wr =jnp.pad(w_router[l], ((0, 0), (0, LANES - N_EXPERTS)))
        wrh = wr.astype(BF16)
        wrl = (wr - wrh.astype(F32)).astype(BF16)
        br = jnp.pad(b_router[l], (0, LANES - N_EXPERTS))[None, :]
        x1, xp, top_e, top_w, cnt = _merge(xs, ys_ctx, ys_lat, gates, mod_l, norm2_g[l][None, :], w_a, w_b, w_c,
                                           w_o, wrh, wrl, br, l)
        row_tok, row_w, blk_e, first, next_e, n_used = _route(top_e[:, 0:TOP_K], top_w[:, 0:TOP_K],
                                                              cnt[0, 0:N_EXPERTS])
        yrows = _moe_experts(blk_e, first, next_e, n_used, row_tok, xp, w_gu, b_gu, w_down, b_down, l)
        moe = _combine(n_used, row_tok, row_w, yrows)
        final = l == DEPTH - 1
        xs = _residual(x1, moe, mod_l, final_g[None, :], final)
        if not final:
            xs = (xs,)
        ks.append(k[0:T_CTX].reshape(BATCH, SEQ, N_KV, HEAD_DIM))
        vs.append(v[0:T_CTX].reshape(BATCH, SEQ, N_KV, HEAD_DIM))
        rs.append(rfin)

    y_prompt = xs[0].reshape(BATCH, SEQ, D_MODEL)
    y_sample = xs[1].reshape(DEC_BATCH, DEC_SEQ, D_MODEL)
    return (y_prompt, y_sample, jnp.stack(ks, axis=1), jnp.stack(vs, axis=1), jnp.stack(rs, axis=1))
```

```python
import functools

import numpy as np
import jax
import jax.numpy as jnp
from jax import lax
from jax.experimental import pallas as pl
from jax.experimental.pallas import tpu as pltpu

F32 = jnp.float32
BF16 = jnp.bfloat16

D_MODEL = 1024
BATCH = 16
SEQ = 256
DEPTH = 2
DEC_BATCH = 2
DEC_SEQ = 2048
PAST_LEN = 256
GRID_W = 64
HEAD_DIM = 64
D_CONV = 256
N_HEADS = 8
N_KV = 2
GROUP = N_HEADS // N_KV
WINDOW = 128
ROPE_BASE = 10000.0
RET_HEADS = 4
RET_DK = 64
RET_DV = 64
CHUNK = 128
N_EXPERTS = 32
TOP_K = 4
D_EXPERT = D_MODEL
SWIGLU_LIMIT = 7.0
SWIGLU_ALPHA = 1.702
EPS = 1e-6
NEG_INF = -1e30

T_CTX = BATCH * SEQ
T_LAT = DEC_BATCH * DEC_SEQ
T = T_CTX + T_LAT
D_ATTN = N_HEADS * HEAD_DIM
D_KV = N_KV * HEAD_DIM
D_RET = RET_HEADS * RET_DK
C_CONV = 0
C_Q = 3 * D_CONV
C_K = C_Q + D_ATTN
C_V = C_K + D_KV
C_RET = C_V + D_KV
C_GATE = C_RET + 4 * D_RET
IN_COLS = C_GATE + 3 * D_MODEL

TM = 512
MOE_M = 256
N_ASSIGN = T * TOP_K
N_BLOCKS = (N_ASSIGN + N_EXPERTS * (MOE_M - 1) + MOE_M - 1) // MOE_M
N_ROWS = N_BLOCKS * MOE_M
LANES = 128
SUBLANES = 8
ROW_VREGS = D_MODEL // LANES
PACK_ROWS = ROW_VREGS // 2
GATHER_STRIDE = MOE_M + SUBLANES
SCATTER_M = 128
SCATTER_STRIDE = SCATTER_M + SUBLANES
COMBINE_BLOCKS = 2
TOK_BITS = 14
assert T < (1 << TOK_BITS)
MIB = 1024 * 1024


def _sigmoid(x):
    return 1.0 / (1.0 + jnp.exp(-x))


def _mod_group(i):
    n_ctx = T_CTX // TM
    per_lat = DEC_SEQ // TM
    g = jnp.zeros_like(i)
    for b in range(DEC_BATCH):
        g = g + (i >= n_ctx + b * per_lat).astype(jnp.int32)
    return g


def _mod_kernel(cond_ref, w_ref, b_ref, o_ref):
    c = cond_ref[...]
    s = c * _sigmoid(c)
    o_ref[0] = jnp.dot(s.astype(BF16), w_ref[0].astype(BF16), preferred_element_type=F32) + b_ref[0]


def _modulation(cond8, w_mod, b_mod):
    n_col = 4
    cw = 6 * D_MODEL // n_col
    return pl.pallas_call(
        _mod_kernel,
        grid=(DEPTH, n_col),
        in_specs=[pl.BlockSpec((8, D_MODEL), lambda l, j: (0, 0)),
                  pl.BlockSpec((1, D_MODEL, cw), lambda l, j: (l, 0, j)),
                  pl.BlockSpec((1, 1, cw), lambda l, j: (l, 0, j))],
        out_specs=pl.BlockSpec((1, 8, cw), lambda l, j: (l, 0, j)),
        out_shape=jax.ShapeDtypeStruct((DEPTH, 8, 6 * D_MODEL), F32),
        compiler_params=pltpu.CompilerParams(vmem_limit_bytes=32 * MIB),
        name="modulation",
    )(cond8, w_mod, b_mod.reshape(DEPTH, 1, 6 * D_MODEL))


N_CTX_TILES = T_CTX // TM


def _ctx_tile(i):
    return (jnp.minimum(i, N_CTX_TILES - 1), 0)


def _lat_tile(i):
    return (jnp.maximum(i - N_CTX_TILES, 0), 0)


def _pick(ctx_ref, lat_ref):
    return jnp.where(pl.program_id(0) < N_CTX_TILES, ctx_ref[...], lat_ref[...])


def _inproj_kernel(*refs, split):
    if split:
        xc_ref, xl_ref, mod_ref, g_ref, w_ref, conv_o, q_o, k_o, v_o, ret_o, gate_o = refs
        x = _pick(xc_ref, xl_ref)
    else:
        x_ref, mod_ref, g_ref, w_ref, conv_o, q_o, k_o, v_o, ret_o, gate_o = refs
        x = x_ref[...]
    ms = jnp.mean(x * x, axis=-1, keepdims=True)
    h = x * lax.rsqrt(ms + EPS) * g_ref[...]
    h = h * (1.0 + mod_ref[0, 1:2, :]) + mod_ref[0, 0:1, :]
    hb = h.astype(BF16)

    def proj(c0, c1):
        return jnp.dot(hb, w_ref[0, :, c0:c1].astype(BF16), preferred_element_type=F32)

    conv_o[...] = proj(C_CONV, C_Q).astype(BF16)
    q_o[...] = (proj(C_Q, C_K) * HEAD_DIM ** -0.5).astype(BF16)
    k_o[...] = proj(C_K, C_V)
    v_o[...] = proj(C_V, C_RET)
    ret_o[:, 0:D_RET] = proj(C_RET, C_RET + D_RET).astype(BF16)
    ret_o[:, D_RET:2 * D_RET] = (proj(C_RET + D_RET, C_RET + 2 * D_RET) * RET_DK ** -0.5).astype(BF16)
    ret_o[:, 2 * D_RET:4 * D_RET] = proj(C_RET + 2 * D_RET, C_GATE).astype(BF16)
    for b in range(3):
        g = proj(C_GATE + b * D_MODEL, C_GATE + (b + 1) * D_MODEL)
        gate_o[:, b * D_MODEL:(b + 1) * D_MODEL] = _sigmoid(g).astype(BF16)


def _x_specs(xs):
    if len(xs) == 2:
        return [pl.BlockSpec((TM, D_MODEL), _ctx_tile), pl.BlockSpec((TM, D_MODEL), _lat_tile)]
    return [pl.BlockSpec((TM, D_MODEL), lambda i: (i, 0))]


def _inproj(xs, mod_l, g1, w_in, layer):
    row = lambda i: (i, 0)
    return pl.pallas_call(
        functools.partial(_inproj_kernel, split=len(xs) == 2),
        grid=(T // TM,),
        in_specs=_x_specs(xs) + [
                  pl.BlockSpec((1, 6, D_MODEL), lambda i: (_mod_group(i), 0, 0)),
                  pl.BlockSpec((1, D_MODEL), lambda i: (0, 0)),
                  pl.BlockSpec((1, D_MODEL, IN_COLS), lambda i: (layer, 0, 0), pipeline_mode=pl.Buffered(1))],
        out_specs=[pl.BlockSpec((TM, 3 * D_CONV), row),
                   pl.BlockSpec((TM, D_ATTN), row),
                   pl.BlockSpec((TM, D_KV), row),
                   pl.BlockSpec((TM, D_KV), row),
                   pl.BlockSpec((TM, 4 * D_RET), row),
                   pl.BlockSpec((TM, 3 * D_MODEL), row)],
        out_shape=[jax.ShapeDtypeStruct((T, 3 * D_CONV), BF16),
                   jax.ShapeDtypeStruct((T, D_ATTN), BF16),
                   jax.ShapeDtypeStruct((T, D_KV), F32),
                   jax.ShapeDtypeStruct((T, D_KV), F32),
                   jax.ShapeDtypeStruct((T, 4 * D_RET), BF16),
                   jax.ShapeDtypeStruct((T, 3 * D_MODEL), BF16)],
        compiler_params=pltpu.CompilerParams(vmem_limit_bytes=56 * MIB),
        name="inproj",
    )(*xs, mod_l, g1, w_in)


def _rope(x, cos, sin_signed):
    lane = lax.broadcasted_iota(jnp.int32, x.shape, 1)
    first = (lane % 32) < 16
    partner = jnp.where(first, pltpu.roll(x, x.shape[1] - 16, 1), pltpu.roll(x, 16, 1))
    return x * cos + partner * sin_signed


def _dot_nt(a, b):
    return lax.dot_general(a, b, (((1,), (1,)), ((), ())), preferred_element_type=F32)


def _dot_tn(a, b):
    return lax.dot_general(a, b, (((0,), (0,)), ((), ())), preferred_element_type=F32)


def _mixer_kernel_v1(*refs, n, latent):
    if latent:
        (sink_ref, conv_ref, q_ref, k_ref, v_ref, ret_ref, cw_ref, rd_ref, cos_ref, sin_ref, ck_ref, cv_ref,
         r0_ref, yc_ref, ya_ref, yr_ref, kb, vb, o_f, o_b, rst, dmat, qdec, kdec) = refs
    else:
        (sink_ref, conv_ref, q_ref, k_ref, v_ref, ret_ref, cw_ref, rd_ref,
         yc_ref, ya_ref, yr_ref, rfin_ref, kb, vb, o_f, o_b, rst, dmat, qdec, kdec) = refs
    nb = n // CHUNK

    cv = conv_ref[...].astype(F32)
    cb, cc, cu = cv[:, 0:D_CONV], cv[:, D_CONV:2 * D_CONV], cv[:, 2 * D_CONV:3 * D_CONV]
    p = cc * cu
    row = lax.broadcasted_iota(jnp.int32, p.shape, 0)
    prev = jnp.where(row == 0, 0.0, pltpu.roll(p, 1, 0))
    nxt = jnp.where(row == n - 1, 0.0, pltpu.roll(p, n - 1, 0))
    cw = cw_ref[...]
    yc_ref[...] = (cb * (prev * cw[0:1, :] + p * cw[1:2, :] + nxt * cw[2:3, :])).astype(BF16)

    if latent:
        kr = _rope(k_ref[...], cos_ref[...], sin_ref[...])
        zpad = jnp.zeros((CHUNK, D_KV), BF16)
        kb[0:CHUNK, :] = zpad
        vb[0:CHUNK, :] = zpad
        kb[CHUNK + n:2 * CHUNK + n, :] = zpad
        vb[CHUNK + n:2 * CHUNK + n, :] = zpad
        kb[CHUNK:CHUNK + n, :] = kr.astype(BF16)
        vb[CHUNK:CHUNK + n, :] = v_ref[...].astype(BF16)
        ckb = ck_ref[0, 0].astype(BF16)
        cvb = cv_ref[0, 0].astype(BF16)
    else:
        kb[...] = k_ref[...].astype(BF16)
        vb[...] = v_ref[...].astype(BF16)

    def attn_block(j):
        r0 = j * CHUNK if isinstance(j, int) else pl.multiple_of(j * CHUNK, CHUNK)
        qj = q_ref[pl.ds(r0, CHUNK), :]
        if latent:
            cosj = cos_ref[pl.ds(r0, CHUNK), :]
            sinj = sin_ref[pl.ds(r0, CHUNK), :]
            qf = qj.astype(F32)
            qj = jnp.concatenate(
                [_rope(qf[:, c * LANES:(c + 1) * LANES], cosj, sinj) for c in range(D_ATTN // LANES)],
                axis=1).astype(BF16)
            kw = kb[pl.ds(r0, 3 * CHUNK), :]
            vw = vb[pl.ds(r0, 3 * CHUNK), :]
            qpos = r0 + lax.broadcasted_iota(jnp.int32, (CHUNK, 3 * CHUNK), 0)
            kpos = r0 - CHUNK + lax.broadcasted_iota(jnp.int32, (CHUNK, 3 * CHUNK), 1)
            ok = (jnp.abs(qpos - kpos) <= WINDOW) & (kpos >= 0) & (kpos < n)
            ok = jnp.concatenate([ok] * GROUP, axis=0)
        else:
            kw = kb[...]
            vw = vb[...]
        for g in range(N_KV):
            lo = g * HEAD_DIM
            qg = jnp.concatenate(
                [qj[:, (GROUP * g + i) * HEAD_DIM:(GROUP * g + i + 1) * HEAD_DIM] for i in range(GROUP)], axis=0)
            sk = jnp.concatenate(
                [jnp.full((CHUNK, 1), sink_ref[GROUP * g + i], F32) for i in range(GROUP)], axis=0)
            s = _dot_nt(qg, kw[:, lo:lo + HEAD_DIM])
            if latent:
                s = jnp.where(ok, s, NEG_INF)
                s2 = _dot_nt(qg, ckb[:, lo:lo + HEAD_DIM])
            m = jnp.maximum(jnp.max(s, axis=-1, keepdims=True), sk)
            if latent:
                m = jnp.maximum(m, jnp.max(s2, axis=-1, keepdims=True))
            pw = jnp.exp(s - m)
            den = jnp.sum(pw, axis=-1, keepdims=True) + jnp.exp(sk - m)
            o = jnp.dot(pw.astype(BF16), vw[:, lo:lo + HEAD_DIM], preferred_element_type=F32)
            if latent:
                p2 = jnp.exp(s2 - m)
                den = den + jnp.sum(p2, axis=-1, keepdims=True)
                o = o + jnp.dot(p2.astype(BF16), cvb[:, lo:lo + HEAD_DIM], preferred_element_type=F32)
            o = o / den
            ya_ref[pl.ds(r0, CHUNK), g * GROUP * HEAD_DIM:(g + 1) * GROUP * HEAD_DIM] = jnp.concatenate(
                [o[i * CHUNK:(i + 1) * CHUNK, :] for i in range(GROUP)], axis=1).astype(BF16)

    if latent:
        def attn_body(j, carry):
            attn_block(j)
            return carry
        lax.fori_loop(0, nb, attn_body, 0)
    else:
        for j in range(nb):
            attn_block(j)

    rd = rd_ref[...]
    log_g = jnp.minimum(rd, 0.0) - jnp.log(1.0 + jnp.exp(-jnp.abs(rd)))
    ii = lax.broadcasted_iota(jnp.int32, (CHUNK, CHUNK), 0).astype(F32)
    jj = lax.broadcasted_iota(jnp.int32, (CHUNK, CHUNK), 1).astype(F32)
    for d in range(2):
        for h in range(RET_HEADS):
            r = d * RET_HEADS + h
            lg = log_g[r:r + 1, :]
            diff = (ii - jj) if d == 0 else (jj - ii)
            dmat[r] = jnp.where(diff >= 0, jnp.exp(jnp.maximum(diff, 0.0) * lg), 0.0)
            if d == 0:
                qdec[r] = jnp.exp((ii + 1.0) * lg)
                kdec[r] = jnp.exp((CHUNK - 1.0 - ii) * lg)
            else:
                qdec[r] = jnp.exp((CHUNK - ii) * lg)
                kdec[r] = jnp.exp(ii * lg)
            if latent:
                rst[r] = r0_ref[0, 0, d, h]
            else:
                rst[r] = jnp.zeros((RET_DK, RET_DV), F32)
    chunk_decay = jnp.exp(float(CHUNK) * log_g)

    def ret_chunk(c0, d, out_ref):
        blk = ret_ref[pl.ds(c0, CHUNK), :]
        outs = []
        for h in range(RET_HEADS):
            r = d * RET_HEADS + h
            qc = blk[:, h * RET_DK:(h + 1) * RET_DK]
            kc = blk[:, D_RET + h * RET_DK:D_RET + (h + 1) * RET_DK]
            vc = blk[:, 2 * D_RET + h * RET_DV:2 * D_RET + (h + 1) * RET_DV]
            inner = (_dot_nt(qc, kc) * dmat[r]).astype(BF16)
            state = rst[r]
            o = jnp.dot(inner, vc, preferred_element_type=F32)
            o = o + jnp.dot(qc, state.astype(BF16), preferred_element_type=F32) * qdec[r][:, 0:RET_DV]
            kd = (kc.astype(F32) * kdec[r][:, 0:RET_DK]).astype(BF16)
            rst[r] = state * chunk_decay[r:r + 1, 0:RET_DV] + _dot_tn(kd, vc)
            outs.append(o)
        out_ref[pl.ds(c0, CHUNK), :] = jnp.concatenate(outs, axis=1)

    def ret_body(j, carry):
        ret_chunk(pl.multiple_of(j * CHUNK, CHUNK), 0, o_f)
        ret_chunk(pl.multiple_of((nb - 1 - j) * CHUNK, CHUNK), 1, o_b)
        return carry

    lax.fori_loop(0, nb, ret_body, 0)

    def norm_body(j, carry):
        c0 = pl.multiple_of(j * CHUNK, CHUNK)
        o = o_f[pl.ds(c0, CHUNK), :] + o_b[pl.ds(c0, CHUNK), :]
        parts = []
        for h in range(RET_HEADS):
            oh = o[:, h * RET_DV:(h + 1) * RET_DV]
            ms = jnp.mean(oh * oh, axis=-1, keepdims=True)
            parts.append(oh * lax.rsqrt(ms + EPS))
        rg = ret_ref[pl.ds(c0, CHUNK), 3 * D_RET:4 * D_RET].astype(F32)
        yr_ref[pl.ds(c0, CHUNK), :] = (rg * _sigmoid(rg) * jnp.concatenate(parts, axis=1)).astype(BF16)
        return carry

    lax.fori_loop(0, nb, norm_body, 0)

    if not latent:
        for d in range(2):
            for h in range(RET_HEADS):
                rfin_ref[0, d, h] = rst[d * RET_HEADS + h]


HALF = LANES // 2
assert HEAD_DIM == HALF and RET_DK == HALF and RET_DV == HALF
N_PAIRS = N_HEADS // 2
RET_PAIRS = RET_HEADS // 2


def _row_variants(kt):
    row = lax.broadcasted_iota(jnp.int32, kt.shape, 0)
    lo0 = jnp.where(row < HALF, kt, 0.0)
    hi1 = jnp.where(row >= HALF, kt, 0.0)
    return lo0, pltpu.roll(lo0, HALF, 0), pltpu.roll(hi1, HALF, 0), hi1


def _lane_variants(v):
    lane = lax.broadcasted_iota(jnp.int32, v.shape, 1)
    lo0 = jnp.where(lane < HALF, v, 0.0)
    hi1 = jnp.where(lane >= HALF, v, 0.0)
    return lo0, pltpu.roll(lo0, HALF, 1), pltpu.roll(hi1, HALF, 1), hi1


def _mixer_kernel(*refs, n, latent):
    if latent:
        (sink_ref, conv_ref, q_ref, k_ref, v_ref, ret_ref, cw_ref, rd_ref, cos_ref, sin_ref, ck_ref, cv_ref,
         r0_ref, yc_ref, ya_ref, yr_ref, ktq, vq, rkp, o_f, o_b, rst, dmat, qdec, kdec, cdec) = refs
    else:
        (sink_ref, conv_ref, q_ref, k_ref, v_ref, ret_ref, cw_ref, rd_ref,
         yc_ref, ya_ref, yr_ref, rfin_ref, ktq, vq, rkp, o_f, o_b, rst, dmat, qdec, kdec, cdec) = refs
    nb = n // CHUNK
    pad = CHUNK if latent else 0

    cv = conv_ref[...].astype(F32)
    cb, cc, cu = cv[:, 0:D_CONV], cv[:, D_CONV:2 * D_CONV], cv[:, 2 * D_CONV:3 * D_CONV]
    p = cc * cu
    row = lax.broadcasted_iota(jnp.int32, p.shape, 0)
    prev = jnp.where(row == 0, 0.0, pltpu.roll(p, 1, 0))
    nxt = jnp.where(row == n - 1, 0.0, pltpu.roll(p, n - 1, 0))
    cw = cw_ref[...]
    yc_ref[...] = (cb * (prev * cw[0:1, :] + p * cw[1:2, :] + nxt * cw[2:3, :])).astype(BF16)

    kf = k_ref[...]
    if latent:
        kf = _rope(kf, cos_ref[...], sin_ref[...])
    for idx, (kk, vv) in enumerate(zip(_row_variants(kf.T), _lane_variants(v_ref[...]))):
        if latent:
            ktq[idx, :, 0:pad] = jnp.zeros((LANES, pad), BF16)
            ktq[idx, :, pad + n:2 * pad + n] = jnp.zeros((LANES, pad), BF16)
            vq[idx, 0:pad, :] = jnp.zeros((pad, LANES), BF16)
            vq[idx, pad + n:2 * pad + n, :] = jnp.zeros((pad, LANES), BF16)
        ktq[idx, :, pad:pad + n] = kk.astype(BF16)
        vq[idx, pad:pad + n, :] = vv.astype(BF16)
    if latent:
        cktq = [t.astype(BF16) for t in _row_variants(ck_ref[0, 0].T)]
        cvq = [t.astype(BF16) for t in _lane_variants(cv_ref[0, 0])]

    def attn(r0, rows):
        qj = q_ref[pl.ds(r0, rows), :]
        if latent:
            cosj = cos_ref[pl.ds(r0, rows), :]
            sinj = sin_ref[pl.ds(r0, rows), :]
            qpos = r0 + lax.broadcasted_iota(jnp.int32, (rows, 3 * CHUNK), 0)
            kpos = r0 - CHUNK + lax.broadcasted_iota(jnp.int32, (rows, 3 * CHUNK), 1)
            ok = (jnp.abs(qpos - kpos) <= WINDOW) & (kpos >= 0) & (kpos < n)
        for m in range(N_PAIRS):
            q2 = qj[:, m * LANES:(m + 1) * LANES]
            if latent:
                q2 = _rope(q2.astype(F32), cosj, sinj).astype(BF16)
            g = (2 * m) // GROUP
            acc = None
            for half in range(2):
                idx = 2 * g + half
                sk = sink_ref[2 * m + half]
                if latent:
                    s = jnp.dot(q2, ktq[idx, :, pl.ds(r0, 3 * CHUNK)], preferred_element_type=F32)
                    s = jnp.where(ok, s, NEG_INF)
                    s2 = jnp.dot(q2, cktq[idx], preferred_element_type=F32)
                    mx = jnp.maximum(jnp.maximum(jnp.max(s, axis=-1, keepdims=True),
                                                 jnp.max(s2, axis=-1, keepdims=True)), sk)
                    pw = jnp.exp(s - mx)
                    p2 = jnp.exp(s2 - mx)
                    den = (jnp.sum(pw, axis=-1, keepdims=True) + jnp.sum(p2, axis=-1, keepdims=True)
                           + jnp.exp(sk - mx))
                    o = (jnp.dot(pw.astype(BF16), vq[idx, pl.ds(r0, 3 * CHUNK), :], preferred_element_type=F32)
                         + jnp.dot(p2.astype(BF16), cvq[idx], preferred_element_type=F32))
                else:
                    s = jnp.dot(q2, ktq[idx], preferred_element_type=F32)
                    mx = jnp.maximum(jnp.max(s, axis=-1, keepdims=True), sk)
                    pw = jnp.exp(s - mx)
                    den = jnp.sum(pw, axis=-1, keepdims=True) + jnp.exp(sk - mx)
                    o = jnp.dot(pw.astype(BF16), vq[idx], preferred_element_type=F32)
                o = o / den
                acc = o if acc is None else acc + o
            ya_ref[pl.ds(r0, rows), m * LANES:(m + 1) * LANES] = acc.astype(BF16)

    rk_t = ret_ref[:, D_RET:2 * D_RET].astype(F32).T
    for m in range(RET_PAIRS):
        rkp[m] = rk_t[m * LANES:(m + 1) * LANES, :].astype(BF16)
    rd = rd_ref[...]
    log_g = jnp.minimum(rd, 0.0) - jnp.log(1.0 + jnp.exp(-jnp.abs(rd)))
    row_c = lax.broadcasted_iota(jnp.int32, (CHUNK, CHUNK), 0)
    lane_c = lax.broadcasted_iota(jnp.int32, (CHUNK, CHUNK), 1)
    ii = row_c.astype(F32)
    jj = lane_c.astype(F32)
    even_row = row_c < HALF
    even_lane = lane_c < HALF
    blockdiag = even_row == even_lane
    for d in range(2):
        for h in range(RET_HEADS):
            r = d * RET_HEADS + h
            diff = (ii - jj) if d == 0 else (jj - ii)
            dmat[r] = jnp.where(diff >= 0, jnp.exp(jnp.maximum(diff, 0.0) * log_g[r:r + 1, :]), 0.0)
        for m in range(RET_PAIRS):
            s = d * RET_PAIRS + m
            lg_e = log_g[d * RET_HEADS + 2 * m:d * RET_HEADS + 2 * m + 1, :]
            lg_o = log_g[d * RET_HEADS + 2 * m + 1:d * RET_HEADS + 2 * m + 2, :]
            qpow = (ii + 1.0) if d == 0 else (CHUNK - ii)
            kpow = (CHUNK - 1.0 - jj) if d == 0 else jj
            qdec[s] = jnp.where(even_lane, jnp.exp(qpow * lg_e), jnp.exp(qpow * lg_o))
            kdec[s] = jnp.where(even_row, jnp.exp(kpow * lg_e), jnp.exp(kpow * lg_o))
            chunk_decay = jnp.where(even_row, jnp.exp(float(CHUNK) * lg_e), jnp.exp(float(CHUNK) * lg_o))
            cdec[s] = jnp.where(blockdiag, chunk_decay, 0.0)
            if latent:
                z = jnp.zeros((HALF, HALF), F32)
                rst[s] = jnp.concatenate(
                    [jnp.concatenate([r0_ref[0, 0, d, 2 * m], z], axis=1),
                     jnp.concatenate([z, r0_ref[0, 0, d, 2 * m + 1]], axis=1)], axis=0)
            else:
                rst[s] = jnp.zeros((LANES, LANES), F32)

    def ret_chunk(c0, d, out_ref):
        for m in range(RET_PAIRS):
            s = d * RET_PAIRS + m
            q2 = ret_ref[pl.ds(c0, CHUNK), m * LANES:(m + 1) * LANES]
            v2 = ret_ref[pl.ds(c0, CHUNK), 2 * D_RET + m * LANES:2 * D_RET + (m + 1) * LANES]
            kt2 = rkp[m, :, pl.ds(c0, CHUNK)].astype(F32)
            v2f = v2.astype(F32)
            state = rst[s]
            o2 = jnp.dot(q2, state.astype(BF16), preferred_element_type=F32) * qdec[s]
            for half in range(2):
                r = d * RET_HEADS + 2 * m + half
                keep_row = even_row if half == 0 else jnp.logical_not(even_row)
                keep_lane = even_lane if half == 0 else jnp.logical_not(even_lane)
                a = jnp.dot(q2, jnp.where(keep_row, kt2, 0.0).astype(BF16), preferred_element_type=F32)
                inner = (a * dmat[r]).astype(BF16)
                o2 = o2 + jnp.dot(inner, jnp.where(keep_lane, v2f, 0.0).astype(BF16), preferred_element_type=F32)
            kd = (kt2 * kdec[s]).astype(BF16)
            upd = jnp.dot(kd, v2, preferred_element_type=F32)
            rst[s] = state * cdec[s] + jnp.where(blockdiag, upd, 0.0)
            out_ref[pl.ds(c0, CHUNK), m * LANES:(m + 1) * LANES] = o2

    gi = lax.broadcasted_iota(jnp.int32, (D_RET, D_RET), 0) // RET_DV
    gj = lax.broadcasted_iota(jnp.int32, (D_RET, D_RET), 1) // RET_DV
    group_mean = jnp.where(gi == gj, 1.0 / RET_DV, 0.0).astype(BF16)

    def norm(c0, rows):
        o = o_f[pl.ds(c0, rows), :] + o_b[pl.ds(c0, rows), :]
        sq = o * o
        hi = sq.astype(BF16)
        lo = (sq - hi.astype(F32)).astype(BF16)
        ms = (jnp.dot(hi, group_mean, preferred_element_type=F32)
              + jnp.dot(lo, group_mean, preferred_element_type=F32))
        rg = ret_ref[pl.ds(c0, rows), 3 * D_RET:4 * D_RET].astype(F32)
        yr_ref[pl.ds(c0, rows), :] = (rg * _sigmoid(rg) * (o * lax.rsqrt(ms + EPS))).astype(BF16)

    if latent:
        def scan_body(j, carry):
            attn(pl.multiple_of(j * CHUNK, CHUNK), CHUNK)
            ret_chunk(pl.multiple_of(j * CHUNK, CHUNK), 0, o_f)
            ret_chunk(pl.multiple_of((nb - 1 - j) * CHUNK, CHUNK), 1, o_b)
            return carry

        def norm_body(j, carry):
            norm(pl.multiple_of(j * CHUNK, CHUNK), CHUNK)
            return carry

        lax.fori_loop(0, nb, scan_body, 0)
        lax.fori_loop(0, nb, norm_body, 0)
    else:
        attn(0, n)
        for j in range(nb):
            ret_chunk(j * CHUNK, 0, o_f)
            ret_chunk((nb - 1 - j) * CHUNK, 1, o_b)
        norm(0, n)
        for d in range(2):
            for h in range(RET_HEADS):
                lo_ = (h % 2) * HALF
                rfin_ref[0, d, h] = rst[d * RET_PAIRS + h // 2][lo_:lo_ + HALF, lo_:lo_ + HALF]


def _mixers(conv_in, q, k, v, ret, conv_w_l, sink_l, rd8, *, latent, cos=None, sin=None, cache_k=None,
            cache_v=None, state=None, layer=0):
    n = DEC_SEQ if latent else SEQ
    nseq = DEC_BATCH if latent else BATCH
    off = T_CTX // n if latent else 0
    seq = lambda s: (s + off, 0)
    const = lambda s: (0, 0)
    in_specs = [pl.BlockSpec(memory_space=pltpu.SMEM),
                pl.BlockSpec((n, 3 * D_CONV), seq),
                pl.BlockSpec((n, D_ATTN), seq),
                pl.BlockSpec((n, D_KV), seq),
                pl.BlockSpec((n, D_KV), seq),
                pl.BlockSpec((n, 4 * D_RET), seq),
                pl.BlockSpec((3, D_CONV), const),
                pl.BlockSpec((8, LANES), const)]
    args = [sink_l, conv_in, q, k, v, ret, conv_w_l, rd8]
    out_specs = [pl.BlockSpec((n, D_CONV), lambda s: (s, 0)),
                 pl.BlockSpec((n, D_ATTN), lambda s: (s, 0)),
                 pl.BlockSpec((n, D_RET), lambda s: (s, 0))]
    out_shape = [jax.ShapeDtypeStruct((nseq * n, D_CONV), BF16),
                 jax.ShapeDtypeStruct((nseq * n, D_ATTN), BF16),
                 jax.ShapeDtypeStruct((nseq * n, D_RET), BF16)]
    if latent:
        in_specs += [pl.BlockSpec((n, LANES), const),
                     pl.BlockSpec((n, LANES), const),
                     pl.BlockSpec((1, 1, PAST_LEN, D_KV), lambda s: (s, layer, 0, 0)),
                     pl.BlockSpec((1, 1, PAST_LEN, D_KV), lambda s: (s, layer, 0, 0)),
                     pl.BlockSpec((1, 1, 2, RET_HEADS, RET_DK, RET_DV), lambda s: (s, layer, 0, 0, 0, 0))]
        args += [cos, sin, cache_k, cache_v, state]
        kv_rows = n + 2 * CHUNK
    else:
        out_specs.append(pl.BlockSpec((1, 2, RET_HEADS, RET_DK, RET_DV), lambda s: (s, 0, 0, 0, 0)))
        out_shape.append(jax.ShapeDtypeStruct((nseq, 2, RET_HEADS, RET_DK, RET_DV), F32))
        kv_rows = n
    scratch = [pltpu.VMEM((2 * N_KV, LANES, kv_rows), BF16),
               pltpu.VMEM((2 * N_KV, kv_rows, LANES), BF16),
               pltpu.VMEM((RET_PAIRS, LANES, n), BF16),
               pltpu.VMEM((n, D_RET), F32),
               pltpu.VMEM((n, D_RET), F32),
               pltpu.VMEM((2 * RET_PAIRS, LANES, LANES), F32),
               pltpu.VMEM((2 * RET_HEADS, CHUNK, CHUNK), F32),
               pltpu.VMEM((2 * RET_PAIRS, CHUNK, CHUNK), F32),
               pltpu.VMEM((2 * RET_PAIRS, CHUNK, CHUNK), F32),
               pltpu.VMEM((2 * RET_PAIRS, CHUNK, CHUNK), F32)]
    return pl.pallas_call(
        functools.partial(_mixer_kernel, n=n, latent=latent),
        grid=(nseq,),
        in_specs=in_specs,
        out_specs=out_specs,
        out_shape=out_shape,
        scratch_shapes=scratch,
        compiler_params=pltpu.CompilerParams(vmem_limit_bytes=56 * MIB),
        name="mixers_latent" if latent else "mixers_context",
    )(*args)


def _merge_kernel(*refs, split):
    if split:
        xc_ref, xl_ref = refs[0:2]
        x = _pick(xc_ref, xl_ref)
        refs = refs[2:]
    else:
        x = refs[0][...]
        refs = refs[1:]
    (ycc_ref, ycl_ref, yac_ref, yal_ref, yrc_ref, yrl_ref, gate_ref, mod_ref, g2_ref, wa_ref, wb_ref, wc_ref,
     wo_ref, wrh_ref, wrl_ref, br_ref, x1_o, xp_o, te_o, tw_o, cnt_o) = refs
    merged = (gate_ref[:, 0:D_MODEL].astype(F32)
              * jnp.dot(_pick(ycc_ref, ycl_ref), wa_ref[0].astype(BF16), preferred_element_type=F32)
              + gate_ref[:, D_MODEL:2 * D_MODEL].astype(F32)
              * jnp.dot(_pick(yac_ref, yal_ref), wb_ref[0].astype(BF16), preferred_element_type=F32)
              + gate_ref[:, 2 * D_MODEL:3 * D_MODEL].astype(F32)
              * jnp.dot(_pick(yrc_ref, yrl_ref), wc_ref[0].astype(BF16), preferred_element_type=F32))
    x1 = x + mod_ref[0, 2:3, :] * jnp.dot(merged.astype(BF16), wo_ref[0].astype(BF16),
                                          preferred_element_type=F32)
    x1_o[...] = x1
    ms = jnp.mean(x1 * x1, axis=-1, keepdims=True)
    h2 = x1 * lax.rsqrt(ms + EPS) * g2_ref[...]
    h2 = h2 * (1.0 + mod_ref[0, 4:5, :]) + mod_ref[0, 3:4, :]
    hh = h2.astype(BF16)
    hf = hh.astype(F32)
    bits = lax.bitcast_convert_type(hf, jnp.uint32)
    for c in range(PACK_ROWS):
        lo = bits[:, c * LANES:(c + 1) * LANES] >> 16
        hi = bits[:, (c + PACK_ROWS) * LANES:(c + PACK_ROWS + 1) * LANES] & jnp.uint32(0xFFFF0000)
        xp_o[pl.ds(c, TM, stride=PACK_ROWS), :] = lax.bitcast_convert_type(lo | hi, jnp.int32)
    hl = (h2 - hf).astype(BF16)
    logits = (jnp.dot(hh, wrh_ref[...], preferred_element_type=F32)
              + jnp.dot(hl, wrh_ref[...], preferred_element_type=F32)
              + jnp.dot(hh, wrl_ref[...], preferred_element_type=F32)
              + br_ref[...])
    lane = lax.broadcasted_iota(jnp.int32, logits.shape, 1)
    work = jnp.where(lane < N_EXPERTS, logits, -jnp.inf)
    vals, idxs = [], []
    for _ in range(TOP_K):
        m = jnp.max(work, axis=-1, keepdims=True)
        am = jnp.min(jnp.where(work == m, lane, LANES), axis=-1, keepdims=True)
        vals.append(m)
        idxs.append(am)
        work = jnp.where(lane == am, -jnp.inf, work)
    es = [jnp.exp(v - vals[0]) for v in vals]
    den = es[0] + es[1] + es[2] + es[3]
    te = jnp.zeros(logits.shape, jnp.int32)
    tw = jnp.zeros(logits.shape, F32)
    for k in range(TOP_K):
        te = jnp.where(lane == k, idxs[k], te)
        tw = jnp.where(lane == k, es[k] / den, tw)
    te_o[...] = te
    tw_o[...] = tw
    sel = (lane == idxs[0]) | (lane == idxs[1]) | (lane == idxs[2]) | (lane == idxs[3])
    part = jnp.sum(sel.astype(jnp.int32), axis=0, keepdims=True)

    @pl.when(pl.program_id(0) == 0)
    def _():
        cnt_o[...] = jnp.zeros_like(cnt_o)

    cnt_o[...] += jnp.broadcast_to(part, cnt_o.shape)


def _merge(xs, ys_ctx, ys_lat, gates, mod_l, g2, wa, wb, wc, wo, wrh, wrl, br, layer):
    row = lambda i: (i, 0)
    const = lambda i: (0, 0)
    wl = lambda i: (layer, 0, 0)
    y_specs, y_args = [], []
    for width, yc, yl in zip((D_CONV, D_ATTN, D_RET), ys_ctx, ys_lat):
        y_specs += [pl.BlockSpec((TM, width), _ctx_tile), pl.BlockSpec((TM, width), _lat_tile)]
        y_args += [yc, yl]
    return pl.pallas_call(
        functools.partial(_merge_kernel, split=len(xs) == 2),
        grid=(T // TM,),
        in_specs=_x_specs(xs) + y_specs + [
                  pl.BlockSpec((TM, 3 * D_MODEL), row),
                  pl.BlockSpec((1, 6, D_MODEL), lambda i: (_mod_group(i), 0, 0)),
                  pl.BlockSpec((1, D_MODEL), const),
                  pl.BlockSpec((1, D_CONV, D_MODEL), wl),
                  pl.BlockSpec((1, D_ATTN, D_MODEL), wl),
                  pl.BlockSpec((1, D_RET, D_MODEL), wl),
                  pl.BlockSpec((1, D_MODEL, D_MODEL), wl),
                  pl.BlockSpec((D_MODEL, LANES), const),
                  pl.BlockSpec((D_MODEL, LANES), const),
                  pl.BlockSpec((1, LANES), const)],
        out_specs=[pl.BlockSpec((TM, D_MODEL), row),
                   pl.BlockSpec((TM * PACK_ROWS, LANES), row),
                   pl.BlockSpec((TM, LANES), row),
                   pl.BlockSpec((TM, LANES), row),
                   pl.BlockSpec((SUBLANES, LANES), const)],
        out_shape=[jax.ShapeDtypeStruct((T, D_MODEL), F32),
                   jax.ShapeDtypeStruct((T * PACK_ROWS, LANES), jnp.int32),
                   jax.ShapeDtypeStruct((T, LANES), jnp.int32),
                   jax.ShapeDtypeStruct((T, LANES), F32),
                   jax.ShapeDtypeStruct((SUBLANES, LANES), jnp.int32)],
        compiler_params=pltpu.CompilerParams(vmem_limit_bytes=48 * MIB),
        name="merge_router",
    )(*xs, *y_args, gates, mod_l, g2, wa, wb, wc, wo, wrh, wrl, br)


def _moe_kernel(blk_e_ref, first_ref, next_e_ref, nu_ref, tok_ref, tok_next_ref, xp_ref, wgu_hbm, bgu_ref, wd_hbm,
                bd_ref, y_ref, tile, wgu_st, wd_st, wgu_bf, wd_bf, sems, *, layer):
    b = pl.program_id(0)
    slot = b % 2

    def weight_copies(e):
        return (pltpu.make_async_copy(wgu_hbm.at[layer, e], wgu_st, sems.at[0]),
                pltpu.make_async_copy(wd_hbm.at[layer, e], wd_st, sems.at[1]))

    def gather(tok, dst_slot):
        for mi in range(MOE_M):
            t = jnp.minimum(tok[0, 0, mi], T - 1)
            slab = xp_ref[pl.ds(pl.multiple_of(t * PACK_ROWS, PACK_ROWS), PACK_ROWS), :]
            tile[dst_slot, pl.ds(mi, PACK_ROWS, stride=GATHER_STRIDE), :] = slab

    @pl.when(b == 0)
    def _():
        for cp in weight_copies(blk_e_ref[0]):
            cp.start()
        gather(tok_ref, 0)

    @pl.when(b < nu_ref[0])
    def _():
        @pl.when(first_ref[b] == 1)
        def _():
            for cp in weight_copies(blk_e_ref[b]):
                cp.wait()
            wgu_bf[...] = wgu_st[...].astype(BF16)
            wd_bf[...] = wd_st[...].astype(BF16)

            @pl.when(next_e_ref[b] >= 0)
            def _():
                for cp in weight_copies(next_e_ref[b]):
                    cp.start()

        lo, hi = [], []
        for c in range(PACK_ROWS):
            bits = lax.bitcast_convert_type(
                tile[slot, c * GATHER_STRIDE:c * GATHER_STRIDE + MOE_M, :], jnp.uint32)
            lo.append(lax.bitcast_convert_type(bits << 16, F32).astype(BF16))
            hi.append(lax.bitcast_convert_type(bits & jnp.uint32(0xFFFF0000), F32).astype(BF16))
        x = jnp.concatenate(lo + hi, axis=1)
        gather(tok_next_ref, 1 - slot)

        gu = jnp.dot(x, wgu_bf[...], preferred_element_type=F32) + bgu_ref[0, 0]
        gate = jnp.minimum(gu[:, 0:D_EXPERT], SWIGLU_LIMIT)
        up = jnp.clip(gu[:, D_EXPERT:2 * D_EXPERT], -SWIGLU_LIMIT, SWIGLU_LIMIT)
        glu = gate * _sigmoid(SWIGLU_ALPHA * gate)
        mid = ((up + 1.0) * glu).astype(BF16)
        y = jnp.dot(mid, wd_bf[...], preferred_element_type=F32) + bd_ref[0, 0]
        y_ref[...] = y.astype(BF16)

    @pl.when(b >= nu_ref[0])
    def _():
        y_ref[...] = jnp.zeros_like(y_ref)


def _moe_experts(blk_e, first, next_e, n_used, row_tok3, xp, w_gu, b_gu, w_down, b_down, layer):
    bias = lambda b, e, f, ne, nu: (layer, e[b], 0, 0)
    grid_spec = pltpu.PrefetchScalarGridSpec(
        num_scalar_prefetch=4,
        grid=(N_BLOCKS,),
        in_specs=[pl.BlockSpec((1, 1, MOE_M), lambda b, e, f, ne, nu: (b, 0, 0), memory_space=pltpu.SMEM),
                  pl.BlockSpec((1, 1, MOE_M), lambda b, e, f, ne, nu: (jnp.minimum(b + 1, N_BLOCKS - 1), 0, 0),
                               memory_space=pltpu.SMEM),
                  pl.BlockSpec((T * PACK_ROWS, LANES), lambda b, e, f, ne, nu: (0, 0), pipeline_mode=pl.Buffered(1)),
                  pl.BlockSpec(memory_space=pl.ANY),
                  pl.BlockSpec((1, 1, 1, 2 * D_EXPERT), bias),
                  pl.BlockSpec(memory_space=pl.ANY),
                  pl.BlockSpec((1, 1, 1, D_MODEL), bias)],
        out_specs=pl.BlockSpec((MOE_M, D_MODEL), lambda b, e, f, ne, nu: (b, 0)),
        scratch_shapes=[pltpu.VMEM((2, PACK_ROWS * GATHER_STRIDE, LANES), jnp.int32),
                        pltpu.VMEM((D_MODEL, 2 * D_EXPERT), F32),
                        pltpu.VMEM((D_EXPERT, D_MODEL), F32),
                        pltpu.VMEM((D_MODEL, 2 * D_EXPERT), BF16),
                        pltpu.VMEM((D_EXPERT, D_MODEL), BF16),
                        pltpu.SemaphoreType.DMA((2,))],
    )
    return pl.pallas_call(
        functools.partial(_moe_kernel, layer=layer),
        grid_spec=grid_spec,
        out_shape=jax.ShapeDtypeStruct((N_ROWS, D_MODEL), BF16),
        compiler_params=pltpu.CompilerParams(vmem_limit_bytes=48 * MIB),
        name="moe_experts",
    )(blk_e, first, next_e, n_used, row_tok3, row_tok3, xp, w_gu, b_gu.reshape(DEPTH, N_EXPERTS, 1, 2 * D_EXPERT),
      w_down, b_down.reshape(DEPTH, N_EXPERTS, 1, D_MODEL))


SCATTER_UNROLL = 8


def _combine_kernel(nu_ref, tok_ref, w_ref, y_ref, o_ref, acc, tile, sem):
    s = pl.program_id(0)

    @pl.when(s == 0)
    def _():
        acc[...] = jnp.zeros_like(acc)

    def sub_block(sb, carry):
        r0 = pl.multiple_of(sb * SCATTER_M, SCATTER_M)
        y = y_ref[pl.ds(r0, SCATTER_M), :].astype(F32)
        for c in range(ROW_VREGS):
            tile[c * SCATTER_STRIDE:c * SCATTER_STRIDE + SCATTER_M, :] = y[:, c * LANES:(c + 1) * LANES]
        for m0 in range(0, SCATTER_M, SCATTER_UNROLL):
            addrs, vals = [], []
            for u in range(SCATTER_UNROLL):
                mi = m0 + u
                a = pl.multiple_of(tok_ref[0, 0, r0 + mi] * ROW_VREGS, ROW_VREGS)
                yv = tile[pl.ds(mi, ROW_VREGS, stride=SCATTER_STRIDE), :]
                addrs.append(a)
                vals.append(acc[pl.ds(a, ROW_VREGS), :] + w_ref[0, 0, r0 + mi] * yv)
            for u in range(SCATTER_UNROLL):
                acc[pl.ds(addrs[u], ROW_VREGS), :] = vals[u]
        return carry

    @pl.when(s * COMBINE_BLOCKS < nu_ref[0])
    def _():
        lax.fori_loop(0, COMBINE_BLOCKS * MOE_M // SCATTER_M, sub_block, 0)

    @pl.when(s == pl.num_programs(0) - 1)
    def _():
        cp = pltpu.make_async_copy(acc.at[pl.ds(0, T * ROW_VREGS)], o_ref, sem)
        cp.start()
        cp.wait()


def _combine(n_used, row_tok3, row_w3, yr):
    rows = COMBINE_BLOCKS * MOE_M
    steps = N_BLOCKS // COMBINE_BLOCKS
    grid_spec = pltpu.PrefetchScalarGridSpec(
        num_scalar_prefetch=1,
        grid=(steps,),
        in_specs=[pl.BlockSpec((1, 1, rows), lambda s, nu: (s, 0, 0), memory_space=pltpu.SMEM),
                  pl.BlockSpec((1, 1, rows), lambda s, nu: (s, 0, 0), memory_space=pltpu.SMEM),
                  pl.BlockSpec((rows, D_MODEL), lambda s, nu: (s, 0))],
        out_specs=pl.BlockSpec(memory_space=pl.ANY),
        scratch_shapes=[pltpu.VMEM(((T + 1) * ROW_VREGS, LANES), F32),
                        pltpu.VMEM((ROW_VREGS * SCATTER_STRIDE, LANES), F32),
                        pltpu.SemaphoreType.DMA(())],
    )
    return pl.pallas_call(
        _combine_kernel,
        grid_spec=grid_spec,
        out_shape=jax.ShapeDtypeStruct((T * ROW_VREGS, LANES), F32),
        compiler_params=pltpu.CompilerParams(vmem_limit_bytes=48 * MIB),
        name="moe_combine",
    )(n_used, row_tok3.reshape(steps, 1, rows), row_w3.reshape(steps, 1, rows), yr)


def _route(top_e, top_w, counts):
    experts = jnp.arange(N_EXPERTS, dtype=jnp.int32)
    padded = (counts + MOE_M - 1) // MOE_M * MOE_M
    pad_end = jnp.cumsum(padded)
    pad_start = pad_end - padded
    n_used = (pad_end[-1] // MOE_M).astype(jnp.int32)
    n_pad = N_ROWS - N_ASSIGN
    pad_cum = jnp.cumsum(padded - counts)
    pad_expert = jnp.sum((pad_cum[:, None] <= jnp.arange(n_pad, dtype=jnp.int32)[None, :]).astype(jnp.int32), axis=0)
    tok = jnp.broadcast_to(jnp.arange(T, dtype=jnp.int32)[:, None], (T, TOP_K))
    keys = jnp.concatenate([((2 * top_e) << TOK_BITS | tok).reshape(N_ASSIGN),
                            (2 * pad_expert + 1) << TOK_BITS | T])
    wts = jnp.concatenate([top_w.reshape(N_ASSIGN), jnp.zeros((n_pad,), F32)])
    keys, row_w = lax.sort((keys, wts), num_keys=1)
    row_tok = keys & ((1 << TOK_BITS) - 1)
    blk0 = jnp.arange(N_BLOCKS, dtype=jnp.int32) * MOE_M
    blk_e = jnp.minimum(jnp.sum((pad_end[:, None] <= blk0[None, :]).astype(jnp.int32), axis=0), N_EXPERTS - 1)
    first = (blk0 == pad_start[blk_e]).astype(jnp.int32)
    later = (experts[None, :] > experts[:, None]) & (counts[None, :] > 0)
    nxt = jnp.min(jnp.where(later, experts[None, :], N_EXPERTS), axis=1)
    next_e = jnp.where(nxt == N_EXPERTS, -1, nxt)[blk_e].astype(jnp.int32)
    return (row_tok.reshape(N_BLOCKS, 1, MOE_M), row_w.reshape(N_BLOCKS, 1, MOE_M), blk_e.astype(jnp.int32),
            first, next_e, n_used.reshape(1))


def _residual_kernel(*refs, final):
    x_ref, moe_ref, mod_ref, g_ref = refs[0:4]
    moe = jnp.concatenate([moe_ref[pl.ds(c, TM, stride=ROW_VREGS), :] for c in range(ROW_VREGS)], axis=1)
    x = x_ref[...] + mod_ref[0, 5:6, :] * moe
    if not final:
        refs[4][...] = x
        return
    ms = jnp.mean(x * x, axis=-1, keepdims=True)
    y = x * lax.rsqrt(ms + EPS) * g_ref[...]
    yc_o, yl_o = refs[4:6]
    i = pl.program_id(0)

    @pl.when(i < N_CTX_TILES)
    def _():
        yc_o[...] = y

    @pl.when(i >= N_CTX_TILES)
    def _():
        yl_o[...] = y


def _residual(x1, moe_tiles, mod_l, g, final):
    row = lambda i: (i, 0)
    if final:
        out_specs = [pl.BlockSpec((TM, D_MODEL), _ctx_tile), pl.BlockSpec((TM, D_MODEL), _lat_tile)]
        out_shape = [jax.ShapeDtypeStruct((T_CTX, D_MODEL), F32), jax.ShapeDtypeStruct((T_LAT, D_MODEL), F32)]
    else:
        out_specs = pl.BlockSpec((TM, D_MODEL), row)
        out_shape = jax.ShapeDtypeStruct((T, D_MODEL), F32)
    return pl.pallas_call(
        functools.partial(_residual_kernel, final=final),
        grid=(T // TM,),
        in_specs=[pl.BlockSpec((TM, D_MODEL), row),
                  pl.BlockSpec((TM * ROW_VREGS, LANES), row),
                  pl.BlockSpec((1, 6, D_MODEL), lambda i: (_mod_group(i), 0, 0)),
                  pl.BlockSpec((1, D_MODEL), lambda i: (0, 0))],
        out_specs=out_specs,
        out_shape=out_shape,
        compiler_params=pltpu.CompilerParams(vmem_limit_bytes=32 * MIB),
        name="residual_final" if final else "residual",
    )(x1, moe_tiles, mod_l, g)


def _rope_tables():
    t = np.arange(DEC_SEQ)
    pos = np.stack([t // GRID_W, t % GRID_W], axis=1).astype(np.float32)
    half = HEAD_DIM // 2
    inv = jnp.asarray(ROPE_BASE, F32) ** (-jnp.arange(0, half, 2, dtype=F32) / half)
    d = np.arange(HEAD_DIM)
    which = d // half
    freq = d % (half // 2)
    sign = np.where((d % half) < half // 2, -1.0, 1.0).astype(np.float32)
    ang = jnp.asarray(pos)[:, which] * inv[freq][None, :]
    cos = jnp.cos(ang)
    sin = jnp.sin(ang) * jnp.asarray(sign)[None, :]
    reps = LANES // HEAD_DIM
    return jnp.tile(cos, (1, reps)), jnp.tile(sin, (1, reps))


def kernel(x_prompt, x_sample, cache_k, cache_v, state_ret, c, c_ctx, norm1_g, norm2_g, w_mod, b_mod, w_in, conv_w, attn_sink, ret_decay, w_a, w_b, w_c, w_o, w_router, b_router, w_gu, b_gu, w_down, b_down, final_g):
    xs = (x_prompt.reshape(T_CTX, D_MODEL), x_sample.reshape(T_LAT, D_MODEL))
    cond8 = jnp.zeros((8, D_MODEL), F32).at[0].set(c_ctx).at[1:1 + DEC_BATCH].set(c)
    mod = _modulation(cond8, w_mod, b_mod)
    cos, sin = _rope_tables()
    ck = cache_k.reshape(DEC_BATCH, DEPTH, PAST_LEN, D_KV)
    cv = cache_v.reshape(DEC_BATCH, DEPTH, PAST_LEN, D_KV)

    ks, vs, rs = [], [], []
    for l in range(DEPTH):
        mod_l = mod[l, 0:1 + DEC_BATCH].reshape(1 + DEC_BATCH, 6, D_MODEL)
        conv_in, q, k, v, ret, gates = _inproj(xs, mod_l, norm1_g[l][None, :], w_in, l)
        rd8 = jnp.broadcast_to(ret_decay[l].reshape(2 * RET_HEADS, 1), (2 * RET_HEADS, LANES))
        *ys_ctx, rfin = _mixers(conv_in, q, k, v, ret, conv_w[l], attn_sink[l], rd8, latent=False)
        ys_lat = _mixers(conv_in, q, k, v, ret, conv_w[l], attn_sink[l], rd8, latent=True,
                         cos=cos, sin=sin, cache_k=ck, cache_v=cv, state=state_ret, layer=l)
        wr =jnp.pad(w_router[l], ((0, 0), (0, LANES - N_EXPERTS)))
        wrh = wr.astype(BF16)
        wrl = (wr - wrh.astype(F32)).astype(BF16)
        br = jnp.pad(b_router[l], (0, LANES - N_EXPERTS))[None, :]
        x1, xp, top_e, top_w, cnt = _merge(xs, ys_ctx, ys_lat, gates, mod_l, norm2_g[l][None, :], w_a, w_b, w_c,
                                           w_o, wrh, wrl, br, l)
        row_tok, row_w, blk_e, first, next_e, n_used = _route(top_e[:, 0:TOP_K], top_w[:, 0:TOP_K],
                                                              cnt[0, 0:N_EXPERTS])
        yrows = _moe_experts(blk_e, first, next_e, n_used, row_tok, xp, w_gu, b_gu, w_down, b_down, l)
        moe = _combine(n_used, row_tok, row_w, yrows)
        final = l == DEPTH - 1
        xs = _residual(x1, moe, mod_l, final_g[None, :], final)
        if not final:
            xs = (xs,)
        ks.append(k[0:T_CTX].reshape(BATCH, SEQ, N_KV, HEAD_DIM))
        vs.append(v[0:T_CTX].reshape(BATCH, SEQ, N_KV, HEAD_DIM))
        rs.append(rfin)

    y_prompt = xs[0].reshape(BATCH, SEQ, D_MODEL)
    y_sample = xs[1].reshape(DEC_BATCH, DEC_SEQ, D_MODEL)
    return (y_prompt, y_sample, jnp.stack(ks, axis=1), jnp.stack(vs, axis=1), jnp.stack(rs, axis=1))
```

```python
import functools

import numpy as np
import jax
import jax.numpy as jnp
from jax import lax
from jax.experimental import pallas as pl
from jax.experimental.pallas import tpu as pltpu

F32 = jnp.float32
BF16 = jnp.bfloat16

D_MODEL = 1024
BATCH = 16
SEQ = 256
DEPTH = 2
DEC_BATCH = 2
DEC_SEQ = 2048
PAST_LEN = 256
GRID_W = 64
HEAD_DIM = 64
D_CONV = 256
N_HEADS = 8
N_KV = 2
GROUP = N_HEADS // N_KV
WINDOW = 128
ROPE_BASE = 10000.0
RET_HEADS = 4
RET_DK = 64
RET_DV = 64
CHUNK = 128
N_EXPERTS = 32
TOP_K = 4
D_EXPERT = D_MODEL
SWIGLU_LIMIT = 7.0
SWIGLU_ALPHA = 1.702
EPS = 1e-6
NEG_INF = -1e30

T_CTX = BATCH * SEQ
T_LAT = DEC_BATCH * DEC_SEQ
T = T_CTX + T_LAT
D_ATTN = N_HEADS * HEAD_DIM
D_KV = N_KV * HEAD_DIM
D_RET = RET_HEADS * RET_DK
C_CONV = 0
C_Q = 3 * D_CONV
C_K = C_Q + D_ATTN
C_V = C_K + D_KV
C_RET = C_V + D_KV
C_GATE = C_RET + 4 * D_RET
IN_COLS = C_GATE + 3 * D_MODEL

TM = 512
MOE_M = 256
N_ASSIGN = T * TOP_K
N_BLOCKS = (N_ASSIGN + N_EXPERTS * (MOE_M - 1) + MOE_M - 1) // MOE_M
N_ROWS = N_BLOCKS * MOE_M
LANES = 128
SUBLANES = 8
ROW_VREGS = D_MODEL // LANES
PACK_ROWS = ROW_VREGS // 2
GATHER_STRIDE = MOE_M + SUBLANES
SCATTER_M = 128
SCATTER_STRIDE = SCATTER_M + SUBLANES
COMBINE_BLOCKS = 2
TOK_BITS = 14
assert T < (1 << TOK_BITS)
MIB = 1024 * 1024


def _sigmoid(x):
    return 1.0 / (1.0 + jnp.exp(-x))


def _mod_group(i):
    n_ctx = T_CTX // TM
    per_lat = DEC_SEQ // TM
    g = jnp.zeros_like(i)
    for b in range(DEC_BATCH):
        g = g + (i >= n_ctx + b * per_lat).astype(jnp.int32)
    return g


def _mod_kernel(cond_ref, w_ref, b_ref, o_ref):
    c = cond_ref[...]
    s = c * _sigmoid(c)
    o_ref[0] = jnp.dot(s.astype(BF16), w_ref[0].astype(BF16), preferred_element_type=F32) + b_ref[0]


def _modulation(cond8, w_mod, b_mod):
    n_col = 4
    cw = 6 * D_MODEL // n_col
    return pl.pallas_call(
        _mod_kernel,
        grid=(DEPTH, n_col),
        in_specs=[pl.BlockSpec((8, D_MODEL), lambda l, j: (0, 0)),
                  pl.BlockSpec((1, D_MODEL, cw), lambda l, j: (l, 0, j)),
                  pl.BlockSpec((1, 1, cw), lambda l, j: (l, 0, j))],
        out_specs=pl.BlockSpec((1, 8, cw), lambda l, j: (l, 0, j)),
        out_shape=jax.ShapeDtypeStruct((DEPTH, 8, 6 * D_MODEL), F32),
        compiler_params=pltpu.CompilerParams(vmem_limit_bytes=32 * MIB),
        name="modulation",
    )(cond8, w_mod, b_mod.reshape(DEPTH, 1, 6 * D_MODEL))


N_CTX_TILES = T_CTX // TM


def _ctx_tile(i):
    return (jnp.minimum(i, N_CTX_TILES - 1), 0)


def _lat_tile(i):
    return (jnp.maximum(i - N_CTX_TILES, 0), 0)


def _pick(ctx_ref, lat_ref):
    return jnp.where(pl.program_id(0) < N_CTX_TILES, ctx_ref[...], lat_ref[...])


def _inproj_kernel(*refs, split):
    if split:
        xc_ref, xl_ref, mod_ref, g_ref, w_ref, conv_o, q_o, k_o, v_o, ret_o, gate_o = refs
        x = _pick(xc_ref, xl_ref)
    else:
        x_ref, mod_ref, g_ref, w_ref, conv_o, q_o, k_o, v_o, ret_o, gate_o = refs
        x = x_ref[...]
    ms = jnp.mean(x * x, axis=-1, keepdims=True)
    h = x * lax.rsqrt(ms + EPS) * g_ref[...]
    h = h * (1.0 + mod_ref[0, 1:2, :]) + mod_ref[0, 0:1, :]
    hb = h.astype(BF16)

    def proj(c0, c1):
        return jnp.dot(hb, w_ref[0, :, c0:c1].astype(BF16), preferred_element_type=F32)

    conv_o[...] = proj(C_CONV, C_Q).astype(BF16)
    q_o[...] = (proj(C_Q, C_K) * HEAD_DIM ** -0.5).astype(BF16)
    k_o[...] = proj(C_K, C_V)
    v_o[...] = proj(C_V, C_RET)
    ret_o[:, 0:D_RET] = proj(C_RET, C_RET + D_RET).astype(BF16)
    ret_o[:, D_RET:2 * D_RET] = (proj(C_RET + D_RET, C_RET + 2 * D_RET) * RET_DK ** -0.5).astype(BF16)
    ret_o[:, 2 * D_RET:4 * D_RET] = proj(C_RET + 2 * D_RET, C_GATE).astype(BF16)
    for b in range(3):
        g = proj(C_GATE + b * D_MODEL, C_GATE + (b + 1) * D_MODEL)
        gate_o[:, b * D_MODEL:(b + 1) * D_MODEL] = _sigmoid(g).astype(BF16)


def _x_specs(xs):
    if len(xs) == 2:
        return [pl.BlockSpec((TM, D_MODEL), _ctx_tile), pl.BlockSpec((TM, D_MODEL), _lat_tile)]
    return [pl.BlockSpec((TM, D_MODEL), lambda i: (i, 0))]


def _inproj(xs, mod_l, g1, w_in, layer):
    row = lambda i: (i, 0)
    return pl.pallas_call(
        functools.partial(_inproj_kernel, split=len(xs) == 2),
        grid=(T // TM,),
        in_specs=_x_specs(xs) + [
                  pl.BlockSpec((1, 6, D_MODEL), lambda i: (_mod_group(i), 0, 0)),
                  pl.BlockSpec((1, D_MODEL), lambda i: (0, 0)),
                  pl.BlockSpec((1, D_MODEL, IN_COLS), lambda i: (layer, 0, 0), pipeline_mode=pl.Buffered(1))],
        out_specs=[pl.BlockSpec((TM, 3 * D_CONV), row),
                   pl.BlockSpec((TM, D_ATTN), row),
                   pl.BlockSpec((TM, D_KV), row),
                   pl.BlockSpec((TM, D_KV), row),
                   pl.BlockSpec((TM, 4 * D_RET), row),
                   pl.BlockSpec((TM, 3 * D_MODEL), row)],
        out_shape=[jax.ShapeDtypeStruct((T, 3 * D_CONV), BF16),
                   jax.ShapeDtypeStruct((T, D_ATTN), BF16),
                   jax.ShapeDtypeStruct((T, D_KV), F32),
                   jax.ShapeDtypeStruct((T, D_KV), F32),
                   jax.ShapeDtypeStruct((T, 4 * D_RET), BF16),
                   jax.ShapeDtypeStruct((T, 3 * D_MODEL), BF16)],
        compiler_params=pltpu.CompilerParams(vmem_limit_bytes=56 * MIB),
        name="inproj",
    )(*xs, mod_l, g1, w_in)


def _rope(x, cos, sin_signed):
    lane = lax.broadcasted_iota(jnp.int32, x.shape, 1)
    first = (lane % 32) < 16
    partner = jnp.where(first, pltpu.roll(x, x.shape[1] - 16, 1), pltpu.roll(x, 16, 1))
    return x * cos + partner * sin_signed


def _dot_nt(a, b):
    return lax.dot_general(a, b, (((1,), (1,)), ((), ())), preferred_element_type=F32)


def _dot_tn(a, b):
    return lax.dot_general(a, b, (((0,), (0,)), ((), ())), preferred_element_type=F32)


def _mixer_kernel_v1(*refs, n, latent):
    if latent:
        (sink_ref, conv_ref, q_ref, k_ref, v_ref, ret_ref, cw_ref, rd_ref, cos_ref, sin_ref, ck_ref, cv_ref,
         r0_ref, yc_ref, ya_ref, yr_ref, kb, vb, o_f, o_b, rst, dmat, qdec, kdec) = refs
    else:
        (sink_ref, conv_ref, q_ref, k_ref, v_ref, ret_ref, cw_ref, rd_ref,
         yc_ref, ya_ref, yr_ref, rfin_ref, kb, vb, o_f, o_b, rst, dmat, qdec, kdec) = refs
    nb = n // CHUNK

    cv = conv_ref[...].astype(F32)
    cb, cc, cu = cv[:, 0:D_CONV], cv[:, D_CONV:2 * D_CONV], cv[:, 2 * D_CONV:3 * D_CONV]
    p = cc * cu
    row = lax.broadcasted_iota(jnp.int32, p.shape, 0)
    prev = jnp.where(row == 0, 0.0, pltpu.roll(p, 1, 0))
    nxt = jnp.where(row == n - 1, 0.0, pltpu.roll(p, n - 1, 0))
    cw = cw_ref[...]
    yc_ref[...] = (cb * (prev * cw[0:1, :] + p * cw[1:2, :] + nxt * cw[2:3, :])).astype(BF16)

    if latent:
        kr = _rope(k_ref[...], cos_ref[...], sin_ref[...])
        zpad = jnp.zeros((CHUNK, D_KV), BF16)
        kb[0:CHUNK, :] = zpad
        vb[0:CHUNK, :] = zpad
        kb[CHUNK + n:2 * CHUNK + n, :] = zpad
        vb[CHUNK + n:2 * CHUNK + n, :] = zpad
        kb[CHUNK:CHUNK + n, :] = kr.astype(BF16)
        vb[CHUNK:CHUNK + n, :] = v_ref[...].astype(BF16)
        ckb = ck_ref[0, 0].astype(BF16)
        cvb = cv_ref[0, 0].astype(BF16)
    else:
        kb[...] = k_ref[...].astype(BF16)
        vb[...] = v_ref[...].astype(BF16)

    def attn_block(j):
        r0 = j * CHUNK if isinstance(j, int) else pl.multiple_of(j * CHUNK, CHUNK)
        qj = q_ref[pl.ds(r0, CHUNK), :]
        if latent:
            cosj = cos_ref[pl.ds(r0, CHUNK), :]
            sinj = sin_ref[pl.ds(r0, CHUNK), :]
            qf = qj.astype(F32)
            qj = jnp.concatenate(
                [_rope(qf[:, c * LANES:(c + 1) * LANES], cosj, sinj) for c in range(D_ATTN // LANES)],
                axis=1).astype(BF16)
            kw = kb[pl.ds(r0, 3 * CHUNK), :]
            vw = vb[pl.ds(r0, 3 * CHUNK), :]
            qpos = r0 + lax.broadcasted_iota(jnp.int32, (CHUNK, 3 * CHUNK), 0)
            kpos = r0 - CHUNK + lax.broadcasted_iota(jnp.int32, (CHUNK, 3 * CHUNK), 1)
            ok = (jnp.abs(qpos - kpos) <= WINDOW) & (kpos >= 0) & (kpos < n)
            ok = jnp.concatenate([ok] * GROUP, axis=0)
        else:
            kw = kb[...]
            vw = vb[...]
        for g in range(N_KV):
            lo = g * HEAD_DIM
            qg = jnp.concatenate(
                [qj[:, (GROUP * g + i) * HEAD_DIM:(GROUP * g + i + 1) * HEAD_DIM] for i in range(GROUP)], axis=0)
            sk = jnp.concatenate(
                [jnp.full((CHUNK, 1), sink_ref[GROUP * g + i], F32) for i in range(GROUP)], axis=0)
            s = _dot_nt(qg, kw[:, lo:lo + HEAD_DIM])
            if latent:
                s = jnp.where(ok, s, NEG_INF)
                s2 = _dot_nt(qg, ckb[:, lo:lo + HEAD_DIM])
            m = jnp.maximum(jnp.max(s, axis=-1, keepdims=True), sk)
            if latent:
                m = jnp.maximum(m, jnp.max(s2, axis=-1, keepdims=True))
            pw = jnp.exp(s - m)
            den = jnp.sum(pw, axis=-1, keepdims=True) + jnp.exp(sk - m)
            o = jnp.dot(pw.astype(BF16), vw[:, lo:lo + HEAD_DIM], preferred_element_type=F32)
            if latent:
                p2 = jnp.exp(s2 - m)
                den = den + jnp.sum(p2, axis=-1, keepdims=True)
                o = o + jnp.dot(p2.astype(BF16), cvb[:, lo:lo + HEAD_DIM], preferred_element_type=F32)
            o = o / den
            ya_ref[pl.ds(r0, CHUNK), g * GROUP * HEAD_DIM:(g + 1) * GROUP * HEAD_DIM] = jnp.concatenate(
                [o[i * CHUNK:(i + 1) * CHUNK, :] for i in range(GROUP)], axis=1).astype(BF16)

    if latent:
        def attn_body(j, carry):
            attn_block(j)
            return carry
        lax.fori_loop(0, nb, attn_body, 0)
    else:
        for j in range(nb):
            attn_block(j)

    rd = rd_ref[...]
    log_g = jnp.minimum(rd, 0.0) - jnp.log(1.0 + jnp.exp(-jnp.abs(rd)))
    ii = lax.broadcasted_iota(jnp.int32, (CHUNK, CHUNK), 0).astype(F32)
    jj = lax.broadcasted_iota(jnp.int32, (CHUNK, CHUNK), 1).astype(F32)
    for d in range(2):
        for h in range(RET_HEADS):
            r = d * RET_HEADS + h
            lg = log_g[r:r + 1, :]
            diff = (ii - jj) if d == 0 else (jj - ii)
            dmat[r] = jnp.where(diff >= 0, jnp.exp(jnp.maximum(diff, 0.0) * lg), 0.0)
            if d == 0:
                qdec[r] = jnp.exp((ii + 1.0) * lg)
                kdec[r] = jnp.exp((CHUNK - 1.0 - ii) * lg)
            else:
                qdec[r] = jnp.exp((CHUNK - ii) * lg)
                kdec[r] = jnp.exp(ii * lg)
            if latent:
                rst[r] = r0_ref[0, 0, d, h]
            else:
                rst[r] = jnp.zeros((RET_DK, RET_DV), F32)
    chunk_decay = jnp.exp(float(CHUNK) * log_g)

    def ret_chunk(c0, d, out_ref):
        blk = ret_ref[pl.ds(c0, CHUNK), :]
        outs = []
        for h in range(RET_HEADS):
            r = d * RET_HEADS + h
            qc = blk[:, h * RET_DK:(h + 1) * RET_DK]
            kc = blk[:, D_RET + h * RET_DK:D_RET + (h + 1) * RET_DK]
            vc = blk[:, 2 * D_RET + h * RET_DV:2 * D_RET + (h + 1) * RET_DV]
            inner = (_dot_nt(qc, kc) * dmat[r]).astype(BF16)
            state = rst[r]
            o = jnp.dot(inner, vc, preferred_element_type=F32)
            o = o + jnp.dot(qc, state.astype(BF16), preferred_element_type=F32) * qdec[r][:, 0:RET_DV]
            kd = (kc.astype(F32) * kdec[r][:, 0:RET_DK]).astype(BF16)
            rst[r] = state * chunk_decay[r:r + 1, 0:RET_DV] + _dot_tn(kd, vc)
            outs.append(o)
        out_ref[pl.ds(c0, CHUNK), :] = jnp.concatenate(outs, axis=1)

    def ret_body(j, carry):
        ret_chunk(pl.multiple_of(j * CHUNK, CHUNK), 0, o_f)
        ret_chunk(pl.multiple_of((nb - 1 - j) * CHUNK, CHUNK), 1, o_b)
        return carry

    lax.fori_loop(0, nb, ret_body, 0)

    def norm_body(j, carry):
        c0 = pl.multiple_of(j * CHUNK, CHUNK)
        o = o_f[pl.ds(c0, CHUNK), :] + o_b[pl.ds(c0, CHUNK), :]
        parts = []
        for h in range(RET_HEADS):
            oh = o[:, h * RET_DV:(h + 1) * RET_DV]
            ms = jnp.mean(oh * oh, axis=-1, keepdims=True)
            parts.append(oh * lax.rsqrt(ms + EPS))
        rg = ret_ref[pl.ds(c0, CHUNK), 3 * D_RET:4 * D_RET].astype(F32)
        yr_ref[pl.ds(c0, CHUNK), :] = (rg * _sigmoid(rg) * jnp.concatenate(parts, axis=1)).astype(BF16)
        return carry

    lax.fori_loop(0, nb, norm_body, 0)

    if not latent:
        for d in range(2):
            for h in range(RET_HEADS):
                rfin_ref[0, d, h] = rst[d * RET_HEADS + h]


HALF = LANES // 2
assert HEAD_DIM == HALF and RET_DK == HALF and RET_DV == HALF
N_PAIRS = N_HEADS // 2
RET_PAIRS = RET_HEADS // 2


def _row_variants(kt):
    row = lax.broadcasted_iota(jnp.int32, kt.shape, 0)
    lo0 = jnp.where(row < HALF, kt, 0.0)
    hi1 = jnp.where(row >= HALF, kt, 0.0)
    return lo0, pltpu.roll(lo0, HALF, 0), pltpu.roll(hi1, HALF, 0), hi1


def _lane_variants(v):
    lane = lax.broadcasted_iota(jnp.int32, v.shape, 1)
    lo0 = jnp.where(lane < HALF, v, 0.0)
    hi1 = jnp.where(lane >= HALF, v, 0.0)
    return lo0, pltpu.roll(lo0, HALF, 1), pltpu.roll(hi1, HALF, 1), hi1


def _mixer_kernel(*refs, n, latent):
    if latent:
        (sink_ref, conv_ref, q_ref, k_ref, v_ref, ret_ref, cw_ref, rd_ref, cos_ref, sin_ref, ck_ref, cv_ref,
         r0_ref, yc_ref, ya_ref, yr_ref, ktq, vq, rkp, o_f, o_b, rst, dmat, qdec, kdec, cdec) = refs
    else:
        (sink_ref, conv_ref, q_ref, k_ref, v_ref, ret_ref, cw_ref, rd_ref,
         yc_ref, ya_ref, yr_ref, rfin_ref, ktq, vq, rkp, o_f, o_b, rst, dmat, qdec, kdec, cdec) = refs
    nb = n // CHUNK
    pad = CHUNK if latent else 0

    cv = conv_ref[...].astype(F32)
    cb, cc, cu = cv[:, 0:D_CONV], cv[:, D_CONV:2 * D_CONV], cv[:, 2 * D_CONV:3 * D_CONV]
    p = cc * cu
    row = lax.broadcasted_iota(jnp.int32, p.shape, 0)
    prev = jnp.where(row == 0, 0.0, pltpu.roll(p, 1, 0))
    nxt = jnp.where(row == n - 1, 0.0, pltpu.roll(p, n - 1, 0))
    cw = cw_ref[...]
    yc_ref[...] = (cb * (prev * cw[0:1, :] + p * cw[1:2, :] + nxt * cw[2:3, :])).astype(BF16)

    kf = k_ref[...]
    if latent:
        kf = _rope(kf, cos_ref[...], sin_ref[...])
    for idx, (kk, vv) in enumerate(zip(_row_variants(kf.T), _lane_variants(v_ref[...]))):
        if latent:
            ktq[idx, :, 0:pad] = jnp.zeros((LANES, pad), BF16)
            ktq[idx, :, pad + n:2 * pad + n] = jnp.zeros((LANES, pad), BF16)
            vq[idx, 0:pad, :] = jnp.zeros((pad, LANES), BF16)
            vq[idx, pad + n:2 * pad + n, :] = jnp.zeros((pad, LANES), BF16)
        ktq[idx, :, pad:pad + n] = kk.astype(BF16)
        vq[idx, pad:pad + n, :] = vv.astype(BF16)
    if latent:
        cktq = [t.astype(BF16) for t in _row_variants(ck_ref[0, 0].T)]
        cvq = [t.astype(BF16) for t in _lane_variants(cv_ref[0, 0])]

    def attn(r0, rows):
        qj = q_ref[pl.ds(r0, rows), :]
        if latent:
            cosj = cos_ref[pl.ds(r0, rows), :]
            sinj = sin_ref[pl.ds(r0, rows), :]
            qpos = r0 + lax.broadcasted_iota(jnp.int32, (rows, 3 * CHUNK), 0)
            kpos = r0 - CHUNK + lax.broadcasted_iota(jnp.int32, (rows, 3 * CHUNK), 1)
            ok = (jnp.abs(qpos - kpos) <= WINDOW) & (kpos >= 0) & (kpos < n)
        for m in range(N_PAIRS):
            q2 = qj[:, m * LANES:(m + 1) * LANES]
            if latent:
                q2 = _rope(q2.astype(F32), cosj, sinj).astype(BF16)
            g = (2 * m) // GROUP
            acc = None
            for half in range(2):
                idx = 2 * g + half
                sk = sink_ref[2 * m + half]
                if latent:
                    s = jnp.dot(q2, ktq[idx, :, pl.ds(r0, 3 * CHUNK)], preferred_element_type=F32)
                    s = jnp.where(ok, s, NEG_INF)
                    s2 = jnp.dot(q2, cktq[idx], preferred_element_type=F32)
                    mx = jnp.maximum(jnp.maximum(jnp.max(s, axis=-1, keepdims=True),
                                                 jnp.max(s2, axis=-1, keepdims=True)), sk)
                    pw = jnp.exp(s - mx)
                    p2 = jnp.exp(s2 - mx)
                    den = (jnp.sum(pw, axis=-1, keepdims=True) + jnp.sum(p2, axis=-1, keepdims=True)
                           + jnp.exp(sk - mx))
                    o = (jnp.dot(pw.astype(BF16), vq[idx, pl.ds(r0, 3 * CHUNK), :], preferred_element_type=F32)
                         + jnp.dot(p2.astype(BF16), cvq[idx], preferred_element_type=F32))
                else:
                    s = jnp.dot(q2, ktq[idx], preferred_element_type=F32)
                    mx = jnp.maximum(jnp.max(s, axis=-1, keepdims=True), sk)
                    pw = jnp.exp(s - mx)
                    den = jnp.sum(pw, axis=-1, keepdims=True) + jnp.exp(sk - mx)
                    o = jnp.dot(pw.astype(BF16), vq[idx], preferred_element_type=F32)
                o = o / den
                acc = o if acc is None else acc + o
            ya_ref[pl.ds(r0, rows), m * LANES:(m + 1) * LANES] = acc.astype(BF16)

    rk_t = ret_ref[:, D_RET:2 * D_RET].astype(F32).T
    for m in range(RET_PAIRS):
        rkp[m] = rk_t[m * LANES:(m + 1) * LANES, :].astype(BF16)
    rd = rd_ref[...]
    log_g = jnp.minimum(rd, 0.0) - jnp.log(1.0 + jnp.exp(-jnp.abs(rd)))
    row_c = lax.broadcasted_iota(jnp.int32, (CHUNK, CHUNK), 0)
    lane_c = lax.broadcasted_iota(jnp.int32, (CHUNK, CHUNK), 1)
    ii = row_c.astype(F32)
    jj = lane_c.astype(F32)
    even_row = row_c < HALF
    even_lane = lane_c < HALF
    blockdiag = even_row == even_lane
    for d in range(2):
        for h in range(RET_HEADS):
            r = d * RET_HEADS + h
            diff = (ii - jj) if d == 0 else (jj - ii)
            dmat[r] = jnp.where(diff >= 0, jnp.exp(jnp.maximum(diff, 0.0) * log_g[r:r + 1, :]), 0.0)
        for m in range(RET_PAIRS):
            s = d * RET_PAIRS + m
            lg_e = log_g[d * RET_HEADS + 2 * m:d * RET_HEADS + 2 * m + 1, :]
            lg_o = log_g[d * RET_HEADS + 2 * m + 1:d * RET_HEADS + 2 * m + 2, :]
            qpow = (ii + 1.0) if d == 0 else (CHUNK - ii)
            kpow = (CHUNK - 1.0 - jj) if d == 0 else jj
            qdec[s] = jnp.where(even_lane, jnp.exp(qpow * lg_e), jnp.exp(qpow * lg_o))
            kdec[s] = jnp.where(even_row, jnp.exp(kpow * lg_e), jnp.exp(kpow * lg_o))
            chunk_decay = jnp.where(even_row, jnp.exp(float(CHUNK) * lg_e), jnp.exp(float(CHUNK) * lg_o))
            cdec[s] = jnp.where(blockdiag, chunk_decay, 0.0)
            if latent:
                z = jnp.zeros((HALF, HALF), F32)
                rst[s] = jnp.concatenate(
                    [jnp.concatenate([r0_ref[0, 0, d, 2 * m], z], axis=1),
                     jnp.concatenate([z, r0_ref[0, 0, d, 2 * m + 1]], axis=1)], axis=0)
            else:
                rst[s] = jnp.zeros((LANES, LANES), F32)

    def ret_chunk(c0, d, out_ref):
        for m in range(RET_PAIRS):
            s = d * RET_PAIRS + m
            q2 = ret_ref[pl.ds(c0, CHUNK), m * LANES:(m + 1) * LANES]
            v2 = ret_ref[pl.ds(c0, CHUNK), 2 * D_RET + m * LANES:2 * D_RET + (m + 1) * LANES]
            kt2 = rkp[m, :, pl.ds(c0, CHUNK)].astype(F32)
            v2f = v2.astype(F32)
            state = rst[s]
            o2 = jnp.dot(q2, state.astype(BF16), preferred_element_type=F32) * qdec[s]
            for half in range(2):
                r = d * RET_HEADS + 2 * m + half
                keep_row = even_row if half == 0 else jnp.logical_not(even_row)
                keep_lane = even_lane if half == 0 else jnp.logical_not(even_lane)
                a = jnp.dot(q2, jnp.where(keep_row, kt2, 0.0).astype(BF16), preferred_element_type=F32)
                inner = (a * dmat[r]).astype(BF16)
                o2 = o2 + jnp.dot(inner, jnp.where(keep_lane, v2f, 0.0).astype(BF16), preferred_element_type=F32)
            kd = (kt2 * kdec[s]).astype(BF16)
            upd = jnp.dot(kd, v2, preferred_element_type=F32)
            rst[s] = state * cdec[s] + jnp.where(blockdiag, upd, 0.0)
            out_ref[pl.ds(c0, CHUNK), m * LANES:(m + 1) * LANES] = o2

    gi = lax.broadcasted_iota(jnp.int32, (D_RET, D_RET), 0) // RET_DV
    gj = lax.broadcasted_iota(jnp.int32, (D_RET, D_RET), 1) // RET_DV
    group_mean = jnp.where(gi == gj, 1.0 / RET_DV, 0.0).astype(BF16)

    def norm(c0, rows):
        o = o_f[pl.ds(c0, rows), :] + o_b[pl.ds(c0, rows), :]
        sq = o * o
        hi = sq.astype(BF16)
        lo = (sq - hi.astype(F32)).astype(BF16)
        ms = (jnp.dot(hi, group_mean, preferred_element_type=F32)
              + jnp.dot(lo, group_mean, preferred_element_type=F32))
        rg = ret_ref[pl.ds(c0, rows), 3 * D_RET:4 * D_RET].astype(F32)
        yr_ref[pl.ds(c0, rows), :] = (rg * _sigmoid(rg) * (o * lax.rsqrt(ms + EPS))).astype(BF16)

    if latent:
        def scan_body(j, carry):
            attn(pl.multiple_of(j * CHUNK, CHUNK), CHUNK)
            ret_chunk(pl.multiple_of(j * CHUNK, CHUNK), 0, o_f)
            ret_chunk(pl.multiple_of((nb - 1 - j) * CHUNK, CHUNK), 1, o_b)
            return carry

        def norm_body(j, carry):
            norm(pl.multiple_of(j * CHUNK, CHUNK), CHUNK)
            return carry

        lax.fori_loop(0, nb, scan_body, 0)
        lax.fori_loop(0, nb, norm_body, 0)
    else:
        attn(0, n)
        for j in range(nb):
            ret_chunk(j * CHUNK, 0, o_f)
            ret_chunk((nb - 1 - j) * CHUNK, 1, o_b)
        norm(0, n)
        for d in range(2):
            for h in range(RET_HEADS):
                lo_ = (h % 2) * HALF
                rfin_ref[0, d, h] = rst[d * RET_PAIRS + h // 2][lo_:lo_ + HALF, lo_:lo_ + HALF]


def _mixers(conv_in, q, k, v, ret, conv_w_l, sink_l, rd8, *, latent, cos=None, sin=None, cache_k=None,
            cache_v=None, state=None, layer=0):
    n = DEC_SEQ if latent else SEQ
    nseq = DEC_BATCH if latent else BATCH
    off = T_CTX // n if latent else 0
    seq = lambda s: (s + off, 0)
    const = lambda s: (0, 0)
    in_specs = [pl.BlockSpec(memory_space=pltpu.SMEM),
                pl.BlockSpec((n, 3 * D_CONV), seq),
                pl.BlockSpec((n, D_ATTN), seq),
                pl.BlockSpec((n, D_KV), seq),
                pl.BlockSpec((n, D_KV), seq),
                pl.BlockSpec((n, 4 * D_RET), seq),
                pl.BlockSpec((3, D_CONV), const),
                pl.BlockSpec((8, LANES), const)]
    args = [sink_l, conv_in, q, k, v, ret, conv_w_l, rd8]
    out_specs = [pl.BlockSpec((n, D_CONV), lambda s: (s, 0)),
                 pl.BlockSpec((n, D_ATTN), lambda s: (s, 0)),
                 pl.BlockSpec((n, D_RET), lambda s: (s, 0))]
    out_shape = [jax.ShapeDtypeStruct((nseq * n, D_CONV), BF16),
                 jax.ShapeDtypeStruct((nseq * n, D_ATTN), BF16),
                 jax.ShapeDtypeStruct((nseq * n, D_RET), BF16)]
    if latent:
        in_specs += [pl.BlockSpec((n, LANES), const),
                     pl.BlockSpec((n, LANES), const),
                     pl.BlockSpec((1, 1, PAST_LEN, D_KV), lambda s: (s, layer, 0, 0)),
                     pl.BlockSpec((1, 1, PAST_LEN, D_KV), lambda s: (s, layer, 0, 0)),
                     pl.BlockSpec((1, 1, 2, RET_HEADS, RET_DK, RET_DV), lambda s: (s, layer, 0, 0, 0, 0))]
        args += [cos, sin, cache_k, cache_v, state]
        kv_rows = n + 2 * CHUNK
    else:
        out_specs.append(pl.BlockSpec((1, 2, RET_HEADS, RET_DK, RET_DV), lambda s: (s, 0, 0, 0, 0)))
        out_shape.append(jax.ShapeDtypeStruct((nseq, 2, RET_HEADS, RET_DK, RET_DV), F32))
        kv_rows = n
    scratch = [pltpu.VMEM((2 * N_KV, LANES, kv_rows), BF16),
               pltpu.VMEM((2 * N_KV, kv_rows, LANES), BF16),
               pltpu.VMEM((RET_PAIRS, LANES, n), BF16),
               pltpu.VMEM((n, D_RET), F32),
               pltpu.VMEM((n, D_RET), F32),
               pltpu.VMEM((2 * RET_PAIRS, LANES, LANES), F32),
               pltpu.VMEM((2 * RET_HEADS, CHUNK, CHUNK), F32),
               pltpu.VMEM((2 * RET_PAIRS, CHUNK, CHUNK), F32),
               pltpu.VMEM((2 * RET_PAIRS, CHUNK, CHUNK), F32),
               pltpu.VMEM((2 * RET_PAIRS, CHUNK, CHUNK), F32)]
    return pl.pallas_call(
        functools.partial(_mixer_kernel, n=n, latent=latent),
        grid=(nseq,),
        in_specs=in_specs,
        out_specs=out_specs,
        out_shape=out_shape,
        scratch_shapes=scratch,
        compiler_params=pltpu.CompilerParams(vmem_limit_bytes=56 * MIB),
        name="mixers_latent" if latent else "mixers_context",
    )(*args)


def _merge_kernel(*refs, split):
    if split:
        xc_ref, xl_ref = refs[0:2]
        x = _pick(xc_ref, xl_ref)
        refs = refs[2:]
    else:
        x = refs[0][...]
        refs = refs[1:]
    (ycc_ref, ycl_ref, yac_ref, yal_ref, yrc_ref, yrl_ref, gate_ref, mod_ref, g2_ref, wa_ref, wb_ref, wc_ref,
     wo_ref, wrh_ref, wrl_ref, br_ref, x1_o, xp_o, te_o, tw_o, cnt_o) = refs
    merged = (gate_ref[:, 0:D_MODEL].astype(F32)
              * jnp.dot(_pick(ycc_ref, ycl_ref), wa_ref[0].astype(BF16), preferred_element_type=F32)
              + gate_ref[:, D_MODEL:2 * D_MODEL].astype(F32)
              * jnp.dot(_pick(yac_ref, yal_ref), wb_ref[0].astype(BF16), preferred_element_type=F32)
              + gate_ref[:, 2 * D_MODEL:3 * D_MODEL].astype(F32)
              * jnp.dot(_pick(yrc_ref, yrl_ref), wc_ref[0].astype(BF16), preferred_element_type=F32))
    x1 = x + mod_ref[0, 2:3, :] * jnp.dot(merged.astype(BF16), wo_ref[0].astype(BF16),
                                          preferred_element_type=F32)
    x1_o[...] = x1
    ms = jnp.mean(x1 * x1, axis=-1, keepdims=True)
    h2 = x1 * lax.rsqrt(ms + EPS) * g2_ref[...]
    h2 = h2 * (1.0 + mod_ref[0, 4:5, :]) + mod_ref[0, 3:4, :]
    hh = h2.astype(BF16)
    hf = hh.astype(F32)
    bits = lax.bitcast_convert_type(hf, jnp.uint32)
    for c in range(PACK_ROWS):
        lo = bits[:, c * LANES:(c + 1) * LANES] >> 16
        hi = bits[:, (c + PACK_ROWS) * LANES:(c + PACK_ROWS + 1) * LANES] & jnp.uint32(0xFFFF0000)
        xp_o[pl.ds(c, TM, stride=PACK_ROWS), :] = lax.bitcast_convert_type(lo | hi, jnp.int32)
    hl = (h2 - hf).astype(BF16)
    logits = (jnp.dot(hh, wrh_ref[...], preferred_element_type=F32)
              + jnp.dot(hl, wrh_ref[...], preferred_element_type=F32)
              + jnp.dot(hh, wrl_ref[...], preferred_element_type=F32)
              + br_ref[...])
    lane = lax.broadcasted_iota(jnp.int32, logits.shape, 1)
    work = jnp.where(lane < N_EXPERTS, logits, -jnp.inf)
    vals, idxs = [], []
    for _ in range(TOP_K):
        m = jnp.max(work, axis=-1, keepdims=True)
        am = jnp.min(jnp.where(work == m, lane, LANES), axis=-1, keepdims=True)
        vals.append(m)
        idxs.append(am)
        work = jnp.where(lane == am, -jnp.inf, work)
    es = [jnp.exp(v - vals[0]) for v in vals]
    den = es[0] + es[1] + es[2] + es[3]
    te = jnp.zeros(logits.shape, jnp.int32)
    tw = jnp.zeros(logits.shape, F32)
    for k in range(TOP_K):
        te = jnp.where(lane == k, idxs[k], te)
        tw = jnp.where(lane == k, es[k] / den, tw)
    te_o[...] = te
    tw_o[...] = tw
    sel = (lane == idxs[0]) | (lane == idxs[1]) | (lane == idxs[2]) | (lane == idxs[3])
    part = jnp.sum(sel.astype(jnp.int32), axis=0, keepdims=True)

    @pl.when(pl.program_id(0) == 0)
    def _():
        cnt_o[...] = jnp.zeros_like(cnt_o)

    cnt_o[...] += jnp.broadcast_to(part, cnt_o.shape)


def _merge(xs, ys_ctx, ys_lat, gates, mod_l, g2, wa, wb, wc, wo, wrh, wrl, br, layer):
    row = lambda i: (i, 0)
    const = lambda i: (0, 0)
    wl = lambda i: (layer, 0, 0)
    y_specs, y_args = [], []
    for width, yc, yl in zip((D_CONV, D_ATTN, D_RET), ys_ctx, ys_lat):
        y_specs += [pl.BlockSpec((TM, width), _ctx_tile), pl.BlockSpec((TM, width), _lat_tile)]
        y_args += [yc, yl]
    return pl.pallas_call(
        functools.partial(_merge_kernel, split=len(xs) == 2),
        grid=(T // TM,),
        in_specs=_x_specs(xs) + y_specs + [
                  pl.BlockSpec((TM, 3 * D_MODEL), row),
                  pl.BlockSpec((1, 6, D_MODEL), lambda i: (_mod_group(i), 0, 0)),
                  pl.BlockSpec((1, D_MODEL), const),
                  pl.BlockSpec((1, D_CONV, D_MODEL), wl),
                  pl.BlockSpec((1, D_ATTN, D_MODEL), wl),
                  pl.BlockSpec((1, D_RET, D_MODEL), wl),
                  pl.BlockSpec((1, D_MODEL, D_MODEL), wl),
                  pl.BlockSpec((D_MODEL, LANES), const),
                  pl.BlockSpec((D_MODEL, LANES), const),
                  pl.BlockSpec((1, LANES), const)],
        out_specs=[pl.BlockSpec((TM, D_MODEL), row),
                   pl.BlockSpec((TM * PACK_ROWS, LANES), row),
                   pl.BlockSpec((TM, LANES), row),
                   pl.BlockSpec((TM, LANES), row),
                   pl.BlockSpec((SUBLANES, LANES), const)],
        out_shape=[jax.ShapeDtypeStruct((T, D_MODEL), F32),
                   jax.ShapeDtypeStruct((T * PACK_ROWS, LANES), jnp.int32),
                   jax.ShapeDtypeStruct((T, LANES), jnp.int32),
                   jax.ShapeDtypeStruct((T, LANES), F32),
                   jax.ShapeDtypeStruct((SUBLANES, LANES), jnp.int32)],
        compiler_params=pltpu.CompilerParams(vmem_limit_bytes=48 * MIB),
        name="merge_router",
    )(*xs, *y_args, gates, mod_l, g2, wa, wb, wc, wo, wrh, wrl, br)


def _moe_kernel(blk_e_ref, first_ref, next_e_ref, nu_ref, tok_ref, tok_next_ref, w_ref, xp_ref, wgu_hbm, bgu_ref,
                wd_hbm, bd_ref, y_ref, tile, wgu_st, wd_st, wgu_bf, wd_bf, sems, *, layer):
    b = pl.program_id(0)
    slot = b % 2

    def weight_copies(e):
        return (pltpu.make_async_copy(wgu_hbm.at[layer, e], wgu_st, sems.at[0]),
                pltpu.make_async_copy(wd_hbm.at[layer, e], wd_st, sems.at[1]))

    def gather(tok, dst_slot):
        for mi in range(MOE_M):
            t = jnp.minimum(tok[0, 0, mi], T - 1)
            slab = xp_ref[pl.ds(pl.multiple_of(t * PACK_ROWS, PACK_ROWS), PACK_ROWS), :]
            tile[dst_slot, pl.ds(mi, PACK_ROWS, stride=GATHER_STRIDE), :] = slab

    @pl.when(b == 0)
    def _():
        for cp in weight_copies(blk_e_ref[0]):
            cp.start()
        gather(tok_ref, 0)

    @pl.when(b < nu_ref[0])
    def _():
        @pl.when(first_ref[b] == 1)
        def _():
            for cp in weight_copies(blk_e_ref[b]):
                cp.wait()
            wgu_bf[...] = wgu_st[...].astype(BF16)
            wd_bf[...] = wd_st[...].astype(BF16)

            @pl.when(next_e_ref[b] >= 0)
            def _():
                for cp in weight_copies(next_e_ref[b]):
                    cp.start()

        lo, hi = [], []
        for c in range(PACK_ROWS):
            bits = lax.bitcast_convert_type(
                tile[slot, c * GATHER_STRIDE:c * GATHER_STRIDE + MOE_M, :], jnp.uint32)
            lo.append(lax.bitcast_convert_type(bits << 16, F32).astype(BF16))
            hi.append(lax.bitcast_convert_type(bits & jnp.uint32(0xFFFF0000), F32).astype(BF16))
        x = jnp.concatenate(lo + hi, axis=1)
        gather(tok_next_ref, 1 - slot)

        gu = jnp.dot(x, wgu_bf[...], preferred_element_type=F32) + bgu_ref[0, 0]
        gate = jnp.minimum(gu[:, 0:D_EXPERT], SWIGLU_LIMIT)
        up = jnp.clip(gu[:, D_EXPERT:2 * D_EXPERT], -SWIGLU_LIMIT, SWIGLU_LIMIT)
        glu = gate * _sigmoid(SWIGLU_ALPHA * gate)
        mid = ((up + 1.0) * glu).astype(BF16)
        y = (jnp.dot(mid, wd_bf[...], preferred_element_type=F32) + bd_ref[0, 0]) * w_ref[0]
        for c in range(ROW_VREGS):
            y_ref[pl.ds(c, MOE_M, stride=ROW_VREGS), :] = y[:, c * LANES:(c + 1) * LANES]

    @pl.when(b >= nu_ref[0])
    def _():
        y_ref[...] = jnp.zeros_like(y_ref)


def _moe_experts(blk_e, first, next_e, n_used, row_tok3, row_w3, xp, w_gu, b_gu, w_down, b_down, layer):
    bias = lambda b, e, f, ne, nu: (layer, e[b], 0, 0)
    grid_spec = pltpu.PrefetchScalarGridSpec(
        num_scalar_prefetch=4,
        grid=(N_BLOCKS,),
        in_specs=[pl.BlockSpec((1, 1, MOE_M), lambda b, e, f, ne, nu: (b, 0, 0), memory_space=pltpu.SMEM),
                  pl.BlockSpec((1, 1, MOE_M), lambda b, e, f, ne, nu: (jnp.minimum(b + 1, N_BLOCKS - 1), 0, 0),
                               memory_space=pltpu.SMEM),
                  pl.BlockSpec((1, MOE_M, 1), lambda b, e, f, ne, nu: (b, 0, 0)),
                  pl.BlockSpec((T * PACK_ROWS, LANES), lambda b, e, f, ne, nu: (0, 0), pipeline_mode=pl.Buffered(1)),
                  pl.BlockSpec(memory_space=pl.ANY),
                  pl.BlockSpec((1, 1, 1, 2 * D_EXPERT), bias),
                  pl.BlockSpec(memory_space=pl.ANY),
                  pl.BlockSpec((1, 1, 1, D_MODEL), bias)],
        out_specs=pl.BlockSpec((MOE_M * ROW_VREGS, LANES), lambda b, e, f, ne, nu: (b, 0)),
        scratch_shapes=[pltpu.VMEM((2, PACK_ROWS * GATHER_STRIDE, LANES), jnp.int32),
                        pltpu.VMEM((D_MODEL, 2 * D_EXPERT), F32),
                        pltpu.VMEM((D_EXPERT, D_MODEL), F32),
                        pltpu.VMEM((D_MODEL, 2 * D_EXPERT), BF16),
                        pltpu.VMEM((D_EXPERT, D_MODEL), BF16),
                        pltpu.SemaphoreType.DMA((2,))],
    )
    return pl.pallas_call(
        functools.partial(_moe_kernel, layer=layer),
        grid_spec=grid_spec,
        out_shape=jax.ShapeDtypeStruct((N_ROWS * ROW_VREGS, LANES), F32),
        compiler_params=pltpu.CompilerParams(vmem_limit_bytes=48 * MIB),
        name="moe_experts",
    )(blk_e, first, next_e, n_used, row_tok3, row_tok3, row_w3.reshape(N_BLOCKS, MOE_M, 1), xp, w_gu,
      b_gu.reshape(DEPTH, N_EXPERTS, 1, 2 * D_EXPERT), w_down, b_down.reshape(DEPTH, N_EXPERTS, 1, D_MODEL))


SCATTER_UNROLL = 8


def _combine_kernel(nu_ref, tok_ref, y_ref, o_ref, acc, sem):
    s = pl.program_id(0)

    @pl.when(s == 0)
    def _():
        acc[...] = jnp.zeros_like(acc)

    def sub_block(sb, carry):
        r0 = pl.multiple_of(sb * SCATTER_M, SCATTER_M)
        for m0 in range(0, SCATTER_M, SCATTER_UNROLL):
            addrs, vals = [], []
            for u in range(SCATTER_UNROLL):
                mi = m0 + u
                a = pl.multiple_of(tok_ref[0, 0, r0 + mi] * ROW_VREGS, ROW_VREGS)
                yv = y_ref[pl.ds(pl.multiple_of((r0 + mi) * ROW_VREGS, ROW_VREGS), ROW_VREGS), :]
                addrs.append(a)
                vals.append(acc[pl.ds(a, ROW_VREGS), :] + yv)
            for u in range(SCATTER_UNROLL):
                acc[pl.ds(addrs[u], ROW_VREGS), :] = vals[u]
        return carry

    @pl.when(s * COMBINE_BLOCKS < nu_ref[0])
    def _():
        lax.fori_loop(0, COMBINE_BLOCKS * MOE_M // SCATTER_M, sub_block, 0)

    @pl.when(s == pl.num_programs(0) - 1)
    def _():
        cp = pltpu.make_async_copy(acc.at[pl.ds(0, T * ROW_VREGS)], o_ref, sem)
        cp.start()
        cp.wait()


def _combine(n_used, row_tok3, y_tiles):
    rows = COMBINE_BLOCKS * MOE_M
    steps = N_BLOCKS // COMBINE_BLOCKS
    grid_spec = pltpu.PrefetchScalarGridSpec(
        num_scalar_prefetch=1,
        grid=(steps,),
        in_specs=[pl.BlockSpec((1, 1, rows), lambda s, nu: (s, 0, 0), memory_space=pltpu.SMEM),
                  pl.BlockSpec((rows * ROW_VREGS, LANES), lambda s, nu: (s, 0))],
        out_specs=pl.BlockSpec(memory_space=pl.ANY),
        scratch_shapes=[pltpu.VMEM(((T + 1) * ROW_VREGS, LANES), F32),
                        pltpu.SemaphoreType.DMA(())],
    )
    return pl.pallas_call(
        _combine_kernel,
        grid_spec=grid_spec,
        out_shape=jax.ShapeDtypeStruct((T * ROW_VREGS, LANES), F32),
        compiler_params=pltpu.CompilerParams(vmem_limit_bytes=48 * MIB),
        name="moe_combine",
    )(n_used, row_tok3.reshape(steps, 1, rows), y_tiles)


def _route(top_e, top_w, counts):
    experts = jnp.arange(N_EXPERTS, dtype=jnp.int32)
    padded = (counts + MOE_M - 1) // MOE_M * MOE_M
    pad_end = jnp.cumsum(padded)
    pad_start = pad_end - padded
    n_used = (pad_end[-1] // MOE_M).astype(jnp.int32)
    n_pad = N_ROWS - N_ASSIGN
    pad_cum = jnp.cumsum(padded - counts)
    pad_expert = jnp.sum((pad_cum[:, None] <= jnp.arange(n_pad, dtype=jnp.int32)[None, :]).astype(jnp.int32), axis=0)
    tok = jnp.broadcast_to(jnp.arange(T, dtype=jnp.int32)[:, None], (T, TOP_K))
    keys = jnp.concatenate([((2 * top_e) << TOK_BITS | tok).reshape(N_ASSIGN),
                            (2 * pad_expert + 1) << TOK_BITS | T])
    wts = jnp.concatenate([top_w.reshape(N_ASSIGN), jnp.zeros((n_pad,), F32)])
    keys, row_w = lax.sort((keys, wts), num_keys=1)
    row_tok = keys & ((1 << TOK_BITS) - 1)
    blk0 = jnp.arange(N_BLOCKS, dtype=jnp.int32) * MOE_M
    blk_e = jnp.minimum(jnp.sum((pad_end[:, None] <= blk0[None, :]).astype(jnp.int32), axis=0), N_EXPERTS - 1)
    first = (blk0 == pad_start[blk_e]).astype(jnp.int32)
    later = (experts[None, :] > experts[:, None]) & (counts[None, :] > 0)
    nxt = jnp.min(jnp.where(later, experts[None, :], N_EXPERTS), axis=1)
    next_e = jnp.where(nxt == N_EXPERTS, -1, nxt)[blk_e].astype(jnp.int32)
    return (row_tok.reshape(N_BLOCKS, 1, MOE_M), row_w.reshape(N_BLOCKS, 1, MOE_M), blk_e.astype(jnp.int32),
            first, next_e, n_used.reshape(1))


def _residual_kernel(*refs, final):
    x_ref, moe_ref, mod_ref, g_ref = refs[0:4]
    moe = jnp.concatenate([moe_ref[pl.ds(c, TM, stride=ROW_VREGS), :] for c in range(ROW_VREGS)], axis=1)
    x = x_ref[...] + mod_ref[0, 5:6, :] * moe
    if not final:
        refs[4][...] = x
        return
    ms = jnp.mean(x * x, axis=-1, keepdims=True)
    y = x * lax.rsqrt(ms + EPS) * g_ref[...]
    yc_o, yl_o = refs[4:6]
    i = pl.program_id(0)

    @pl.when(i < N_CTX_TILES)
    def _():
        yc_o[...] = y

    @pl.when(i >= N_CTX_TILES)
    def _():
        yl_o[...] = y


def _residual(x1, moe_tiles, mod_l, g, final):
    row = lambda i: (i, 0)
    if final:
        out_specs = [pl.BlockSpec((TM, D_MODEL), _ctx_tile), pl.BlockSpec((TM, D_MODEL), _lat_tile)]
        out_shape = [jax.ShapeDtypeStruct((T_CTX, D_MODEL), F32), jax.ShapeDtypeStruct((T_LAT, D_MODEL), F32)]
    else:
        out_specs = pl.BlockSpec((TM, D_MODEL), row)
        out_shape = jax.ShapeDtypeStruct((T, D_MODEL), F32)
    return pl.pallas_call(
        functools.partial(_residual_kernel, final=final),
        grid=(T // TM,),
        in_specs=[pl.BlockSpec((TM, D_MODEL), row),
                  pl.BlockSpec((TM * ROW_VREGS, LANES), row),
                  pl.BlockSpec((1, 6, D_MODEL), lambda i: (_mod_group(i), 0, 0)),
                  pl.BlockSpec((1, D_MODEL), lambda i: (0, 0))],
        out_specs=out_specs,
        out_shape=out_shape,
        compiler_params=pltpu.CompilerParams(vmem_limit_bytes=32 * MIB),
        name="residual_final" if final else "residual",
    )(x1, moe_tiles, mod_l, g)


def _rope_tables():
    t = np.arange(DEC_SEQ)
    pos = np.stack([t // GRID_W, t % GRID_W], axis=1).astype(np.float32)
    half = HEAD_DIM // 2
    inv = jnp.asarray(ROPE_BASE, F32) ** (-jnp.arange(0, half, 2, dtype=F32) / half)
    d = np.arange(HEAD_DIM)
    which = d // half
    freq = d % (half // 2)
    sign = np.where((d % half) < half // 2, -1.0, 1.0).astype(np.float32)
    ang = jnp.asarray(pos)[:, which] * inv[freq][None, :]
    cos = jnp.cos(ang)
    sin = jnp.sin(ang) * jnp.asarray(sign)[None, :]
    reps = LANES // HEAD_DIM
    return jnp.tile(cos, (1, reps)), jnp.tile(sin, (1, reps))


def kernel(x_prompt, x_sample, cache_k, cache_v, state_ret, c, c_ctx, norm1_g, norm2_g, w_mod, b_mod, w_in, conv_w, attn_sink, ret_decay, w_a, w_b, w_c, w_o, w_router, b_router, w_gu, b_gu, w_down, b_down, final_g):
    xs = (x_prompt.reshape(T_CTX, D_MODEL), x_sample.reshape(T_LAT, D_MODEL))
    cond8 = jnp.zeros((8, D_MODEL), F32).at[0].set(c_ctx).at[1:1 + DEC_BATCH].set(c)
    mod = _modulation(cond8, w_mod, b_mod)
    cos, sin = _rope_tables()
    ck = cache_k.reshape(DEC_BATCH, DEPTH, PAST_LEN, D_KV)
    cv = cache_v.reshape(DEC_BATCH, DEPTH, PAST_LEN, D_KV)

    ks, vs, rs = [], [], []
    for l in range(DEPTH):
        mod_l = mod[l, 0:1 + DEC_BATCH].reshape(1 + DEC_BATCH, 6, D_MODEL)
        conv_in, q, k, v, ret, gates = _inproj(xs, mod_l, norm1_g[l][None, :], w_in, l)
        rd8 = jnp.broadcast_to(ret_decay[l].reshape(2 * RET_HEADS, 1), (2 * RET_HEADS, LANES))
        *ys_ctx, rfin = _mixers(conv_in, q, k, v, ret, conv_w[l], attn_sink[l], rd8, latent=False)
        ys_lat = _mixers(conv_in, q, k, v, ret, conv_w[l], attn_sink[l], rd8, latent=True,
                         cos=cos, sin=sin, cache_k=ck, cache_v=cv, state=state_ret, layer=l)
        wr =jnp.pad(w_router[l], ((0, 0), (0, LANES - N_EXPERTS)))
        wrh = wr.astype(BF16)
        wrl = (wr - wrh.astype(F32)).astype(BF16)
        br = jnp.pad(b_router[l], (0, LANES - N_EXPERTS))[None, :]
        x1, xp, top_e, top_w, cnt = _merge(xs, ys_ctx, ys_lat, gates, mod_l, norm2_g[l][None, :], w_a, w_b, w_c,
                                           w_o, wrh, wrl, br, l)
        row_tok, row_w, blk_e, first, next_e, n_used = _route(top_e[:, 0:TOP_K], top_w[:, 0:TOP_K],
                                                              cnt[0, 0:N_EXPERTS])
        y_tiles = _moe_experts(blk_e, first, next_e, n_used, row_tok, row_w, xp, w_gu, b_gu, w_down, b_down, l)
        moe = _combine(n_used, row_tok, y_tiles)
        final = l == DEPTH - 1
        xs = _residual(x1, moe, mod_l, final_g[None, :], final)
        if not final:
            xs = (xs,)
        ks.append(k[0:T_CTX].reshape(BATCH, SEQ, N_KV, HEAD_DIM))
        vs.append(v[0:T_CTX].reshape(BATCH, SEQ, N_KV, HEAD_DIM))
        rs.append(rfin)

    y_prompt = xs[0].reshape(BATCH, SEQ, D_MODEL)
    y_sample = xs[1].reshape(DEC_BATCH, DEC_SEQ, D_MODEL)
    return (y_prompt, y_sample, jnp.stack(ks, axis=1), jnp.stack(vs, axis=1), jnp.stack(rs, axis=1))
```

```python
import functools

import numpy as np
import jax
import jax.numpy as jnp
from jax import lax
from jax.experimental import pallas as pl
from jax.experimental.pallas import tpu as pltpu

F32 = jnp.float32
BF16 = jnp.bfloat16

D_MODEL = 1024
BATCH = 16
SEQ = 256
DEPTH = 2
DEC_BATCH = 2
DEC_SEQ = 2048
PAST_LEN = 256
GRID_W = 64
HEAD_DIM = 64
D_CONV = 256
N_HEADS = 8
N_KV = 2
GROUP = N_HEADS // N_KV
WINDOW = 128
ROPE_BASE = 10000.0
RET_HEADS = 4
RET_DK = 64
RET_DV = 64
CHUNK = 128
N_EXPERTS = 32
TOP_K = 4
D_EXPERT = D_MODEL
SWIGLU_LIMIT = 7.0
SWIGLU_ALPHA = 1.702
EPS = 1e-6
NEG_INF = -1e30

T_CTX = BATCH * SEQ
T_LAT = DEC_BATCH * DEC_SEQ
T = T_CTX + T_LAT
D_ATTN = N_HEADS * HEAD_DIM
D_KV = N_KV * HEAD_DIM
D_RET = RET_HEADS * RET_DK
C_CONV = 0
C_Q = 3 * D_CONV
C_K = C_Q + D_ATTN
C_V = C_K + D_KV
C_RET = C_V + D_KV
C_GATE = C_RET + 4 * D_RET
IN_COLS = C_GATE + 3 * D_MODEL

TM = 512
MOE_M = 256
N_ASSIGN = T * TOP_K
N_BLOCKS = (N_ASSIGN + N_EXPERTS * (MOE_M - 1) + MOE_M - 1) // MOE_M
N_ROWS = N_BLOCKS * MOE_M
LANES = 128
SUBLANES = 8
ROW_VREGS = D_MODEL // LANES
PACK_ROWS = ROW_VREGS // 2
GATHER_STRIDE = MOE_M + SUBLANES
SCATTER_M = 128
SCATTER_STRIDE = SCATTER_M + SUBLANES
COMBINE_BLOCKS = 4
TOK_BITS = 14
assert T < (1 << TOK_BITS)
MIB = 1024 * 1024


def _sigmoid(x):
    return 1.0 / (1.0 + jnp.exp(-x))


def _mod_group(i):
    n_ctx = T_CTX // TM
    per_lat = DEC_SEQ // TM
    g = jnp.zeros_like(i)
    for b in range(DEC_BATCH):
        g = g + (i >= n_ctx + b * per_lat).astype(jnp.int32)
    return g


def _mod_kernel(cond_ref, w_ref, b_ref, o_ref):
    c = cond_ref[...]
    s = c * _sigmoid(c)
    o_ref[0] = jnp.dot(s.astype(BF16), w_ref[0].astype(BF16), preferred_element_type=F32) + b_ref[0]


def _modulation(cond8, w_mod, b_mod):
    n_col = 4
    cw = 6 * D_MODEL // n_col
    return pl.pallas_call(
        _mod_kernel,
        grid=(DEPTH, n_col),
        in_specs=[pl.BlockSpec((8, D_MODEL), lambda l, j: (0, 0)),
                  pl.BlockSpec((1, D_MODEL, cw), lambda l, j: (l, 0, j)),
                  pl.BlockSpec((1, 1, cw), lambda l, j: (l, 0, j))],
        out_specs=pl.BlockSpec((1, 8, cw), lambda l, j: (l, 0, j)),
        out_shape=jax.ShapeDtypeStruct((DEPTH, 8, 6 * D_MODEL), F32),
        compiler_params=pltpu.CompilerParams(vmem_limit_bytes=32 * MIB),
        name="modulation",
    )(cond8, w_mod, b_mod.reshape(DEPTH, 1, 6 * D_MODEL))


N_CTX_TILES = T_CTX // TM


def _ctx_tile(i):
    return (jnp.minimum(i, N_CTX_TILES - 1), 0)


def _lat_tile(i):
    return (jnp.maximum(i - N_CTX_TILES, 0), 0)


def _pick(ctx_ref, lat_ref):
    return jnp.where(pl.program_id(0) < N_CTX_TILES, ctx_ref[...], lat_ref[...])


def _tiles_to_rows(tiles_ref):
    return jnp.concatenate([tiles_ref[pl.ds(c, TM, stride=ROW_VREGS), :] for c in range(ROW_VREGS)], axis=1)


def _inproj_kernel(*refs, first_layer):
    if first_layer:
        xc_ref, xl_ref, mod_ref, g_ref, w_ref, conv_o, q_o, k_o, v_o, ret_o, gate_o = refs
        x = _pick(xc_ref, xl_ref)
    else:
        x1_ref, moe_ref, modp_ref, mod_ref, g_ref, w_ref, x_o, conv_o, q_o, k_o, v_o, ret_o, gate_o = refs
        x = x1_ref[...] + modp_ref[0, 5:6, :] * _tiles_to_rows(moe_ref)
        x_o[...] = x
    ms = jnp.mean(x * x, axis=-1, keepdims=True)
    h = x * lax.rsqrt(ms + EPS) * g_ref[...]
    h = h * (1.0 + mod_ref[0, 1:2, :]) + mod_ref[0, 0:1, :]
    hb = h.astype(BF16)

    def proj(c0, c1):
        return jnp.dot(hb, w_ref[0, :, c0:c1].astype(BF16), preferred_element_type=F32)

    conv_o[...] = proj(C_CONV, C_Q).astype(BF16)
    q_o[...] = (proj(C_Q, C_K) * HEAD_DIM ** -0.5).astype(BF16)
    k_o[...] = proj(C_K, C_V)
    v_o[...] = proj(C_V, C_RET)
    ret_o[:, 0:D_RET] = proj(C_RET, C_RET + D_RET).astype(BF16)
    ret_o[:, D_RET:2 * D_RET] = (proj(C_RET + D_RET, C_RET + 2 * D_RET) * RET_DK ** -0.5).astype(BF16)
    ret_o[:, 2 * D_RET:4 * D_RET] = proj(C_RET + 2 * D_RET, C_GATE).astype(BF16)
    for b in range(3):
        g = proj(C_GATE + b * D_MODEL, C_GATE + (b + 1) * D_MODEL)
        gate_o[:, b * D_MODEL:(b + 1) * D_MODEL] = _sigmoid(g).astype(BF16)


def _x_specs(xs):
    if len(xs) == 2:
        return [pl.BlockSpec((TM, D_MODEL), _ctx_tile), pl.BlockSpec((TM, D_MODEL), _lat_tile)]
    return [pl.BlockSpec((TM, D_MODEL), lambda i: (i, 0))]


def _inproj(xs, mod_l, g1, w_in, layer, prev=None):
    row = lambda i: (i, 0)
    mod_spec = pl.BlockSpec((1, 6, D_MODEL), lambda i: (_mod_group(i), 0, 0))
    out_specs = [pl.BlockSpec((TM, 3 * D_CONV), row),
                 pl.BlockSpec((TM, D_ATTN), row),
                 pl.BlockSpec((TM, D_KV), row),
                 pl.BlockSpec((TM, D_KV), row),
                 pl.BlockSpec((TM, 4 * D_RET), row),
                 pl.BlockSpec((TM, 3 * D_MODEL), row)]
    out_shape = [jax.ShapeDtypeStruct((T, 3 * D_CONV), BF16),
                 jax.ShapeDtypeStruct((T, D_ATTN), BF16),
                 jax.ShapeDtypeStruct((T, D_KV), F32),
                 jax.ShapeDtypeStruct((T, D_KV), F32),
                 jax.ShapeDtypeStruct((T, 4 * D_RET), BF16),
                 jax.ShapeDtypeStruct((T, 3 * D_MODEL), BF16)]
    if prev is None:
        in_specs, args = _x_specs(xs), list(xs)
    else:
        in_specs = [pl.BlockSpec((TM, D_MODEL), row), pl.BlockSpec((TM * ROW_VREGS, LANES), row), mod_spec]
        args = list(prev)
        out_specs = [pl.BlockSpec((TM, D_MODEL), row)] + out_specs
        out_shape = [jax.ShapeDtypeStruct((T, D_MODEL), F32)] + out_shape
    return pl.pallas_call(
        functools.partial(_inproj_kernel, first_layer=prev is None),
        grid=(T // TM,),
        in_specs=in_specs + [
            mod_spec,
            pl.BlockSpec((1, D_MODEL), lambda i: (0, 0)),
            pl.BlockSpec((1, D_MODEL, IN_COLS), lambda i: (layer, 0, 0), pipeline_mode=pl.Buffered(1))],
        out_specs=out_specs,
        out_shape=out_shape,
        compiler_params=pltpu.CompilerParams(vmem_limit_bytes=60 * MIB),
        name="inproj",
    )(*args, mod_l, g1, w_in)


def _rope(x, cos, sin_signed):
    lane = lax.broadcasted_iota(jnp.int32, x.shape, 1)
    first = (lane % 32) < 16
    partner = jnp.where(first, pltpu.roll(x, x.shape[1] - 16, 1), pltpu.roll(x, 16, 1))
    return x * cos + partner * sin_signed


HALF = LANES // 2
assert HEAD_DIM == HALF and RET_DK == HALF and RET_DV == HALF
N_PAIRS = N_HEADS // 2
RET_PAIRS = RET_HEADS // 2


def _row_variants(kt):
    row = lax.broadcasted_iota(jnp.int32, kt.shape, 0)
    lo0 = jnp.where(row < HALF, kt, 0.0)
    hi1 = jnp.where(row >= HALF, kt, 0.0)
    return lo0, pltpu.roll(lo0, HALF, 0), pltpu.roll(hi1, HALF, 0), hi1


def _lane_variants(v):
    lane = lax.broadcasted_iota(jnp.int32, v.shape, 1)
    lo0 = jnp.where(lane < HALF, v, 0.0)
    hi1 = jnp.where(lane >= HALF, v, 0.0)
    return lo0, pltpu.roll(lo0, HALF, 1), pltpu.roll(hi1, HALF, 1), hi1


def _mixer_kernel(*refs, n, latent):
    if latent:
        (sink_ref, conv_ref, q_ref, k_ref, v_ref, ret_ref, cw_ref, rd_ref, cos_ref, sin_ref, ck_ref, cv_ref,
         r0_ref, yc_ref, ya_ref, yr_ref, ktq, vq, rkp, o_f, o_b, rst, dmat, qdec, kdec, cdec) = refs
    else:
        (sink_ref, conv_ref, q_ref, k_ref, v_ref, ret_ref, cw_ref, rd_ref,
         yc_ref, ya_ref, yr_ref, rfin_ref, ktq, vq, rkp, o_f, o_b, rst, dmat, qdec, kdec, cdec) = refs
    nb = n // CHUNK
    pad = CHUNK if latent else 0

    cv = conv_ref[...].astype(F32)
    cb, cc, cu = cv[:, 0:D_CONV], cv[:, D_CONV:2 * D_CONV], cv[:, 2 * D_CONV:3 * D_CONV]
    p = cc * cu
    row = lax.broadcasted_iota(jnp.int32, p.shape, 0)
    prev = jnp.where(row == 0, 0.0, pltpu.roll(p, 1, 0))
    nxt = jnp.where(row == n - 1, 0.0, pltpu.roll(p, n - 1, 0))
    cw = cw_ref[...]
    yc_ref[...] = (cb * (prev * cw[0:1, :] + p * cw[1:2, :] + nxt * cw[2:3, :])).astype(BF16)

    kf = k_ref[...]
    if latent:
        kf = _rope(kf, cos_ref[...], sin_ref[...])
    for idx, (kk, vv) in enumerate(zip(_row_variants(kf.T), _lane_variants(v_ref[...]))):
        if latent:
            ktq[idx, :, 0:pad] = jnp.zeros((LANES, pad), BF16)
            ktq[idx, :, pad + n:2 * pad + n] = jnp.zeros((LANES, pad), BF16)
            vq[idx, 0:pad, :] = jnp.zeros((pad, LANES), BF16)
            vq[idx, pad + n:2 * pad + n, :] = jnp.zeros((pad, LANES), BF16)
        ktq[idx, :, pad:pad + n] = kk.astype(BF16)
        vq[idx, pad:pad + n, :] = vv.astype(BF16)
    if latent:
        cktq = [t.astype(BF16) for t in _row_variants(ck_ref[0, 0].T)]
        cvq = [t.astype(BF16) for t in _lane_variants(cv_ref[0, 0])]

    def attn(r0, rows):
        qj = q_ref[pl.ds(r0, rows), :]
        if latent:
            cosj = cos_ref[pl.ds(r0, rows), :]
            sinj = sin_ref[pl.ds(r0, rows), :]
            qpos = r0 + lax.broadcasted_iota(jnp.int32, (rows, 3 * CHUNK), 0)
            kpos = r0 - CHUNK + lax.broadcasted_iota(jnp.int32, (rows, 3 * CHUNK), 1)
            ok = (jnp.abs(qpos - kpos) <= WINDOW) & (kpos >= 0) & (kpos < n)
        for m in range(N_PAIRS):
            q2 = qj[:, m * LANES:(m + 1) * LANES]
            if latent:
                q2 = _rope(q2.astype(F32), cosj, sinj).astype(BF16)
            g = (2 * m) // GROUP
            acc = None
            for half in range(2):
                idx = 2 * g + half
                sk = sink_ref[2 * m + half]
                if latent:
                    s = jnp.dot(q2, ktq[idx, :, pl.ds(r0, 3 * CHUNK)], preferred_element_type=F32)
                    s = jnp.where(ok, s, NEG_INF)
                    s2 = jnp.dot(q2, cktq[idx], preferred_element_type=F32)
                    mx = jnp.maximum(jnp.maximum(jnp.max(s, axis=-1, keepdims=True),
                                                 jnp.max(s2, axis=-1, keepdims=True)), sk)
                    pw = jnp.exp(s - mx)
                    p2 = jnp.exp(s2 - mx)
                    den = (jnp.sum(pw, axis=-1, keepdims=True) + jnp.sum(p2, axis=-1, keepdims=True)
                           + jnp.exp(sk - mx))
                    o = (jnp.dot(pw.astype(BF16), vq[idx, pl.ds(r0, 3 * CHUNK), :], preferred_element_type=F32)
                         + jnp.dot(p2.astype(BF16), cvq[idx], preferred_element_type=F32))
                else:
                    s = jnp.dot(q2, ktq[idx], preferred_element_type=F32)
                    mx = jnp.maximum(jnp.max(s, axis=-1, keepdims=True), sk)
                    pw = jnp.exp(s - mx)
                    den = jnp.sum(pw, axis=-1, keepdims=True) + jnp.exp(sk - mx)
                    o = jnp.dot(pw.astype(BF16), vq[idx], preferred_element_type=F32)
                o = o / den
                acc = o if acc is None else acc + o
            ya_ref[pl.ds(r0, rows), m * LANES:(m + 1) * LANES] = acc.astype(BF16)

    rk_t = ret_ref[:, D_RET:2 * D_RET].astype(F32).T
    for m in range(RET_PAIRS):
        rkp[m] = rk_t[m * LANES:(m + 1) * LANES, :].astype(BF16)
    rd = rd_ref[...]
    log_g = jnp.minimum(rd, 0.0) - jnp.log(1.0 + jnp.exp(-jnp.abs(rd)))
    row_c = lax.broadcasted_iota(jnp.int32, (CHUNK, CHUNK), 0)
    lane_c = lax.broadcasted_iota(jnp.int32, (CHUNK, CHUNK), 1)
    ii = row_c.astype(F32)
    jj = lane_c.astype(F32)
    even_row = row_c < HALF
    even_lane = lane_c < HALF
    blockdiag = even_row == even_lane
    for d in range(2):
        for h in range(RET_HEADS):
            r = d * RET_HEADS + h
            diff = (ii - jj) if d == 0 else (jj - ii)
            dmat[r] = jnp.where(diff >= 0, jnp.exp(jnp.maximum(diff, 0.0) * log_g[r:r + 1, :]), 0.0)
        for m in range(RET_PAIRS):
            s = d * RET_PAIRS + m
            lg_e = log_g[d * RET_HEADS + 2 * m:d * RET_HEADS + 2 * m + 1, :]
            lg_o = log_g[d * RET_HEADS + 2 * m + 1:d * RET_HEADS + 2 * m + 2, :]
            qpow = (ii + 1.0) if d == 0 else (CHUNK - ii)
            kpow = (CHUNK - 1.0 - jj) if d == 0 else jj
            qdec[s] = jnp.where(even_lane, jnp.exp(qpow * lg_e), jnp.exp(qpow * lg_o))
            kdec[s] = jnp.where(even_row, jnp.exp(kpow * lg_e), jnp.exp(kpow * lg_o))
            chunk_decay = jnp.where(even_row, jnp.exp(float(CHUNK) * lg_e), jnp.exp(float(CHUNK) * lg_o))
            cdec[s] = jnp.where(blockdiag, chunk_decay, 0.0)
            if latent:
                z = jnp.zeros((HALF, HALF), F32)
                rst[s] = jnp.concatenate(
                    [jnp.concatenate([r0_ref[0, 0, d, 2 * m], z], axis=1),
                     jnp.concatenate([z, r0_ref[0, 0, d, 2 * m + 1]], axis=1)], axis=0)
            else:
                rst[s] = jnp.zeros((LANES, LANES), F32)

    def ret_chunk(c0, d, out_ref):
        for m in range(RET_PAIRS):
            s = d * RET_PAIRS + m
            q2 = ret_ref[pl.ds(c0, CHUNK), m * LANES:(m + 1) * LANES]
            v2 = ret_ref[pl.ds(c0, CHUNK), 2 * D_RET + m * LANES:2 * D_RET + (m + 1) * LANES]
            kt2 = rkp[m, :, pl.ds(c0, CHUNK)].astype(F32)
            v2f = v2.astype(F32)
            state = rst[s]
            o2 = jnp.dot(q2, state.astype(BF16), preferred_element_type=F32) * qdec[s]
            for half in range(2):
                r = d * RET_HEADS + 2 * m + half
                keep_row = even_row if half == 0 else jnp.logical_not(even_row)
                keep_lane = even_lane if half == 0 else jnp.logical_not(even_lane)
                a = jnp.dot(q2, jnp.where(keep_row, kt2, 0.0).astype(BF16), preferred_element_type=F32)
                inner = (a * dmat[r]).astype(BF16)
                o2 = o2 + jnp.dot(inner, jnp.where(keep_lane, v2f, 0.0).astype(BF16), preferred_element_type=F32)
            kd = (kt2 * kdec[s]).astype(BF16)
            upd = jnp.dot(kd, v2, preferred_element_type=F32)
            rst[s] = state * cdec[s] + jnp.where(blockdiag, upd, 0.0)
            out_ref[pl.ds(c0, CHUNK), m * LANES:(m + 1) * LANES] = o2

    gi = lax.broadcasted_iota(jnp.int32, (D_RET, D_RET), 0) // RET_DV
    gj = lax.broadcasted_iota(jnp.int32, (D_RET, D_RET), 1) // RET_DV
    group_mean = jnp.where(gi == gj, 1.0 / RET_DV, 0.0).astype(BF16)

    def norm(c0, rows):
        o = o_f[pl.ds(c0, rows), :] + o_b[pl.ds(c0, rows), :]
        sq = o * o
        hi = sq.astype(BF16)
        lo = (sq - hi.astype(F32)).astype(BF16)
        ms = (jnp.dot(hi, group_mean, preferred_element_type=F32)
              + jnp.dot(lo, group_mean, preferred_element_type=F32))
        rg = ret_ref[pl.ds(c0, rows), 3 * D_RET:4 * D_RET].astype(F32)
        yr_ref[pl.ds(c0, rows), :] = (rg * _sigmoid(rg) * (o * lax.rsqrt(ms + EPS))).astype(BF16)

    if latent:
        def scan_body(j, carry):
            attn(pl.multiple_of(j * CHUNK, CHUNK), CHUNK)
            ret_chunk(pl.multiple_of(j * CHUNK, CHUNK), 0, o_f)
            ret_chunk(pl.multiple_of((nb - 1 - j) * CHUNK, CHUNK), 1, o_b)
            return carry

        def norm_body(j, carry):
            norm(pl.multiple_of(j * CHUNK, CHUNK), CHUNK)
            return carry

        lax.fori_loop(0, nb, scan_body, 0)
        lax.fori_loop(0, nb, norm_body, 0)
    else:
        attn(0, n)
        for j in range(nb):
            ret_chunk(j * CHUNK, 0, o_f)
            ret_chunk((nb - 1 - j) * CHUNK, 1, o_b)
        norm(0, n)
        for d in range(2):
            for h in range(RET_HEADS):
                lo_ = (h % 2) * HALF
                rfin_ref[0, d, h] = rst[d * RET_PAIRS + h // 2][lo_:lo_ + HALF, lo_:lo_ + HALF]


def _mixers(conv_in, q, k, v, ret, conv_w_l, sink_l, rd8, *, latent, cos=None, sin=None, cache_k=None,
            cache_v=None, state=None, layer=0):
    n = DEC_SEQ if latent else SEQ
    nseq = DEC_BATCH if latent else BATCH
    off = T_CTX // n if latent else 0
    seq = lambda s: (s + off, 0)
    const = lambda s: (0, 0)
    in_specs = [pl.BlockSpec(memory_space=pltpu.SMEM),
                pl.BlockSpec((n, 3 * D_CONV), seq),
                pl.BlockSpec((n, D_ATTN), seq),
                pl.BlockSpec((n, D_KV), seq),
                pl.BlockSpec((n, D_KV), seq),
                pl.BlockSpec((n, 4 * D_RET), seq),
                pl.BlockSpec((3, D_CONV), const),
                pl.BlockSpec((8, LANES), const)]
    args = [sink_l, conv_in, q, k, v, ret, conv_w_l, rd8]
    out_specs = [pl.BlockSpec((n, D_CONV), lambda s: (s, 0)),
                 pl.BlockSpec((n, D_ATTN), lambda s: (s, 0)),
                 pl.BlockSpec((n, D_RET), lambda s: (s, 0))]
    out_shape = [jax.ShapeDtypeStruct((nseq * n, D_CONV), BF16),
                 jax.ShapeDtypeStruct((nseq * n, D_ATTN), BF16),
                 jax.ShapeDtypeStruct((nseq * n, D_RET), BF16)]
    if latent:
        in_specs += [pl.BlockSpec((n, LANES), const),
                     pl.BlockSpec((n, LANES), const),
                     pl.BlockSpec((1, 1, PAST_LEN, D_KV), lambda s: (s, layer, 0, 0)),
                     pl.BlockSpec((1, 1, PAST_LEN, D_KV), lambda s: (s, layer, 0, 0)),
                     pl.BlockSpec((1, 1, 2, RET_HEADS, RET_DK, RET_DV), lambda s: (s, layer, 0, 0, 0, 0))]
        args += [cos, sin, cache_k, cache_v, state]
        kv_rows = n + 2 * CHUNK
    else:
        out_specs.append(pl.BlockSpec((1, 2, RET_HEADS, RET_DK, RET_DV), lambda s: (s, 0, 0, 0, 0)))
        out_shape.append(jax.ShapeDtypeStruct((nseq, 2, RET_HEADS, RET_DK, RET_DV), F32))
        kv_rows = n
    scratch = [pltpu.VMEM((2 * N_KV, LANES, kv_rows), BF16),
               pltpu.VMEM((2 * N_KV, kv_rows, LANES), BF16),
               pltpu.VMEM((RET_PAIRS, LANES, n), BF16),
               pltpu.VMEM((n, D_RET), F32),
               pltpu.VMEM((n, D_RET), F32),
               pltpu.VMEM((2 * RET_PAIRS, LANES, LANES), F32),
               pltpu.VMEM((2 * RET_HEADS, CHUNK, CHUNK), F32),
               pltpu.VMEM((2 * RET_PAIRS, CHUNK, CHUNK), F32),
               pltpu.VMEM((2 * RET_PAIRS, CHUNK, CHUNK), F32),
               pltpu.VMEM((2 * RET_PAIRS, CHUNK, CHUNK), F32)]
    return pl.pallas_call(
        functools.partial(_mixer_kernel, n=n, latent=latent),
        grid=(nseq,),
        in_specs=in_specs,
        out_specs=out_specs,
        out_shape=out_shape,
        scratch_shapes=scratch,
        compiler_params=pltpu.CompilerParams(vmem_limit_bytes=56 * MIB),
        name="mixers_latent" if latent else "mixers_context",
    )(*args)


def _merge_kernel(*refs, split):
    if split:
        xc_ref, xl_ref = refs[0:2]
        x = _pick(xc_ref, xl_ref)
        refs = refs[2:]
    else:
        x = refs[0][...]
        refs = refs[1:]
    (ycc_ref, ycl_ref, yac_ref, yal_ref, yrc_ref, yrl_ref, gate_ref, mod_ref, g2_ref, wa_ref, wb_ref, wc_ref,
     wo_ref, wrh_ref, wrl_ref, br_ref, x1_o, xp_o, te_o, tw_o, cnt_o) = refs
    merged = (gate_ref[:, 0:D_MODEL].astype(F32)
              * jnp.dot(_pick(ycc_ref, ycl_ref), wa_ref[0].astype(BF16), preferred_element_type=F32)
              + gate_ref[:, D_MODEL:2 * D_MODEL].astype(F32)
              * jnp.dot(_pick(yac_ref, yal_ref), wb_ref[0].astype(BF16), preferred_element_type=F32)
              + gate_ref[:, 2 * D_MODEL:3 * D_MODEL].astype(F32)
              * jnp.dot(_pick(yrc_ref, yrl_ref), wc_ref[0].astype(BF16), preferred_element_type=F32))
    x1 = x + mod_ref[0, 2:3, :] * jnp.dot(merged.astype(BF16), wo_ref[0].astype(BF16),
                                          preferred_element_type=F32)
    x1_o[...] = x1
    ms = jnp.mean(x1 * x1, axis=-1, keepdims=True)
    h2 = x1 * lax.rsqrt(ms + EPS) * g2_ref[...]
    h2 = h2 * (1.0 + mod_ref[0, 4:5, :]) + mod_ref[0, 3:4, :]
    hh = h2.astype(BF16)
    hf = hh.astype(F32)
    bits = lax.bitcast_convert_type(hf, jnp.uint32)
    for c in range(PACK_ROWS):
        lo = bits[:, c * LANES:(c + 1) * LANES] >> 16
        hi = bits[:, (c + PACK_ROWS) * LANES:(c + PACK_ROWS + 1) * LANES] & jnp.uint32(0xFFFF0000)
        xp_o[pl.ds(c, TM, stride=PACK_ROWS), :] = lax.bitcast_convert_type(lo | hi, jnp.int32)
    hl = (h2 - hf).astype(BF16)
    logits = (jnp.dot(hh, wrh_ref[...], preferred_element_type=F32)
              + jnp.dot(hl, wrh_ref[...], preferred_element_type=F32)
              + jnp.dot(hh, wrl_ref[...], preferred_element_type=F32)
              + br_ref[...])
    lane = lax.broadcasted_iota(jnp.int32, logits.shape, 1)
    work = jnp.where(lane < N_EXPERTS, logits, -jnp.inf)
    vals, idxs = [], []
    for _ in range(TOP_K):
        m = jnp.max(work, axis=-1, keepdims=True)
        am = jnp.min(jnp.where(work == m, lane, LANES), axis=-1, keepdims=True)
        vals.append(m)
        idxs.append(am)
        work = jnp.where(lane == am, -jnp.inf, work)
    es = [jnp.exp(v - vals[0]) for v in vals]
    den = es[0] + es[1] + es[2] + es[3]
    te = jnp.zeros(logits.shape, jnp.int32)
    tw = jnp.zeros(logits.shape, F32)
    for k in range(TOP_K):
        te = jnp.where(lane == k, idxs[k], te)
        tw = jnp.where(lane == k, es[k] / den, tw)
    te_o[...] = te
    tw_o[...] = tw
    sel = (lane == idxs[0]) | (lane == idxs[1]) | (lane == idxs[2]) | (lane == idxs[3])
    part = jnp.sum(sel.astype(jnp.int32), axis=0, keepdims=True)

    @pl.when(pl.program_id(0) == 0)
    def _():
        cnt_o[...] = jnp.zeros_like(cnt_o)

    cnt_o[...] += jnp.broadcast_to(part, cnt_o.shape)


def _merge(xs, ys_ctx, ys_lat, gates, mod_l, g2, wa, wb, wc, wo, wrh, wrl, br, layer):
    row = lambda i: (i, 0)
    const = lambda i: (0, 0)
    wl = lambda i: (layer, 0, 0)
    y_specs, y_args = [], []
    for width, yc, yl in zip((D_CONV, D_ATTN, D_RET), ys_ctx, ys_lat):
        y_specs += [pl.BlockSpec((TM, width), _ctx_tile), pl.BlockSpec((TM, width), _lat_tile)]
        y_args += [yc, yl]
    return pl.pallas_call(
        functools.partial(_merge_kernel, split=len(xs) == 2),
        grid=(T // TM,),
        in_specs=_x_specs(xs) + y_specs + [
                  pl.BlockSpec((TM, 3 * D_MODEL), row),
                  pl.BlockSpec((1, 6, D_MODEL), lambda i: (_mod_group(i), 0, 0)),
                  pl.BlockSpec((1, D_MODEL), const),
                  pl.BlockSpec((1, D_CONV, D_MODEL), wl),
                  pl.BlockSpec((1, D_ATTN, D_MODEL), wl),
                  pl.BlockSpec((1, D_RET, D_MODEL), wl),
                  pl.BlockSpec((1, D_MODEL, D_MODEL), wl),
                  pl.BlockSpec((D_MODEL, LANES), const),
                  pl.BlockSpec((D_MODEL, LANES), const),
                  pl.BlockSpec((1, LANES), const)],
        out_specs=[pl.BlockSpec((TM, D_MODEL), row),
                   pl.BlockSpec((TM * PACK_ROWS, LANES), row),
                   pl.BlockSpec((TM, LANES), row),
                   pl.BlockSpec((TM, LANES), row),
                   pl.BlockSpec((SUBLANES, LANES), const)],
        out_shape=[jax.ShapeDtypeStruct((T, D_MODEL), F32),
                   jax.ShapeDtypeStruct((T * PACK_ROWS, LANES), jnp.int32),
                   jax.ShapeDtypeStruct((T, LANES), jnp.int32),
                   jax.ShapeDtypeStruct((T, LANES), F32),
                   jax.ShapeDtypeStruct((SUBLANES, LANES), jnp.int32)],
        compiler_params=pltpu.CompilerParams(vmem_limit_bytes=48 * MIB),
        name="merge_router",
    )(*xs, *y_args, gates, mod_l, g2, wa, wb, wc, wo, wrh, wrl, br)


def _moe_kernel(blk_e_ref, first_ref, next_e_ref, nu_ref, tok_ref, tok_next_ref, xp_ref, wgu_hbm, bgu_ref, wd_hbm,
                bd_ref, y_ref, tile, wgu_st, wd_st, wgu_bf, wd_bf, sems, *, layer):
    b = pl.program_id(0)
    slot = b % 2

    def weight_copies(e):
        return (pltpu.make_async_copy(wgu_hbm.at[layer, e], wgu_st, sems.at[0]),
                pltpu.make_async_copy(wd_hbm.at[layer, e], wd_st, sems.at[1]))

    def gather(tok, dst_slot):
        for mi in range(MOE_M):
            t = jnp.minimum(tok[0, 0, mi], T - 1)
            slab = xp_ref[pl.ds(pl.multiple_of(t * PACK_ROWS, PACK_ROWS), PACK_ROWS), :]
            tile[dst_slot, pl.ds(mi, PACK_ROWS, stride=GATHER_STRIDE), :] = slab

    @pl.when(b == 0)
    def _():
        for cp in weight_copies(blk_e_ref[0]):
            cp.start()
        gather(tok_ref, 0)

    @pl.when(b < nu_ref[0])
    def _():
        @pl.when(first_ref[b] == 1)
        def _():
            for cp in weight_copies(blk_e_ref[b]):
                cp.wait()
            wgu_bf[...] = wgu_st[...].astype(BF16)
            wd_bf[...] = wd_st[...].astype(BF16)

            @pl.when(next_e_ref[b] >= 0)
            def _():
                for cp in weight_copies(next_e_ref[b]):
                    cp.start()

        lo, hi = [], []
        for c in range(PACK_ROWS):
            bits = lax.bitcast_convert_type(
                tile[slot, c * GATHER_STRIDE:c * GATHER_STRIDE + MOE_M, :], jnp.uint32)
            lo.append(lax.bitcast_convert_type(bits << 16, F32).astype(BF16))
            hi.append(lax.bitcast_convert_type(bits & jnp.uint32(0xFFFF0000), F32).astype(BF16))
        x = jnp.concatenate(lo + hi, axis=1)
        gather(tok_next_ref, 1 - slot)

        gu = jnp.dot(x, wgu_bf[...], preferred_element_type=F32) + bgu_ref[0, 0]
        gate = jnp.minimum(gu[:, 0:D_EXPERT], SWIGLU_LIMIT)
        up = jnp.clip(gu[:, D_EXPERT:2 * D_EXPERT], -SWIGLU_LIMIT, SWIGLU_LIMIT)
        glu = gate * _sigmoid(SWIGLU_ALPHA * gate)
        mid = ((up + 1.0) * glu).astype(BF16)
        y = jnp.dot(mid, wd_bf[...], preferred_element_type=F32) + bd_ref[0, 0]
        y_ref[...] = y.astype(BF16)

    @pl.when(b >= nu_ref[0])
    def _():
        y_ref[...] = jnp.zeros_like(y_ref)


def _moe_experts(blk_e, first, next_e, n_used, row_tok3, xp, w_gu, b_gu, w_down, b_down, layer):
    bias = lambda b, e, f, ne, nu: (layer, e[b], 0, 0)
    grid_spec = pltpu.PrefetchScalarGridSpec(
        num_scalar_prefetch=4,
        grid=(N_BLOCKS,),
        in_specs=[pl.BlockSpec((1, 1, MOE_M), lambda b, e, f, ne, nu: (b, 0, 0), memory_space=pltpu.SMEM),
                  pl.BlockSpec((1, 1, MOE_M), lambda b, e, f, ne, nu: (jnp.minimum(b + 1, N_BLOCKS - 1), 0, 0),
                               memory_space=pltpu.SMEM),
                  pl.BlockSpec((T * PACK_ROWS, LANES), lambda b, e, f, ne, nu: (0, 0), pipeline_mode=pl.Buffered(1)),
                  pl.BlockSpec(memory_space=pl.ANY),
                  pl.BlockSpec((1, 1, 1, 2 * D_EXPERT), bias),
                  pl.BlockSpec(memory_space=pl.ANY),
                  pl.BlockSpec((1, 1, 1, D_MODEL), bias)],
        out_specs=pl.BlockSpec((MOE_M, D_MODEL), lambda b, e, f, ne, nu: (b, 0)),
        scratch_shapes=[pltpu.VMEM((2, PACK_ROWS * GATHER_STRIDE, LANES), jnp.int32),
                        pltpu.VMEM((D_MODEL, 2 * D_EXPERT), F32),
                        pltpu.VMEM((D_EXPERT, D_MODEL), F32),
                        pltpu.VMEM((D_MODEL, 2 * D_EXPERT), BF16),
                        pltpu.VMEM((D_EXPERT, D_MODEL), BF16),
                        pltpu.SemaphoreType.DMA((2,))],
    )
    return pl.pallas_call(
        functools.partial(_moe_kernel, layer=layer),
        grid_spec=grid_spec,
        out_shape=jax.ShapeDtypeStruct((N_ROWS, D_MODEL), BF16),
        compiler_params=pltpu.CompilerParams(vmem_limit_bytes=48 * MIB),
        name="moe_experts",
    )(blk_e, first, next_e, n_used, row_tok3, row_tok3, xp, w_gu, b_gu.reshape(DEPTH, N_EXPERTS, 1, 2 * D_EXPERT),
      w_down, b_down.reshape(DEPTH, N_EXPERTS, 1, D_MODEL))


SCATTER_UNROLL = 8


def _combine_kernel(nu_ref, tok_ref, w_ref, y_ref, o_ref, acc, tile, sem):
    s = pl.program_id(0)

    @pl.when(s == 0)
    def _():
        acc[...] = jnp.zeros_like(acc)

    def sub_block(sb, carry):
        r0 = pl.multiple_of(sb * SCATTER_M, SCATTER_M)
        y = y_ref[pl.ds(r0, SCATTER_M), :].astype(F32)
        for c in range(ROW_VREGS):
            tile[c * SCATTER_STRIDE:c * SCATTER_STRIDE + SCATTER_M, :] = y[:, c * LANES:(c + 1) * LANES]
        for m0 in range(0, SCATTER_M, SCATTER_UNROLL):
            addrs, vals = [], []
            for u in range(SCATTER_UNROLL):
                mi = m0 + u
                a = pl.multiple_of(tok_ref[0, 0, r0 + mi] * ROW_VREGS, ROW_VREGS)
                yv = tile[pl.ds(mi, ROW_VREGS, stride=SCATTER_STRIDE), :]
                addrs.append(a)
                vals.append(acc[pl.ds(a, ROW_VREGS), :] + w_ref[0, 0, r0 + mi] * yv)
            for u in range(SCATTER_UNROLL):
                acc[pl.ds(addrs[u], ROW_VREGS), :] = vals[u]
        return carry

    @pl.when(s * COMBINE_BLOCKS < nu_ref[0])
    def _():
        lax.fori_loop(0, COMBINE_BLOCKS * MOE_M // SCATTER_M, sub_block, 0)

    @pl.when(s == pl.num_programs(0) - 1)
    def _():
        cp = pltpu.make_async_copy(acc.at[pl.ds(0, T * ROW_VREGS)], o_ref, sem)
        cp.start()
        cp.wait()


def _combine(n_used, row_tok3, row_w3, yr):
    rows = COMBINE_BLOCKS * MOE_M
    steps = N_BLOCKS // COMBINE_BLOCKS
    grid_spec = pltpu.PrefetchScalarGridSpec(
        num_scalar_prefetch=1,
        grid=(steps,),
        in_specs=[pl.BlockSpec((1, 1, rows), lambda s, nu: (s, 0, 0), memory_space=pltpu.SMEM),
                  pl.BlockSpec((1, 1, rows), lambda s, nu: (s, 0, 0), memory_space=pltpu.SMEM),
                  pl.BlockSpec((rows, D_MODEL), lambda s, nu: (s, 0))],
        out_specs=pl.BlockSpec(memory_space=pl.ANY),
        scratch_shapes=[pltpu.VMEM(((T + 1) * ROW_VREGS, LANES), F32),
                        pltpu.VMEM((ROW_VREGS * SCATTER_STRIDE, LANES), F32),
                        pltpu.SemaphoreType.DMA(())],
    )
    return pl.pallas_call(
        _combine_kernel,
        grid_spec=grid_spec,
        out_shape=jax.ShapeDtypeStruct((T * ROW_VREGS, LANES), F32),
        compiler_params=pltpu.CompilerParams(vmem_limit_bytes=48 * MIB),
        name="moe_combine",
    )(n_used, row_tok3.reshape(steps, 1, rows), row_w3.reshape(steps, 1, rows), yr)


def _route(top_e, top_w, counts):
    experts = jnp.arange(N_EXPERTS, dtype=jnp.int32)
    padded = (counts + MOE_M - 1) // MOE_M * MOE_M
    pad_end = jnp.cumsum(padded)
    pad_start = pad_end - padded
    n_used = (pad_end[-1] // MOE_M).astype(jnp.int32)
    n_pad = N_ROWS - N_ASSIGN
    pad_cum = jnp.cumsum(padded - counts)
    pad_expert = jnp.sum((pad_cum[:, None] <= jnp.arange(n_pad, dtype=jnp.int32)[None, :]).astype(jnp.int32), axis=0)
    tok = jnp.broadcast_to(jnp.arange(T, dtype=jnp.int32)[:, None], (T, TOP_K))
    keys = jnp.concatenate([((2 * top_e) << TOK_BITS | tok).reshape(N_ASSIGN),
                            (2 * pad_expert + 1) << TOK_BITS | T])
    wts = jnp.concatenate([top_w.reshape(N_ASSIGN), jnp.zeros((n_pad,), F32)])
    keys, row_w = lax.sort((keys, wts), num_keys=1)
    row_tok = keys & ((1 << TOK_BITS) - 1)
    blk0 = jnp.arange(N_BLOCKS, dtype=jnp.int32) * MOE_M
    blk_e = jnp.minimum(jnp.sum((pad_end[:, None] <= blk0[None, :]).astype(jnp.int32), axis=0), N_EXPERTS - 1)
    first = (blk0 == pad_start[blk_e]).astype(jnp.int32)
    later = (experts[None, :] > experts[:, None]) & (counts[None, :] > 0)
    nxt = jnp.min(jnp.where(later, experts[None, :], N_EXPERTS), axis=1)
    next_e = jnp.where(nxt == N_EXPERTS, -1, nxt)[blk_e].astype(jnp.int32)
    return (row_tok.reshape(N_BLOCKS, 1, MOE_M), row_w.reshape(N_BLOCKS, 1, MOE_M), blk_e.astype(jnp.int32),
            first, next_e, n_used.reshape(1))


def _final_kernel(x_ref, moe_ref, mod_ref, g_ref, yc_o, yl_o):
    x = x_ref[...] + mod_ref[0, 5:6, :] * _tiles_to_rows(moe_ref)
    ms = jnp.mean(x * x, axis=-1, keepdims=True)
    y = x * lax.rsqrt(ms + EPS) * g_ref[...]
    i = pl.program_id(0)

    @pl.when(i < N_CTX_TILES)
    def _():
        yc_o[...] = y

    @pl.when(i >= N_CTX_TILES)
    def _():
        yl_o[...] = y


def _final(x1, moe_tiles, mod_l, g):
    row = lambda i: (i, 0)
    return pl.pallas_call(
        _final_kernel,
        grid=(T // TM,),
        in_specs=[pl.BlockSpec((TM, D_MODEL), row),
                  pl.BlockSpec((TM * ROW_VREGS, LANES), row),
                  pl.BlockSpec((1, 6, D_MODEL), lambda i: (_mod_group(i), 0, 0)),
                  pl.BlockSpec((1, D_MODEL), lambda i: (0, 0))],
        out_specs=[pl.BlockSpec((TM, D_MODEL), _ctx_tile), pl.BlockSpec((TM, D_MODEL), _lat_tile)],
        out_shape=[jax.ShapeDtypeStruct((T_CTX, D_MODEL), F32), jax.ShapeDtypeStruct((T_LAT, D_MODEL), F32)],
        compiler_params=pltpu.CompilerParams(vmem_limit_bytes=32 * MIB),
        name="residual_final",
    )(x1, moe_tiles, mod_l, g)


def _rope_tables():
    t = np.arange(DEC_SEQ)
    pos = np.stack([t // GRID_W, t % GRID_W], axis=1).astype(np.float32)
    half = HEAD_DIM // 2
    inv = np.float32(ROPE_BASE) ** (-np.arange(0, half, 2, dtype=np.float32) / np.float32(half))
    d = np.arange(HEAD_DIM)
    which = d // half
    freq = d % (half // 2)
    sign = np.where((d % half) < half // 2, -1.0, 1.0).astype(np.float32)
    ang = (pos[:, which] * inv[freq][None, :]).astype(np.float32)
    reps = LANES // HEAD_DIM
    cos = np.tile(np.cos(ang).astype(np.float32), (1, reps))
    sin = np.tile(np.sin(ang).astype(np.float32) * sign[None, :], (1, reps))
    return jnp.asarray(cos), jnp.asarray(sin)


def kernel(x_prompt, x_sample, cache_k, cache_v, state_ret, c, c_ctx, norm1_g, norm2_g, w_mod, b_mod, w_in, conv_w, attn_sink, ret_decay, w_a, w_b, w_c, w_o, w_router, b_router, w_gu, b_gu, w_down, b_down, final_g):
    xs = (x_prompt.reshape(T_CTX, D_MODEL), x_sample.reshape(T_LAT, D_MODEL))
    cond8 = jnp.zeros((8, D_MODEL), F32).at[0].set(c_ctx).at[1:1 + DEC_BATCH].set(c)
    mod = _modulation(cond8, w_mod, b_mod)
    cos, sin = _rope_tables()
    ck = cache_k.reshape(DEC_BATCH, DEPTH, PAST_LEN, D_KV)
    cv = cache_v.reshape(DEC_BATCH, DEPTH, PAST_LEN, D_KV)

    ks, vs, rs = [], [], []
    prev = None
    for l in range(DEPTH):
        mod_l = mod[l, 0:1 + DEC_BATCH].reshape(1 + DEC_BATCH, 6, D_MODEL)
        if prev is None:
            conv_in, q, k, v, ret, gates = _inproj(xs, mod_l, norm1_g[l][None, :], w_in, l)
        else:
            x, conv_in, q, k, v, ret, gates = _inproj(None, mod_l, norm1_g[l][None, :], w_in, l, prev=prev)
            xs = (x,)
        rd8 = jnp.broadcast_to(ret_decay[l].reshape(2 * RET_HEADS, 1), (2 * RET_HEADS, LANES))
        *ys_ctx, rfin = _mixers(conv_in, q, k, v, ret, conv_w[l], attn_sink[l], rd8, latent=False)
        ys_lat = _mixers(conv_in, q, k, v, ret, conv_w[l], attn_sink[l], rd8, latent=True,
                         cos=cos, sin=sin, cache_k=ck, cache_v=cv, state=state_ret, layer=l)
        wr = jnp.pad(w_router[l], ((0, 0), (0, LANES - N_EXPERTS)))
        wrh = wr.astype(BF16)
        wrl = (wr - wrh.astype(F32)).astype(BF16)
        br = jnp.pad(b_router[l], (0, LANES - N_EXPERTS))[None, :]
        x1, xp, top_e, top_w, cnt = _merge(xs, ys_ctx, ys_lat, gates, mod_l, norm2_g[l][None, :], w_a, w_b, w_c,
                                           w_o, wrh, wrl, br, l)
        row_tok, row_w, blk_e, first, next_e, n_used = _route(top_e[:, 0:TOP_K], top_w[:, 0:TOP_K],
                                                              cnt[0, 0:N_EXPERTS])
        yrows = _moe_experts(blk_e, first, next_e, n_used, row_tok, xp, w_gu, b_gu, w_down, b_down, l)
        moe = _combine(n_used, row_tok, row_w, yrows)
        prev = (x1, moe, mod_l)
        ks.append(k[0:T_CTX].reshape(BATCH, SEQ, N_KV, HEAD_DIM))
        vs.append(v[0:T_CTX].reshape(BATCH, SEQ, N_KV, HEAD_DIM))
        rs.append(rfin)

    yc, yl = _final(*prev, final_g[None, :])
    y_prompt = yc.reshape(BATCH, SEQ, D_MODEL)
    y_sample = yl.reshape(DEC_BATCH, DEC_SEQ, D_MODEL)
    return (y_prompt, y_sample, jnp.stack(ks, axis=1), jnp.stack(vs, axis=1), jnp.stack(rs, axis=1))
```

```python
import functools

import numpy as np
import jax
import jax.numpy as jnp
from jax import lax
from jax.experimental import pallas as pl
from jax.experimental.pallas import tpu as pltpu

F32 = jnp.float32
BF16 = jnp.bfloat16

D_MODEL = 1024
BATCH = 16
SEQ = 256
DEPTH = 2
DEC_BATCH = 2
DEC_SEQ = 2048
PAST_LEN = 256
GRID_W = 64
HEAD_DIM = 64
D_CONV = 256
N_HEADS = 8
N_KV = 2
GROUP = N_HEADS // N_KV
WINDOW = 128
ROPE_BASE = 10000.0
RET_HEADS = 4
RET_DK = 64
RET_DV = 64
CHUNK = 128
N_EXPERTS = 32
TOP_K = 4
D_EXPERT = D_MODEL
SWIGLU_LIMIT = 7.0
SWIGLU_ALPHA = 1.702
EPS = 1e-6
NEG_INF = -1e30

T_CTX = BATCH * SEQ
T_LAT = DEC_BATCH * DEC_SEQ
T = T_CTX + T_LAT
D_ATTN = N_HEADS * HEAD_DIM
D_KV = N_KV * HEAD_DIM
D_RET = RET_HEADS * RET_DK
C_CONV = 0
C_Q = 3 * D_CONV
C_K = C_Q + D_ATTN
C_V = C_K + D_KV
C_RET = C_V + D_KV
C_GATE = C_RET + 4 * D_RET
IN_COLS = C_GATE + 3 * D_MODEL

TM = 512
MOE_M = 256
N_ASSIGN = T * TOP_K
N_BLOCKS = (N_ASSIGN + N_EXPERTS * (MOE_M - 1) + MOE_M - 1) // MOE_M
N_ROWS = N_BLOCKS * MOE_M
LANES = 128
SUBLANES = 8
ROW_VREGS = D_MODEL // LANES
PACK_ROWS = ROW_VREGS // 2
GATHER_STRIDE = MOE_M + SUBLANES
SCATTER_M = 128
SCATTER_STRIDE = SCATTER_M + SUBLANES
COMBINE_BLOCKS = 4
TOK_BITS = 14
assert T < (1 << TOK_BITS)
MIB = 1024 * 1024


def _sigmoid(x):
    return 1.0 / (1.0 + jnp.exp(-x))


def _mod_group(i):
    n_ctx = T_CTX // TM
    per_lat = DEC_SEQ // TM
    g = jnp.zeros_like(i)
    for b in range(DEC_BATCH):
        g = g + (i >= n_ctx + b * per_lat).astype(jnp.int32)
    return g


def _mod_kernel(cond_ref, w_ref, b_ref, o_ref):
    c = cond_ref[...]
    s = c * _sigmoid(c)
    o_ref[0] = jnp.dot(s.astype(BF16), w_ref[0].astype(BF16), preferred_element_type=F32) + b_ref[0]


def _modulation(cond8, w_mod, b_mod):
    n_col = 4
    cw = 6 * D_MODEL // n_col
    return pl.pallas_call(
        _mod_kernel,
        grid=(DEPTH, n_col),
        in_specs=[pl.BlockSpec((8, D_MODEL), lambda l, j: (0, 0)),
                  pl.BlockSpec((1, D_MODEL, cw), lambda l, j: (l, 0, j)),
                  pl.BlockSpec((1, 1, cw), lambda l, j: (l, 0, j))],
        out_specs=pl.BlockSpec((1, 8, cw), lambda l, j: (l, 0, j)),
        out_shape=jax.ShapeDtypeStruct((DEPTH, 8, 6 * D_MODEL), F32),
        compiler_params=pltpu.CompilerParams(vmem_limit_bytes=32 * MIB),
        name="modulation",
    )(cond8, w_mod, b_mod.reshape(DEPTH, 1, 6 * D_MODEL))


N_CTX_TILES = T_CTX // TM


def _ctx_tile(i):
    return (jnp.minimum(i, N_CTX_TILES - 1), 0)


def _lat_tile(i):
    return (jnp.maximum(i - N_CTX_TILES, 0), 0)


def _pick(ctx_ref, lat_ref):
    return jnp.where(pl.program_id(0) < N_CTX_TILES, ctx_ref[...], lat_ref[...])


def _tiles_to_rows(tiles_ref):
    return jnp.concatenate([tiles_ref[pl.ds(c, TM, stride=ROW_VREGS), :] for c in range(ROW_VREGS)], axis=1)


def _inproj_kernel(*refs, first_layer):
    if first_layer:
        xc_ref, xl_ref, mod_ref, g_ref, w_ref, conv_o, q_o, k_o, v_o, ret_o, gate_o = refs
        x = _pick(xc_ref, xl_ref)
    else:
        x1_ref, moe_ref, modp_ref, mod_ref, g_ref, w_ref, x_o, conv_o, q_o, k_o, v_o, ret_o, gate_o = refs
        x = x1_ref[...] + modp_ref[0, 5:6, :] * _tiles_to_rows(moe_ref)
        x_o[...] = x
    ms = jnp.mean(x * x, axis=-1, keepdims=True)
    h = x * lax.rsqrt(ms + EPS) * g_ref[...]
    h = h * (1.0 + mod_ref[0, 1:2, :]) + mod_ref[0, 0:1, :]
    hb = h.astype(BF16)

    def proj(c0, c1):
        return jnp.dot(hb, w_ref[0, :, c0:c1].astype(BF16), preferred_element_type=F32)

    a = proj(C_CONV, C_RET)
    conv_o[...] = a[:, C_CONV:C_Q].astype(BF16)
    q_o[...] = (a[:, C_Q:C_K] * HEAD_DIM ** -0.5).astype(BF16)
    k_o[...] = a[:, C_K:C_V]
    v_o[...] = a[:, C_V:C_RET]
    r = proj(C_RET, C_GATE)
    ret_o[:, 0:D_RET] = r[:, 0:D_RET].astype(BF16)
    ret_o[:, D_RET:2 * D_RET] = (r[:, D_RET:2 * D_RET] * RET_DK ** -0.5).astype(BF16)
    ret_o[:, 2 * D_RET:4 * D_RET] = r[:, 2 * D_RET:4 * D_RET].astype(BF16)
    gate_half = 3 * D_MODEL // 2
    for b in range(2):
        g = proj(C_GATE + b * gate_half, C_GATE + (b + 1) * gate_half)
        gate_o[:, b * gate_half:(b + 1) * gate_half] = _sigmoid(g).astype(BF16)


def _x_specs(xs):
    if len(xs) == 2:
        return [pl.BlockSpec((TM, D_MODEL), _ctx_tile), pl.BlockSpec((TM, D_MODEL), _lat_tile)]
    return [pl.BlockSpec((TM, D_MODEL), lambda i: (i, 0))]


def _inproj(xs, mod_l, g1, w_in, layer, prev=None):
    row = lambda i: (i, 0)
    mod_spec = pl.BlockSpec((1, 6, D_MODEL), lambda i: (_mod_group(i), 0, 0))
    out_specs = [pl.BlockSpec((TM, 3 * D_CONV), row),
                 pl.BlockSpec((TM, D_ATTN), row),
                 pl.BlockSpec((TM, D_KV), row),
                 pl.BlockSpec((TM, D_KV), row),
                 pl.BlockSpec((TM, 4 * D_RET), row),
                 pl.BlockSpec((TM, 3 * D_MODEL), row)]
    out_shape = [jax.ShapeDtypeStruct((T, 3 * D_CONV), BF16),
                 jax.ShapeDtypeStruct((T, D_ATTN), BF16),
                 jax.ShapeDtypeStruct((T, D_KV), F32),
                 jax.ShapeDtypeStruct((T, D_KV), F32),
                 jax.ShapeDtypeStruct((T, 4 * D_RET), BF16),
                 jax.ShapeDtypeStruct((T, 3 * D_MODEL), BF16)]
    if prev is None:
        in_specs, args = _x_specs(xs), list(xs)
    else:
        in_specs = [pl.BlockSpec((TM, D_MODEL), row), pl.BlockSpec((TM * ROW_VREGS, LANES), row), mod_spec]
        args = list(prev)
        out_specs = [pl.BlockSpec((TM, D_MODEL), row)] + out_specs
        out_shape = [jax.ShapeDtypeStruct((T, D_MODEL), F32)] + out_shape
    return pl.pallas_call(
        functools.partial(_inproj_kernel, first_layer=prev is None),
        grid=(T // TM,),
        in_specs=in_specs + [
            mod_spec,
            pl.BlockSpec((1, D_MODEL), lambda i: (0, 0)),
            pl.BlockSpec((1, D_MODEL, IN_COLS), lambda i: (layer, 0, 0), pipeline_mode=pl.Buffered(1))],
        out_specs=out_specs,
        out_shape=out_shape,
        compiler_params=pltpu.CompilerParams(vmem_limit_bytes=60 * MIB),
        name="inproj",
    )(*args, mod_l, g1, w_in)


def _rope(x, cos, sin_signed):
    lane = lax.broadcasted_iota(jnp.int32, x.shape, 1)
    first = (lane % 32) < 16
    partner = jnp.where(first, pltpu.roll(x, x.shape[1] - 16, 1), pltpu.roll(x, 16, 1))
    return x * cos + partner * sin_signed


HALF = LANES // 2
assert HEAD_DIM == HALF and RET_DK == HALF and RET_DV == HALF
N_PAIRS = N_HEADS // 2
RET_PAIRS = RET_HEADS // 2


def _row_variants(kt):
    row = lax.broadcasted_iota(jnp.int32, kt.shape, 0)
    lo0 = jnp.where(row < HALF, kt, 0.0)
    hi1 = jnp.where(row >= HALF, kt, 0.0)
    return lo0, pltpu.roll(lo0, HALF, 0), pltpu.roll(hi1, HALF, 0), hi1


def _lane_variants(v):
    lane = lax.broadcasted_iota(jnp.int32, v.shape, 1)
    lo0 = jnp.where(lane < HALF, v, 0.0)
    hi1 = jnp.where(lane >= HALF, v, 0.0)
    return lo0, pltpu.roll(lo0, HALF, 1), pltpu.roll(hi1, HALF, 1), hi1


def _mixer_kernel(*refs, n, latent):
    if latent:
        (sink_ref, conv_ref, q_ref, k_ref, v_ref, ret_ref, cw_ref, rd_ref, cos_ref, sin_ref, ck_ref, cv_ref,
         r0_ref, yc_ref, ya_ref, yr_ref, ktq, vq, rkp, o_f, o_b, rst, dmat, qdec, kdec, cdec) = refs
    else:
        (sink_ref, conv_ref, q_ref, k_ref, v_ref, ret_ref, cw_ref, rd_ref,
         yc_ref, ya_ref, yr_ref, rfin_ref, ktq, vq, rkp, o_f, o_b, rst, dmat, qdec, kdec, cdec) = refs
    nb = n // CHUNK
    pad = CHUNK if latent else 0

    cv = conv_ref[...].astype(F32)
    cb, cc, cu = cv[:, 0:D_CONV], cv[:, D_CONV:2 * D_CONV], cv[:, 2 * D_CONV:3 * D_CONV]
    p = cc * cu
    row = lax.broadcasted_iota(jnp.int32, p.shape, 0)
    prev = jnp.where(row == 0, 0.0, pltpu.roll(p, 1, 0))
    nxt = jnp.where(row == n - 1, 0.0, pltpu.roll(p, n - 1, 0))
    cw = cw_ref[...]
    yc_ref[...] = (cb * (prev * cw[0:1, :] + p * cw[1:2, :] + nxt * cw[2:3, :])).astype(BF16)

    kf = k_ref[...]
    if latent:
        kf = _rope(kf, cos_ref[...], sin_ref[...])
    for idx, (kk, vv) in enumerate(zip(_row_variants(kf.T), _lane_variants(v_ref[...]))):
        if latent:
            ktq[idx, :, 0:pad] = jnp.zeros((LANES, pad), BF16)
            ktq[idx, :, pad + n:2 * pad + n] = jnp.zeros((LANES, pad), BF16)
            vq[idx, 0:pad, :] = jnp.zeros((pad, LANES), BF16)
            vq[idx, pad + n:2 * pad + n, :] = jnp.zeros((pad, LANES), BF16)
        ktq[idx, :, pad:pad + n] = kk.astype(BF16)
        vq[idx, pad:pad + n, :] = vv.astype(BF16)
    if latent:
        cktq = [t.astype(BF16) for t in _row_variants(ck_ref[0, 0].T)]
        cvq = [t.astype(BF16) for t in _lane_variants(cv_ref[0, 0])]

    def attn(r0, rows):
        qj = q_ref[pl.ds(r0, rows), :]
        if latent:
            cosj = cos_ref[pl.ds(r0, rows), :]
            sinj = sin_ref[pl.ds(r0, rows), :]
            qpos = r0 + lax.broadcasted_iota(jnp.int32, (rows, 3 * CHUNK), 0)
            kpos = r0 - CHUNK + lax.broadcasted_iota(jnp.int32, (rows, 3 * CHUNK), 1)
            ok = (jnp.abs(qpos - kpos) <= WINDOW) & (kpos >= 0) & (kpos < n)
        for m in range(N_PAIRS):
            q2 = qj[:, m * LANES:(m + 1) * LANES]
            if latent:
                q2 = _rope(q2.astype(F32), cosj, sinj).astype(BF16)
            g = (2 * m) // GROUP
            acc = None
            for half in range(2):
                idx = 2 * g + half
                sk = sink_ref[2 * m + half]
                if latent:
                    s = jnp.dot(q2, ktq[idx, :, pl.ds(r0, 3 * CHUNK)], preferred_element_type=F32)
                    s = jnp.where(ok, s, NEG_INF)
                    s2 = jnp.dot(q2, cktq[idx], preferred_element_type=F32)
                    mx = jnp.maximum(jnp.maximum(jnp.max(s, axis=-1, keepdims=True),
                                                 jnp.max(s2, axis=-1, keepdims=True)), sk)
                    pw = jnp.exp(s - mx)
                    p2 = jnp.exp(s2 - mx)
                    den = (jnp.sum(pw, axis=-1, keepdims=True) + jnp.sum(p2, axis=-1, keepdims=True)
                           + jnp.exp(sk - mx))
                    o = (jnp.dot(pw.astype(BF16), vq[idx, pl.ds(r0, 3 * CHUNK), :], preferred_element_type=F32)
                         + jnp.dot(p2.astype(BF16), cvq[idx], preferred_element_type=F32))
                else:
                    s = jnp.dot(q2, ktq[idx], preferred_element_type=F32)
                    mx = jnp.maximum(jnp.max(s, axis=-1, keepdims=True), sk)
                    pw = jnp.exp(s - mx)
                    den = jnp.sum(pw, axis=-1, keepdims=True) + jnp.exp(sk - mx)
                    o = jnp.dot(pw.astype(BF16), vq[idx], preferred_element_type=F32)
                o = o / den
                acc = o if acc is None else acc + o
            ya_ref[pl.ds(r0, rows), m * LANES:(m + 1) * LANES] = acc.astype(BF16)

    rk_t = ret_ref[:, D_RET:2 * D_RET].astype(F32).T
    for m in range(RET_PAIRS):
        rkp[m] = rk_t[m * LANES:(m + 1) * LANES, :].astype(BF16)
    rd = rd_ref[...]
    log_g = jnp.minimum(rd, 0.0) - jnp.log(1.0 + jnp.exp(-jnp.abs(rd)))
    row_c = lax.broadcasted_iota(jnp.int32, (CHUNK, CHUNK), 0)
    lane_c = lax.broadcasted_iota(jnp.int32, (CHUNK, CHUNK), 1)
    ii = row_c.astype(F32)
    jj = lane_c.astype(F32)
    even_row = row_c < HALF
    even_lane = lane_c < HALF
    blockdiag = even_row == even_lane
    for d in range(2):
        for h in range(RET_HEADS):
            r = d * RET_HEADS + h
            diff = (ii - jj) if d == 0 else (jj - ii)
            dmat[r] = jnp.where(diff >= 0, jnp.exp(jnp.maximum(diff, 0.0) * log_g[r:r + 1, :]), 0.0)
        for m in range(RET_PAIRS):
            s = d * RET_PAIRS + m
            lg_e = log_g[d * RET_HEADS + 2 * m:d * RET_HEADS + 2 * m + 1, :]
            lg_o = log_g[d * RET_HEADS + 2 * m + 1:d * RET_HEADS + 2 * m + 2, :]
            qpow = (ii + 1.0) if d == 0 else (CHUNK - ii)
            kpow = (CHUNK - 1.0 - jj) if d == 0 else jj
            qdec[s] = jnp.where(even_lane, jnp.exp(qpow * lg_e), jnp.exp(qpow * lg_o))
            kdec[s] = jnp.where(even_row, jnp.exp(kpow * lg_e), jnp.exp(kpow * lg_o))
            chunk_decay = jnp.where(even_row, jnp.exp(float(CHUNK) * lg_e), jnp.exp(float(CHUNK) * lg_o))
            cdec[s] = jnp.where(blockdiag, chunk_decay, 0.0)
            if latent:
                z = jnp.zeros((HALF, HALF), F32)
                rst[s] = jnp.concatenate(
                    [jnp.concatenate([r0_ref[0, 0, d, 2 * m], z], axis=1),
                     jnp.concatenate([z, r0_ref[0, 0, d, 2 * m + 1]], axis=1)], axis=0)
            else:
                rst[s] = jnp.zeros((LANES, LANES), F32)

    def ret_chunk(c0, d, out_ref):
        for m in range(RET_PAIRS):
            s = d * RET_PAIRS + m
            q2 = ret_ref[pl.ds(c0, CHUNK), m * LANES:(m + 1) * LANES]
            v2 = ret_ref[pl.ds(c0, CHUNK), 2 * D_RET + m * LANES:2 * D_RET + (m + 1) * LANES]
            kt2 = rkp[m, :, pl.ds(c0, CHUNK)].astype(F32)
            v2f = v2.astype(F32)
            state = rst[s]
            o2 = jnp.dot(q2, state.astype(BF16), preferred_element_type=F32) * qdec[s]
            for half in range(2):
                r = d * RET_HEADS + 2 * m + half
                keep_row = even_row if half == 0 else jnp.logical_not(even_row)
                keep_lane = even_lane if half == 0 else jnp.logical_not(even_lane)
                a = jnp.dot(q2, jnp.where(keep_row, kt2, 0.0).astype(BF16), preferred_element_type=F32)
                inner = (a * dmat[r]).astype(BF16)
                o2 = o2 + jnp.dot(inner, jnp.where(keep_lane, v2f, 0.0).astype(BF16), preferred_element_type=F32)
            kd = (kt2 * kdec[s]).astype(BF16)
            upd = jnp.dot(kd, v2, preferred_element_type=F32)
            rst[s] = state * cdec[s] + jnp.where(blockdiag, upd, 0.0)
            out_ref[pl.ds(c0, CHUNK), m * LANES:(m + 1) * LANES] = o2

    gi = lax.broadcasted_iota(jnp.int32, (D_RET, D_RET), 0) // RET_DV
    gj = lax.broadcasted_iota(jnp.int32, (D_RET, D_RET), 1) // RET_DV
    group_mean = jnp.where(gi == gj, 1.0 / RET_DV, 0.0).astype(BF16)

    def norm(c0, rows):
        o = o_f[pl.ds(c0, rows), :] + o_b[pl.ds(c0, rows), :]
        sq = o * o
        hi = sq.astype(BF16)
        lo = (sq - hi.astype(F32)).astype(BF16)
        ms = (jnp.dot(hi, group_mean, preferred_element_type=F32)
              + jnp.dot(lo, group_mean, preferred_element_type=F32))
        rg = ret_ref[pl.ds(c0, rows), 3 * D_RET:4 * D_RET].astype(F32)
        yr_ref[pl.ds(c0, rows), :] = (rg * _sigmoid(rg) * (o * lax.rsqrt(ms + EPS))).astype(BF16)

    if latent:
        def scan_body(j, carry):
            attn(pl.multiple_of(j * CHUNK, CHUNK), CHUNK)
            ret_chunk(pl.multiple_of(j * CHUNK, CHUNK), 0, o_f)
            ret_chunk(pl.multiple_of((nb - 1 - j) * CHUNK, CHUNK), 1, o_b)
            return carry

        def norm_body(j, carry):
            norm(pl.multiple_of(j * CHUNK, CHUNK), CHUNK)
            return carry

        lax.fori_loop(0, nb, scan_body, 0)
        lax.fori_loop(0, nb, norm_body, 0)
    else:
        attn(0, n)
        for j in range(nb):
            ret_chunk(j * CHUNK, 0, o_f)
            ret_chunk((nb - 1 - j) * CHUNK, 1, o_b)
        norm(0, n)
        for d in range(2):
            for h in range(RET_HEADS):
                lo_ = (h % 2) * HALF
                rfin_ref[0, d, h] = rst[d * RET_PAIRS + h // 2][lo_:lo_ + HALF, lo_:lo_ + HALF]


def _mixers(conv_in, q, k, v, ret, conv_w_l, sink_l, rd8, *, latent, cos=None, sin=None, cache_k=None,
            cache_v=None, state=None, layer=0):
    n = DEC_SEQ if latent else SEQ
    nseq = DEC_BATCH if latent else BATCH
    off = T_CTX // n if latent else 0
    seq = lambda s: (s + off, 0)
    const = lambda s: (0, 0)
    in_specs = [pl.BlockSpec(memory_space=pltpu.SMEM),
                pl.BlockSpec((n, 3 * D_CONV), seq),
                pl.BlockSpec((n, D_ATTN), seq),
                pl.BlockSpec((n, D_KV), seq),
                pl.BlockSpec((n, D_KV), seq),
                pl.BlockSpec((n, 4 * D_RET), seq),
                pl.BlockSpec((3, D_CONV), const),
                pl.BlockSpec((8, LANES), const)]
    args = [sink_l, conv_in, q, k, v, ret, conv_w_l, rd8]
    out_specs = [pl.BlockSpec((n, D_CONV), lambda s: (s, 0)),
                 pl.BlockSpec((n, D_ATTN), lambda s: (s, 0)),
                 pl.BlockSpec((n, D_RET), lambda s: (s, 0))]
    out_shape = [jax.ShapeDtypeStruct((nseq * n, D_CONV), BF16),
                 jax.ShapeDtypeStruct((nseq * n, D_ATTN), BF16),
                 jax.ShapeDtypeStruct((nseq * n, D_RET), BF16)]
    if latent:
        in_specs += [pl.BlockSpec((n, LANES), const),
                     pl.BlockSpec((n, LANES), const),
                     pl.BlockSpec((1, 1, PAST_LEN, D_KV), lambda s: (s, layer, 0, 0)),
                     pl.BlockSpec((1, 1, PAST_LEN, D_KV), lambda s: (s, layer, 0, 0)),
                     pl.BlockSpec((1, 1, 2, RET_HEADS, RET_DK, RET_DV), lambda s: (s, layer, 0, 0, 0, 0))]
        args += [cos, sin, cache_k, cache_v, state]
        kv_rows = n + 2 * CHUNK
    else:
        out_specs.append(pl.BlockSpec((1, 2, RET_HEADS, RET_DK, RET_DV), lambda s: (s, 0, 0, 0, 0)))
        out_shape.append(jax.ShapeDtypeStruct((nseq, 2, RET_HEADS, RET_DK, RET_DV), F32))
        kv_rows = n
    scratch = [pltpu.VMEM((2 * N_KV, LANES, kv_rows), BF16),
               pltpu.VMEM((2 * N_KV, kv_rows, LANES), BF16),
               pltpu.VMEM((RET_PAIRS, LANES, n), BF16),
               pltpu.VMEM((n, D_RET), F32),
               pltpu.VMEM((n, D_RET), F32),
               pltpu.VMEM((2 * RET_PAIRS, LANES, LANES), F32),
               pltpu.VMEM((2 * RET_HEADS, CHUNK, CHUNK), F32),
               pltpu.VMEM((2 * RET_PAIRS, CHUNK, CHUNK), F32),
               pltpu.VMEM((2 * RET_PAIRS, CHUNK, CHUNK), F32),
               pltpu.VMEM((2 * RET_PAIRS, CHUNK, CHUNK), F32)]
    return pl.pallas_call(
        functools.partial(_mixer_kernel, n=n, latent=latent),
        grid=(nseq,),
        in_specs=in_specs,
        out_specs=out_specs,
        out_shape=out_shape,
        scratch_shapes=scratch,
        compiler_params=pltpu.CompilerParams(vmem_limit_bytes=56 * MIB),
        name="mixers_latent" if latent else "mixers_context",
    )(*args)


def _merge_kernel(*refs, split):
    if split:
        xc_ref, xl_ref = refs[0:2]
        x = _pick(xc_ref, xl_ref)
        refs = refs[2:]
    else:
        x = refs[0][...]
        refs = refs[1:]
    (ycc_ref, ycl_ref, yac_ref, yal_ref, yrc_ref, yrl_ref, gate_ref, mod_ref, g2_ref, wa_ref, wb_ref, wc_ref,
     wo_ref, wrh_ref, wrl_ref, br_ref, x1_o, xp_o, te_o, tw_o, cnt_o) = refs
    merged = (gate_ref[:, 0:D_MODEL].astype(F32)
              * jnp.dot(_pick(ycc_ref, ycl_ref), wa_ref[0].astype(BF16), preferred_element_type=F32)
              + gate_ref[:, D_MODEL:2 * D_MODEL].astype(F32)
              * jnp.dot(_pick(yac_ref, yal_ref), wb_ref[0].astype(BF16), preferred_element_type=F32)
              + gate_ref[:, 2 * D_MODEL:3 * D_MODEL].astype(F32)
              * jnp.dot(_pick(yrc_ref, yrl_ref), wc_ref[0].astype(BF16), preferred_element_type=F32))
    x1 = x + mod_ref[0, 2:3, :] * jnp.dot(merged.astype(BF16), wo_ref[0].astype(BF16),
                                          preferred_element_type=F32)
    x1_o[...] = x1
    ms = jnp.mean(x1 * x1, axis=-1, keepdims=True)
    h2 = x1 * lax.rsqrt(ms + EPS) * g2_ref[...]
    h2 = h2 * (1.0 + mod_ref[0, 4:5, :]) + mod_ref[0, 3:4, :]
    hh = h2.astype(BF16)
    hf = hh.astype(F32)
    bits = lax.bitcast_convert_type(hf, jnp.uint32)
    for c in range(PACK_ROWS):
        lo = bits[:, c * LANES:(c + 1) * LANES] >> 16
        hi = bits[:, (c + PACK_ROWS) * LANES:(c + PACK_ROWS + 1) * LANES] & jnp.uint32(0xFFFF0000)
        xp_o[pl.ds(c, TM, stride=PACK_ROWS), :] = lax.bitcast_convert_type(lo | hi, jnp.int32)
    hl = (h2 - hf).astype(BF16)
    logits = (jnp.dot(hh, wrh_ref[...], preferred_element_type=F32)
              + jnp.dot(hl, wrh_ref[...], preferred_element_type=F32)
              + jnp.dot(hh, wrl_ref[...], preferred_element_type=F32)
              + br_ref[...])
    lane = lax.broadcasted_iota(jnp.int32, logits.shape, 1)
    work = jnp.where(lane < N_EXPERTS, logits, -jnp.inf)
    vals, idxs = [], []
    for _ in range(TOP_K):
        m = jnp.max(work, axis=-1, keepdims=True)
        am = jnp.min(jnp.where(work == m, lane, LANES), axis=-1, keepdims=True)
        vals.append(m)
        idxs.append(am)
        work = jnp.where(lane == am, -jnp.inf, work)
    es = [jnp.exp(v - vals[0]) for v in vals]
    den = es[0] + es[1] + es[2] + es[3]
    te = jnp.zeros(logits.shape, jnp.int32)
    tw = jnp.zeros(logits.shape, F32)
    for k in range(TOP_K):
        te = jnp.where(lane == k, idxs[k], te)
        tw = jnp.where(lane == k, es[k] / den, tw)
    te_o[...] = te
    tw_o[...] = tw
    sel = (lane == idxs[0]) | (lane == idxs[1]) | (lane == idxs[2]) | (lane == idxs[3])
    part = jnp.sum(sel.astype(jnp.int32), axis=0, keepdims=True)

    @pl.when(pl.program_id(0) == 0)
    def _():
        cnt_o[...] = jnp.zeros_like(cnt_o)

    cnt_o[...] += jnp.broadcast_to(part, cnt_o.shape)


def _merge(xs, ys_ctx, ys_lat, gates, mod_l, g2, wa, wb, wc, wo, wrh, wrl, br, layer):
    row = lambda i: (i, 0)
    const = lambda i: (0, 0)
    wl = lambda i: (layer, 0, 0)
    y_specs, y_args = [], []
    for width, yc, yl in zip((D_CONV, D_ATTN, D_RET), ys_ctx, ys_lat):
        y_specs += [pl.BlockSpec((TM, width), _ctx_tile), pl.BlockSpec((TM, width), _lat_tile)]
        y_args += [yc, yl]
    return pl.pallas_call(
        functools.partial(_merge_kernel, split=len(xs) == 2),
        grid=(T // TM,),
        in_specs=_x_specs(xs) + y_specs + [
                  pl.BlockSpec((TM, 3 * D_MODEL), row),
                  pl.BlockSpec((1, 6, D_MODEL), lambda i: (_mod_group(i), 0, 0)),
                  pl.BlockSpec((1, D_MODEL), const),
                  pl.BlockSpec((1, D_CONV, D_MODEL), wl),
                  pl.BlockSpec((1, D_ATTN, D_MODEL), wl),
                  pl.BlockSpec((1, D_RET, D_MODEL), wl),
                  pl.BlockSpec((1, D_MODEL, D_MODEL), wl),
                  pl.BlockSpec((D_MODEL, LANES), const),
                  pl.BlockSpec((D_MODEL, LANES), const),
                  pl.BlockSpec((1, LANES), const)],
        out_specs=[pl.BlockSpec((TM, D_MODEL), row),
                   pl.BlockSpec((TM * PACK_ROWS, LANES), row),
                   pl.BlockSpec((TM, LANES), row),
                   pl.BlockSpec((TM, LANES), row),
                   pl.BlockSpec((SUBLANES, LANES), const)],
        out_shape=[jax.ShapeDtypeStruct((T, D_MODEL), F32),
                   jax.ShapeDtypeStruct((T * PACK_ROWS, LANES), jnp.int32),
                   jax.ShapeDtypeStruct((T, LANES), jnp.int32),
                   jax.ShapeDtypeStruct((T, LANES), F32),
                   jax.ShapeDtypeStruct((SUBLANES, LANES), jnp.int32)],
        compiler_params=pltpu.CompilerParams(vmem_limit_bytes=48 * MIB),
        name="merge_router",
    )(*xs, *y_args, gates, mod_l, g2, wa, wb, wc, wo, wrh, wrl, br)


def _moe_kernel(blk_e_ref, first_ref, next_e_ref, nu_ref, tok_ref, tok_next_ref, xp_ref, wgu_hbm, bgu_ref, wd_hbm,
                bd_ref, y_ref, tile, wgu_st, wd_st, wgu_bf, wd_bf, sems, *, layer):
    b = pl.program_id(0)
    slot = b % 2

    def weight_copies(e):
        return (pltpu.make_async_copy(wgu_hbm.at[layer, e], wgu_st, sems.at[0]),
                pltpu.make_async_copy(wd_hbm.at[layer, e], wd_st, sems.at[1]))

    def gather(tok, dst_slot):
        for mi in range(MOE_M):
            t = jnp.minimum(tok[0, 0, mi], T - 1)
            slab = xp_ref[pl.ds(pl.multiple_of(t * PACK_ROWS, PACK_ROWS), PACK_ROWS), :]
            tile[dst_slot, pl.ds(mi, PACK_ROWS, stride=GATHER_STRIDE), :] = slab

    @pl.when(b == 0)
    def _():
        for cp in weight_copies(blk_e_ref[0]):
            cp.start()
        gather(tok_ref, 0)

    @pl.when(b < nu_ref[0])
    def _():
        @pl.when(first_ref[b] == 1)
        def _():
            for cp in weight_copies(blk_e_ref[b]):
                cp.wait()
            wgu_bf[...] = wgu_st[...].astype(BF16)
            wd_bf[...] = wd_st[...].astype(BF16)

            @pl.when(next_e_ref[b] >= 0)
            def _():
                for cp in weight_copies(next_e_ref[b]):
                    cp.start()

        lo, hi = [], []
        for c in range(PACK_ROWS):
            bits = lax.bitcast_convert_type(
                tile[slot, c * GATHER_STRIDE:c * GATHER_STRIDE + MOE_M, :], jnp.uint32)
            lo.append(lax.bitcast_convert_type(bits << 16, F32).astype(BF16))
            hi.append(lax.bitcast_convert_type(bits & jnp.uint32(0xFFFF0000), F32).astype(BF16))
        x = jnp.concatenate(lo + hi, axis=1)
        gather(tok_next_ref, 1 - slot)

        gu = jnp.dot(x, wgu_bf[...], preferred_element_type=F32) + bgu_ref[0, 0]
        gate = jnp.minimum(gu[:, 0:D_EXPERT], SWIGLU_LIMIT)
        up = jnp.clip(gu[:, D_EXPERT:2 * D_EXPERT], -SWIGLU_LIMIT, SWIGLU_LIMIT)
        glu = gate * _sigmoid(SWIGLU_ALPHA * gate)
        mid = ((up + 1.0) * glu).astype(BF16)
        y = jnp.dot(mid, wd_bf[...], preferred_element_type=F32) + bd_ref[0, 0]
        y_ref[...] = y.astype(BF16)

    @pl.when(b >= nu_ref[0])
    def _():
        y_ref[...] = jnp.zeros_like(y_ref)


def _moe_experts(blk_e, first, next_e, n_used, row_tok3, xp, w_gu, b_gu, w_down, b_down, layer):
    bias = lambda b, e, f, ne, nu: (layer, e[b], 0, 0)
    grid_spec = pltpu.PrefetchScalarGridSpec(
        num_scalar_prefetch=4,
        grid=(N_BLOCKS,),
        in_specs=[pl.BlockSpec((1, 1, MOE_M), lambda b, e, f, ne, nu: (b, 0, 0), memory_space=pltpu.SMEM),
                  pl.BlockSpec((1, 1, MOE_M), lambda b, e, f, ne, nu: (jnp.minimum(b + 1, N_BLOCKS - 1), 0, 0),
                               memory_space=pltpu.SMEM),
                  pl.BlockSpec((T * PACK_ROWS, LANES), lambda b, e, f, ne, nu: (0, 0), pipeline_mode=pl.Buffered(1)),
                  pl.BlockSpec(memory_space=pl.ANY),
                  pl.BlockSpec((1, 1, 1, 2 * D_EXPERT), bias),
                  pl.BlockSpec(memory_space=pl.ANY),
                  pl.BlockSpec((1, 1, 1, D_MODEL), bias)],
        out_specs=pl.BlockSpec((MOE_M, D_MODEL), lambda b, e, f, ne, nu: (b, 0)),
        scratch_shapes=[pltpu.VMEM((2, PACK_ROWS * GATHER_STRIDE, LANES), jnp.int32),
                        pltpu.VMEM((D_MODEL, 2 * D_EXPERT), F32),
                        pltpu.VMEM((D_EXPERT, D_MODEL), F32),
                        pltpu.VMEM((D_MODEL, 2 * D_EXPERT), BF16),
                        pltpu.VMEM((D_EXPERT, D_MODEL), BF16),
                        pltpu.SemaphoreType.DMA((2,))],
    )
    return pl.pallas_call(
        functools.partial(_moe_kernel, layer=layer),
        grid_spec=grid_spec,
        out_shape=jax.ShapeDtypeStruct((N_ROWS, D_MODEL), BF16),
        compiler_params=pltpu.CompilerParams(vmem_limit_bytes=48 * MIB),
        name="moe_experts",
    )(blk_e, first, next_e, n_used, row_tok3, row_tok3, xp, w_gu, b_gu.reshape(DEPTH, N_EXPERTS, 1, 2 * D_EXPERT),
      w_down, b_down.reshape(DEPTH, N_EXPERTS, 1, D_MODEL))


SCATTER_UNROLL = 8


def _combine_kernel(nu_ref, tok_ref, w_ref, y_ref, o_ref, acc, tile, sem):
    s = pl.program_id(0)

    @pl.when(s == 0)
    def _():
        acc[...] = jnp.zeros_like(acc)

    def sub_block(sb, carry):
        r0 = pl.multiple_of(sb * SCATTER_M, SCATTER_M)
        y = y_ref[pl.ds(r0, SCATTER_M), :].astype(F32)
        for c in range(ROW_VREGS):
            tile[c * SCATTER_STRIDE:c * SCATTER_STRIDE + SCATTER_M, :] = y[:, c * LANES:(c + 1) * LANES]
        for m0 in range(0, SCATTER_M, SCATTER_UNROLL):
            addrs, vals = [], []
            for u in range(SCATTER_UNROLL):
                mi = m0 + u
                a = pl.multiple_of(tok_ref[0, 0, r0 + mi] * ROW_VREGS, ROW_VREGS)
                yv = tile[pl.ds(mi, ROW_VREGS, stride=SCATTER_STRIDE), :]
                addrs.append(a)
                vals.append(acc[pl.ds(a, ROW_VREGS), :] + w_ref[0, 0, r0 + mi] * yv)
            for u in range(SCATTER_UNROLL):
                acc[pl.ds(addrs[u], ROW_VREGS), :] = vals[u]
        return carry

    @pl.when(s * COMBINE_BLOCKS < nu_ref[0])
    def _():
        lax.fori_loop(0, COMBINE_BLOCKS * MOE_M // SCATTER_M, sub_block, 0)

    @pl.when(s == pl.num_programs(0) - 1)
    def _():
        cp = pltpu.make_async_copy(acc.at[pl.ds(0, T * ROW_VREGS)], o_ref, sem)
        cp.start()
        cp.wait()


def _combine(n_used, row_tok3, row_w3, yr):
    rows = COMBINE_BLOCKS * MOE_M
    steps = N_BLOCKS // COMBINE_BLOCKS
    grid_spec = pltpu.PrefetchScalarGridSpec(
        num_scalar_prefetch=1,
        grid=(steps,),
        in_specs=[pl.BlockSpec((1, 1, rows), lambda s, nu: (s, 0, 0), memory_space=pltpu.SMEM),
                  pl.BlockSpec((1, 1, rows), lambda s, nu: (s, 0, 0), memory_space=pltpu.SMEM),
                  pl.BlockSpec((rows, D_MODEL), lambda s, nu: (s, 0))],
        out_specs=pl.BlockSpec(memory_space=pl.ANY),
        scratch_shapes=[pltpu.VMEM(((T + 1) * ROW_VREGS, LANES), F32),
                        pltpu.VMEM((ROW_VREGS * SCATTER_STRIDE, LANES), F32),
                        pltpu.SemaphoreType.DMA(())],
    )
    return pl.pallas_call(
        _combine_kernel,
        grid_spec=grid_spec,
        out_shape=jax.ShapeDtypeStruct((T * ROW_VREGS, LANES), F32),
        compiler_params=pltpu.CompilerParams(vmem_limit_bytes=48 * MIB),
        name="moe_combine",
    )(n_used, row_tok3.reshape(steps, 1, rows), row_w3.reshape(steps, 1, rows), yr)


def _route(top_e, top_w, counts):
    experts = jnp.arange(N_EXPERTS, dtype=jnp.int32)
    padded = (counts + MOE_M - 1) // MOE_M * MOE_M
    pad_end = jnp.cumsum(padded)
    pad_start = pad_end - padded
    n_used = (pad_end[-1] // MOE_M).astype(jnp.int32)
    assert N_ROWS - N_ASSIGN == N_EXPERTS * MOE_M
    used = jnp.arange(MOE_M, dtype=jnp.int32)[None, :] < (padded - counts)[:, None]
    pad_keys = jnp.where(used, 2 * experts[:, None] + 1, 2 * N_EXPERTS + 1) << TOK_BITS | T
    tok = jnp.broadcast_to(jnp.arange(T, dtype=jnp.int32)[:, None], (T, TOP_K))
    keys = jnp.concatenate([((2 * top_e) << TOK_BITS | tok).reshape(N_ASSIGN), pad_keys.reshape(N_EXPERTS * MOE_M)])
    wts = jnp.concatenate([top_w.reshape(N_ASSIGN), jnp.zeros((N_EXPERTS * MOE_M,), F32)])
    keys, row_w = lax.sort((keys, wts), num_keys=1)
    row_tok = keys & ((1 << TOK_BITS) - 1)
    blk0 = jnp.arange(N_BLOCKS, dtype=jnp.int32) * MOE_M
    blk_e = jnp.minimum(jnp.sum((pad_end[:, None] <= blk0[None, :]).astype(jnp.int32), axis=0), N_EXPERTS - 1)
    first = (blk0 == pad_start[blk_e]).astype(jnp.int32)
    later = (experts[None, :] > experts[:, None]) & (counts[None, :] > 0)
    nxt = jnp.min(jnp.where(later, experts[None, :], N_EXPERTS), axis=1)
    next_e = jnp.where(nxt == N_EXPERTS, -1, nxt)[blk_e].astype(jnp.int32)
    return (row_tok.reshape(N_BLOCKS, 1, MOE_M), row_w.reshape(N_BLOCKS, 1, MOE_M), blk_e.astype(jnp.int32),
            first, next_e, n_used.reshape(1))


def _final_kernel(x_ref, moe_ref, mod_ref, g_ref, yc_o, yl_o):
    x = x_ref[...] + mod_ref[0, 5:6, :] * _tiles_to_rows(moe_ref)
    ms = jnp.mean(x * x, axis=-1, keepdims=True)
    y = x * lax.rsqrt(ms + EPS) * g_ref[...]
    i = pl.program_id(0)

    @pl.when(i < N_CTX_TILES)
    def _():
        yc_o[...] = y

    @pl.when(i >= N_CTX_TILES)
    def _():
        yl_o[...] = y


def _final(x1, moe_tiles, mod_l, g):
    row = lambda i: (i, 0)
    return pl.pallas_call(
        _final_kernel,
        grid=(T // TM,),
        in_specs=[pl.BlockSpec((TM, D_MODEL), row),
                  pl.BlockSpec((TM * ROW_VREGS, LANES), row),
                  pl.BlockSpec((1, 6, D_MODEL), lambda i: (_mod_group(i), 0, 0)),
                  pl.BlockSpec((1, D_MODEL), lambda i: (0, 0))],
        out_specs=[pl.BlockSpec((TM, D_MODEL), _ctx_tile), pl.BlockSpec((TM, D_MODEL), _lat_tile)],
        out_shape=[jax.ShapeDtypeStruct((T_CTX, D_MODEL), F32), jax.ShapeDtypeStruct((T_LAT, D_MODEL), F32)],
        compiler_params=pltpu.CompilerParams(vmem_limit_bytes=32 * MIB),
        name="residual_final",
    )(x1, moe_tiles, mod_l, g)


def _rope_tables():
    t = np.arange(DEC_SEQ)
    pos = np.stack([t // GRID_W, t % GRID_W], axis=1).astype(np.float32)
    half = HEAD_DIM // 2
    inv = np.float32(ROPE_BASE) ** (-np.arange(0, half, 2, dtype=np.float32) / np.float32(half))
    d = np.arange(HEAD_DIM)
    which = d // half
    freq = d % (half // 2)
    sign = np.where((d % half) < half // 2, -1.0, 1.0).astype(np.float32)
    ang = (pos[:, which] * inv[freq][None, :]).astype(np.float32)
    reps = LANES // HEAD_DIM
    cos = np.tile(np.cos(ang).astype(np.float32), (1, reps))
    sin = np.tile(np.sin(ang).astype(np.float32) * sign[None, :], (1, reps))
    return jnp.asarray(cos), jnp.asarray(sin)


def kernel(x_prompt, x_sample, cache_k, cache_v, state_ret, c, c_ctx, norm1_g, norm2_g, w_mod, b_mod, w_in, conv_w, attn_sink, ret_decay, w_a, w_b, w_c, w_o, w_router, b_router, w_gu, b_gu, w_down, b_down, final_g):
    xs = (x_prompt.reshape(T_CTX, D_MODEL), x_sample.reshape(T_LAT, D_MODEL))
    cond8 = jnp.zeros((8, D_MODEL), F32).at[0].set(c_ctx).at[1:1 + DEC_BATCH].set(c)
    mod = _modulation(cond8, w_mod, b_mod)
    cos, sin = _rope_tables()
    ck = cache_k.reshape(DEC_BATCH, DEPTH, PAST_LEN, D_KV)
    cv = cache_v.reshape(DEC_BATCH, DEPTH, PAST_LEN, D_KV)

    ks, vs, rs = [], [], []
    prev = None
    for l in range(DEPTH):
        mod_l = mod[l, 0:1 + DEC_BATCH].reshape(1 + DEC_BATCH, 6, D_MODEL)
        if prev is None:
            conv_in, q, k, v, ret, gates = _inproj(xs, mod_l, norm1_g[l][None, :], w_in, l)
        else:
            x, conv_in, q, k, v, ret, gates = _inproj(None, mod_l, norm1_g[l][None, :], w_in, l, prev=prev)
            xs = (x,)
        rd8 = jnp.broadcast_to(ret_decay[l].reshape(2 * RET_HEADS, 1), (2 * RET_HEADS, LANES))
        *ys_ctx, rfin = _mixers(conv_in, q, k, v, ret, conv_w[l], attn_sink[l], rd8, latent=False)
        ys_lat = _mixers(conv_in, q, k, v, ret, conv_w[l], attn_sink[l], rd8, latent=True,
                         cos=cos, sin=sin, cache_k=ck, cache_v=cv, state=state_ret, layer=l)
        wr = jnp.pad(w_router[l], ((0, 0), (0, LANES - N_EXPERTS)))
        wrh = wr.astype(BF16)
        wrl = (wr - wrh.astype(F32)).astype(BF16)
        br = jnp.pad(b_router[l], (0, LANES - N_EXPERTS))[None, :]
        x1, xp, top_e, top_w, cnt = _merge(xs, ys_ctx, ys_lat, gates, mod_l, norm2_g[l][None, :], w_a, w_b, w_c,
                                           w_o, wrh, wrl, br, l)
        row_tok, row_w, blk_e, first, next_e, n_used = _route(top_e[:, 0:TOP_K], top_w[:, 0:TOP_K],
                                                              cnt[0, 0:N_EXPERTS])
        yrows = _moe_experts(blk_e, first, next_e, n_used, row_tok, xp, w_gu, b_gu, w_down, b_down, l)
        moe = _combine(n_used, row_tok, row_w, yrows)
        prev = (x1, moe, mod_l)
        ks.append(k[0:T_CTX].reshape(BATCH, SEQ, N_KV, HEAD_DIM))
        vs.append(v[0:T_CTX].reshape(BATCH, SEQ, N_KV, HEAD_DIM))
        rs.append(rfin)

    yc, yl = _final(*prev, final_g[None, :])
    y_prompt = yc.reshape(BATCH, SEQ, D_MODEL)
    y_sample = yl.reshape(DEC_BATCH, DEC_SEQ, D_MODEL)
    return (y_prompt, y_sample, jnp.stack(ks, axis=1), jnp.stack(vs, axis=1), jnp.stack(rs, axis=1))
```

```python
import functools

import numpy as np
import jax
import jax.numpy as jnp
from jax import lax
from jax.experimental import pallas as pl
from jax.experimental.pallas import tpu as pltpu

F32 = jnp.float32
BF16 = jnp.bfloat16

D_MODEL = 1024
BATCH = 16
SEQ = 256
DEPTH = 2
DEC_BATCH = 2
DEC_SEQ = 2048
PAST_LEN = 256
GRID_W = 64
HEAD_DIM = 64
D_CONV = 256
N_HEADS = 8
N_KV = 2
GROUP = N_HEADS // N_KV
WINDOW = 128
ROPE_BASE = 10000.0
RET_HEADS = 4
RET_DK = 64
RET_DV = 64
CHUNK = 128
N_EXPERTS = 32
TOP_K = 4
D_EXPERT = D_MODEL
SWIGLU_LIMIT = 7.0
SWIGLU_ALPHA = 1.702
EPS = 1e-6
NEG_INF = -1e30

T_CTX = BATCH * SEQ
T_LAT = DEC_BATCH * DEC_SEQ
T = T_CTX + T_LAT
D_ATTN = N_HEADS * HEAD_DIM
D_KV = N_KV * HEAD_DIM
D_RET = RET_HEADS * RET_DK
C_CONV = 0
C_Q = 3 * D_CONV
C_K = C_Q + D_ATTN
C_V = C_K + D_KV
C_RET = C_V + D_KV
C_GATE = C_RET + 4 * D_RET
IN_COLS = C_GATE + 3 * D_MODEL

TM = 512
MOE_M = 256
N_ASSIGN = T * TOP_K
N_BLOCKS = (N_ASSIGN + N_EXPERTS * (MOE_M - 1) + MOE_M - 1) // MOE_M
N_ROWS = N_BLOCKS * MOE_M
LANES = 128
SUBLANES = 8
ROW_VREGS = D_MODEL // LANES
PACK_ROWS = ROW_VREGS // 2
GATHER_STRIDE = MOE_M + SUBLANES
SCATTER_M = 128
SCATTER_STRIDE = SCATTER_M + SUBLANES
COMBINE_BLOCKS = 4
TOK_BITS = 14
assert T < (1 << TOK_BITS)
MIB = 1024 * 1024


def _sigmoid(x):
    return 1.0 / (1.0 + jnp.exp(-x))


def _mod_group(i):
    n_ctx = T_CTX // TM
    per_lat = DEC_SEQ // TM
    g = jnp.zeros_like(i)
    for b in range(DEC_BATCH):
        g = g + (i >= n_ctx + b * per_lat).astype(jnp.int32)
    return g


def _mod_kernel(cond_ref, w_ref, b_ref, o_ref):
    c = cond_ref[...]
    s = c * _sigmoid(c)
    o_ref[0] = jnp.dot(s.astype(BF16), w_ref[0].astype(BF16), preferred_element_type=F32) + b_ref[0]


def _modulation(cond8, w_mod, b_mod):
    n_col = 4
    cw = 6 * D_MODEL // n_col
    return pl.pallas_call(
        _mod_kernel,
        grid=(DEPTH, n_col),
        in_specs=[pl.BlockSpec((8, D_MODEL), lambda l, j: (0, 0)),
                  pl.BlockSpec((1, D_MODEL, cw), lambda l, j: (l, 0, j)),
                  pl.BlockSpec((1, 1, cw), lambda l, j: (l, 0, j))],
        out_specs=pl.BlockSpec((1, 8, cw), lambda l, j: (l, 0, j)),
        out_shape=jax.ShapeDtypeStruct((DEPTH, 8, 6 * D_MODEL), F32),
        compiler_params=pltpu.CompilerParams(vmem_limit_bytes=32 * MIB),
        name="modulation",
    )(cond8, w_mod, b_mod.reshape(DEPTH, 1, 6 * D_MODEL))


N_CTX_TILES = T_CTX // TM


def _ctx_tile(i):
    return (jnp.minimum(i, N_CTX_TILES - 1), 0)


def _lat_tile(i):
    return (jnp.maximum(i - N_CTX_TILES, 0), 0)


def _pick(ctx_ref, lat_ref):
    return jnp.where(pl.program_id(0) < N_CTX_TILES, ctx_ref[...], lat_ref[...])


def _tiles_to_rows(tiles_ref):
    return jnp.concatenate([tiles_ref[pl.ds(c, TM, stride=ROW_VREGS), :] for c in range(ROW_VREGS)], axis=1)


def _inproj_kernel(*refs, first_layer):
    if first_layer:
        xc_ref, xl_ref, mod_ref, g_ref, w_ref, conv_o, q_o, k_o, v_o, ret_o, gate_o = refs
        x = _pick(xc_ref, xl_ref)
    else:
        x1_ref, moe_ref, modp_ref, mod_ref, g_ref, w_ref, x_o, conv_o, q_o, k_o, v_o, ret_o, gate_o = refs
        x = x1_ref[...] + modp_ref[0, 5:6, :] * _tiles_to_rows(moe_ref)
        x_o[...] = x
    ms = jnp.mean(x * x, axis=-1, keepdims=True)
    h = x * lax.rsqrt(ms + EPS) * g_ref[...]
    h = h * (1.0 + mod_ref[0, 1:2, :]) + mod_ref[0, 0:1, :]
    hb = h.astype(BF16)

    def proj(c0, c1):
        return jnp.dot(hb, w_ref[0, :, c0:c1].astype(BF16), preferred_element_type=F32)

    a = proj(C_CONV, C_RET)
    conv_o[...] = a[:, C_CONV:C_Q].astype(BF16)
    q_o[...] = (a[:, C_Q:C_K] * HEAD_DIM ** -0.5).astype(BF16)
    k_o[...] = a[:, C_K:C_V]
    v_o[...] = a[:, C_V:C_RET]
    r = proj(C_RET, C_GATE)
    ret_o[:, 0:D_RET] = r[:, 0:D_RET].astype(BF16)
    ret_o[:, D_RET:2 * D_RET] = (r[:, D_RET:2 * D_RET] * RET_DK ** -0.5).astype(BF16)
    ret_o[:, 2 * D_RET:4 * D_RET] = r[:, 2 * D_RET:4 * D_RET].astype(BF16)
    gate_half = 3 * D_MODEL // 2
    for b in range(2):
        g = proj(C_GATE + b * gate_half, C_GATE + (b + 1) * gate_half)
        gate_o[:, b * gate_half:(b + 1) * gate_half] = _sigmoid(g).astype(BF16)


def _x_specs(xs):
    if len(xs) == 2:
        return [pl.BlockSpec((TM, D_MODEL), _ctx_tile), pl.BlockSpec((TM, D_MODEL), _lat_tile)]
    return [pl.BlockSpec((TM, D_MODEL), lambda i: (i, 0))]


def _inproj(xs, mod_l, g1, w_in, layer, prev=None):
    row = lambda i: (i, 0)
    mod_spec = pl.BlockSpec((1, 6, D_MODEL), lambda i: (_mod_group(i), 0, 0))
    out_specs = [pl.BlockSpec((TM, 3 * D_CONV), row),
                 pl.BlockSpec((TM, D_ATTN), row),
                 pl.BlockSpec((TM, D_KV), row),
                 pl.BlockSpec((TM, D_KV), row),
                 pl.BlockSpec((TM, 4 * D_RET), row),
                 pl.BlockSpec((TM, 3 * D_MODEL), row)]
    out_shape = [jax.ShapeDtypeStruct((T, 3 * D_CONV), BF16),
                 jax.ShapeDtypeStruct((T, D_ATTN), BF16),
                 jax.ShapeDtypeStruct((T, D_KV), F32),
                 jax.ShapeDtypeStruct((T, D_KV), F32),
                 jax.ShapeDtypeStruct((T, 4 * D_RET), BF16),
                 jax.ShapeDtypeStruct((T, 3 * D_MODEL), BF16)]
    if prev is None:
        in_specs, args = _x_specs(xs), list(xs)
    else:
        in_specs = [pl.BlockSpec((TM, D_MODEL), row), pl.BlockSpec((TM * ROW_VREGS, LANES), row), mod_spec]
        args = list(prev)
        out_specs = [pl.BlockSpec((TM, D_MODEL), row)] + out_specs
        out_shape = [jax.ShapeDtypeStruct((T, D_MODEL), F32)] + out_shape
    return pl.pallas_call(
        functools.partial(_inproj_kernel, first_layer=prev is None),
        grid=(T // TM,),
        in_specs=in_specs + [
            mod_spec,
            pl.BlockSpec((1, D_MODEL), lambda i: (0, 0)),
            pl.BlockSpec((1, D_MODEL, IN_COLS), lambda i: (layer, 0, 0), pipeline_mode=pl.Buffered(1))],
        out_specs=out_specs,
        out_shape=out_shape,
        compiler_params=pltpu.CompilerParams(vmem_limit_bytes=60 * MIB),
        name="inproj",
    )(*args, mod_l, g1, w_in)


def _rope(x, cos, sin_signed):
    lane = lax.broadcasted_iota(jnp.int32, x.shape, 1)
    first = (lane % 32) < 16
    partner = jnp.where(first, pltpu.roll(x, x.shape[1] - 16, 1), pltpu.roll(x, 16, 1))
    return x * cos + partner * sin_signed


HALF = LANES // 2
assert HEAD_DIM == HALF and RET_DK == HALF and RET_DV == HALF
N_PAIRS = N_HEADS // 2
RET_PAIRS = RET_HEADS // 2


def _row_variants(kt):
    row = lax.broadcasted_iota(jnp.int32, kt.shape, 0)
    lo0 = jnp.where(row < HALF, kt, 0.0)
    hi1 = jnp.where(row >= HALF, kt, 0.0)
    return lo0, pltpu.roll(lo0, HALF, 0), pltpu.roll(hi1, HALF, 0), hi1


def _lane_variants(v):
    lane = lax.broadcasted_iota(jnp.int32, v.shape, 1)
    lo0 = jnp.where(lane < HALF, v, 0.0)
    hi1 = jnp.where(lane >= HALF, v, 0.0)
    return lo0, pltpu.roll(lo0, HALF, 1), pltpu.roll(hi1, HALF, 1), hi1


def _mixer_kernel(*refs, n, latent):
    if latent:
        (sink_ref, conv_ref, q_ref, k_ref, v_ref, ret_ref, cw_ref, rd_ref, cos_ref, sin_ref, ck_ref, cv_ref,
         r0_ref, yc_ref, ya_ref, yr_ref, ktq, vq, rkp, o_f, o_b, rst, dmat, qdec, kdec, cdec) = refs
    else:
        (sink_ref, conv_ref, q_ref, k_ref, v_ref, ret_ref, cw_ref, rd_ref,
         yc_ref, ya_ref, yr_ref, rfin_ref, ktq, vq, rkp, o_f, o_b, rst, dmat, qdec, kdec, cdec) = refs
    nb = n // CHUNK
    pad = CHUNK if latent else 0

    cv = conv_ref[...].astype(F32)
    cb, cc, cu = cv[:, 0:D_CONV], cv[:, D_CONV:2 * D_CONV], cv[:, 2 * D_CONV:3 * D_CONV]
    p = cc * cu
    row = lax.broadcasted_iota(jnp.int32, p.shape, 0)
    prev = jnp.where(row == 0, 0.0, pltpu.roll(p, 1, 0))
    nxt = jnp.where(row == n - 1, 0.0, pltpu.roll(p, n - 1, 0))
    cw = cw_ref[...]
    yc_ref[...] = (cb * (prev * cw[0:1, :] + p * cw[1:2, :] + nxt * cw[2:3, :])).astype(BF16)

    kf = k_ref[...]
    if latent:
        kf = _rope(kf, cos_ref[...], sin_ref[...])
    for idx, (kk, vv) in enumerate(zip(_row_variants(kf.T), _lane_variants(v_ref[...]))):
        if latent:
            ktq[idx, :, 0:pad] = jnp.zeros((LANES, pad), BF16)
            ktq[idx, :, pad + n:2 * pad + n] = jnp.zeros((LANES, pad), BF16)
            vq[idx, 0:pad, :] = jnp.zeros((pad, LANES), BF16)
            vq[idx, pad + n:2 * pad + n, :] = jnp.zeros((pad, LANES), BF16)
        ktq[idx, :, pad:pad + n] = kk.astype(BF16)
        vq[idx, pad:pad + n, :] = vv.astype(BF16)
    if latent:
        cktq = [t.astype(BF16) for t in _row_variants(ck_ref[0, 0].T)]
        cvq = [t.astype(BF16) for t in _lane_variants(cv_ref[0, 0])]

    def attn(r0, rows):
        qj = q_ref[pl.ds(r0, rows), :]
        if latent:
            cosj = cos_ref[pl.ds(r0, rows), :]
            sinj = sin_ref[pl.ds(r0, rows), :]
            qpos = r0 + lax.broadcasted_iota(jnp.int32, (rows, 3 * CHUNK), 0)
            kpos = r0 - CHUNK + lax.broadcasted_iota(jnp.int32, (rows, 3 * CHUNK), 1)
            ok = (jnp.abs(qpos - kpos) <= WINDOW) & (kpos >= 0) & (kpos < n)
        for m in range(N_PAIRS):
            q2 = qj[:, m * LANES:(m + 1) * LANES]
            if latent:
                q2 = _rope(q2.astype(F32), cosj, sinj).astype(BF16)
            g = (2 * m) // GROUP
            acc = None
            for half in range(2):
                idx = 2 * g + half
                sk = sink_ref[2 * m + half]
                if latent:
                    s = jnp.dot(q2, ktq[idx, :, pl.ds(r0, 3 * CHUNK)], preferred_element_type=F32)
                    s = jnp.where(ok, s, NEG_INF)
                    s2 = jnp.dot(q2, cktq[idx], preferred_element_type=F32)
                    mx = jnp.maximum(jnp.maximum(jnp.max(s, axis=-1, keepdims=True),
                                                 jnp.max(s2, axis=-1, keepdims=True)), sk)
                    pw = jnp.exp(s - mx)
                    p2 = jnp.exp(s2 - mx)
                    den = (jnp.sum(pw, axis=-1, keepdims=True) + jnp.sum(p2, axis=-1, keepdims=True)
                           + jnp.exp(sk - mx))
                    o = (jnp.dot(pw.astype(BF16), vq[idx, pl.ds(r0, 3 * CHUNK), :], preferred_element_type=F32)
                         + jnp.dot(p2.astype(BF16), cvq[idx], preferred_element_type=F32))
                else:
                    s = jnp.dot(q2, ktq[idx], preferred_element_type=F32)
                    mx = jnp.maximum(jnp.max(s, axis=-1, keepdims=True), sk)
                    pw = jnp.exp(s - mx)
                    den = jnp.sum(pw, axis=-1, keepdims=True) + jnp.exp(sk - mx)
                    o = jnp.dot(pw.astype(BF16), vq[idx], preferred_element_type=F32)
                o = o / den
                acc = o if acc is None else acc + o
            ya_ref[pl.ds(r0, rows), m * LANES:(m + 1) * LANES] = acc.astype(BF16)

    rk_t = ret_ref[:, D_RET:2 * D_RET].astype(F32).T
    for m in range(RET_PAIRS):
        rkp[m] = rk_t[m * LANES:(m + 1) * LANES, :].astype(BF16)
    rd = rd_ref[...]
    log_g = jnp.minimum(rd, 0.0) - jnp.log(1.0 + jnp.exp(-jnp.abs(rd)))
    row_c = lax.broadcasted_iota(jnp.int32, (CHUNK, CHUNK), 0)
    lane_c = lax.broadcasted_iota(jnp.int32, (CHUNK, CHUNK), 1)
    ii = row_c.astype(F32)
    jj = lane_c.astype(F32)
    even_row = row_c < HALF
    even_lane = lane_c < HALF
    blockdiag = even_row == even_lane
    for d in range(2):
        for h in range(RET_HEADS):
            r = d * RET_HEADS + h
            diff = (ii - jj) if d == 0 else (jj - ii)
            dmat[r] = jnp.where(diff >= 0, jnp.exp(jnp.maximum(diff, 0.0) * log_g[r:r + 1, :]), 0.0)
        for m in range(RET_PAIRS):
            s = d * RET_PAIRS + m
            lg_e = log_g[d * RET_HEADS + 2 * m:d * RET_HEADS + 2 * m + 1, :]
            lg_o = log_g[d * RET_HEADS + 2 * m + 1:d * RET_HEADS + 2 * m + 2, :]
            qpow = (ii + 1.0) if d == 0 else (CHUNK - ii)
            kpow = (CHUNK - 1.0 - jj) if d == 0 else jj
            qdec[s] = jnp.where(even_lane, jnp.exp(qpow * lg_e), jnp.exp(qpow * lg_o))
            kdec[s] = jnp.where(even_row, jnp.exp(kpow * lg_e), jnp.exp(kpow * lg_o))
            chunk_decay = jnp.where(even_row, jnp.exp(float(CHUNK) * lg_e), jnp.exp(float(CHUNK) * lg_o))
            cdec[s] = jnp.where(blockdiag, chunk_decay, 0.0)
            if latent:
                z = jnp.zeros((HALF, HALF), F32)
                rst[s] = jnp.concatenate(
                    [jnp.concatenate([r0_ref[0, 0, d, 2 * m], z], axis=1),
                     jnp.concatenate([z, r0_ref[0, 0, d, 2 * m + 1]], axis=1)], axis=0)
            else:
                rst[s] = jnp.zeros((LANES, LANES), F32)

    def ret_chunk(c0, d, out_ref):
        for m in range(RET_PAIRS):
            s = d * RET_PAIRS + m
            q2 = ret_ref[pl.ds(c0, CHUNK), m * LANES:(m + 1) * LANES]
            v2 = ret_ref[pl.ds(c0, CHUNK), 2 * D_RET + m * LANES:2 * D_RET + (m + 1) * LANES]
            kt2 = rkp[m, :, pl.ds(c0, CHUNK)].astype(F32)
            v2f = v2.astype(F32)
            state = rst[s]
            o2 = jnp.dot(q2, state.astype(BF16), preferred_element_type=F32) * qdec[s]
            for half in range(2):
                r = d * RET_HEADS + 2 * m + half
                keep_row = even_row if half == 0 else jnp.logical_not(even_row)
                keep_lane = even_lane if half == 0 else jnp.logical_not(even_lane)
                a = jnp.dot(q2, jnp.where(keep_row, kt2, 0.0).astype(BF16), preferred_element_type=F32)
                inner = (a * dmat[r]).astype(BF16)
                o2 = o2 + jnp.dot(inner, jnp.where(keep_lane, v2f, 0.0).astype(BF16), preferred_element_type=F32)
            kd = (kt2 * kdec[s]).astype(BF16)
            upd = jnp.dot(kd, v2, preferred_element_type=F32)
            rst[s] = state * cdec[s] + jnp.where(blockdiag, upd, 0.0)
            out_ref[pl.ds(c0, CHUNK), m * LANES:(m + 1) * LANES] = o2

    gi = lax.broadcasted_iota(jnp.int32, (D_RET, D_RET), 0) // RET_DV
    gj = lax.broadcasted_iota(jnp.int32, (D_RET, D_RET), 1) // RET_DV
    group_mean = jnp.where(gi == gj, 1.0 / RET_DV, 0.0).astype(BF16)

    def norm(c0, rows):
        o = o_f[pl.ds(c0, rows), :] + o_b[pl.ds(c0, rows), :]
        sq = o * o
        hi = sq.astype(BF16)
        lo = (sq - hi.astype(F32)).astype(BF16)
        ms = (jnp.dot(hi, group_mean, preferred_element_type=F32)
              + jnp.dot(lo, group_mean, preferred_element_type=F32))
        rg = ret_ref[pl.ds(c0, rows), 3 * D_RET:4 * D_RET].astype(F32)
        yr_ref[pl.ds(c0, rows), :] = (rg * _sigmoid(rg) * (o * lax.rsqrt(ms + EPS))).astype(BF16)

    if latent:
        def scan_body(j, carry):
            attn(pl.multiple_of(j * CHUNK, CHUNK), CHUNK)
            ret_chunk(pl.multiple_of(j * CHUNK, CHUNK), 0, o_f)
            ret_chunk(pl.multiple_of((nb - 1 - j) * CHUNK, CHUNK), 1, o_b)
            return carry

        def norm_body(j, carry):
            norm(pl.multiple_of(j * CHUNK, CHUNK), CHUNK)
            return carry

        lax.fori_loop(0, nb, scan_body, 0)
        lax.fori_loop(0, nb, norm_body, 0)
    else:
        attn(0, n)
        for j in range(nb):
            ret_chunk(j * CHUNK, 0, o_f)
            ret_chunk((nb - 1 - j) * CHUNK, 1, o_b)
        norm(0, n)
        for d in range(2):
            for h in range(RET_HEADS):
                lo_ = (h % 2) * HALF
                rfin_ref[0, d, h] = rst[d * RET_PAIRS + h // 2][lo_:lo_ + HALF, lo_:lo_ + HALF]


def _mixers(conv_in, q, k, v, ret, conv_w_l, sink_l, rd8, *, latent, cos=None, sin=None, cache_k=None,
            cache_v=None, state=None, layer=0):
    n = DEC_SEQ if latent else SEQ
    nseq = DEC_BATCH if latent else BATCH
    off = T_CTX // n if latent else 0
    seq = lambda s: (s + off, 0)
    const = lambda s: (0, 0)
    in_specs = [pl.BlockSpec(memory_space=pltpu.SMEM),
                pl.BlockSpec((n, 3 * D_CONV), seq),
                pl.BlockSpec((n, D_ATTN), seq),
                pl.BlockSpec((n, D_KV), seq),
                pl.BlockSpec((n, D_KV), seq),
                pl.BlockSpec((n, 4 * D_RET), seq),
                pl.BlockSpec((3, D_CONV), const),
                pl.BlockSpec((8, LANES), const)]
    args = [sink_l, conv_in, q, k, v, ret, conv_w_l, rd8]
    out_specs = [pl.BlockSpec((n, D_CONV), lambda s: (s, 0)),
                 pl.BlockSpec((n, D_ATTN), lambda s: (s, 0)),
                 pl.BlockSpec((n, D_RET), lambda s: (s, 0))]
    out_shape = [jax.ShapeDtypeStruct((nseq * n, D_CONV), BF16),
                 jax.ShapeDtypeStruct((nseq * n, D_ATTN), BF16),
                 jax.ShapeDtypeStruct((nseq * n, D_RET), BF16)]
    if latent:
        in_specs += [pl.BlockSpec((n, LANES), const),
                     pl.BlockSpec((n, LANES), const),
                     pl.BlockSpec((1, 1, PAST_LEN, D_KV), lambda s: (s, layer, 0, 0)),
                     pl.BlockSpec((1, 1, PAST_LEN, D_KV), lambda s: (s, layer, 0, 0)),
                     pl.BlockSpec((1, 1, 2, RET_HEADS, RET_DK, RET_DV), lambda s: (s, layer, 0, 0, 0, 0))]
        args += [cos, sin, cache_k, cache_v, state]
        kv_rows = n + 2 * CHUNK
    else:
        out_specs.append(pl.BlockSpec((1, 2, RET_HEADS, RET_DK, RET_DV), lambda s: (s, 0, 0, 0, 0)))
        out_shape.append(jax.ShapeDtypeStruct((nseq, 2, RET_HEADS, RET_DK, RET_DV), F32))
        kv_rows = n
    scratch = [pltpu.VMEM((2 * N_KV, LANES, kv_rows), BF16),
               pltpu.VMEM((2 * N_KV, kv_rows, LANES), BF16),
               pltpu.VMEM((RET_PAIRS, LANES, n), BF16),
               pltpu.VMEM((n, D_RET), F32),
               pltpu.VMEM((n, D_RET), F32),
               pltpu.VMEM((2 * RET_PAIRS, LANES, LANES), F32),
               pltpu.VMEM((2 * RET_HEADS, CHUNK, CHUNK), F32),
               pltpu.VMEM((2 * RET_PAIRS, CHUNK, CHUNK), F32),
               pltpu.VMEM((2 * RET_PAIRS, CHUNK, CHUNK), F32),
               pltpu.VMEM((2 * RET_PAIRS, CHUNK, CHUNK), F32)]
    return pl.pallas_call(
        functools.partial(_mixer_kernel, n=n, latent=latent),
        grid=(nseq,),
        in_specs=in_specs,
        out_specs=out_specs,
        out_shape=out_shape,
        scratch_shapes=scratch,
        compiler_params=pltpu.CompilerParams(vmem_limit_bytes=56 * MIB),
        name="mixers_latent" if latent else "mixers_context",
    )(*args)


def _merge_kernel(*refs, split):
    if split:
        xc_ref, xl_ref = refs[0:2]
        x = _pick(xc_ref, xl_ref)
        refs = refs[2:]
    else:
        x = refs[0][...]
        refs = refs[1:]
    (ycc_ref, ycl_ref, yac_ref, yal_ref, yrc_ref, yrl_ref, gate_ref, mod_ref, g2_ref, wa_ref, wb_ref, wc_ref,
     wo_ref, wrh_ref, wrl_ref, br_ref, x1_o, xp_o, key_o, tw_o, cnt_o) = refs
    merged = (gate_ref[:, 0:D_MODEL].astype(F32)
              * jnp.dot(_pick(ycc_ref, ycl_ref), wa_ref[0].astype(BF16), preferred_element_type=F32)
              + gate_ref[:, D_MODEL:2 * D_MODEL].astype(F32)
              * jnp.dot(_pick(yac_ref, yal_ref), wb_ref[0].astype(BF16), preferred_element_type=F32)
              + gate_ref[:, 2 * D_MODEL:3 * D_MODEL].astype(F32)
              * jnp.dot(_pick(yrc_ref, yrl_ref), wc_ref[0].astype(BF16), preferred_element_type=F32))
    x1 = x + mod_ref[0, 2:3, :] * jnp.dot(merged.astype(BF16), wo_ref[0].astype(BF16),
                                          preferred_element_type=F32)
    x1_o[...] = x1
    ms = jnp.mean(x1 * x1, axis=-1, keepdims=True)
    h2 = x1 * lax.rsqrt(ms + EPS) * g2_ref[...]
    h2 = h2 * (1.0 + mod_ref[0, 4:5, :]) + mod_ref[0, 3:4, :]
    hh = h2.astype(BF16)
    hf = hh.astype(F32)
    bits = lax.bitcast_convert_type(hf, jnp.uint32)
    for c in range(PACK_ROWS):
        lo = bits[:, c * LANES:(c + 1) * LANES] >> 16
        hi = bits[:, (c + PACK_ROWS) * LANES:(c + PACK_ROWS + 1) * LANES] & jnp.uint32(0xFFFF0000)
        xp_o[pl.ds(c, TM, stride=PACK_ROWS), :] = lax.bitcast_convert_type(lo | hi, jnp.int32)
    hl = (h2 - hf).astype(BF16)
    logits = (jnp.dot(hh, wrh_ref[...], preferred_element_type=F32)
              + jnp.dot(hl, wrh_ref[...], preferred_element_type=F32)
              + jnp.dot(hh, wrl_ref[...], preferred_element_type=F32)
              + br_ref[...])
    lane = lax.broadcasted_iota(jnp.int32, logits.shape, 1)
    work = jnp.where(lane < N_EXPERTS, logits, -jnp.inf)
    vals, idxs = [], []
    for _ in range(TOP_K):
        m = jnp.max(work, axis=-1, keepdims=True)
        am = jnp.min(jnp.where(work == m, lane, LANES), axis=-1, keepdims=True)
        vals.append(m)
        idxs.append(am)
        work = jnp.where(lane == am, -jnp.inf, work)
    es = [jnp.exp(v - vals[0]) for v in vals]
    den = es[0] + es[1] + es[2] + es[3]
    tok = pl.program_id(0) * TM + lax.broadcasted_iota(jnp.int32, (TM, 1), 0)
    keys = jnp.zeros(logits.shape, jnp.int32)
    tw = jnp.zeros(logits.shape, F32)
    for k in range(TOP_K):
        keys = jnp.where(lane == k, (2 * idxs[k]) << TOK_BITS | tok, keys)
        tw = jnp.where(lane == k, es[k] / den, tw)
    key_o[...] = keys.astype(F32).T[0:SUBLANES, :].astype(jnp.int32)
    tw_o[...] = tw.T[0:SUBLANES, :]
    sel = (lane == idxs[0]) | (lane == idxs[1]) | (lane == idxs[2]) | (lane == idxs[3])
    part = jnp.sum(sel.astype(jnp.int32), axis=0, keepdims=True)

    @pl.when(pl.program_id(0) == 0)
    def _():
        cnt_o[...] = jnp.zeros_like(cnt_o)

    cnt_o[...] += jnp.broadcast_to(part, cnt_o.shape)


def _merge(xs, ys_ctx, ys_lat, gates, mod_l, g2, wa, wb, wc, wo, wrh, wrl, br, layer):
    row = lambda i: (i, 0)
    const = lambda i: (0, 0)
    wl = lambda i: (layer, 0, 0)
    y_specs, y_args = [], []
    for width, yc, yl in zip((D_CONV, D_ATTN, D_RET), ys_ctx, ys_lat):
        y_specs += [pl.BlockSpec((TM, width), _ctx_tile), pl.BlockSpec((TM, width), _lat_tile)]
        y_args += [yc, yl]
    return pl.pallas_call(
        functools.partial(_merge_kernel, split=len(xs) == 2),
        grid=(T // TM,),
        in_specs=_x_specs(xs) + y_specs + [
                  pl.BlockSpec((TM, 3 * D_MODEL), row),
                  pl.BlockSpec((1, 6, D_MODEL), lambda i: (_mod_group(i), 0, 0)),
                  pl.BlockSpec((1, D_MODEL), const),
                  pl.BlockSpec((1, D_CONV, D_MODEL), wl),
                  pl.BlockSpec((1, D_ATTN, D_MODEL), wl),
                  pl.BlockSpec((1, D_RET, D_MODEL), wl),
                  pl.BlockSpec((1, D_MODEL, D_MODEL), wl),
                  pl.BlockSpec((D_MODEL, LANES), const),
                  pl.BlockSpec((D_MODEL, LANES), const),
                  pl.BlockSpec((1, LANES), const)],
        out_specs=[pl.BlockSpec((TM, D_MODEL), row),
                   pl.BlockSpec((TM * PACK_ROWS, LANES), row),
                   pl.BlockSpec((SUBLANES, TM), lambda i: (0, i)),
                   pl.BlockSpec((SUBLANES, TM), lambda i: (0, i)),
                   pl.BlockSpec((SUBLANES, LANES), const)],
        out_shape=[jax.ShapeDtypeStruct((T, D_MODEL), F32),
                   jax.ShapeDtypeStruct((T * PACK_ROWS, LANES), jnp.int32),
                   jax.ShapeDtypeStruct((SUBLANES, T), jnp.int32),
                   jax.ShapeDtypeStruct((SUBLANES, T), F32),
                   jax.ShapeDtypeStruct((SUBLANES, LANES), jnp.int32)],
        compiler_params=pltpu.CompilerParams(vmem_limit_bytes=48 * MIB),
        name="merge_router",
    )(*xs, *y_args, gates, mod_l, g2, wa, wb, wc, wo, wrh, wrl, br)


def _moe_kernel(blk_e_ref, first_ref, next_e_ref, nu_ref, tok_ref, tok_next_ref, xp_ref, wgu_hbm, bgu_ref, wd_hbm,
                bd_ref, y_ref, tile, wgu_st, wd_st, wgu_bf, wd_bf, sems, *, layer):
    b = pl.program_id(0)
    slot = b % 2

    def weight_copies(e):
        return (pltpu.make_async_copy(wgu_hbm.at[layer, e], wgu_st, sems.at[0]),
                pltpu.make_async_copy(wd_hbm.at[layer, e], wd_st, sems.at[1]))

    def gather(tok, dst_slot):
        for mi in range(MOE_M):
            t = jnp.minimum(tok[0, 0, mi], T - 1)
            slab = xp_ref[pl.ds(pl.multiple_of(t * PACK_ROWS, PACK_ROWS), PACK_ROWS), :]
            tile[dst_slot, pl.ds(mi, PACK_ROWS, stride=GATHER_STRIDE), :] = slab

    @pl.when(b == 0)
    def _():
        for cp in weight_copies(blk_e_ref[0]):
            cp.start()
        gather(tok_ref, 0)

    @pl.when(b < nu_ref[0])
    def _():
        @pl.when(first_ref[b] == 1)
        def _():
            for cp in weight_copies(blk_e_ref[b]):
                cp.wait()
            wgu_bf[...] = wgu_st[...].astype(BF16)
            wd_bf[...] = wd_st[...].astype(BF16)

            @pl.when(next_e_ref[b] >= 0)
            def _():
                for cp in weight_copies(next_e_ref[b]):
                    cp.start()

        lo, hi = [], []
        for c in range(PACK_ROWS):
            bits = lax.bitcast_convert_type(
                tile[slot, c * GATHER_STRIDE:c * GATHER_STRIDE + MOE_M, :], jnp.uint32)
            lo.append(lax.bitcast_convert_type(bits << 16, F32).astype(BF16))
            hi.append(lax.bitcast_convert_type(bits & jnp.uint32(0xFFFF0000), F32).astype(BF16))
        x = jnp.concatenate(lo + hi, axis=1)
        gather(tok_next_ref, 1 - slot)

        gu = jnp.dot(x, wgu_bf[...], preferred_element_type=F32) + bgu_ref[0, 0]
        gate = jnp.minimum(gu[:, 0:D_EXPERT], SWIGLU_LIMIT)
        up = jnp.clip(gu[:, D_EXPERT:2 * D_EXPERT], -SWIGLU_LIMIT, SWIGLU_LIMIT)
        glu = gate * _sigmoid(SWIGLU_ALPHA * gate)
        mid = ((up + 1.0) * glu).astype(BF16)
        y = jnp.dot(mid, wd_bf[...], preferred_element_type=F32) + bd_ref[0, 0]
        y_ref[...] = y.astype(BF16)

    @pl.when(b >= nu_ref[0])
    def _():
        y_ref[...] = jnp.zeros_like(y_ref)


def _moe_experts(blk_e, first, next_e, n_used, row_tok3, xp, w_gu, b_gu, w_down, b_down, layer):
    bias = lambda b, e, f, ne, nu: (layer, e[b], 0, 0)
    grid_spec = pltpu.PrefetchScalarGridSpec(
        num_scalar_prefetch=4,
        grid=(N_BLOCKS,),
        in_specs=[pl.BlockSpec((1, 1, MOE_M), lambda b, e, f, ne, nu: (b, 0, 0), memory_space=pltpu.SMEM),
                  pl.BlockSpec((1, 1, MOE_M), lambda b, e, f, ne, nu: (jnp.minimum(b + 1, N_BLOCKS - 1), 0, 0),
                               memory_space=pltpu.SMEM),
                  pl.BlockSpec((T * PACK_ROWS, LANES), lambda b, e, f, ne, nu: (0, 0), pipeline_mode=pl.Buffered(1)),
                  pl.BlockSpec(memory_space=pl.ANY),
                  pl.BlockSpec((1, 1, 1, 2 * D_EXPERT), bias),
                  pl.BlockSpec(memory_space=pl.ANY),
                  pl.BlockSpec((1, 1, 1, D_MODEL), bias)],
        out_specs=pl.BlockSpec((MOE_M, D_MODEL), lambda b, e, f, ne, nu: (b, 0)),
        scratch_shapes=[pltpu.VMEM((2, PACK_ROWS * GATHER_STRIDE, LANES), jnp.int32),
                        pltpu.VMEM((D_MODEL, 2 * D_EXPERT), F32),
                        pltpu.VMEM((D_EXPERT, D_MODEL), F32),
                        pltpu.VMEM((D_MODEL, 2 * D_EXPERT), BF16),
                        pltpu.VMEM((D_EXPERT, D_MODEL), BF16),
                        pltpu.SemaphoreType.DMA((2,))],
    )
    return pl.pallas_call(
        functools.partial(_moe_kernel, layer=layer),
        grid_spec=grid_spec,
        out_shape=jax.ShapeDtypeStruct((N_ROWS, D_MODEL), BF16),
        compiler_params=pltpu.CompilerParams(vmem_limit_bytes=48 * MIB),
        name="moe_experts",
    )(blk_e, first, next_e, n_used, row_tok3, row_tok3, xp, w_gu, b_gu.reshape(DEPTH, N_EXPERTS, 1, 2 * D_EXPERT),
      w_down, b_down.reshape(DEPTH, N_EXPERTS, 1, D_MODEL))


SCATTER_UNROLL = 8


def _combine_kernel(nu_ref, tok_ref, w_ref, y_ref, o_ref, acc, tile, sem):
    s = pl.program_id(0)

    @pl.when(s == 0)
    def _():
        acc[...] = jnp.zeros_like(acc)

    def sub_block(sb, carry):
        r0 = pl.multiple_of(sb * SCATTER_M, SCATTER_M)
        y = y_ref[pl.ds(r0, SCATTER_M), :].astype(F32)
        for c in range(ROW_VREGS):
            tile[c * SCATTER_STRIDE:c * SCATTER_STRIDE + SCATTER_M, :] = y[:, c * LANES:(c + 1) * LANES]
        for m0 in range(0, SCATTER_M, SCATTER_UNROLL):
            addrs, vals = [], []
            for u in range(SCATTER_UNROLL):
                mi = m0 + u
                a = pl.multiple_of(tok_ref[0, 0, r0 + mi] * ROW_VREGS, ROW_VREGS)
                yv = tile[pl.ds(mi, ROW_VREGS, stride=SCATTER_STRIDE), :]
                addrs.append(a)
                vals.append(acc[pl.ds(a, ROW_VREGS), :] + w_ref[0, 0, r0 + mi] * yv)
            for u in range(SCATTER_UNROLL):
                acc[pl.ds(addrs[u], ROW_VREGS), :] = vals[u]
        return carry

    @pl.when(s * COMBINE_BLOCKS < nu_ref[0])
    def _():
        lax.fori_loop(0, COMBINE_BLOCKS * MOE_M // SCATTER_M, sub_block, 0)

    @pl.when(s == pl.num_programs(0) - 1)
    def _():
        cp = pltpu.make_async_copy(acc.at[pl.ds(0, T * ROW_VREGS)], o_ref, sem)
        cp.start()
        cp.wait()


def _combine(n_used, row_tok3, row_w3, yr):
    rows = COMBINE_BLOCKS * MOE_M
    steps = N_BLOCKS // COMBINE_BLOCKS
    grid_spec = pltpu.PrefetchScalarGridSpec(
        num_scalar_prefetch=1,
        grid=(steps,),
        in_specs=[pl.BlockSpec((1, 1, rows), lambda s, nu: (s, 0, 0), memory_space=pltpu.SMEM),
                  pl.BlockSpec((1, 1, rows), lambda s, nu: (s, 0, 0), memory_space=pltpu.SMEM),
                  pl.BlockSpec((rows, D_MODEL), lambda s, nu: (s, 0))],
        out_specs=pl.BlockSpec(memory_space=pl.ANY),
        scratch_shapes=[pltpu.VMEM(((T + 1) * ROW_VREGS, LANES), F32),
                        pltpu.VMEM((ROW_VREGS * SCATTER_STRIDE, LANES), F32),
                        pltpu.SemaphoreType.DMA(())],
    )
    return pl.pallas_call(
        _combine_kernel,
        grid_spec=grid_spec,
        out_shape=jax.ShapeDtypeStruct((T * ROW_VREGS, LANES), F32),
        compiler_params=pltpu.CompilerParams(vmem_limit_bytes=48 * MIB),
        name="moe_combine",
    )(n_used, row_tok3.reshape(steps, 1, rows), row_w3.reshape(steps, 1, rows), yr)


def _route(top_keys, top_w, counts):
    experts = jnp.arange(N_EXPERTS, dtype=jnp.int32)
    padded = (counts + MOE_M - 1) // MOE_M * MOE_M
    pad_end = jnp.cumsum(padded)
    pad_start = pad_end - padded
    n_used = (pad_end[-1] // MOE_M).astype(jnp.int32)
    assert N_ROWS - N_ASSIGN == N_EXPERTS * MOE_M
    used = jnp.arange(MOE_M, dtype=jnp.int32)[None, :] < (padded - counts)[:, None]
    pad_keys = jnp.where(used, 2 * experts[:, None] + 1, 2 * N_EXPERTS + 1) << TOK_BITS | T
    keys = jnp.concatenate([top_keys.reshape(N_ASSIGN), pad_keys.reshape(N_EXPERTS * MOE_M)])
    wts = jnp.concatenate([top_w.reshape(N_ASSIGN), jnp.zeros((N_EXPERTS * MOE_M,), F32)])
    keys, row_w = lax.sort((keys, wts), num_keys=1)
    row_tok = keys & ((1 << TOK_BITS) - 1)
    blk0 = jnp.arange(N_BLOCKS, dtype=jnp.int32) * MOE_M
    blk_e = jnp.minimum(jnp.sum((pad_end[:, None] <= blk0[None, :]).astype(jnp.int32), axis=0), N_EXPERTS - 1)
    first = (blk0 == pad_start[blk_e]).astype(jnp.int32)
    later = (experts[None, :] > experts[:, None]) & (counts[None, :] > 0)
    nxt = jnp.min(jnp.where(later, experts[None, :], N_EXPERTS), axis=1)
    next_e = jnp.where(nxt == N_EXPERTS, -1, nxt)[blk_e].astype(jnp.int32)
    return (row_tok.reshape(N_BLOCKS, 1, MOE_M), row_w.reshape(N_BLOCKS, 1, MOE_M), blk_e.astype(jnp.int32),
            first, next_e, n_used.reshape(1))


def _final_kernel(x_ref, moe_ref, mod_ref, g_ref, yc_o, yl_o):
    x = x_ref[...] + mod_ref[0, 5:6, :] * _tiles_to_rows(moe_ref)
    ms = jnp.mean(x * x, axis=-1, keepdims=True)
    y = x * lax.rsqrt(ms + EPS) * g_ref[...]
    i = pl.program_id(0)

    @pl.when(i < N_CTX_TILES)
    def _():
        yc_o[...] = y

    @pl.when(i >= N_CTX_TILES)
    def _():
        yl_o[...] = y


def _final(x1, moe_tiles, mod_l, g):
    row = lambda i: (i, 0)
    return pl.pallas_call(
        _final_kernel,
        grid=(T // TM,),
        in_specs=[pl.BlockSpec((TM, D_MODEL), row),
                  pl.BlockSpec((TM * ROW_VREGS, LANES), row),
                  pl.BlockSpec((1, 6, D_MODEL), lambda i: (_mod_group(i), 0, 0)),
                  pl.BlockSpec((1, D_MODEL), lambda i: (0, 0))],
        out_specs=[pl.BlockSpec((TM, D_MODEL), _ctx_tile), pl.BlockSpec((TM, D_MODEL), _lat_tile)],
        out_shape=[jax.ShapeDtypeStruct((T_CTX, D_MODEL), F32), jax.ShapeDtypeStruct((T_LAT, D_MODEL), F32)],
        compiler_params=pltpu.CompilerParams(vmem_limit_bytes=32 * MIB),
        name="residual_final",
    )(x1, moe_tiles, mod_l, g)


def _rope_tables():
    t = np.arange(DEC_SEQ)
    pos = np.stack([t // GRID_W, t % GRID_W], axis=1).astype(np.float32)
    half = HEAD_DIM // 2
    inv = np.float32(ROPE_BASE) ** (-np.arange(0, half, 2, dtype=np.float32) / np.float32(half))
    d = np.arange(HEAD_DIM)
    which = d // half
    freq = d % (half // 2)
    sign = np.where((d % half) < half // 2, -1.0, 1.0).astype(np.float32)
    ang = (pos[:, which] * inv[freq][None, :]).astype(np.float32)
    reps = LANES // HEAD_DIM
    cos = np.tile(np.cos(ang).astype(np.float32), (1, reps))
    sin = np.tile(np.sin(ang).astype(np.float32) * sign[None, :], (1, reps))
    return jnp.asarray(cos), jnp.asarray(sin)


def kernel(x_prompt, x_sample, cache_k, cache_v, state_ret, c, c_ctx, norm1_g, norm2_g, w_mod, b_mod, w_in, conv_w, attn_sink, ret_decay, w_a, w_b, w_c, w_o, w_router, b_router, w_gu, b_gu, w_down, b_down, final_g):
    xs = (x_prompt.reshape(T_CTX, D_MODEL), x_sample.reshape(T_LAT, D_MODEL))
    cond8 = jnp.zeros((8, D_MODEL), F32).at[0].set(c_ctx).at[1:1 + DEC_BATCH].set(c)
    mod = _modulation(cond8, w_mod, b_mod)
    cos, sin = _rope_tables()
    ck = cache_k.reshape(DEC_BATCH, DEPTH, PAST_LEN, D_KV)
    cv = cache_v.reshape(DEC_BATCH, DEPTH, PAST_LEN, D_KV)

    ks, vs, rs = [], [], []
    prev = None
    for l in range(DEPTH):
        mod_l = mod[l, 0:1 + DEC_BATCH].reshape(1 + DEC_BATCH, 6, D_MODEL)
        if prev is None:
            conv_in, q, k, v, ret, gates = _inproj(xs, mod_l, norm1_g[l][None, :], w_in, l)
        else:
            x, conv_in, q, k, v, ret, gates = _inproj(None, mod_l, norm1_g[l][None, :], w_in, l, prev=prev)
            xs = (x,)
        rd8 = jnp.broadcast_to(ret_decay[l].reshape(2 * RET_HEADS, 1), (2 * RET_HEADS, LANES))
        *ys_ctx, rfin = _mixers(conv_in, q, k, v, ret, conv_w[l], attn_sink[l], rd8, latent=False)
        ys_lat = _mixers(conv_in, q, k, v, ret, conv_w[l], attn_sink[l], rd8, latent=True,
                         cos=cos, sin=sin, cache_k=ck, cache_v=cv, state=state_ret, layer=l)
        wr = jnp.pad(w_router[l], ((0, 0), (0, LANES - N_EXPERTS)))
        wrh = wr.astype(BF16)
        wrl = (wr - wrh.astype(F32)).astype(BF16)
        br = jnp.pad(b_router[l], (0, LANES - N_EXPERTS))[None, :]
        x1, xp, top_keys, top_w, cnt = _merge(xs, ys_ctx, ys_lat, gates, mod_l, norm2_g[l][None, :], w_a, w_b, w_c,
                                           w_o, wrh, wrl, br, l)
        row_tok, row_w, blk_e, first, next_e, n_used = _route(top_keys[0:TOP_K], top_w[0:TOP_K],
                                                              cnt[0, 0:N_EXPERTS])
        yrows = _moe_experts(blk_e, first, next_e, n_used, row_tok, xp, w_gu, b_gu, w_down, b_down, l)
        moe = _combine(n_used, row_tok, row_w, yrows)
        prev = (x1, moe, mod_l)
        ks.append(k[0:T_CTX].reshape(BATCH, SEQ, N_KV, HEAD_DIM))
        vs.append(v[0:T_CTX].reshape(BATCH, SEQ, N_KV, HEAD_DIM))
        rs.append(rfin)

    yc, yl = _final(*prev, final_g[None, :])
    y_prompt = yc.reshape(BATCH, SEQ, D_MODEL)
    y_sample = yl.reshape(DEC_BATCH, DEC_SEQ, D_MODEL)
    return (y_prompt, y_sample, jnp.stack(ks, axis=1), jnp.stack(vs, axis=1), jnp.stack(rs, axis=1))
```

```python
import functools

import numpy as np
import jax
import jax.numpy as jnp
from jax import lax
from jax.experimental import pallas as pl
from jax.experimental.pallas import tpu as pltpu

F32 = jnp.float32
BF16 = jnp.bfloat16

D_MODEL = 1024
BATCH = 16
SEQ = 256
DEPTH = 2
DEC_BATCH = 2
DEC_SEQ = 2048
PAST_LEN = 256
GRID_W = 64
HEAD_DIM = 64
D_CONV = 256
N_HEADS = 8
N_KV = 2
GROUP = N_HEADS // N_KV
WINDOW = 128
ROPE_BASE = 10000.0
RET_HEADS = 4
RET_DK = 64
RET_DV = 64
CHUNK = 128
N_EXPERTS = 32
TOP_K = 4
D_EXPERT = D_MODEL
SWIGLU_LIMIT = 7.0
SWIGLU_ALPHA = 1.702
EPS = 1e-6
NEG_INF = -1e30

T_CTX = BATCH * SEQ
T_LAT = DEC_BATCH * DEC_SEQ
T = T_CTX + T_LAT
D_ATTN = N_HEADS * HEAD_DIM
D_KV = N_KV * HEAD_DIM
D_RET = RET_HEADS * RET_DK
C_CONV = 0
C_Q = 3 * D_CONV
C_K = C_Q + D_ATTN
C_V = C_K + D_KV
C_RET = C_V + D_KV
C_GATE = C_RET + 4 * D_RET
IN_COLS = C_GATE + 3 * D_MODEL

TM = 512
MOE_M = 256
N_ASSIGN = T * TOP_K
N_BLOCKS = (N_ASSIGN + N_EXPERTS * (MOE_M - 1) + MOE_M - 1) // MOE_M
N_ROWS = N_BLOCKS * MOE_M
LANES = 128
SUBLANES = 8
ROW_VREGS = D_MODEL // LANES
PACK_ROWS = ROW_VREGS // 2
GATHER_STRIDE = MOE_M + SUBLANES
SCATTER_M = 128
SCATTER_STRIDE = SCATTER_M + SUBLANES
COMBINE_BLOCKS = 4
TOK_BITS = 14
assert T < (1 << TOK_BITS)
MIB = 1024 * 1024


def _sigmoid(x):
    return 1.0 / (1.0 + jnp.exp(-x))


def _mod_group(i):
    n_ctx = T_CTX // TM
    per_lat = DEC_SEQ // TM
    g = jnp.zeros_like(i)
    for b in range(DEC_BATCH):
        g = g + (i >= n_ctx + b * per_lat).astype(jnp.int32)
    return g


def _mod_kernel(cond_ref, w_ref, b_ref, o_ref):
    c = cond_ref[...]
    s = c * _sigmoid(c)
    o_ref[0] = jnp.dot(s.astype(BF16), w_ref[0].astype(BF16), preferred_element_type=F32) + b_ref[0]


def _modulation(cond8, w_mod, b_mod):
    n_col = 4
    cw = 6 * D_MODEL // n_col
    return pl.pallas_call(
        _mod_kernel,
        grid=(DEPTH, n_col),
        in_specs=[pl.BlockSpec((8, D_MODEL), lambda l, j: (0, 0)),
                  pl.BlockSpec((1, D_MODEL, cw), lambda l, j: (l, 0, j)),
                  pl.BlockSpec((1, 1, cw), lambda l, j: (l, 0, j))],
        out_specs=pl.BlockSpec((1, 8, cw), lambda l, j: (l, 0, j)),
        out_shape=jax.ShapeDtypeStruct((DEPTH, 8, 6 * D_MODEL), F32),
        compiler_params=pltpu.CompilerParams(vmem_limit_bytes=32 * MIB),
        name="modulation",
    )(cond8, w_mod, b_mod.reshape(DEPTH, 1, 6 * D_MODEL))


N_CTX_TILES = T_CTX // TM


def _ctx_tile(i):
    return (jnp.minimum(i, N_CTX_TILES - 1), 0)


def _lat_tile(i):
    return (jnp.maximum(i - N_CTX_TILES, 0), 0)


def _pick(ctx_ref, lat_ref):
    return jnp.where(pl.program_id(0) < N_CTX_TILES, ctx_ref[...], lat_ref[...])


def _tiles_to_rows(tiles_ref):
    return jnp.concatenate([tiles_ref[pl.ds(c, TM, stride=ROW_VREGS), :] for c in range(ROW_VREGS)], axis=1)


def _inproj_kernel(*refs, first_layer):
    if first_layer:
        xc_ref, xl_ref, mod_ref, g_ref, w_ref, conv_o, q_o, k_o, v_o, ret_o, gate_o = refs
        x = _pick(xc_ref, xl_ref)
    else:
        x1_ref, moe_ref, modp_ref, mod_ref, g_ref, w_ref, x_o, conv_o, q_o, k_o, v_o, ret_o, gate_o = refs
        x = x1_ref[...] + modp_ref[0, 5:6, :] * _tiles_to_rows(moe_ref)
        x_o[...] = x
    ms = jnp.mean(x * x, axis=-1, keepdims=True)
    h = x * lax.rsqrt(ms + EPS) * g_ref[...]
    h = h * (1.0 + mod_ref[0, 1:2, :]) + mod_ref[0, 0:1, :]
    hb = h.astype(BF16)

    def proj(c0, c1):
        return jnp.dot(hb, w_ref[0, :, c0:c1].astype(BF16), preferred_element_type=F32)

    a = proj(C_CONV, C_RET)
    conv_o[...] = a[:, C_CONV:C_Q].astype(BF16)
    q_o[...] = (a[:, C_Q:C_K] * HEAD_DIM ** -0.5).astype(BF16)
    k_o[...] = a[:, C_K:C_V]
    v_o[...] = a[:, C_V:C_RET]
    r = proj(C_RET, C_GATE)
    ret_o[:, 0:D_RET] = r[:, 0:D_RET].astype(BF16)
    ret_o[:, D_RET:2 * D_RET] = (r[:, D_RET:2 * D_RET] * RET_DK ** -0.5).astype(BF16)
    ret_o[:, 2 * D_RET:4 * D_RET] = r[:, 2 * D_RET:4 * D_RET].astype(BF16)
    gate_half = 3 * D_MODEL // 2
    for b in range(2):
        g = proj(C_GATE + b * gate_half, C_GATE + (b + 1) * gate_half)
        gate_o[:, b * gate_half:(b + 1) * gate_half] = _sigmoid(g).astype(BF16)


def _x_specs(xs):
    if len(xs) == 2:
        return [pl.BlockSpec((TM, D_MODEL), _ctx_tile), pl.BlockSpec((TM, D_MODEL), _lat_tile)]
    return [pl.BlockSpec((TM, D_MODEL), lambda i: (i, 0))]


def _inproj(xs, mod_l, g1, w_in, layer, prev=None):
    row = lambda i: (i, 0)
    mod_spec = pl.BlockSpec((1, 6, D_MODEL), lambda i: (_mod_group(i), 0, 0))
    out_specs = [pl.BlockSpec((TM, 3 * D_CONV), row),
                 pl.BlockSpec((TM, D_ATTN), row),
                 pl.BlockSpec((TM, D_KV), row),
                 pl.BlockSpec((TM, D_KV), row),
                 pl.BlockSpec((TM, 4 * D_RET), row),
                 pl.BlockSpec((TM, 3 * D_MODEL), row)]
    out_shape = [jax.ShapeDtypeStruct((T, 3 * D_CONV), BF16),
                 jax.ShapeDtypeStruct((T, D_ATTN), BF16),
                 jax.ShapeDtypeStruct((T, D_KV), F32),
                 jax.ShapeDtypeStruct((T, D_KV), F32),
                 jax.ShapeDtypeStruct((T, 4 * D_RET), BF16),
                 jax.ShapeDtypeStruct((T, 3 * D_MODEL), BF16)]
    if prev is None:
        in_specs, args = _x_specs(xs), list(xs)
    else:
        in_specs = [pl.BlockSpec((TM, D_MODEL), row), pl.BlockSpec((TM * ROW_VREGS, LANES), row), mod_spec]
        args = list(prev)
        out_specs = [pl.BlockSpec((TM, D_MODEL), row)] + out_specs
        out_shape = [jax.ShapeDtypeStruct((T, D_MODEL), F32)] + out_shape
    return pl.pallas_call(
        functools.partial(_inproj_kernel, first_layer=prev is None),
        grid=(T // TM,),
        in_specs=in_specs + [
            mod_spec,
            pl.BlockSpec((1, D_MODEL), lambda i: (0, 0)),
            pl.BlockSpec((1, D_MODEL, IN_COLS), lambda i: (layer, 0, 0), pipeline_mode=pl.Buffered(1))],
        out_specs=out_specs,
        out_shape=out_shape,
        compiler_params=pltpu.CompilerParams(vmem_limit_bytes=60 * MIB),
        name="inproj",
    )(*args, mod_l, g1, w_in)


def _rope(x, cos, sin_signed):
    lane = lax.broadcasted_iota(jnp.int32, x.shape, 1)
    first = (lane % 32) < 16
    partner = jnp.where(first, pltpu.roll(x, x.shape[1] - 16, 1), pltpu.roll(x, 16, 1))
    return x * cos + partner * sin_signed


HALF = LANES // 2
assert HEAD_DIM == HALF and RET_DK == HALF and RET_DV == HALF
N_PAIRS = N_HEADS // 2
RET_PAIRS = RET_HEADS // 2


def _row_variants(kt):
    row = lax.broadcasted_iota(jnp.int32, kt.shape, 0)
    lo0 = jnp.where(row < HALF, kt, 0.0)
    hi1 = jnp.where(row >= HALF, kt, 0.0)
    return lo0, pltpu.roll(lo0, HALF, 0), pltpu.roll(hi1, HALF, 0), hi1


def _lane_variants(v):
    lane = lax.broadcasted_iota(jnp.int32, v.shape, 1)
    lo0 = jnp.where(lane < HALF, v, 0.0)
    hi1 = jnp.where(lane >= HALF, v, 0.0)
    return lo0, pltpu.roll(lo0, HALF, 1), pltpu.roll(hi1, HALF, 1), hi1


def _mixer_kernel(*refs, n, latent):
    if latent:
        (sink_ref, conv_ref, q_ref, k_ref, v_ref, ret_ref, cw_ref, rd_ref, cos_ref, sin_ref, ck_ref, cv_ref,
         r0_ref, yc_ref, ya_ref, yr_ref, ktq, vq, rkp, o_f, o_b, rst, dmat, qdec, kdec, cdec) = refs
    else:
        (sink_ref, conv_ref, q_ref, k_ref, v_ref, ret_ref, cw_ref, rd_ref,
         yc_ref, ya_ref, yr_ref, rfin_ref, ktq, vq, rkp, o_f, o_b, rst, dmat, qdec, kdec, cdec) = refs
    nb = n // CHUNK
    pad = CHUNK if latent else 0

    cv = conv_ref[...].astype(F32)
    cb, cc, cu = cv[:, 0:D_CONV], cv[:, D_CONV:2 * D_CONV], cv[:, 2 * D_CONV:3 * D_CONV]
    p = cc * cu
    row = lax.broadcasted_iota(jnp.int32, p.shape, 0)
    prev = jnp.where(row == 0, 0.0, pltpu.roll(p, 1, 0))
    nxt = jnp.where(row == n - 1, 0.0, pltpu.roll(p, n - 1, 0))
    cw = cw_ref[...]
    yc_ref[...] = (cb * (prev * cw[0:1, :] + p * cw[1:2, :] + nxt * cw[2:3, :])).astype(BF16)

    kf = k_ref[...]
    if latent:
        kf = _rope(kf, cos_ref[...], sin_ref[...])
    for idx, (kk, vv) in enumerate(zip(_row_variants(kf.T), _lane_variants(v_ref[...]))):
        if latent:
            ktq[idx, :, 0:pad] = jnp.zeros((LANES, pad), BF16)
            ktq[idx, :, pad + n:2 * pad + n] = jnp.zeros((LANES, pad), BF16)
            vq[idx, 0:pad, :] = jnp.zeros((pad, LANES), BF16)
            vq[idx, pad + n:2 * pad + n, :] = jnp.zeros((pad, LANES), BF16)
        ktq[idx, :, pad:pad + n] = kk.astype(BF16)
        vq[idx, pad:pad + n, :] = vv.astype(BF16)
    if latent:
        cktq = [t.astype(BF16) for t in _row_variants(ck_ref[0, 0].T)]
        cvq = [t.astype(BF16) for t in _lane_variants(cv_ref[0, 0])]

    def attn(r0, rows):
        qj = q_ref[pl.ds(r0, rows), :]
        if latent:
            cosj = cos_ref[pl.ds(r0, rows), :]
            sinj = sin_ref[pl.ds(r0, rows), :]
            qpos = r0 + lax.broadcasted_iota(jnp.int32, (rows, 3 * CHUNK), 0)
            kpos = r0 - CHUNK + lax.broadcasted_iota(jnp.int32, (rows, 3 * CHUNK), 1)
            ok = (jnp.abs(qpos - kpos) <= WINDOW) & (kpos >= 0) & (kpos < n)
        for m in range(N_PAIRS):
            q2 = qj[:, m * LANES:(m + 1) * LANES]
            if latent:
                q2 = _rope(q2.astype(F32), cosj, sinj).astype(BF16)
            g = (2 * m) // GROUP
            acc = None
            for half in range(2):
                idx = 2 * g + half
                sk = sink_ref[2 * m + half]
                if latent:
                    s = jnp.dot(q2, ktq[idx, :, pl.ds(r0, 3 * CHUNK)], preferred_element_type=F32)
                    s = jnp.where(ok, s, NEG_INF)
                    s2 = jnp.dot(q2, cktq[idx], preferred_element_type=F32)
                    mx = jnp.maximum(jnp.maximum(jnp.max(s, axis=-1, keepdims=True),
                                                 jnp.max(s2, axis=-1, keepdims=True)), sk)
                    pw = jnp.exp(s - mx)
                    p2 = jnp.exp(s2 - mx)
                    den = (jnp.sum(pw, axis=-1, keepdims=True) + jnp.sum(p2, axis=-1, keepdims=True)
                           + jnp.exp(sk - mx))
                    o = (jnp.dot(pw.astype(BF16), vq[idx, pl.ds(r0, 3 * CHUNK), :], preferred_element_type=F32)
                         + jnp.dot(p2.astype(BF16), cvq[idx], preferred_element_type=F32))
                else:
                    s = jnp.dot(q2, ktq[idx], preferred_element_type=F32)
                    mx = jnp.maximum(jnp.max(s, axis=-1, keepdims=True), sk)
                    pw = jnp.exp(s - mx)
                    den = jnp.sum(pw, axis=-1, keepdims=True) + jnp.exp(sk - mx)
                    o = jnp.dot(pw.astype(BF16), vq[idx], preferred_element_type=F32)
                o = o / den
                acc = o if acc is None else acc + o
            ya_ref[pl.ds(r0, rows), m * LANES:(m + 1) * LANES] = acc.astype(BF16)

    rk_t = ret_ref[:, D_RET:2 * D_RET].astype(F32).T
    for m in range(RET_PAIRS):
        rkp[m] = rk_t[m * LANES:(m + 1) * LANES, :].astype(BF16)
    rd = rd_ref[...]
    log_g = jnp.minimum(rd, 0.0) - jnp.log(1.0 + jnp.exp(-jnp.abs(rd)))
    row_c = lax.broadcasted_iota(jnp.int32, (CHUNK, CHUNK), 0)
    lane_c = lax.broadcasted_iota(jnp.int32, (CHUNK, CHUNK), 1)
    ii = row_c.astype(F32)
    jj = lane_c.astype(F32)
    even_row = row_c < HALF
    even_lane = lane_c < HALF
    blockdiag = even_row == even_lane
    for d in range(2):
        for h in range(RET_HEADS):
            r = d * RET_HEADS + h
            diff = (ii - jj) if d == 0 else (jj - ii)
            dmat[r] = jnp.where(diff >= 0, jnp.exp(jnp.maximum(diff, 0.0) * log_g[r:r + 1, :]), 0.0)
        for m in range(RET_PAIRS):
            s = d * RET_PAIRS + m
            lg_e = log_g[d * RET_HEADS + 2 * m:d * RET_HEADS + 2 * m + 1, :]
            lg_o = log_g[d * RET_HEADS + 2 * m + 1:d * RET_HEADS + 2 * m + 2, :]
            qpow = (ii + 1.0) if d == 0 else (CHUNK - ii)
            kpow = (CHUNK - 1.0 - jj) if d == 0 else jj
            qdec[s] = jnp.where(even_lane, jnp.exp(qpow * lg_e), jnp.exp(qpow * lg_o))
            kdec[s] = jnp.where(even_row, jnp.exp(kpow * lg_e), jnp.exp(kpow * lg_o))
            chunk_decay = jnp.where(even_row, jnp.exp(float(CHUNK) * lg_e), jnp.exp(float(CHUNK) * lg_o))
            cdec[s] = jnp.where(blockdiag, chunk_decay, 0.0)
            if latent:
                z = jnp.zeros((HALF, HALF), F32)
                rst[s] = jnp.concatenate(
                    [jnp.concatenate([r0_ref[0, 0, d, 2 * m], z], axis=1),
                     jnp.concatenate([z, r0_ref[0, 0, d, 2 * m + 1]], axis=1)], axis=0)
            else:
                rst[s] = jnp.zeros((LANES, LANES), F32)

    def ret_chunk(c0, d, out_ref):
        for m in range(RET_PAIRS):
            s = d * RET_PAIRS + m
            q2 = ret_ref[pl.ds(c0, CHUNK), m * LANES:(m + 1) * LANES]
            v2 = ret_ref[pl.ds(c0, CHUNK), 2 * D_RET + m * LANES:2 * D_RET + (m + 1) * LANES]
            kt2 = rkp[m, :, pl.ds(c0, CHUNK)].astype(F32)
            v2f = v2.astype(F32)
            state = rst[s]
            o2 = jnp.dot(q2, state.astype(BF16), preferred_element_type=F32) * qdec[s]
            for half in range(2):
                r = d * RET_HEADS + 2 * m + half
                keep_row = even_row if half == 0 else jnp.logical_not(even_row)
                keep_lane = even_lane if half == 0 else jnp.logical_not(even_lane)
                a = jnp.dot(q2, jnp.where(keep_row, kt2, 0.0).astype(BF16), preferred_element_type=F32)
                inner = (a * dmat[r]).astype(BF16)
                o2 = o2 + jnp.dot(inner, jnp.where(keep_lane, v2f, 0.0).astype(BF16), preferred_element_type=F32)
            kd = (kt2 * kdec[s]).astype(BF16)
            upd = jnp.dot(kd, v2, preferred_element_type=F32)
            rst[s] = state * cdec[s] + jnp.where(blockdiag, upd, 0.0)
            out_ref[pl.ds(c0, CHUNK), m * LANES:(m + 1) * LANES] = o2

    gi = lax.broadcasted_iota(jnp.int32, (D_RET, D_RET), 0) // RET_DV
    gj = lax.broadcasted_iota(jnp.int32, (D_RET, D_RET), 1) // RET_DV
    group_mean = jnp.where(gi == gj, 1.0 / RET_DV, 0.0).astype(BF16)

    def norm(c0, rows):
        o = o_f[pl.ds(c0, rows), :] + o_b[pl.ds(c0, rows), :]
        sq = o * o
        hi = sq.astype(BF16)
        lo = (sq - hi.astype(F32)).astype(BF16)
        ms = (jnp.dot(hi, group_mean, preferred_element_type=F32)
              + jnp.dot(lo, group_mean, preferred_element_type=F32))
        rg = ret_ref[pl.ds(c0, rows), 3 * D_RET:4 * D_RET].astype(F32)
        yr_ref[pl.ds(c0, rows), :] = (rg * _sigmoid(rg) * (o * lax.rsqrt(ms + EPS))).astype(BF16)

    if latent:
        def scan_body(j, carry):
            attn(pl.multiple_of(j * CHUNK, CHUNK), CHUNK)
            ret_chunk(pl.multiple_of(j * CHUNK, CHUNK), 0, o_f)
            ret_chunk(pl.multiple_of((nb - 1 - j) * CHUNK, CHUNK), 1, o_b)
            return carry

        def norm_body(j, carry):
            norm(pl.multiple_of(j * CHUNK, CHUNK), CHUNK)
            return carry

        lax.fori_loop(0, nb, scan_body, 0)
        lax.fori_loop(0, nb, norm_body, 0)
    else:
        attn(0, n)
        for j in range(nb):
            ret_chunk(j * CHUNK, 0, o_f)
            ret_chunk((nb - 1 - j) * CHUNK, 1, o_b)
        norm(0, n)
        for d in range(2):
            for h in range(RET_HEADS):
                lo_ = (h % 2) * HALF
                rfin_ref[0, d, h] = rst[d * RET_PAIRS + h // 2][lo_:lo_ + HALF, lo_:lo_ + HALF]


def _mixers(conv_in, q, k, v, ret, conv_w_l, sink_l, rd8, *, latent, cos=None, sin=None, cache_k=None,
            cache_v=None, state=None, layer=0):
    n = DEC_SEQ if latent else SEQ
    nseq = DEC_BATCH if latent else BATCH
    off = T_CTX // n if latent else 0
    seq = lambda s: (s + off, 0)
    const = lambda s: (0, 0)
    in_specs = [pl.BlockSpec(memory_space=pltpu.SMEM),
                pl.BlockSpec((n, 3 * D_CONV), seq),
                pl.BlockSpec((n, D_ATTN), seq),
                pl.BlockSpec((n, D_KV), seq),
                pl.BlockSpec((n, D_KV), seq),
                pl.BlockSpec((n, 4 * D_RET), seq),
                pl.BlockSpec((3, D_CONV), const),
                pl.BlockSpec((8, LANES), const)]
    args = [sink_l, conv_in, q, k, v, ret, conv_w_l, rd8]
    out_specs = [pl.BlockSpec((n, D_CONV), lambda s: (s, 0)),
                 pl.BlockSpec((n, D_ATTN), lambda s: (s, 0)),
                 pl.BlockSpec((n, D_RET), lambda s: (s, 0))]
    out_shape = [jax.ShapeDtypeStruct((nseq * n, D_CONV), BF16),
                 jax.ShapeDtypeStruct((nseq * n, D_ATTN), BF16),
                 jax.ShapeDtypeStruct((nseq * n, D_RET), BF16)]
    if latent:
        in_specs += [pl.BlockSpec((n, LANES), const),
                     pl.BlockSpec((n, LANES), const),
                     pl.BlockSpec((1, 1, PAST_LEN, D_KV), lambda s: (s, layer, 0, 0)),
                     pl.BlockSpec((1, 1, PAST_LEN, D_KV), lambda s: (s, layer, 0, 0)),
                     pl.BlockSpec((1, 1, 2, RET_HEADS, RET_DK, RET_DV), lambda s: (s, layer, 0, 0, 0, 0))]
        args += [cos, sin, cache_k, cache_v, state]
        kv_rows = n + 2 * CHUNK
    else:
        out_specs.append(pl.BlockSpec((1, 2, RET_HEADS, RET_DK, RET_DV), lambda s: (s, 0, 0, 0, 0)))
        out_shape.append(jax.ShapeDtypeStruct((nseq, 2, RET_HEADS, RET_DK, RET_DV), F32))
        kv_rows = n
    scratch = [pltpu.VMEM((2 * N_KV, LANES, kv_rows), BF16),
               pltpu.VMEM((2 * N_KV, kv_rows, LANES), BF16),
               pltpu.VMEM((RET_PAIRS, LANES, n), BF16),
               pltpu.VMEM((n, D_RET), F32),
               pltpu.VMEM((n, D_RET), F32),
               pltpu.VMEM((2 * RET_PAIRS, LANES, LANES), F32),
               pltpu.VMEM((2 * RET_HEADS, CHUNK, CHUNK), F32),
               pltpu.VMEM((2 * RET_PAIRS, CHUNK, CHUNK), F32),
               pltpu.VMEM((2 * RET_PAIRS, CHUNK, CHUNK), F32),
               pltpu.VMEM((2 * RET_PAIRS, CHUNK, CHUNK), F32)]
    return pl.pallas_call(
        functools.partial(_mixer_kernel, n=n, latent=latent),
        grid=(nseq,),
        in_specs=in_specs,
        out_specs=out_specs,
        out_shape=out_shape,
        scratch_shapes=scratch,
        compiler_params=pltpu.CompilerParams(vmem_limit_bytes=56 * MIB),
        name="mixers_latent" if latent else "mixers_context",
    )(*args)


def _merge_kernel(*refs, split):
    if split:
        xc_ref, xl_ref = refs[0:2]
        x = _pick(xc_ref, xl_ref)
        refs = refs[2:]
    else:
        x = refs[0][...]
        refs = refs[1:]
    (ycc_ref, ycl_ref, yac_ref, yal_ref, yrc_ref, yrl_ref, gate_ref, mod_ref, g2_ref, wa_ref, wb_ref, wc_ref,
     wo_ref, wrh_ref, wrl_ref, br_ref, x1_o, xp_o, key_o, tw_o, cnt_o) = refs
    merged = (gate_ref[:, 0:D_MODEL].astype(F32)
              * jnp.dot(_pick(ycc_ref, ycl_ref), wa_ref[0].astype(BF16), preferred_element_type=F32)
              + gate_ref[:, D_MODEL:2 * D_MODEL].astype(F32)
              * jnp.dot(_pick(yac_ref, yal_ref), wb_ref[0].astype(BF16), preferred_element_type=F32)
              + gate_ref[:, 2 * D_MODEL:3 * D_MODEL].astype(F32)
              * jnp.dot(_pick(yrc_ref, yrl_ref), wc_ref[0].astype(BF16), preferred_element_type=F32))
    x1 = x + mod_ref[0, 2:3, :] * jnp.dot(merged.astype(BF16), wo_ref[0].astype(BF16),
                                          preferred_element_type=F32)
    x1_o[...] = x1
    ms = jnp.mean(x1 * x1, axis=-1, keepdims=True)
    h2 = x1 * lax.rsqrt(ms + EPS) * g2_ref[...]
    h2 = h2 * (1.0 + mod_ref[0, 4:5, :]) + mod_ref[0, 3:4, :]
    hh = h2.astype(BF16)
    hf = hh.astype(F32)
    bits = lax.bitcast_convert_type(hf, jnp.uint32)
    for c in range(PACK_ROWS):
        lo = bits[:, c * LANES:(c + 1) * LANES] >> 16
        hi = bits[:, (c + PACK_ROWS) * LANES:(c + PACK_ROWS + 1) * LANES] & jnp.uint32(0xFFFF0000)
        xp_o[pl.ds(c, TM, stride=PACK_ROWS), :] = lax.bitcast_convert_type(lo | hi, jnp.int32)
    hl = (h2 - hf).astype(BF16)
    logits = (jnp.dot(hh, wrh_ref[...], preferred_element_type=F32)
              + jnp.dot(hl, wrh_ref[...], preferred_element_type=F32)
              + jnp.dot(hh, wrl_ref[...], preferred_element_type=F32)
              + br_ref[...])
    lane = lax.broadcasted_iota(jnp.int32, logits.shape, 1)
    work = jnp.where(lane < N_EXPERTS, logits, -jnp.inf)
    vals, idxs = [], []
    for _ in range(TOP_K):
        m = jnp.max(work, axis=-1, keepdims=True)
        am = jnp.min(jnp.where(work == m, lane, LANES), axis=-1, keepdims=True)
        vals.append(m)
        idxs.append(am)
        work = jnp.where(lane == am, -jnp.inf, work)
    es = [jnp.exp(v - vals[0]) for v in vals]
    den = es[0] + es[1] + es[2] + es[3]
    tok = pl.program_id(0) * TM + lax.broadcasted_iota(jnp.int32, (TM, 1), 0)
    keys = jnp.zeros(logits.shape, jnp.int32)
    tw = jnp.zeros(logits.shape, F32)
    for k in range(TOP_K):
        keys = jnp.where(lane == k, (2 * idxs[k]) << TOK_BITS | tok, keys)
        tw = jnp.where(lane == idxs[k], es[k] / den, tw)
    key_o[...] = keys.astype(F32).T[0:SUBLANES, :].astype(jnp.int32)
    tw_o[...] = tw
    sel = (lane == idxs[0]) | (lane == idxs[1]) | (lane == idxs[2]) | (lane == idxs[3])
    part = jnp.sum(sel.astype(jnp.int32), axis=0, keepdims=True)

    @pl.when(pl.program_id(0) == 0)
    def _():
        cnt_o[...] = jnp.zeros_like(cnt_o)

    cnt_o[...] += jnp.broadcast_to(part, cnt_o.shape)


def _merge(xs, ys_ctx, ys_lat, gates, mod_l, g2, wa, wb, wc, wo, wrh, wrl, br, layer):
    row = lambda i: (i, 0)
    const = lambda i: (0, 0)
    wl = lambda i: (layer, 0, 0)
    y_specs, y_args = [], []
    for width, yc, yl in zip((D_CONV, D_ATTN, D_RET), ys_ctx, ys_lat):
        y_specs += [pl.BlockSpec((TM, width), _ctx_tile), pl.BlockSpec((TM, width), _lat_tile)]
        y_args += [yc, yl]
    return pl.pallas_call(
        functools.partial(_merge_kernel, split=len(xs) == 2),
        grid=(T // TM,),
        in_specs=_x_specs(xs) + y_specs + [
                  pl.BlockSpec((TM, 3 * D_MODEL), row),
                  pl.BlockSpec((1, 6, D_MODEL), lambda i: (_mod_group(i), 0, 0)),
                  pl.BlockSpec((1, D_MODEL), const),
                  pl.BlockSpec((1, D_CONV, D_MODEL), wl),
                  pl.BlockSpec((1, D_ATTN, D_MODEL), wl),
                  pl.BlockSpec((1, D_RET, D_MODEL), wl),
                  pl.BlockSpec((1, D_MODEL, D_MODEL), wl),
                  pl.BlockSpec((D_MODEL, LANES), const),
                  pl.BlockSpec((D_MODEL, LANES), const),
                  pl.BlockSpec((1, LANES), const)],
        out_specs=[pl.BlockSpec((TM, D_MODEL), row),
                   pl.BlockSpec((TM * PACK_ROWS, LANES), row),
                   pl.BlockSpec((SUBLANES, TM), lambda i: (0, i)),
                   pl.BlockSpec((TM, LANES), row),
                   pl.BlockSpec((SUBLANES, LANES), const)],
        out_shape=[jax.ShapeDtypeStruct((T, D_MODEL), F32),
                   jax.ShapeDtypeStruct((T * PACK_ROWS, LANES), jnp.int32),
                   jax.ShapeDtypeStruct((SUBLANES, T), jnp.int32),
                   jax.ShapeDtypeStruct((T, LANES), F32),
                   jax.ShapeDtypeStruct((SUBLANES, LANES), jnp.int32)],
        compiler_params=pltpu.CompilerParams(vmem_limit_bytes=48 * MIB),
        name="merge_router",
    )(*xs, *y_args, gates, mod_l, g2, wa, wb, wc, wo, wrh, wrl, br)


def _moe_kernel(blk_e_ref, first_ref, next_e_ref, nu_ref, tok_ref, tok_next_ref, xp_ref, rw_ref, wgu_hbm, bgu_ref,
                wd_hbm, bd_ref, y_ref, tile, wtile, wgu_st, wd_st, wgu_bf, wd_bf, sems, *, layer):
    b = pl.program_id(0)
    slot = b % 2

    def weight_copies(e):
        return (pltpu.make_async_copy(wgu_hbm.at[layer, e], wgu_st, sems.at[0]),
                pltpu.make_async_copy(wd_hbm.at[layer, e], wd_st, sems.at[1]))

    def gather(tok, dst_slot):
        for mi in range(MOE_M):
            t = jnp.minimum(tok[0, 0, mi], T - 1)
            slab = xp_ref[pl.ds(pl.multiple_of(t * PACK_ROWS, PACK_ROWS), PACK_ROWS), :]
            tile[dst_slot, pl.ds(mi, PACK_ROWS, stride=GATHER_STRIDE), :] = slab
            wtile[dst_slot, pl.ds(mi, 1), :] = rw_ref[t]

    @pl.when(b == 0)
    def _():
        for cp in weight_copies(blk_e_ref[0]):
            cp.start()
        gather(tok_ref, 0)

    @pl.when(b < nu_ref[0])
    def _():
        @pl.when(first_ref[b] == 1)
        def _():
            for cp in weight_copies(blk_e_ref[b]):
                cp.wait()
            wgu_bf[...] = wgu_st[...].astype(BF16)
            wd_bf[...] = wd_st[...].astype(BF16)

            @pl.when(next_e_ref[b] >= 0)
            def _():
                for cp in weight_copies(next_e_ref[b]):
                    cp.start()

        lo, hi = [], []
        for c in range(PACK_ROWS):
            bits = lax.bitcast_convert_type(
                tile[slot, c * GATHER_STRIDE:c * GATHER_STRIDE + MOE_M, :], jnp.uint32)
            lo.append(lax.bitcast_convert_type(bits << 16, F32).astype(BF16))
            hi.append(lax.bitcast_convert_type(bits & jnp.uint32(0xFFFF0000), F32).astype(BF16))
        x = jnp.concatenate(lo + hi, axis=1)
        elane = lax.broadcasted_iota(jnp.int32, (MOE_M, LANES), 1)
        w_col = jnp.sum(jnp.where(elane == blk_e_ref[b], wtile[slot], 0.0), axis=1, keepdims=True)
        gather(tok_next_ref, 1 - slot)

        gu = jnp.dot(x, wgu_bf[...], preferred_element_type=F32) + bgu_ref[0, 0]
        gate = jnp.minimum(gu[:, 0:D_EXPERT], SWIGLU_LIMIT)
        up = jnp.clip(gu[:, D_EXPERT:2 * D_EXPERT], -SWIGLU_LIMIT, SWIGLU_LIMIT)
        glu = gate * _sigmoid(SWIGLU_ALPHA * gate)
        mid = ((up + 1.0) * glu).astype(BF16)
        y = jnp.dot(mid, wd_bf[...], preferred_element_type=F32) + bd_ref[0, 0]
        y_ref[...] = (y * w_col).astype(BF16)

    @pl.when(b >= nu_ref[0])
    def _():
        y_ref[...] = jnp.zeros_like(y_ref)


def _moe_experts(blk_e, first, next_e, n_used, row_tok3, xp, route_w, w_gu, b_gu, w_down, b_down, layer):
    bias = lambda b, e, f, ne, nu: (layer, e[b], 0, 0)
    grid_spec = pltpu.PrefetchScalarGridSpec(
        num_scalar_prefetch=4,
        grid=(N_BLOCKS,),
        in_specs=[pl.BlockSpec((1, 1, MOE_M), lambda b, e, f, ne, nu: (b, 0, 0), memory_space=pltpu.SMEM),
                  pl.BlockSpec((1, 1, MOE_M), lambda b, e, f, ne, nu: (jnp.minimum(b + 1, N_BLOCKS - 1), 0, 0),
                               memory_space=pltpu.SMEM),
                  pl.BlockSpec((T * PACK_ROWS, LANES), lambda b, e, f, ne, nu: (0, 0), pipeline_mode=pl.Buffered(1)),
                  pl.BlockSpec((T, 1, LANES), lambda b, e, f, ne, nu: (0, 0, 0), pipeline_mode=pl.Buffered(1)),
                  pl.BlockSpec(memory_space=pl.ANY),
                  pl.BlockSpec((1, 1, 1, 2 * D_EXPERT), bias),
                  pl.BlockSpec(memory_space=pl.ANY),
                  pl.BlockSpec((1, 1, 1, D_MODEL), bias)],
        out_specs=pl.BlockSpec((MOE_M, D_MODEL), lambda b, e, f, ne, nu: (b, 0)),
        scratch_shapes=[pltpu.VMEM((2, PACK_ROWS * GATHER_STRIDE, LANES), jnp.int32),
                        pltpu.VMEM((2, MOE_M, LANES), F32),
                        pltpu.VMEM((D_MODEL, 2 * D_EXPERT), F32),
                        pltpu.VMEM((D_EXPERT, D_MODEL), F32),
                        pltpu.VMEM((D_MODEL, 2 * D_EXPERT), BF16),
                        pltpu.VMEM((D_EXPERT, D_MODEL), BF16),
                        pltpu.SemaphoreType.DMA((2,))],
    )
    return pl.pallas_call(
        functools.partial(_moe_kernel, layer=layer),
        grid_spec=grid_spec,
        out_shape=jax.ShapeDtypeStruct((N_ROWS, D_MODEL), BF16),
        compiler_params=pltpu.CompilerParams(vmem_limit_bytes=48 * MIB),
        name="moe_experts",
    )(blk_e, first, next_e, n_used, row_tok3, row_tok3, xp, route_w.reshape(T, 1, LANES), w_gu,
      b_gu.reshape(DEPTH, N_EXPERTS, 1, 2 * D_EXPERT), w_down, b_down.reshape(DEPTH, N_EXPERTS, 1, D_MODEL))


SCATTER_UNROLL = 8


def _combine_kernel(nu_ref, tok_ref, y_ref, o_ref, acc, tile, sem):
    s = pl.program_id(0)

    @pl.when(s == 0)
    def _():
        acc[...] = jnp.zeros_like(acc)

    def sub_block(sb, carry):
        r0 = pl.multiple_of(sb * SCATTER_M, SCATTER_M)
        y = y_ref[pl.ds(r0, SCATTER_M), :].astype(F32)
        for c in range(ROW_VREGS):
            tile[c * SCATTER_STRIDE:c * SCATTER_STRIDE + SCATTER_M, :] = y[:, c * LANES:(c + 1) * LANES]
        for m0 in range(0, SCATTER_M, SCATTER_UNROLL):
            addrs, vals = [], []
            for u in range(SCATTER_UNROLL):
                mi = m0 + u
                a = pl.multiple_of(tok_ref[0, 0, r0 + mi] * ROW_VREGS, ROW_VREGS)
                yv = tile[pl.ds(mi, ROW_VREGS, stride=SCATTER_STRIDE), :]
                addrs.append(a)
                vals.append(acc[pl.ds(a, ROW_VREGS), :] + yv)
            for u in range(SCATTER_UNROLL):
                acc[pl.ds(addrs[u], ROW_VREGS), :] = vals[u]
        return carry

    @pl.when(s * COMBINE_BLOCKS < nu_ref[0])
    def _():
        lax.fori_loop(0, COMBINE_BLOCKS * MOE_M // SCATTER_M, sub_block, 0)

    @pl.when(s == pl.num_programs(0) - 1)
    def _():
        cp = pltpu.make_async_copy(acc.at[pl.ds(0, T * ROW_VREGS)], o_ref, sem)
        cp.start()
        cp.wait()


def _combine(n_used, row_tok3, yr):
    rows = COMBINE_BLOCKS * MOE_M
    steps = N_BLOCKS // COMBINE_BLOCKS
    grid_spec = pltpu.PrefetchScalarGridSpec(
        num_scalar_prefetch=1,
        grid=(steps,),
        in_specs=[pl.BlockSpec((1, 1, rows), lambda s, nu: (s, 0, 0), memory_space=pltpu.SMEM),
                  pl.BlockSpec((rows, D_MODEL), lambda s, nu: (s, 0))],
        out_specs=pl.BlockSpec(memory_space=pl.ANY),
        scratch_shapes=[pltpu.VMEM(((T + 1) * ROW_VREGS, LANES), F32),
                        pltpu.VMEM((ROW_VREGS * SCATTER_STRIDE, LANES), F32),
                        pltpu.SemaphoreType.DMA(())],
    )
    return pl.pallas_call(
        _combine_kernel,
        grid_spec=grid_spec,
        out_shape=jax.ShapeDtypeStruct((T * ROW_VREGS, LANES), F32),
        compiler_params=pltpu.CompilerParams(vmem_limit_bytes=48 * MIB),
        name="moe_combine",
    )(n_used, row_tok3.reshape(steps, 1, rows), yr)


def _route(top_keys, counts):
    experts = jnp.arange(N_EXPERTS, dtype=jnp.int32)
    padded = (counts + MOE_M - 1) // MOE_M * MOE_M
    pad_end = jnp.cumsum(padded)
    pad_start = pad_end - padded
    n_used = (pad_end[-1] // MOE_M).astype(jnp.int32)
    assert N_ROWS - N_ASSIGN == N_EXPERTS * MOE_M
    used = jnp.arange(MOE_M, dtype=jnp.int32)[None, :] < (padded - counts)[:, None]
    pad_keys = jnp.where(used, 2 * experts[:, None] + 1, 2 * N_EXPERTS + 1) << TOK_BITS | T
    keys = jnp.concatenate([top_keys.reshape(N_ASSIGN), pad_keys.reshape(N_EXPERTS * MOE_M)])
    keys = lax.sort(keys)
    row_tok = keys & ((1 << TOK_BITS) - 1)
    blk0 = jnp.arange(N_BLOCKS, dtype=jnp.int32) * MOE_M
    blk_e = jnp.minimum(jnp.sum((pad_end[:, None] <= blk0[None, :]).astype(jnp.int32), axis=0), N_EXPERTS - 1)
    first = (blk0 == pad_start[blk_e]).astype(jnp.int32)
    later = (experts[None, :] > experts[:, None]) & (counts[None, :] > 0)
    nxt = jnp.min(jnp.where(later, experts[None, :], N_EXPERTS), axis=1)
    next_e = jnp.where(nxt == N_EXPERTS, -1, nxt)[blk_e].astype(jnp.int32)
    return row_tok.reshape(N_BLOCKS, 1, MOE_M), blk_e.astype(jnp.int32), first, next_e, n_used.reshape(1)


def _final_kernel(x_ref, moe_ref, mod_ref, g_ref, yc_o, yl_o):
    x = x_ref[...] + mod_ref[0, 5:6, :] * _tiles_to_rows(moe_ref)
    ms = jnp.mean(x * x, axis=-1, keepdims=True)
    y = x * lax.rsqrt(ms + EPS) * g_ref[...]
    i = pl.program_id(0)

    @pl.when(i < N_CTX_TILES)
    def _():
        yc_o[...] = y

    @pl.when(i >= N_CTX_TILES)
    def _():
        yl_o[...] = y


def _final(x1, moe_tiles, mod_l, g):
    row = lambda i: (i, 0)
    return pl.pallas_call(
        _final_kernel,
        grid=(T // TM,),
        in_specs=[pl.BlockSpec((TM, D_MODEL), row),
                  pl.BlockSpec((TM * ROW_VREGS, LANES), row),
                  pl.BlockSpec((1, 6, D_MODEL), lambda i: (_mod_group(i), 0, 0)),
                  pl.BlockSpec((1, D_MODEL), lambda i: (0, 0))],
        out_specs=[pl.BlockSpec((TM, D_MODEL), _ctx_tile), pl.BlockSpec((TM, D_MODEL), _lat_tile)],
        out_shape=[jax.ShapeDtypeStruct((T_CTX, D_MODEL), F32), jax.ShapeDtypeStruct((T_LAT, D_MODEL), F32)],
        compiler_params=pltpu.CompilerParams(vmem_limit_bytes=32 * MIB),
        name="residual_final",
    )(x1, moe_tiles, mod_l, g)


def _rope_tables():
    t = np.arange(DEC_SEQ)
    pos = np.stack([t // GRID_W, t % GRID_W], axis=1).astype(np.float32)
    half = HEAD_DIM // 2
    inv = np.float32(ROPE_BASE) ** (-np.arange(0, half, 2, dtype=np.float32) / np.float32(half))
    d = np.arange(HEAD_DIM)
    which = d // half
    freq = d % (half // 2)
    sign = np.where((d % half) < half // 2, -1.0, 1.0).astype(np.float32)
    ang = (pos[:, which] * inv[freq][None, :]).astype(np.float32)
    reps = LANES // HEAD_DIM
    cos = np.tile(np.cos(ang).astype(np.float32), (1, reps))
    sin = np.tile(np.sin(ang).astype(np.float32) * sign[None, :], (1, reps))
    return jnp.asarray(cos), jnp.asarray(sin)


def kernel(x_prompt, x_sample, cache_k, cache_v, state_ret, c, c_ctx, norm1_g, norm2_g, w_mod, b_mod, w_in, conv_w, attn_sink, ret_decay, w_a, w_b, w_c, w_o, w_router, b_router, w_gu, b_gu, w_down, b_down, final_g):
    xs = (x_prompt.reshape(T_CTX, D_MODEL), x_sample.reshape(T_LAT, D_MODEL))
    cond8 = jnp.zeros((8, D_MODEL), F32).at[0].set(c_ctx).at[1:1 + DEC_BATCH].set(c)
    mod = _modulation(cond8, w_mod, b_mod)
    cos, sin = _rope_tables()
    ck = cache_k.reshape(DEC_BATCH, DEPTH, PAST_LEN, D_KV)
    cv = cache_v.reshape(DEC_BATCH, DEPTH, PAST_LEN, D_KV)

    ks, vs, rs = [], [], []
    prev = None
    for l in range(DEPTH):
        mod_l = mod[l, 0:1 + DEC_BATCH].reshape(1 + DEC_BATCH, 6, D_MODEL)
        if prev is None:
            conv_in, q, k, v, ret, gates = _inproj(xs, mod_l, norm1_g[l][None, :], w_in, l)
        else:
            x, conv_in, q, k, v, ret, gates = _inproj(None, mod_l, norm1_g[l][None, :], w_in, l, prev=prev)
            xs = (x,)
        rd8 = jnp.broadcast_to(ret_decay[l].reshape(2 * RET_HEADS, 1), (2 * RET_HEADS, LANES))
        *ys_ctx, rfin = _mixers(conv_in, q, k, v, ret, conv_w[l], attn_sink[l], rd8, latent=False)
        ys_lat = _mixers(conv_in, q, k, v, ret, conv_w[l], attn_sink[l], rd8, latent=True,
                         cos=cos, sin=sin, cache_k=ck, cache_v=cv, state=state_ret, layer=l)
        wr = jnp.pad(w_router[l], ((0, 0), (0, LANES - N_EXPERTS)))
        wrh = wr.astype(BF16)
        wrl = (wr - wrh.astype(F32)).astype(BF16)
        br = jnp.pad(b_router[l], (0, LANES - N_EXPERTS))[None, :]
        x1, xp, top_keys, route_w, cnt = _merge(xs, ys_ctx, ys_lat, gates, mod_l, norm2_g[l][None, :], w_a, w_b, w_c,
                                           w_o, wrh, wrl, br, l)
        row_tok, blk_e, first, next_e, n_used = _route(top_keys[0:TOP_K], cnt[0, 0:N_EXPERTS])
        yrows = _moe_experts(blk_e, first, next_e, n_used, row_tok, xp, route_w, w_gu, b_gu, w_down, b_down, l)
        moe = _combine(n_used, row_tok, yrows)
        prev = (x1, moe, mod_l)
        ks.append(k[0:T_CTX].reshape(BATCH, SEQ, N_KV, HEAD_DIM))
        vs.append(v[0:T_CTX].reshape(BATCH, SEQ, N_KV, HEAD_DIM))
        rs.append(rfin)

    yc, yl = _final(*prev, final_g[None, :])
    y_prompt = yc.reshape(BATCH, SEQ, D_MODEL)
    y_sample = yl.reshape(DEC_BATCH, DEC_SEQ, D_MODEL)
    return (y_prompt, y_sample, jnp.stack(ks, axis=1), jnp.stack(vs, axis=1), jnp.stack(rs, axis=1))
```

```python
import functools

import numpy as np
import jax
import jax.numpy as jnp
from jax import lax
from jax.experimental import pallas as pl
from jax.experimental.pallas import tpu as pltpu

F32 = jnp.float32
BF16 = jnp.bfloat16

D_MODEL = 1024
BATCH = 16
SEQ = 256
DEPTH = 2
DEC_BATCH = 2
DEC_SEQ = 2048
PAST_LEN = 256
GRID_W = 64
HEAD_DIM = 64
D_CONV = 256
N_HEADS = 8
N_KV = 2
GROUP = N_HEADS // N_KV
WINDOW = 128
ROPE_BASE = 10000.0
RET_HEADS = 4
RET_DK = 64
RET_DV = 64
CHUNK = 128
N_EXPERTS = 32
TOP_K = 4
D_EXPERT = D_MODEL
SWIGLU_LIMIT = 7.0
SWIGLU_ALPHA = 1.702
EPS = 1e-6
NEG_INF = -1e30

T_CTX = BATCH * SEQ
T_LAT = DEC_BATCH * DEC_SEQ
T = T_CTX + T_LAT
D_ATTN = N_HEADS * HEAD_DIM
D_KV = N_KV * HEAD_DIM
D_RET = RET_HEADS * RET_DK
C_CONV = 0
C_Q = 3 * D_CONV
C_K = C_Q + D_ATTN
C_V = C_K + D_KV
C_RET = C_V + D_KV
C_GATE = C_RET + 4 * D_RET
IN_COLS = C_GATE + 3 * D_MODEL

TM = 512
MOE_M = 256
N_ASSIGN = T * TOP_K
N_BLOCKS = (N_ASSIGN + N_EXPERTS * (MOE_M - 1) + MOE_M - 1) // MOE_M
N_ROWS = N_BLOCKS * MOE_M
LANES = 128
SUBLANES = 8
ROW_VREGS = D_MODEL // LANES
PACK_ROWS = ROW_VREGS // 2
GATHER_STRIDE = MOE_M + SUBLANES
SCATTER_M = 128
SCATTER_STRIDE = SCATTER_M + SUBLANES
COMBINE_BLOCKS = 4
TOK_BITS = 14
assert T < (1 << TOK_BITS)
MIB = 1024 * 1024


def _sigmoid(x):
    return 1.0 / (1.0 + jnp.exp(-x))


def _mod_group(i):
    n_ctx = T_CTX // TM
    per_lat = DEC_SEQ // TM
    g = jnp.zeros_like(i)
    for b in range(DEC_BATCH):
        g = g + (i >= n_ctx + b * per_lat).astype(jnp.int32)
    return g


def _mod_kernel(cond_ref, w_ref, b_ref, o_ref):
    c = cond_ref[...]
    s = c * _sigmoid(c)
    o_ref[0] = jnp.dot(s.astype(BF16), w_ref[0].astype(BF16), preferred_element_type=F32) + b_ref[0]


def _modulation(cond8, w_mod, b_mod):
    n_col = 4
    cw = 6 * D_MODEL // n_col
    return pl.pallas_call(
        _mod_kernel,
        grid=(DEPTH, n_col),
        in_specs=[pl.BlockSpec((8, D_MODEL), lambda l, j: (0, 0)),
                  pl.BlockSpec((1, D_MODEL, cw), lambda l, j: (l, 0, j)),
                  pl.BlockSpec((1, 1, cw), lambda l, j: (l, 0, j))],
        out_specs=pl.BlockSpec((1, 8, cw), lambda l, j: (l, 0, j)),
        out_shape=jax.ShapeDtypeStruct((DEPTH, 8, 6 * D_MODEL), F32),
        compiler_params=pltpu.CompilerParams(vmem_limit_bytes=32 * MIB),
        name="modulation",
    )(cond8, w_mod, b_mod.reshape(DEPTH, 1, 6 * D_MODEL))


N_CTX_TILES = T_CTX // TM


def _ctx_tile(i):
    return (jnp.minimum(i, N_CTX_TILES - 1), 0)


def _lat_tile(i):
    return (jnp.maximum(i - N_CTX_TILES, 0), 0)


def _pick(ctx_ref, lat_ref):
    return jnp.where(pl.program_id(0) < N_CTX_TILES, ctx_ref[...], lat_ref[...])


def _tiles_to_rows(tiles_ref):
    return jnp.concatenate([tiles_ref[pl.ds(c, TM, stride=ROW_VREGS), :] for c in range(ROW_VREGS)], axis=1)


def _inproj_kernel(*refs, first_layer):
    if first_layer:
        xc_ref, xl_ref, mod_ref, g_ref, w_ref, conv_o, q_o, k_o, v_o, ret_o, gate_o = refs
        x = _pick(xc_ref, xl_ref)
    else:
        x1_ref, moe_ref, modp_ref, mod_ref, g_ref, w_ref, x_o, conv_o, q_o, k_o, v_o, ret_o, gate_o = refs
        x = x1_ref[...] + modp_ref[0, 5:6, :] * _tiles_to_rows(moe_ref)
        x_o[...] = x
    ms = jnp.mean(x * x, axis=-1, keepdims=True)
    h = x * lax.rsqrt(ms + EPS) * g_ref[...]
    h = h * (1.0 + mod_ref[0, 1:2, :]) + mod_ref[0, 0:1, :]
    hb = h.astype(BF16)

    def proj(c0, c1):
        return jnp.dot(hb, w_ref[0, :, c0:c1].astype(BF16), preferred_element_type=F32)

    a = proj(C_CONV, C_RET)
    conv_o[...] = a[:, C_CONV:C_Q].astype(BF16)
    q_o[...] = (a[:, C_Q:C_K] * HEAD_DIM ** -0.5).astype(BF16)
    k_o[...] = a[:, C_K:C_V]
    v_o[...] = a[:, C_V:C_RET]
    r = proj(C_RET, C_GATE)
    ret_o[:, 0:D_RET] = r[:, 0:D_RET].astype(BF16)
    ret_o[:, D_RET:2 * D_RET] = (r[:, D_RET:2 * D_RET] * RET_DK ** -0.5).astype(BF16)
    ret_o[:, 2 * D_RET:4 * D_RET] = r[:, 2 * D_RET:4 * D_RET].astype(BF16)
    gate_half = 3 * D_MODEL // 2
    for b in range(2):
        g = proj(C_GATE + b * gate_half, C_GATE + (b + 1) * gate_half)
        gate_o[:, b * gate_half:(b + 1) * gate_half] = _sigmoid(g).astype(BF16)


def _x_specs(xs):
    if len(xs) == 2:
        return [pl.BlockSpec((TM, D_MODEL), _ctx_tile), pl.BlockSpec((TM, D_MODEL), _lat_tile)]
    return [pl.BlockSpec((TM, D_MODEL), lambda i: (i, 0))]


def _inproj(xs, mod_l, g1, w_in, layer, prev=None):
    row = lambda i: (i, 0)
    mod_spec = pl.BlockSpec((1, 6, D_MODEL), lambda i: (_mod_group(i), 0, 0))
    out_specs = [pl.BlockSpec((TM, 3 * D_CONV), row),
                 pl.BlockSpec((TM, D_ATTN), row),
                 pl.BlockSpec((TM, D_KV), row),
                 pl.BlockSpec((TM, D_KV), row),
                 pl.BlockSpec((TM, 4 * D_RET), row),
                 pl.BlockSpec((TM, 3 * D_MODEL), row)]
    out_shape = [jax.ShapeDtypeStruct((T, 3 * D_CONV), BF16),
                 jax.ShapeDtypeStruct((T, D_ATTN), BF16),
                 jax.ShapeDtypeStruct((T, D_KV), F32),
                 jax.ShapeDtypeStruct((T, D_KV), F32),
                 jax.ShapeDtypeStruct((T, 4 * D_RET), BF16),
                 jax.ShapeDtypeStruct((T, 3 * D_MODEL), BF16)]
    if prev is None:
        in_specs, args = _x_specs(xs), list(xs)
    else:
        in_specs = [pl.BlockSpec((TM, D_MODEL), row), pl.BlockSpec((TM * ROW_VREGS, LANES), row), mod_spec]
        args = list(prev)
        out_specs = [pl.BlockSpec((TM, D_MODEL), row)] + out_specs
        out_shape = [jax.ShapeDtypeStruct((T, D_MODEL), F32)] + out_shape
    return pl.pallas_call(
        functools.partial(_inproj_kernel, first_layer=prev is None),
        grid=(T // TM,),
        in_specs=in_specs + [
            mod_spec,
            pl.BlockSpec((1, D_MODEL), lambda i: (0, 0)),
            pl.BlockSpec((1, D_MODEL, IN_COLS), lambda i: (layer, 0, 0), pipeline_mode=pl.Buffered(1))],
        out_specs=out_specs,
        out_shape=out_shape,
        compiler_params=pltpu.CompilerParams(vmem_limit_bytes=60 * MIB),
        name="inproj",
    )(*args, mod_l, g1, w_in)


def _rope(x, cos, sin_signed):
    lane = lax.broadcasted_iota(jnp.int32, x.shape, 1)
    first = (lane % 32) < 16
    partner = jnp.where(first, pltpu.roll(x, x.shape[1] - 16, 1), pltpu.roll(x, 16, 1))
    return x * cos + partner * sin_signed


HALF = LANES // 2
assert HEAD_DIM == HALF and RET_DK == HALF and RET_DV == HALF
N_PAIRS = N_HEADS // 2
RET_PAIRS = RET_HEADS // 2


def _row_variants(kt):
    row = lax.broadcasted_iota(jnp.int32, kt.shape, 0)
    lo0 = jnp.where(row < HALF, kt, 0.0)
    hi1 = jnp.where(row >= HALF, kt, 0.0)
    return lo0, pltpu.roll(lo0, HALF, 0), pltpu.roll(hi1, HALF, 0), hi1


def _lane_variants(v):
    lane = lax.broadcasted_iota(jnp.int32, v.shape, 1)
    lo0 = jnp.where(lane < HALF, v, 0.0)
    hi1 = jnp.where(lane >= HALF, v, 0.0)
    return lo0, pltpu.roll(lo0, HALF, 1), pltpu.roll(hi1, HALF, 1), hi1


def _mixer_kernel(*refs, n, latent):
    if latent:
        (sink_ref, conv_ref, q_ref, k_ref, v_ref, ret_ref, cw_ref, rd_ref, cos_ref, sin_ref, ck_ref, cv_ref,
         r0_ref, yc_ref, ya_ref, yr_ref, ktq, vq, rkp, o_f, o_b, rst, dmat, qdec, kdec, cdec) = refs
    else:
        (sink_ref, conv_ref, q_ref, k_ref, v_ref, ret_ref, cw_ref, rd_ref,
         yc_ref, ya_ref, yr_ref, rfin_ref, ktq, vq, rkp, o_f, o_b, rst, dmat, qdec, kdec, cdec) = refs
    nb = n // CHUNK
    pad = CHUNK if latent else 0

    cv = conv_ref[...].astype(F32)
    cb, cc, cu = cv[:, 0:D_CONV], cv[:, D_CONV:2 * D_CONV], cv[:, 2 * D_CONV:3 * D_CONV]
    p = cc * cu
    row = lax.broadcasted_iota(jnp.int32, p.shape, 0)
    prev = jnp.where(row == 0, 0.0, pltpu.roll(p, 1, 0))
    nxt = jnp.where(row == n - 1, 0.0, pltpu.roll(p, n - 1, 0))
    cw = cw_ref[...]
    yc_ref[...] = (cb * (prev * cw[0:1, :] + p * cw[1:2, :] + nxt * cw[2:3, :])).astype(BF16)

    kf = k_ref[...]
    if latent:
        kf = _rope(kf, cos_ref[...], sin_ref[...])
    for idx, (kk, vv) in enumerate(zip(_row_variants(kf.T), _lane_variants(v_ref[...]))):
        if latent:
            ktq[idx, :, 0:pad] = jnp.zeros((LANES, pad), BF16)
            ktq[idx, :, pad + n:2 * pad + n] = jnp.zeros((LANES, pad), BF16)
            vq[idx, 0:pad, :] = jnp.zeros((pad, LANES), BF16)
            vq[idx, pad + n:2 * pad + n, :] = jnp.zeros((pad, LANES), BF16)
        ktq[idx, :, pad:pad + n] = kk.astype(BF16)
        vq[idx, pad:pad + n, :] = vv.astype(BF16)
    if latent:
        cktq = [t.astype(BF16) for t in _row_variants(ck_ref[0, 0].T)]
        cvq = [t.astype(BF16) for t in _lane_variants(cv_ref[0, 0])]

    def attn(r0, rows):
        qj = q_ref[pl.ds(r0, rows), :]
        if latent:
            cosj = cos_ref[pl.ds(r0, rows), :]
            sinj = sin_ref[pl.ds(r0, rows), :]
            qpos = r0 + lax.broadcasted_iota(jnp.int32, (rows, 3 * CHUNK), 0)
            kpos = r0 - CHUNK + lax.broadcasted_iota(jnp.int32, (rows, 3 * CHUNK), 1)
            ok = (jnp.abs(qpos - kpos) <= WINDOW) & (kpos >= 0) & (kpos < n)
        for m in range(N_PAIRS):
            q2 = qj[:, m * LANES:(m + 1) * LANES]
            if latent:
                q2 = _rope(q2.astype(F32), cosj, sinj).astype(BF16)
            g = (2 * m) // GROUP
            acc = None
            for half in range(2):
                idx = 2 * g + half
                sk = sink_ref[2 * m + half]
                if latent:
                    s = jnp.dot(q2, ktq[idx, :, pl.ds(r0, 3 * CHUNK)], preferred_element_type=F32)
                    s = jnp.where(ok, s, NEG_INF)
                    s2 = jnp.dot(q2, cktq[idx], preferred_element_type=F32)
                    mx = jnp.maximum(jnp.maximum(jnp.max(s, axis=-1, keepdims=True),
                                                 jnp.max(s2, axis=-1, keepdims=True)), sk)
                    pw = jnp.exp(s - mx)
                    p2 = jnp.exp(s2 - mx)
                    den = (jnp.sum(pw, axis=-1, keepdims=True) + jnp.sum(p2, axis=-1, keepdims=True)
                           + jnp.exp(sk - mx))
                    o = (jnp.dot(pw.astype(BF16), vq[idx, pl.ds(r0, 3 * CHUNK), :], preferred_element_type=F32)
                         + jnp.dot(p2.astype(BF16), cvq[idx], preferred_element_type=F32))
                else:
                    s = jnp.dot(q2, ktq[idx], preferred_element_type=F32)
                    mx = jnp.maximum(jnp.max(s, axis=-1, keepdims=True), sk)
                    pw = jnp.exp(s - mx)
                    den = jnp.sum(pw, axis=-1, keepdims=True) + jnp.exp(sk - mx)
                    o = jnp.dot(pw.astype(BF16), vq[idx], preferred_element_type=F32)
                o = o / den
                acc = o if acc is None else acc + o
            ya_ref[pl.ds(r0, rows), m * LANES:(m + 1) * LANES] = acc.astype(BF16)

    rk_t = ret_ref[:, D_RET:2 * D_RET].astype(F32).T
    for m in range(RET_PAIRS):
        rkp[m] = rk_t[m * LANES:(m + 1) * LANES, :].astype(BF16)
    rd = rd_ref[...]
    log_g = jnp.minimum(rd, 0.0) - jnp.log(1.0 + jnp.exp(-jnp.abs(rd)))
    row_c = lax.broadcasted_iota(jnp.int32, (CHUNK, CHUNK), 0)
    lane_c = lax.broadcasted_iota(jnp.int32, (CHUNK, CHUNK), 1)
    ii = row_c.astype(F32)
    jj = lane_c.astype(F32)
    even_row = row_c < HALF
    even_lane = lane_c < HALF
    blockdiag = even_row == even_lane
    for d in range(2):
        for h in range(RET_HEADS):
            r = d * RET_HEADS + h
            diff = (ii - jj) if d == 0 else (jj - ii)
            dmat[r] = jnp.where(diff >= 0, jnp.exp(jnp.maximum(diff, 0.0) * log_g[r:r + 1, :]), 0.0)
        for m in range(RET_PAIRS):
            s = d * RET_PAIRS + m
            lg_e = log_g[d * RET_HEADS + 2 * m:d * RET_HEADS + 2 * m + 1, :]
            lg_o = log_g[d * RET_HEADS + 2 * m + 1:d * RET_HEADS + 2 * m + 2, :]
            qpow = (ii + 1.0) if d == 0 else (CHUNK - ii)
            kpow = (CHUNK - 1.0 - jj) if d == 0 else jj
            qdec[s] = jnp.where(even_lane, jnp.exp(qpow * lg_e), jnp.exp(qpow * lg_o))
            kdec[s] = jnp.where(even_row, jnp.exp(kpow * lg_e), jnp.exp(kpow * lg_o))
            chunk_decay = jnp.where(even_row, jnp.exp(float(CHUNK) * lg_e), jnp.exp(float(CHUNK) * lg_o))
            cdec[s] = jnp.where(blockdiag, chunk_decay, 0.0)
            if latent:
                z = jnp.zeros((HALF, HALF), F32)
                rst[s] = jnp.concatenate(
                    [jnp.concatenate([r0_ref[0, 0, d, 2 * m], z], axis=1),
                     jnp.concatenate([z, r0_ref[0, 0, d, 2 * m + 1]], axis=1)], axis=0)
            else:
                rst[s] = jnp.zeros((LANES, LANES), F32)

    def ret_chunk(c0, d, out_ref):
        for m in range(RET_PAIRS):
            s = d * RET_PAIRS + m
            q2 = ret_ref[pl.ds(c0, CHUNK), m * LANES:(m + 1) * LANES]
            v2 = ret_ref[pl.ds(c0, CHUNK), 2 * D_RET + m * LANES:2 * D_RET + (m + 1) * LANES]
            kt2 = rkp[m, :, pl.ds(c0, CHUNK)].astype(F32)
            v2f = v2.astype(F32)
            state = rst[s]
            o2 = jnp.dot(q2, state.astype(BF16), preferred_element_type=F32) * qdec[s]
            for half in range(2):
                r = d * RET_HEADS + 2 * m + half
                keep_row = even_row if half == 0 else jnp.logical_not(even_row)
                keep_lane = even_lane if half == 0 else jnp.logical_not(even_lane)
                a = jnp.dot(q2, jnp.where(keep_row, kt2, 0.0).astype(BF16), preferred_element_type=F32)
                inner = (a * dmat[r]).astype(BF16)
                o2 = o2 + jnp.dot(inner, jnp.where(keep_lane, v2f, 0.0).astype(BF16), preferred_element_type=F32)
            kd = (kt2 * kdec[s]).astype(BF16)
            upd = jnp.dot(kd, v2, preferred_element_type=F32)
            rst[s] = state * cdec[s] + jnp.where(blockdiag, upd, 0.0)
            out_ref[pl.ds(c0, CHUNK), m * LANES:(m + 1) * LANES] = o2

    gi = lax.broadcasted_iota(jnp.int32, (D_RET, D_RET), 0) // RET_DV
    gj = lax.broadcasted_iota(jnp.int32, (D_RET, D_RET), 1) // RET_DV
    group_mean = jnp.where(gi == gj, 1.0 / RET_DV, 0.0).astype(BF16)

    def norm(c0, rows):
        o = o_f[pl.ds(c0, rows), :] + o_b[pl.ds(c0, rows), :]
        sq = o * o
        hi = sq.astype(BF16)
        lo = (sq - hi.astype(F32)).astype(BF16)
        ms = (jnp.dot(hi, group_mean, preferred_element_type=F32)
              + jnp.dot(lo, group_mean, preferred_element_type=F32))
        rg = ret_ref[pl.ds(c0, rows), 3 * D_RET:4 * D_RET].astype(F32)
        yr_ref[pl.ds(c0, rows), :] = (rg * _sigmoid(rg) * (o * lax.rsqrt(ms + EPS))).astype(BF16)

    if latent:
        def scan_body(j, carry):
            attn(pl.multiple_of(j * CHUNK, CHUNK), CHUNK)
            ret_chunk(pl.multiple_of(j * CHUNK, CHUNK), 0, o_f)
            ret_chunk(pl.multiple_of((nb - 1 - j) * CHUNK, CHUNK), 1, o_b)
            return carry

        def norm_body(j, carry):
            norm(pl.multiple_of(j * CHUNK, CHUNK), CHUNK)
            return carry

        lax.fori_loop(0, nb, scan_body, 0)
        lax.fori_loop(0, nb, norm_body, 0)
    else:
        attn(0, n)
        for j in range(nb):
            ret_chunk(j * CHUNK, 0, o_f)
            ret_chunk((nb - 1 - j) * CHUNK, 1, o_b)
        norm(0, n)
        for d in range(2):
            for h in range(RET_HEADS):
                lo_ = (h % 2) * HALF
                rfin_ref[0, d, h] = rst[d * RET_PAIRS + h // 2][lo_:lo_ + HALF, lo_:lo_ + HALF]


def _mixers(conv_in, q, k, v, ret, conv_w_l, sink_l, rd8, *, latent, cos=None, sin=None, cache_k=None,
            cache_v=None, state=None, layer=0):
    n = DEC_SEQ if latent else SEQ
    nseq = DEC_BATCH if latent else BATCH
    off = T_CTX // n if latent else 0
    seq = lambda s: (s + off, 0)
    const = lambda s: (0, 0)
    in_specs = [pl.BlockSpec(memory_space=pltpu.SMEM),
                pl.BlockSpec((n, 3 * D_CONV), seq),
                pl.BlockSpec((n, D_ATTN), seq),
                pl.BlockSpec((n, D_KV), seq),
                pl.BlockSpec((n, D_KV), seq),
                pl.BlockSpec((n, 4 * D_RET), seq),
                pl.BlockSpec((3, D_CONV), const),
                pl.BlockSpec((8, LANES), const)]
    args = [sink_l, conv_in, q, k, v, ret, conv_w_l, rd8]
    out_specs = [pl.BlockSpec((n, D_CONV), lambda s: (s, 0)),
                 pl.BlockSpec((n, D_ATTN), lambda s: (s, 0)),
                 pl.BlockSpec((n, D_RET), lambda s: (s, 0))]
    out_shape = [jax.ShapeDtypeStruct((nseq * n, D_CONV), BF16),
                 jax.ShapeDtypeStruct((nseq * n, D_ATTN), BF16),
                 jax.ShapeDtypeStruct((nseq * n, D_RET), BF16)]
    if latent:
        in_specs += [pl.BlockSpec((n, LANES), const),
                     pl.BlockSpec((n, LANES), const),
                     pl.BlockSpec((1, 1, PAST_LEN, D_KV), lambda s: (s, layer, 0, 0)),
                     pl.BlockSpec((1, 1, PAST_LEN, D_KV), lambda s: (s, layer, 0, 0)),
                     pl.BlockSpec((1, 1, 2, RET_HEADS, RET_DK, RET_DV), lambda s: (s, layer, 0, 0, 0, 0))]
        args += [cos, sin, cache_k, cache_v, state]
        kv_rows = n + 2 * CHUNK
    else:
        out_specs.append(pl.BlockSpec((1, 2, RET_HEADS, RET_DK, RET_DV), lambda s: (s, 0, 0, 0, 0)))
        out_shape.append(jax.ShapeDtypeStruct((nseq, 2, RET_HEADS, RET_DK, RET_DV), F32))
        kv_rows = n
    scratch = [pltpu.VMEM((2 * N_KV, LANES, kv_rows), BF16),
               pltpu.VMEM((2 * N_KV, kv_rows, LANES), BF16),
               pltpu.VMEM((RET_PAIRS, LANES, n), BF16),
               pltpu.VMEM((n, D_RET), F32),
               pltpu.VMEM((n, D_RET), F32),
               pltpu.VMEM((2 * RET_PAIRS, LANES, LANES), F32),
               pltpu.VMEM((2 * RET_HEADS, CHUNK, CHUNK), F32),
               pltpu.VMEM((2 * RET_PAIRS, CHUNK, CHUNK), F32),
               pltpu.VMEM((2 * RET_PAIRS, CHUNK, CHUNK), F32),
               pltpu.VMEM((2 * RET_PAIRS, CHUNK, CHUNK), F32)]
    return pl.pallas_call(
        functools.partial(_mixer_kernel, n=n, latent=latent),
        grid=(nseq,),
        in_specs=in_specs,
        out_specs=out_specs,
        out_shape=out_shape,
        scratch_shapes=scratch,
        compiler_params=pltpu.CompilerParams(vmem_limit_bytes=56 * MIB),
        name="mixers_latent" if latent else "mixers_context",
    )(*args)


def _merge_kernel(*refs, split):
    if split:
        xc_ref, xl_ref = refs[0:2]
        x = _pick(xc_ref, xl_ref)
        refs = refs[2:]
    else:
        x = refs[0][...]
        refs = refs[1:]
    (ycc_ref, ycl_ref, yac_ref, yal_ref, yrc_ref, yrl_ref, gate_ref, mod_ref, g2_ref, wa_ref, wb_ref, wc_ref,
     wo_ref, wrh_ref, wrl_ref, br_ref, x1_o, xp_o, key_o, tw_o, cnt_o) = refs
    merged = (gate_ref[:, 0:D_MODEL].astype(F32)
              * jnp.dot(_pick(ycc_ref, ycl_ref), wa_ref[0].astype(BF16), preferred_element_type=F32)
              + gate_ref[:, D_MODEL:2 * D_MODEL].astype(F32)
              * jnp.dot(_pick(yac_ref, yal_ref), wb_ref[0].astype(BF16), preferred_element_type=F32)
              + gate_ref[:, 2 * D_MODEL:3 * D_MODEL].astype(F32)
              * jnp.dot(_pick(yrc_ref, yrl_ref), wc_ref[0].astype(BF16), preferred_element_type=F32))
    x1 = x + mod_ref[0, 2:3, :] * jnp.dot(merged.astype(BF16), wo_ref[0].astype(BF16),
                                          preferred_element_type=F32)
    x1_o[...] = x1
    ms = jnp.mean(x1 * x1, axis=-1, keepdims=True)
    h2 = x1 * lax.rsqrt(ms + EPS) * g2_ref[...]
    h2 = h2 * (1.0 + mod_ref[0, 4:5, :]) + mod_ref[0, 3:4, :]
    hh = h2.astype(BF16)
    hf = hh.astype(F32)
    bits = lax.bitcast_convert_type(hf, jnp.uint32)
    for c in range(PACK_ROWS):
        lo = bits[:, c * LANES:(c + 1) * LANES] >> 16
        hi = bits[:, (c + PACK_ROWS) * LANES:(c + PACK_ROWS + 1) * LANES] & jnp.uint32(0xFFFF0000)
        xp_o[pl.ds(c, TM, stride=PACK_ROWS), :] = lax.bitcast_convert_type(lo | hi, jnp.int32)
    hl = (h2 - hf).astype(BF16)
    logits = (jnp.dot(hh, wrh_ref[...], preferred_element_type=F32)
              + jnp.dot(hl, wrh_ref[...], preferred_element_type=F32)
              + jnp.dot(hh, wrl_ref[...], preferred_element_type=F32)
              + br_ref[...])
    lane = lax.broadcasted_iota(jnp.int32, logits.shape, 1)
    work = jnp.where(lane < N_EXPERTS, logits, -jnp.inf)
    vals, idxs = [], []
    for _ in range(TOP_K):
        m = jnp.max(work, axis=-1, keepdims=True)
        am = jnp.min(jnp.where(work == m, lane, LANES), axis=-1, keepdims=True)
        vals.append(m)
        idxs.append(am)
        work = jnp.where(lane == am, -jnp.inf, work)
    es = [jnp.exp(v - vals[0]) for v in vals]
    den = es[0] + es[1] + es[2] + es[3]
    tok = pl.program_id(0) * TM + lax.broadcasted_iota(jnp.int32, (TM, 1), 0)
    keys = jnp.zeros(logits.shape, jnp.int32)
    tw = jnp.zeros(logits.shape, F32)
    for k in range(TOP_K):
        keys = jnp.where(lane == k, (2 * idxs[k]) << TOK_BITS | tok, keys)
        tw = jnp.where(lane == k, es[k] / den, tw)
    key_o[...] = keys.astype(F32).T[0:SUBLANES, :].astype(jnp.int32)
    tw_o[...] = tw.T[0:SUBLANES, :]
    sel = (lane == idxs[0]) | (lane == idxs[1]) | (lane == idxs[2]) | (lane == idxs[3])
    part = jnp.sum(sel.astype(jnp.int32), axis=0, keepdims=True)

    @pl.when(pl.program_id(0) == 0)
    def _():
        cnt_o[...] = jnp.zeros_like(cnt_o)

    cnt_o[...] += jnp.broadcast_to(part, cnt_o.shape)


def _merge(xs, ys_ctx, ys_lat, gates, mod_l, g2, wa, wb, wc, wo, wrh, wrl, br, layer):
    row = lambda i: (i, 0)
    const = lambda i: (0, 0)
    wl = lambda i: (layer, 0, 0)
    y_specs, y_args = [], []
    for width, yc, yl in zip((D_CONV, D_ATTN, D_RET), ys_ctx, ys_lat):
        y_specs += [pl.BlockSpec((TM, width), _ctx_tile), pl.BlockSpec((TM, width), _lat_tile)]
        y_args += [yc, yl]
    return pl.pallas_call(
        functools.partial(_merge_kernel, split=len(xs) == 2),
        grid=(T // TM,),
        in_specs=_x_specs(xs) + y_specs + [
                  pl.BlockSpec((TM, 3 * D_MODEL), row),
                  pl.BlockSpec((1, 6, D_MODEL), lambda i: (_mod_group(i), 0, 0)),
                  pl.BlockSpec((1, D_MODEL), const),
                  pl.BlockSpec((1, D_CONV, D_MODEL), wl),
                  pl.BlockSpec((1, D_ATTN, D_MODEL), wl),
                  pl.BlockSpec((1, D_RET, D_MODEL), wl),
                  pl.BlockSpec((1, D_MODEL, D_MODEL), wl),
                  pl.BlockSpec((D_MODEL, LANES), const),
                  pl.BlockSpec((D_MODEL, LANES), const),
                  pl.BlockSpec((1, LANES), const)],
        out_specs=[pl.BlockSpec((TM, D_MODEL), row),
                   pl.BlockSpec((TM * PACK_ROWS, LANES), row),
                   pl.BlockSpec((SUBLANES, TM), lambda i: (0, i)),
                   pl.BlockSpec((SUBLANES, TM), lambda i: (0, i)),
                   pl.BlockSpec((SUBLANES, LANES), const)],
        out_shape=[jax.ShapeDtypeStruct((T, D_MODEL), F32),
                   jax.ShapeDtypeStruct((T * PACK_ROWS, LANES), jnp.int32),
                   jax.ShapeDtypeStruct((SUBLANES, T), jnp.int32),
                   jax.ShapeDtypeStruct((SUBLANES, T), F32),
                   jax.ShapeDtypeStruct((SUBLANES, LANES), jnp.int32)],
        compiler_params=pltpu.CompilerParams(vmem_limit_bytes=48 * MIB),
        name="merge_router",
    )(*xs, *y_args, gates, mod_l, g2, wa, wb, wc, wo, wrh, wrl, br)


def _moe_kernel(blk_e_ref, first_ref, next_e_ref, nu_ref, tok_ref, tok_next_ref, xp_ref, wgu_hbm, bgu_ref, wd_hbm,
                bd_ref, y_ref, tile, wgu_st, wd_st, wgu_bf, wd_bf, sems, *, layer):
    b = pl.program_id(0)
    slot = b % 2

    def weight_copies(e):
        return (pltpu.make_async_copy(wgu_hbm.at[layer, e], wgu_st, sems.at[0]),
                pltpu.make_async_copy(wd_hbm.at[layer, e], wd_st, sems.at[1]))

    def gather(tok, dst_slot):
        for mi in range(MOE_M):
            t = jnp.minimum(tok[0, 0, mi], T - 1)
            slab = xp_ref[pl.ds(pl.multiple_of(t * PACK_ROWS, PACK_ROWS), PACK_ROWS), :]
            tile[dst_slot, pl.ds(mi, PACK_ROWS, stride=GATHER_STRIDE), :] = slab

    @pl.when(b == 0)
    def _():
        for cp in weight_copies(blk_e_ref[0]):
            cp.start()
        gather(tok_ref, 0)

    @pl.when(b < nu_ref[0])
    def _():
        @pl.when(first_ref[b] == 1)
        def _():
            for cp in weight_copies(blk_e_ref[b]):
                cp.wait()
            wgu_bf[...] = wgu_st[...].astype(BF16)
            wd_bf[...] = wd_st[...].astype(BF16)

            @pl.when(next_e_ref[b] >= 0)
            def _():
                for cp in weight_copies(next_e_ref[b]):
                    cp.start()

        lo, hi = [], []
        for c in range(PACK_ROWS):
            bits = lax.bitcast_convert_type(
                tile[slot, c * GATHER_STRIDE:c * GATHER_STRIDE + MOE_M, :], jnp.uint32)
            lo.append(lax.bitcast_convert_type(bits << 16, F32).astype(BF16))
            hi.append(lax.bitcast_convert_type(bits & jnp.uint32(0xFFFF0000), F32).astype(BF16))
        x = jnp.concatenate(lo + hi, axis=1)
        gather(tok_next_ref, 1 - slot)

        gu = jnp.dot(x, wgu_bf[...], preferred_element_type=F32) + bgu_ref[0, 0]
        gate = jnp.minimum(gu[:, 0:D_EXPERT], SWIGLU_LIMIT)
        up = jnp.clip(gu[:, D_EXPERT:2 * D_EXPERT], -SWIGLU_LIMIT, SWIGLU_LIMIT)
        glu = gate * _sigmoid(SWIGLU_ALPHA * gate)
        mid = ((up + 1.0) * glu).astype(BF16)
        y = jnp.dot(mid, wd_bf[...], preferred_element_type=F32) + bd_ref[0, 0]
        y_ref[...] = y.astype(BF16)

    @pl.when(b >= nu_ref[0])
    def _():
        y_ref[...] = jnp.zeros_like(y_ref)


def _moe_experts(blk_e, first, next_e, n_used, row_tok3, xp, w_gu, b_gu, w_down, b_down, layer):
    bias = lambda b, e, f, ne, nu: (layer, e[b], 0, 0)
    grid_spec = pltpu.PrefetchScalarGridSpec(
        num_scalar_prefetch=4,
        grid=(N_BLOCKS,),
        in_specs=[pl.BlockSpec((1, 1, MOE_M), lambda b, e, f, ne, nu: (b, 0, 0), memory_space=pltpu.SMEM),
                  pl.BlockSpec((1, 1, MOE_M), lambda b, e, f, ne, nu: (jnp.minimum(b + 1, N_BLOCKS - 1), 0, 0),
                               memory_space=pltpu.SMEM),
                  pl.BlockSpec((T * PACK_ROWS, LANES), lambda b, e, f, ne, nu: (0, 0), pipeline_mode=pl.Buffered(1)),
                  pl.BlockSpec(memory_space=pl.ANY),
                  pl.BlockSpec((1, 1, 1, 2 * D_EXPERT), bias),
                  pl.BlockSpec(memory_space=pl.ANY),
                  pl.BlockSpec((1, 1, 1, D_MODEL), bias)],
        out_specs=pl.BlockSpec((MOE_M, D_MODEL), lambda b, e, f, ne, nu: (b, 0)),
        scratch_shapes=[pltpu.VMEM((2, PACK_ROWS * GATHER_STRIDE, LANES), jnp.int32),
                        pltpu.VMEM((D_MODEL, 2 * D_EXPERT), F32),
                        pltpu.VMEM((D_EXPERT, D_MODEL), F32),
                        pltpu.VMEM((D_MODEL, 2 * D_EXPERT), BF16),
                        pltpu.VMEM((D_EXPERT, D_MODEL), BF16),
                        pltpu.SemaphoreType.DMA((2,))],
    )
    return pl.pallas_call(
        functools.partial(_moe_kernel, layer=layer),
        grid_spec=grid_spec,
        out_shape=jax.ShapeDtypeStruct((N_ROWS, D_MODEL), BF16),
        compiler_params=pltpu.CompilerParams(vmem_limit_bytes=48 * MIB),
        name="moe_experts",
    )(blk_e, first, next_e, n_used, row_tok3, row_tok3, xp, w_gu, b_gu.reshape(DEPTH, N_EXPERTS, 1, 2 * D_EXPERT),
      w_down, b_down.reshape(DEPTH, N_EXPERTS, 1, D_MODEL))


SCATTER_UNROLL = 8


def _combine_kernel(nu_ref, tok_ref, w_ref, y_ref, o_ref, acc, tile, sem):
    s = pl.program_id(0)

    @pl.when(s == 0)
    def _():
        acc[...] = jnp.zeros_like(acc)

    def sub_block(sb, carry):
        r0 = pl.multiple_of(sb * SCATTER_M, SCATTER_M)
        y = y_ref[pl.ds(r0, SCATTER_M), :].astype(F32)
        for c in range(ROW_VREGS):
            tile[c * SCATTER_STRIDE:c * SCATTER_STRIDE + SCATTER_M, :] = y[:, c * LANES:(c + 1) * LANES]
        for m0 in range(0, SCATTER_M, SCATTER_UNROLL):
            addrs, vals = [], []
            for u in range(SCATTER_UNROLL):
                mi = m0 + u
                a = pl.multiple_of(tok_ref[0, 0, r0 + mi] * ROW_VREGS, ROW_VREGS)
                yv = tile[pl.ds(mi, ROW_VREGS, stride=SCATTER_STRIDE), :]
                addrs.append(a)
                vals.append(acc[pl.ds(a, ROW_VREGS), :] + w_ref[0, 0, r0 + mi] * yv)
            for u in range(SCATTER_UNROLL):
                acc[pl.ds(addrs[u], ROW_VREGS), :] = vals[u]
        return carry

    @pl.when(s * COMBINE_BLOCKS < nu_ref[0])
    def _():
        lax.fori_loop(0, COMBINE_BLOCKS * MOE_M // SCATTER_M, sub_block, 0)

    @pl.when(s == pl.num_programs(0) - 1)
    def _():
        cp = pltpu.make_async_copy(acc.at[pl.ds(0, T * ROW_VREGS)], o_ref, sem)
        cp.start()
        cp.wait()


def _combine(n_used, row_tok3, row_w3, yr):
    rows = COMBINE_BLOCKS * MOE_M
    steps = N_BLOCKS // COMBINE_BLOCKS
    grid_spec = pltpu.PrefetchScalarGridSpec(
        num_scalar_prefetch=1,
        grid=(steps,),
        in_specs=[pl.BlockSpec((1, 1, rows), lambda s, nu: (s, 0, 0), memory_space=pltpu.SMEM),
                  pl.BlockSpec((1, 1, rows), lambda s, nu: (s, 0, 0), memory_space=pltpu.SMEM),
                  pl.BlockSpec((rows, D_MODEL), lambda s, nu: (s, 0))],
        out_specs=pl.BlockSpec(memory_space=pl.ANY),
        scratch_shapes=[pltpu.VMEM(((T + 1) * ROW_VREGS, LANES), F32),
                        pltpu.VMEM((ROW_VREGS * SCATTER_STRIDE, LANES), F32),
                        pltpu.SemaphoreType.DMA(())],
    )
    return pl.pallas_call(
        _combine_kernel,
        grid_spec=grid_spec,
        out_shape=jax.ShapeDtypeStruct((T * ROW_VREGS, LANES), F32),
        compiler_params=pltpu.CompilerParams(vmem_limit_bytes=48 * MIB),
        name="moe_combine",
    )(n_used, row_tok3.reshape(steps, 1, rows), row_w3.reshape(steps, 1, rows), yr)


def _route(top_keys, top_w, counts):
    experts = jnp.arange(N_EXPERTS, dtype=jnp.int32)
    padded = (counts + MOE_M - 1) // MOE_M * MOE_M
    pad_end = jnp.cumsum(padded)
    pad_start = pad_end - padded
    n_used = (pad_end[-1] // MOE_M).astype(jnp.int32)
    assert N_ROWS - N_ASSIGN == N_EXPERTS * MOE_M
    used = jnp.arange(MOE_M, dtype=jnp.int32)[None, :] < (padded - counts)[:, None]
    pad_keys = jnp.where(used, 2 * experts[:, None] + 1, 2 * N_EXPERTS + 1) << TOK_BITS | T
    keys = jnp.concatenate([top_keys.reshape(N_ASSIGN), pad_keys.reshape(N_EXPERTS * MOE_M)])
    wts = jnp.concatenate([top_w.reshape(N_ASSIGN), jnp.zeros((N_EXPERTS * MOE_M,), F32)])
    keys, row_w = lax.sort((keys, wts), num_keys=1)
    row_tok = keys & ((1 << TOK_BITS) - 1)
    blk0 = jnp.arange(N_BLOCKS, dtype=jnp.int32) * MOE_M
    blk_e = jnp.minimum(jnp.sum((pad_end[:, None] <= blk0[None, :]).astype(jnp.int32), axis=0), N_EXPERTS - 1)
    mine = experts[:, None] == blk_e[None, :]
    first = (blk0 == jnp.sum(jnp.where(mine, pad_start[:, None], 0), axis=0)).astype(jnp.int32)
    later = (experts[None, :] > experts[:, None]) & (counts[None, :] > 0)
    nxt = jnp.min(jnp.where(later, experts[None, :], N_EXPERTS), axis=1)
    nxt = jnp.where(nxt == N_EXPERTS, -1, nxt)
    next_e = jnp.sum(jnp.where(mine, nxt[:, None], 0), axis=0).astype(jnp.int32)
    return (row_tok.reshape(N_BLOCKS, 1, MOE_M), row_w.reshape(N_BLOCKS, 1, MOE_M), blk_e.astype(jnp.int32),
            first, next_e, n_used.reshape(1))


def _final_kernel(x_ref, moe_ref, mod_ref, g_ref, yc_o, yl_o):
    x = x_ref[...] + mod_ref[0, 5:6, :] * _tiles_to_rows(moe_ref)
    ms = jnp.mean(x * x, axis=-1, keepdims=True)
    y = x * lax.rsqrt(ms + EPS) * g_ref[...]
    i = pl.program_id(0)

    @pl.when(i < N_CTX_TILES)
    def _():
        yc_o[...] = y

    @pl.when(i >= N_CTX_TILES)
    def _():
        yl_o[...] = y


def _final(x1, moe_tiles, mod_l, g):
    row = lambda i: (i, 0)
    return pl.pallas_call(
        _final_kernel,
        grid=(T // TM,),
        in_specs=[pl.BlockSpec((TM, D_MODEL), row),
                  pl.BlockSpec((TM * ROW_VREGS, LANES), row),
                  pl.BlockSpec((1, 6, D_MODEL), lambda i: (_mod_group(i), 0, 0)),
                  pl.BlockSpec((1, D_MODEL), lambda i: (0, 0))],
        out_specs=[pl.BlockSpec((TM, D_MODEL), _ctx_tile), pl.BlockSpec((TM, D_MODEL), _lat_tile)],
        out_shape=[jax.ShapeDtypeStruct((T_CTX, D_MODEL), F32), jax.ShapeDtypeStruct((T_LAT, D_MODEL), F32)],
        compiler_params=pltpu.CompilerParams(vmem_limit_bytes=32 * MIB),
        name="residual_final",
    )(x1, moe_tiles, mod_l, g)


def _rope_tables():
    t = np.arange(DEC_SEQ)
    pos = np.stack([t // GRID_W, t % GRID_W], axis=1).astype(np.float32)
    half = HEAD_DIM // 2
    inv = np.float32(ROPE_BASE) ** (-np.arange(0, half, 2, dtype=np.float32) / np.float32(half))
    d = np.arange(HEAD_DIM)
    which = d // half
    freq = d % (half // 2)
    sign = np.where((d % half) < half // 2, -1.0, 1.0).astype(np.float32)
    ang = (pos[:, which] * inv[freq][None, :]).astype(np.float32)
    reps = LANES // HEAD_DIM
    cos = np.tile(np.cos(ang).astype(np.float32), (1, reps))
    sin = np.tile(np.sin(ang).astype(np.float32) * sign[None, :], (1, reps))
    return jnp.asarray(cos), jnp.asarray(sin)


def kernel(x_prompt, x_sample, cache_k, cache_v, state_ret, c, c_ctx, norm1_g, norm2_g, w_mod, b_mod, w_in, conv_w, attn_sink, ret_decay, w_a, w_b, w_c, w_o, w_router, b_router, w_gu, b_gu, w_down, b_down, final_g):
    xs = (x_prompt.reshape(T_CTX, D_MODEL), x_sample.reshape(T_LAT, D_MODEL))
    cond8 = jnp.zeros((8, D_MODEL), F32).at[0].set(c_ctx).at[1:1 + DEC_BATCH].set(c)
    mod = _modulation(cond8, w_mod, b_mod)
    cos, sin = _rope_tables()
    ck = cache_k.reshape(DEC_BATCH, DEPTH, PAST_LEN, D_KV)
    cv = cache_v.reshape(DEC_BATCH, DEPTH, PAST_LEN, D_KV)

    ks, vs, rs = [], [], []
    prev = None
    for l in range(DEPTH):
        mod_l = mod[l, 0:1 + DEC_BATCH].reshape(1 + DEC_BATCH, 6, D_MODEL)
        if prev is None:
            conv_in, q, k, v, ret, gates = _inproj(xs, mod_l, norm1_g[l][None, :], w_in, l)
        else:
            x, conv_in, q, k, v, ret, gates = _inproj(None, mod_l, norm1_g[l][None, :], w_in, l, prev=prev)
            xs = (x,)
        rd8 = jnp.broadcast_to(ret_decay[l].reshape(2 * RET_HEADS, 1), (2 * RET_HEADS, LANES))
        *ys_ctx, rfin = _mixers(conv_in, q, k, v, ret, conv_w[l], attn_sink[l], rd8, latent=False)
        ys_lat = _mixers(conv_in, q, k, v, ret, conv_w[l], attn_sink[l], rd8, latent=True,
                         cos=cos, sin=sin, cache_k=ck, cache_v=cv, state=state_ret, layer=l)
        wr = jnp.pad(w_router[l], ((0, 0), (0, LANES - N_EXPERTS)))
        wrh = wr.astype(BF16)
        wrl = (wr - wrh.astype(F32)).astype(BF16)
        br = jnp.pad(b_router[l], (0, LANES - N_EXPERTS))[None, :]
        x1, xp, top_keys, top_w, cnt = _merge(xs, ys_ctx, ys_lat, gates, mod_l, norm2_g[l][None, :], w_a, w_b, w_c,
                                           w_o, wrh, wrl, br, l)
        row_tok, row_w, blk_e, first, next_e, n_used = _route(top_keys[0:TOP_K], top_w[0:TOP_K],
                                                              cnt[0, 0:N_EXPERTS])
        yrows = _moe_experts(blk_e, first, next_e, n_used, row_tok, xp, w_gu, b_gu, w_down, b_down, l)
        moe = _combine(n_used, row_tok, row_w, yrows)
        prev = (x1, moe, mod_l)
        ks.append(k[0:T_CTX].reshape(BATCH, SEQ, N_KV, HEAD_DIM))
        vs.append(v[0:T_CTX].reshape(BATCH, SEQ, N_KV, HEAD_DIM))
        rs.append(rfin)

    yc, yl = _final(*prev, final_g[None, :])
    y_prompt = yc.reshape(BATCH, SEQ, D_MODEL)
    y_sample = yl.reshape(DEC_BATCH, DEC_SEQ, D_MODEL)
    return (y_prompt, y_sample, jnp.stack(ks, axis=1), jnp.stack(vs, axis=1), jnp.stack(rs, axis=1))
```

```python
import functools

import numpy as np
import jax
import jax.numpy as jnp
from jax import lax
from jax.experimental import pallas as pl
from jax.experimental.pallas import tpu as pltpu

F32 = jnp.float32
BF16 = jnp.bfloat16

D_MODEL = 1024
BATCH = 16
SEQ = 256
DEPTH = 2
DEC_BATCH = 2
DEC_SEQ = 2048
PAST_LEN = 256
GRID_W = 64
HEAD_DIM = 64
D_CONV = 256
N_HEADS = 8
N_KV = 2
GROUP = N_HEADS // N_KV
WINDOW = 128
ROPE_BASE = 10000.0
RET_HEADS = 4
RET_DK = 64
RET_DV = 64
CHUNK = 128
N_EXPERTS = 32
TOP_K = 4
D_EXPERT = D_MODEL
SWIGLU_LIMIT = 7.0
SWIGLU_ALPHA = 1.702
EPS = 1e-6
NEG_INF = -1e30

T_CTX = BATCH * SEQ
T_LAT = DEC_BATCH * DEC_SEQ
T = T_CTX + T_LAT
D_ATTN = N_HEADS * HEAD_DIM
D_KV = N_KV * HEAD_DIM
D_RET = RET_HEADS * RET_DK
C_CONV = 0
C_Q = 3 * D_CONV
C_K = C_Q + D_ATTN
C_V = C_K + D_KV
C_RET = C_V + D_KV
C_GATE = C_RET + 4 * D_RET
IN_COLS = C_GATE + 3 * D_MODEL

TM = 512
MOE_M = 256
N_ASSIGN = T * TOP_K
N_BLOCKS = (N_ASSIGN + N_EXPERTS * (MOE_M - 1) + MOE_M - 1) // MOE_M
N_ROWS = N_BLOCKS * MOE_M
LANES = 128
SUBLANES = 8
ROW_VREGS = D_MODEL // LANES
PACK_ROWS = ROW_VREGS // 2
GATHER_STRIDE = MOE_M + SUBLANES
SCATTER_M = 128
SCATTER_STRIDE = SCATTER_M + SUBLANES
COMBINE_BLOCKS = 4
TOK_BITS = 14
assert T < (1 << TOK_BITS)
MIB = 1024 * 1024


def _sigmoid(x):
    return 1.0 / (1.0 + jnp.exp(-x))


def _mod_group(i):
    n_ctx = T_CTX // TM
    per_lat = DEC_SEQ // TM
    g = jnp.zeros_like(i)
    for b in range(DEC_BATCH):
        g = g + (i >= n_ctx + b * per_lat).astype(jnp.int32)
    return g


def _mod_kernel(cond_ref, w_ref, b_ref, o_ref):
    c = cond_ref[...]
    s = c * _sigmoid(c)
    o_ref[0] = jnp.dot(s.astype(BF16), w_ref[0].astype(BF16), preferred_element_type=F32) + b_ref[0]


def _modulation(cond8, w_mod, b_mod):
    n_col = 4
    cw = 6 * D_MODEL // n_col
    return pl.pallas_call(
        _mod_kernel,
        grid=(DEPTH, n_col),
        in_specs=[pl.BlockSpec((8, D_MODEL), lambda l, j: (0, 0)),
                  pl.BlockSpec((1, D_MODEL, cw), lambda l, j: (l, 0, j)),
                  pl.BlockSpec((1, 1, cw), lambda l, j: (l, 0, j))],
        out_specs=pl.BlockSpec((1, 8, cw), lambda l, j: (l, 0, j)),
        out_shape=jax.ShapeDtypeStruct((DEPTH, 8, 6 * D_MODEL), F32),
        compiler_params=pltpu.CompilerParams(vmem_limit_bytes=32 * MIB),
        name="modulation",
    )(cond8, w_mod, b_mod.reshape(DEPTH, 1, 6 * D_MODEL))


N_CTX_TILES = T_CTX // TM


def _ctx_tile(i):
    return (jnp.minimum(i, N_CTX_TILES - 1), 0)


def _lat_tile(i):
    return (jnp.maximum(i - N_CTX_TILES, 0), 0)


def _pick(ctx_ref, lat_ref):
    return jnp.where(pl.program_id(0) < N_CTX_TILES, ctx_ref[...], lat_ref[...])


def _tiles_to_rows(tiles_ref):
    return jnp.concatenate([tiles_ref[pl.ds(c, TM, stride=ROW_VREGS), :] for c in range(ROW_VREGS)], axis=1)


def _inproj_kernel(*refs, first_layer):
    if first_layer:
        xc_ref, xl_ref, mod_ref, g_ref, w_ref, conv_o, q_o, k_o, v_o, ret_o, gate_o = refs
        x = _pick(xc_ref, xl_ref)
    else:
        x1_ref, moe_ref, modp_ref, mod_ref, g_ref, w_ref, x_o, conv_o, q_o, k_o, v_o, ret_o, gate_o = refs
        x = x1_ref[...] + modp_ref[0, 5:6, :] * _tiles_to_rows(moe_ref)
        x_o[...] = x
    ms = jnp.mean(x * x, axis=-1, keepdims=True)
    h = x * lax.rsqrt(ms + EPS) * g_ref[...]
    h = h * (1.0 + mod_ref[0, 1:2, :]) + mod_ref[0, 0:1, :]
    hb = h.astype(BF16)

    def proj(c0, c1):
        return jnp.dot(hb, w_ref[0, :, c0:c1].astype(BF16), preferred_element_type=F32)

    a = proj(C_CONV, C_RET)
    conv_o[...] = a[:, C_CONV:C_Q].astype(BF16)
    q_o[...] = (a[:, C_Q:C_K] * HEAD_DIM ** -0.5).astype(BF16)
    k_o[...] = a[:, C_K:C_V]
    v_o[...] = a[:, C_V:C_RET]
    r = proj(C_RET, C_GATE)
    ret_o[:, 0:D_RET] = r[:, 0:D_RET].astype(BF16)
    ret_o[:, D_RET:2 * D_RET] = (r[:, D_RET:2 * D_RET] * RET_DK ** -0.5).astype(BF16)
    ret_o[:, 2 * D_RET:4 * D_RET] = r[:, 2 * D_RET:4 * D_RET].astype(BF16)
    gate_half = 3 * D_MODEL // 2
    for b in range(2):
        g = proj(C_GATE + b * gate_half, C_GATE + (b + 1) * gate_half)
        gate_o[:, b * gate_half:(b + 1) * gate_half] = _sigmoid(g).astype(BF16)


def _x_specs(xs):
    if len(xs) == 2:
        return [pl.BlockSpec((TM, D_MODEL), _ctx_tile), pl.BlockSpec((TM, D_MODEL), _lat_tile)]
    return [pl.BlockSpec((TM, D_MODEL), lambda i: (i, 0))]


def _inproj(xs, mod_l, g1, w_in, layer, prev=None):
    row = lambda i: (i, 0)
    mod_spec = pl.BlockSpec((1, 6, D_MODEL), lambda i: (_mod_group(i), 0, 0))
    out_specs = [pl.BlockSpec((TM, 3 * D_CONV), row),
                 pl.BlockSpec((TM, D_ATTN), row),
                 pl.BlockSpec((TM, D_KV), row),
                 pl.BlockSpec((TM, D_KV), row),
                 pl.BlockSpec((TM, 4 * D_RET), row),
                 pl.BlockSpec((TM, 3 * D_MODEL), row)]
    out_shape = [jax.ShapeDtypeStruct((T, 3 * D_CONV), BF16),
                 jax.ShapeDtypeStruct((T, D_ATTN), BF16),
                 jax.ShapeDtypeStruct((T, D_KV), F32),
                 jax.ShapeDtypeStruct((T, D_KV), F32),
                 jax.ShapeDtypeStruct((T, 4 * D_RET), BF16),
                 jax.ShapeDtypeStruct((T, 3 * D_MODEL), BF16)]
    if prev is None:
        in_specs, args = _x_specs(xs), list(xs)
    else:
        in_specs = [pl.BlockSpec((TM, D_MODEL), row), pl.BlockSpec((TM * ROW_VREGS, LANES), row), mod_spec]
        args = list(prev)
        out_specs = [pl.BlockSpec((TM, D_MODEL), row)] + out_specs
        out_shape = [jax.ShapeDtypeStruct((T, D_MODEL), F32)] + out_shape
    return pl.pallas_call(
        functools.partial(_inproj_kernel, first_layer=prev is None),
        grid=(T // TM,),
        in_specs=in_specs + [
            mod_spec,
            pl.BlockSpec((1, D_MODEL), lambda i: (0, 0)),
            pl.BlockSpec((1, D_MODEL, IN_COLS), lambda i: (layer, 0, 0), pipeline_mode=pl.Buffered(1))],
        out_specs=out_specs,
        out_shape=out_shape,
        compiler_params=pltpu.CompilerParams(vmem_limit_bytes=60 * MIB),
        name="inproj",
    )(*args, mod_l, g1, w_in)


def _rope(x, cos, sin_signed):
    lane = lax.broadcasted_iota(jnp.int32, x.shape, 1)
    first = (lane % 32) < 16
    partner = jnp.where(first, pltpu.roll(x, x.shape[1] - 16, 1), pltpu.roll(x, 16, 1))
    return x * cos + partner * sin_signed


HALF = LANES // 2
assert HEAD_DIM == HALF and RET_DK == HALF and RET_DV == HALF
N_PAIRS = N_HEADS // 2
RET_PAIRS = RET_HEADS // 2


def _row_variants(kt):
    row = lax.broadcasted_iota(jnp.int32, kt.shape, 0)
    lo0 = jnp.where(row < HALF, kt, 0.0)
    hi1 = jnp.where(row >= HALF, kt, 0.0)
    return lo0, pltpu.roll(lo0, HALF, 0), pltpu.roll(hi1, HALF, 0), hi1


def _lane_variants(v):
    lane = lax.broadcasted_iota(jnp.int32, v.shape, 1)
    lo0 = jnp.where(lane < HALF, v, 0.0)
    hi1 = jnp.where(lane >= HALF, v, 0.0)
    return lo0, pltpu.roll(lo0, HALF, 1), pltpu.roll(hi1, HALF, 1), hi1


def _mixer_kernel(*refs, n, latent):
    if latent:
        (sink_ref, conv_ref, q_ref, k_ref, v_ref, ret_ref, cw_ref, rd_ref, cos_ref, sin_ref, ck_ref, cv_ref,
         r0_ref, yc_ref, ya_ref, yr_ref, ktq, vq, rkp, o_f, o_b, rst, dmat, qdec, kdec, cdec) = refs
    else:
        (sink_ref, conv_ref, q_ref, k_ref, v_ref, ret_ref, cw_ref, rd_ref,
         yc_ref, ya_ref, yr_ref, rfin_ref, ktq, vq, rkp, o_f, o_b, rst, dmat, qdec, kdec, cdec) = refs
    nb = n // CHUNK
    pad = CHUNK if latent else 0

    cv = conv_ref[...].astype(F32)
    cb, cc, cu = cv[:, 0:D_CONV], cv[:, D_CONV:2 * D_CONV], cv[:, 2 * D_CONV:3 * D_CONV]
    p = cc * cu
    row = lax.broadcasted_iota(jnp.int32, p.shape, 0)
    prev = jnp.where(row == 0, 0.0, pltpu.roll(p, 1, 0))
    nxt = jnp.where(row == n - 1, 0.0, pltpu.roll(p, n - 1, 0))
    cw = cw_ref[...]
    yc_ref[...] = (cb * (prev * cw[0:1, :] + p * cw[1:2, :] + nxt * cw[2:3, :])).astype(BF16)

    kf = k_ref[...]
    if latent:
        kf = _rope(kf, cos_ref[...], sin_ref[...])
    for idx, (kk, vv) in enumerate(zip(_row_variants(kf.T), _lane_variants(v_ref[...]))):
        if latent:
            ktq[idx, :, 0:pad] = jnp.zeros((LANES, pad), BF16)
            ktq[idx, :, pad + n:2 * pad + n] = jnp.zeros((LANES, pad), BF16)
            vq[idx, 0:pad, :] = jnp.zeros((pad, LANES), BF16)
            vq[idx, pad + n:2 * pad + n, :] = jnp.zeros((pad, LANES), BF16)
        ktq[idx, :, pad:pad + n] = kk.astype(BF16)
        vq[idx, pad:pad + n, :] = vv.astype(BF16)
    if latent:
        cktq = [t.astype(BF16) for t in _row_variants(ck_ref[0, 0].T)]
        cvq = [t.astype(BF16) for t in _lane_variants(cv_ref[0, 0])]

    def attn(r0, rows):
        qj = q_ref[pl.ds(r0, rows), :]
        if latent:
            cosj = cos_ref[pl.ds(r0, rows), :]
            sinj = sin_ref[pl.ds(r0, rows), :]
            qpos = r0 + lax.broadcasted_iota(jnp.int32, (rows, 3 * CHUNK), 0)
            kpos = r0 - CHUNK + lax.broadcasted_iota(jnp.int32, (rows, 3 * CHUNK), 1)
            ok = (jnp.abs(qpos - kpos) <= WINDOW) & (kpos >= 0) & (kpos < n)
        for m in range(N_PAIRS):
            q2 = qj[:, m * LANES:(m + 1) * LANES]
            if latent:
                q2 = _rope(q2.astype(F32), cosj, sinj).astype(BF16)
            g = (2 * m) // GROUP
            acc = None
            for half in range(2):
                idx = 2 * g + half
                sk = sink_ref[2 * m + half]
                if latent:
                    s = jnp.dot(q2, ktq[idx, :, pl.ds(r0, 3 * CHUNK)], preferred_element_type=F32)
                    s = jnp.where(ok, s, NEG_INF)
                    s2 = jnp.dot(q2, cktq[idx], preferred_element_type=F32)
                    mx = jnp.maximum(jnp.maximum(jnp.max(s, axis=-1, keepdims=True),
                                                 jnp.max(s2, axis=-1, keepdims=True)), sk)
                    pw = jnp.exp(s - mx)
                    p2 = jnp.exp(s2 - mx)
                    den = (jnp.sum(pw, axis=-1, keepdims=True) + jnp.sum(p2, axis=-1, keepdims=True)
                           + jnp.exp(sk - mx))
                    o = (jnp.dot(pw.astype(BF16), vq[idx, pl.ds(r0, 3 * CHUNK), :], preferred_element_type=F32)
                         + jnp.dot(p2.astype(BF16), cvq[idx], preferred_element_type=F32))
                else:
                    s = jnp.dot(q2, ktq[idx], preferred_element_type=F32)
                    mx = jnp.maximum(jnp.max(s, axis=-1, keepdims=True), sk)
                    pw = jnp.exp(s - mx)
                    den = jnp.sum(pw, axis=-1, keepdims=True) + jnp.exp(sk - mx)
                    o = jnp.dot(pw.astype(BF16), vq[idx], preferred_element_type=F32)
                o = o / den
                acc = o if acc is None else acc + o
            ya_ref[pl.ds(r0, rows), m * LANES:(m + 1) * LANES] = acc.astype(BF16)

    rk_t = ret_ref[:, D_RET:2 * D_RET].astype(F32).T
    for m in range(RET_PAIRS):
        rkp[m] = rk_t[m * LANES:(m + 1) * LANES, :].astype(BF16)
    rd = rd_ref[...]
    log_g = jnp.minimum(rd, 0.0) - jnp.log(1.0 + jnp.exp(-jnp.abs(rd)))
    row_c = lax.broadcasted_iota(jnp.int32, (CHUNK, CHUNK), 0)
    lane_c = lax.broadcasted_iota(jnp.int32, (CHUNK, CHUNK), 1)
    ii = row_c.astype(F32)
    jj = lane_c.astype(F32)
    even_row = row_c < HALF
    even_lane = lane_c < HALF
    blockdiag = even_row == even_lane
    for d in range(2):
        for h in range(RET_HEADS):
            r = d * RET_HEADS + h
            diff = (ii - jj) if d == 0 else (jj - ii)
            dmat[r] = jnp.where(diff >= 0, jnp.exp(jnp.maximum(diff, 0.0) * log_g[r:r + 1, :]), 0.0)
        for m in range(RET_PAIRS):
            s = d * RET_PAIRS + m
            lg_e = log_g[d * RET_HEADS + 2 * m:d * RET_HEADS + 2 * m + 1, :]
            lg_o = log_g[d * RET_HEADS + 2 * m + 1:d * RET_HEADS + 2 * m + 2, :]
            qpow = (ii + 1.0) if d == 0 else (CHUNK - ii)
            kpow = (CHUNK - 1.0 - jj) if d == 0 else jj
            qdec[s] = jnp.where(even_lane, jnp.exp(qpow * lg_e), jnp.exp(qpow * lg_o))
            kdec[s] = jnp.where(even_row, jnp.exp(kpow * lg_e), jnp.exp(kpow * lg_o))
            chunk_decay = jnp.where(even_row, jnp.exp(float(CHUNK) * lg_e), jnp.exp(float(CHUNK) * lg_o))
            cdec[s] = jnp.where(blockdiag, chunk_decay, 0.0)
            if latent:
                z = jnp.zeros((HALF, HALF), F32)
                rst[s] = jnp.concatenate(
                    [jnp.concatenate([r0_ref[0, 0, d, 2 * m], z], axis=1),
                     jnp.concatenate([z, r0_ref[0, 0, d, 2 * m + 1]], axis=1)], axis=0)
            else:
                rst[s] = jnp.zeros((LANES, LANES), F32)

    def ret_chunk(c0, d, out_ref):
        for m in range(RET_PAIRS):
            s = d * RET_PAIRS + m
            q2 = ret_ref[pl.ds(c0, CHUNK), m * LANES:(m + 1) * LANES]
            v2 = ret_ref[pl.ds(c0, CHUNK), 2 * D_RET + m * LANES:2 * D_RET + (m + 1) * LANES]
            kt2 = rkp[m, :, pl.ds(c0, CHUNK)].astype(F32)
            v2f = v2.astype(F32)
            state = rst[s]
            o2 = jnp.dot(q2, state.astype(BF16), preferred_element_type=F32) * qdec[s]
            for half in range(2):
                r = d * RET_HEADS + 2 * m + half
                keep_row = even_row if half == 0 else jnp.logical_not(even_row)
                keep_lane = even_lane if half == 0 else jnp.logical_not(even_lane)
                a = jnp.dot(q2, jnp.where(keep_row, kt2, 0.0).astype(BF16), preferred_element_type=F32)
                inner = (a * dmat[r]).astype(BF16)
                o2 = o2 + jnp.dot(inner, jnp.where(keep_lane, v2f, 0.0).astype(BF16), preferred_element_type=F32)
            kd = (kt2 * kdec[s]).astype(BF16)
            upd = jnp.dot(kd, v2, preferred_element_type=F32)
            rst[s] = state * cdec[s] + jnp.where(blockdiag, upd, 0.0)
            out_ref[pl.ds(c0, CHUNK), m * LANES:(m + 1) * LANES] = o2

    gi = lax.broadcasted_iota(jnp.int32, (D_RET, D_RET), 0) // RET_DV
    gj = lax.broadcasted_iota(jnp.int32, (D_RET, D_RET), 1) // RET_DV
    group_mean = jnp.where(gi == gj, 1.0 / RET_DV, 0.0).astype(BF16)

    def norm(c0, rows):
        o = o_f[pl.ds(c0, rows), :] + o_b[pl.ds(c0, rows), :]
        sq = o * o
        hi = sq.astype(BF16)
        lo = (sq - hi.astype(F32)).astype(BF16)
        ms = (jnp.dot(hi, group_mean, preferred_element_type=F32)
              + jnp.dot(lo, group_mean, preferred_element_type=F32))
        rg = ret_ref[pl.ds(c0, rows), 3 * D_RET:4 * D_RET].astype(F32)
        yr_ref[pl.ds(c0, rows), :] = (rg * _sigmoid(rg) * (o * lax.rsqrt(ms + EPS))).astype(BF16)

    if latent:
        def scan_body(j, carry):
            attn(pl.multiple_of(j * CHUNK, CHUNK), CHUNK)
            ret_chunk(pl.multiple_of(j * CHUNK, CHUNK), 0, o_f)
            ret_chunk(pl.multiple_of((nb - 1 - j) * CHUNK, CHUNK), 1, o_b)
            return carry

        def norm_body(j, carry):
            norm(pl.multiple_of(j * CHUNK, CHUNK), CHUNK)
            return carry

        lax.fori_loop(0, nb, scan_body, 0)
        lax.fori_loop(0, nb, norm_body, 0)
    else:
        attn(0, n)
        for j in range(nb):
            ret_chunk(j * CHUNK, 0, o_f)
            ret_chunk((nb - 1 - j) * CHUNK, 1, o_b)
        norm(0, n)
        for d in range(2):
            for h in range(RET_HEADS):
                lo_ = (h % 2) * HALF
                rfin_ref[0, d, h] = rst[d * RET_PAIRS + h // 2][lo_:lo_ + HALF, lo_:lo_ + HALF]


def _mixers(conv_in, q, k, v, ret, conv_w_l, sink_l, rd8, *, latent, cos=None, sin=None, cache_k=None,
            cache_v=None, state=None, layer=0):
    n = DEC_SEQ if latent else SEQ
    nseq = DEC_BATCH if latent else BATCH
    off = T_CTX // n if latent else 0
    seq = lambda s: (s + off, 0)
    const = lambda s: (0, 0)
    in_specs = [pl.BlockSpec(memory_space=pltpu.SMEM),
                pl.BlockSpec((n, 3 * D_CONV), seq),
                pl.BlockSpec((n, D_ATTN), seq),
                pl.BlockSpec((n, D_KV), seq),
                pl.BlockSpec((n, D_KV), seq),
                pl.BlockSpec((n, 4 * D_RET), seq),
                pl.BlockSpec((3, D_CONV), const),
                pl.BlockSpec((8, LANES), const)]
    args = [sink_l, conv_in, q, k, v, ret, conv_w_l, rd8]
    out_specs = [pl.BlockSpec((n, D_CONV), lambda s: (s, 0)),
                 pl.BlockSpec((n, D_ATTN), lambda s: (s, 0)),
                 pl.BlockSpec((n, D_RET), lambda s: (s, 0))]
    out_shape = [jax.ShapeDtypeStruct((nseq * n, D_CONV), BF16),
                 jax.ShapeDtypeStruct((nseq * n, D_ATTN), BF16),
                 jax.ShapeDtypeStruct((nseq * n, D_RET), BF16)]
    if latent:
        in_specs += [pl.BlockSpec((n, LANES), const),
                     pl.BlockSpec((n, LANES), const),
                     pl.BlockSpec((1, 1, PAST_LEN, D_KV), lambda s: (s, layer, 0, 0)),
                     pl.BlockSpec((1, 1, PAST_LEN, D_KV), lambda s: (s, layer, 0, 0)),
                     pl.BlockSpec((1, 1, 2, RET_HEADS, RET_DK, RET_DV), lambda s: (s, layer, 0, 0, 0, 0))]
        args += [cos, sin, cache_k, cache_v, state]
        kv_rows = n + 2 * CHUNK
    else:
        out_specs.append(pl.BlockSpec((1, 2, RET_HEADS, RET_DK, RET_DV), lambda s: (s, 0, 0, 0, 0)))
        out_shape.append(jax.ShapeDtypeStruct((nseq, 2, RET_HEADS, RET_DK, RET_DV), F32))
        kv_rows = n
    scratch = [pltpu.VMEM((2 * N_KV, LANES, kv_rows), BF16),
               pltpu.VMEM((2 * N_KV, kv_rows, LANES), BF16),
               pltpu.VMEM((RET_PAIRS, LANES, n), BF16),
               pltpu.VMEM((n, D_RET), F32),
               pltpu.VMEM((n, D_RET), F32),
               pltpu.VMEM((2 * RET_PAIRS, LANES, LANES), F32),
               pltpu.VMEM((2 * RET_HEADS, CHUNK, CHUNK), F32),
               pltpu.VMEM((2 * RET_PAIRS, CHUNK, CHUNK), F32),
               pltpu.VMEM((2 * RET_PAIRS, CHUNK, CHUNK), F32),
               pltpu.VMEM((2 * RET_PAIRS, CHUNK, CHUNK), F32)]
    return pl.pallas_call(
        functools.partial(_mixer_kernel, n=n, latent=latent),
        grid=(nseq,),
        in_specs=in_specs,
        out_specs=out_specs,
        out_shape=out_shape,
        scratch_shapes=scratch,
        compiler_params=pltpu.CompilerParams(vmem_limit_bytes=56 * MIB),
        name="mixers_latent" if latent else "mixers_context",
    )(*args)


def _merge_kernel(*refs, split):
    if split:
        xc_ref, xl_ref = refs[0:2]
        x = _pick(xc_ref, xl_ref)
        refs = refs[2:]
    else:
        x = refs[0][...]
        refs = refs[1:]
    (ycc_ref, ycl_ref, yac_ref, yal_ref, yrc_ref, yrl_ref, gate_ref, mod_ref, g2_ref, wa_ref, wb_ref, wc_ref,
     wo_ref, wrh_ref, wrl_ref, br_ref, x1_o, xp_o, key_o, tw_o, cnt_o) = refs
    merged = (gate_ref[:, 0:D_MODEL].astype(F32)
              * jnp.dot(_pick(ycc_ref, ycl_ref), wa_ref[0].astype(BF16), preferred_element_type=F32)
              + gate_ref[:, D_MODEL:2 * D_MODEL].astype(F32)
              * jnp.dot(_pick(yac_ref, yal_ref), wb_ref[0].astype(BF16), preferred_element_type=F32)
              + gate_ref[:, 2 * D_MODEL:3 * D_MODEL].astype(F32)
              * jnp.dot(_pick(yrc_ref, yrl_ref), wc_ref[0].astype(BF16), preferred_element_type=F32))
    x1 = x + mod_ref[0, 2:3, :] * jnp.dot(merged.astype(BF16), wo_ref[0].astype(BF16),
                                          preferred_element_type=F32)
    x1_o[...] = x1
    ms = jnp.mean(x1 * x1, axis=-1, keepdims=True)
    h2 = x1 * lax.rsqrt(ms + EPS) * g2_ref[...]
    h2 = h2 * (1.0 + mod_ref[0, 4:5, :]) + mod_ref[0, 3:4, :]
    hh = h2.astype(BF16)
    hf = hh.astype(F32)
    bits = lax.bitcast_convert_type(hf, jnp.uint32)
    for c in range(PACK_ROWS):
        lo = bits[:, c * LANES:(c + 1) * LANES] >> 16
        hi = bits[:, (c + PACK_ROWS) * LANES:(c + PACK_ROWS + 1) * LANES] & jnp.uint32(0xFFFF0000)
        xp_o[pl.ds(c, TM, stride=PACK_ROWS), :] = lax.bitcast_convert_type(lo | hi, jnp.int32)
    hl = (h2 - hf).astype(BF16)
    logits = (jnp.dot(hh, wrh_ref[...], preferred_element_type=F32)
              + jnp.dot(hl, wrh_ref[...], preferred_element_type=F32)
              + jnp.dot(hh, wrl_ref[...], preferred_element_type=F32)
              + br_ref[...])
    lane = lax.broadcasted_iota(jnp.int32, logits.shape, 1)
    work = jnp.where(lane < N_EXPERTS, logits, -jnp.inf)
    vals, idxs = [], []
    for _ in range(TOP_K):
        m = jnp.max(work, axis=-1, keepdims=True)
        am = jnp.min(jnp.where(work == m, lane, LANES), axis=-1, keepdims=True)
        vals.append(m)
        idxs.append(am)
        work = jnp.where(lane == am, -jnp.inf, work)
    es = [jnp.exp(v - vals[0]) for v in vals]
    den = es[0] + es[1] + es[2] + es[3]
    tok = pl.program_id(0) * TM + lax.broadcasted_iota(jnp.int32, (TM, 1), 0)
    keys = jnp.zeros(logits.shape, jnp.int32)
    tw = jnp.zeros(logits.shape, F32)
    for k in range(TOP_K):
        keys = jnp.where(lane == k, (2 * idxs[k]) << TOK_BITS | tok, keys)
        tw = jnp.where(lane == k, es[k] / den, tw)
    key_o[...] = keys.astype(F32).T[0:SUBLANES, :].astype(jnp.int32)
    tw_o[...] = tw.T[0:SUBLANES, :]
    sel = (lane == idxs[0]) | (lane == idxs[1]) | (lane == idxs[2]) | (lane == idxs[3])
    part = jnp.sum(sel.astype(jnp.int32), axis=0, keepdims=True)

    @pl.when(pl.program_id(0) == 0)
    def _():
        cnt_o[...] = jnp.zeros_like(cnt_o)

    cnt_o[...] += jnp.broadcast_to(part, cnt_o.shape)


def _merge(xs, ys_ctx, ys_lat, gates, mod_l, g2, wa, wb, wc, wo, wrh, wrl, br, layer):
    row = lambda i: (i, 0)
    const = lambda i: (0, 0)
    wl = lambda i: (layer, 0, 0)
    y_specs, y_args = [], []
    for width, yc, yl in zip((D_CONV, D_ATTN, D_RET), ys_ctx, ys_lat):
        y_specs += [pl.BlockSpec((TM, width), _ctx_tile), pl.BlockSpec((TM, width), _lat_tile)]
        y_args += [yc, yl]
    return pl.pallas_call(
        functools.partial(_merge_kernel, split=len(xs) == 2),
        grid=(T // TM,),
        in_specs=_x_specs(xs) + y_specs + [
                  pl.BlockSpec((TM, 3 * D_MODEL), row),
                  pl.BlockSpec((1, 6, D_MODEL), lambda i: (_mod_group(i), 0, 0)),
                  pl.BlockSpec((1, D_MODEL), const),
                  pl.BlockSpec((1, D_CONV, D_MODEL), wl),
                  pl.BlockSpec((1, D_ATTN, D_MODEL), wl),
                  pl.BlockSpec((1, D_RET, D_MODEL), wl),
                  pl.BlockSpec((1, D_MODEL, D_MODEL), wl),
                  pl.BlockSpec((D_MODEL, LANES), const),
                  pl.BlockSpec((D_MODEL, LANES), const),
                  pl.BlockSpec((1, LANES), const)],
        out_specs=[pl.BlockSpec((TM, D_MODEL), row),
                   pl.BlockSpec((TM * PACK_ROWS, LANES), row),
                   pl.BlockSpec((SUBLANES, TM), lambda i: (0, i)),
                   pl.BlockSpec((SUBLANES, TM), lambda i: (0, i)),
                   pl.BlockSpec((SUBLANES, LANES), const)],
        out_shape=[jax.ShapeDtypeStruct((T, D_MODEL), F32),
                   jax.ShapeDtypeStruct((T * PACK_ROWS, LANES), jnp.int32),
                   jax.ShapeDtypeStruct((SUBLANES, T), jnp.int32),
                   jax.ShapeDtypeStruct((SUBLANES, T), F32),
                   jax.ShapeDtypeStruct((SUBLANES, LANES), jnp.int32)],
        compiler_params=pltpu.CompilerParams(vmem_limit_bytes=48 * MIB),
        name="merge_router",
    )(*xs, *y_args, gates, mod_l, g2, wa, wb, wc, wo, wrh, wrl, br)


def _moe_kernel(blk_e_ref, first_ref, next_e_ref, nu_ref, tok_ref, tok_next_ref, xp_ref, wgu_hbm, bgu_ref, wd_hbm,
                bd_ref, y_ref, tile, wgu_st, wd_st, wgu_bf, wd_bf, sems, *, layer):
    b = pl.program_id(0)
    slot = b % 2

    def weight_copies(e):
        return (pltpu.make_async_copy(wgu_hbm.at[layer, e], wgu_st, sems.at[0]),
                pltpu.make_async_copy(wd_hbm.at[layer, e], wd_st, sems.at[1]))

    def gather(tok, dst_slot):
        for mi in range(MOE_M):
            t = jnp.minimum(tok[0, 0, mi], T - 1)
            slab = xp_ref[pl.ds(pl.multiple_of(t * PACK_ROWS, PACK_ROWS), PACK_ROWS), :]
            tile[dst_slot, pl.ds(mi, PACK_ROWS, stride=GATHER_STRIDE), :] = slab

    @pl.when(b == 0)
    def _():
        for cp in weight_copies(blk_e_ref[0]):
            cp.start()
        gather(tok_ref, 0)

    @pl.when(b < nu_ref[0])
    def _():
        @pl.when(first_ref[b] == 1)
        def _():
            for cp in weight_copies(blk_e_ref[b]):
                cp.wait()
            wgu_bf[...] = wgu_st[...].astype(BF16)
            wd_bf[...] = wd_st[...].astype(BF16)

            @pl.when(next_e_ref[b] >= 0)
            def _():
                for cp in weight_copies(next_e_ref[b]):
                    cp.start()

        lo, hi = [], []
        for c in range(PACK_ROWS):
            bits = lax.bitcast_convert_type(
                tile[slot, c * GATHER_STRIDE:c * GATHER_STRIDE + MOE_M, :], jnp.uint32)
            lo.append(lax.bitcast_convert_type(bits << 16, F32).astype(BF16))
            hi.append(lax.bitcast_convert_type(bits & jnp.uint32(0xFFFF0000), F32).astype(BF16))
        x = jnp.concatenate(lo + hi, axis=1)
        gather(tok_next_ref, 1 - slot)

        gu = jnp.dot(x, wgu_bf[...], preferred_element_type=F32) + bgu_ref[0, 0]
        gate = jnp.minimum(gu[:, 0:D_EXPERT], SWIGLU_LIMIT)
        up = jnp.clip(gu[:, D_EXPERT:2 * D_EXPERT], -SWIGLU_LIMIT, SWIGLU_LIMIT)
        glu = gate * _sigmoid(SWIGLU_ALPHA * gate)
        mid = ((up + 1.0) * glu).astype(BF16)
        y = jnp.dot(mid, wd_bf[...], preferred_element_type=F32) + bd_ref[0, 0]
        y_ref[...] = y.astype(BF16)

    @pl.when(b >= nu_ref[0])
    def _():
        y_ref[...] = jnp.zeros_like(y_ref)


def _moe_experts(blk_e, first, next_e, n_used, row_tok3, xp, w_gu, b_gu, w_down, b_down, layer):
    bias = lambda b, e, f, ne, nu: (layer, e[b], 0, 0)
    grid_spec = pltpu.PrefetchScalarGridSpec(
        num_scalar_prefetch=4,
        grid=(N_BLOCKS,),
        in_specs=[pl.BlockSpec((1, 1, MOE_M), lambda b, e, f, ne, nu: (b, 0, 0), memory_space=pltpu.SMEM),
                  pl.BlockSpec((1, 1, MOE_M), lambda b, e, f, ne, nu: (jnp.minimum(b + 1, N_BLOCKS - 1), 0, 0),
                               memory_space=pltpu.SMEM),
                  pl.BlockSpec((T * PACK_ROWS, LANES), lambda b, e, f, ne, nu: (0, 0), pipeline_mode=pl.Buffered(1)),
                  pl.BlockSpec(memory_space=pl.ANY),
                  pl.BlockSpec((1, 1, 1, 2 * D_EXPERT), bias),
                  pl.BlockSpec(memory_space=pl.ANY),
                  pl.BlockSpec((1, 1, 1, D_MODEL), bias)],
        out_specs=pl.BlockSpec((MOE_M, D_MODEL), lambda b, e, f, ne, nu: (b, 0)),
        scratch_shapes=[pltpu.VMEM((2, PACK_ROWS * GATHER_STRIDE, LANES), jnp.int32),
                        pltpu.VMEM((D_MODEL, 2 * D_EXPERT), F32),
                        pltpu.VMEM((D_EXPERT, D_MODEL), F32),
                        pltpu.VMEM((D_MODEL, 2 * D_EXPERT), BF16),
                        pltpu.VMEM((D_EXPERT, D_MODEL), BF16),
                        pltpu.SemaphoreType.DMA((2,))],
    )
    return pl.pallas_call(
        functools.partial(_moe_kernel, layer=layer),
        grid_spec=grid_spec,
        out_shape=jax.ShapeDtypeStruct((N_ROWS, D_MODEL), BF16),
        compiler_params=pltpu.CompilerParams(vmem_limit_bytes=48 * MIB),
        name="moe_experts",
    )(blk_e, first, next_e, n_used, row_tok3, row_tok3, xp, w_gu, b_gu.reshape(DEPTH, N_EXPERTS, 1, 2 * D_EXPERT),
      w_down, b_down.reshape(DEPTH, N_EXPERTS, 1, D_MODEL))


SCATTER_UNROLL = 8


def _combine_kernel(nu_ref, tok_ref, w_ref, y_ref, o_ref, acc, tile, sem):
    s = pl.program_id(0)

    @pl.when(s == 0)
    def _():
        acc[...] = jnp.zeros_like(acc)

    def sub_block(sb, carry):
        r0 = pl.multiple_of(sb * SCATTER_M, SCATTER_M)
        y = y_ref[pl.ds(r0, SCATTER_M), :].astype(F32)
        for c in range(ROW_VREGS):
            tile[c * SCATTER_STRIDE:c * SCATTER_STRIDE + SCATTER_M, :] = y[:, c * LANES:(c + 1) * LANES]
        for m0 in range(0, SCATTER_M, SCATTER_UNROLL):
            addrs, vals = [], []
            for u in range(SCATTER_UNROLL):
                mi = m0 + u
                a = pl.multiple_of(tok_ref[0, 0, r0 + mi] * ROW_VREGS, ROW_VREGS)
                yv = tile[pl.ds(mi, ROW_VREGS, stride=SCATTER_STRIDE), :]
                addrs.append(a)
                vals.append(acc[pl.ds(a, ROW_VREGS), :] + w_ref[0, 0, r0 + mi] * yv)
            for u in range(SCATTER_UNROLL):
                acc[pl.ds(addrs[u], ROW_VREGS), :] = vals[u]
        return carry

    @pl.when(s * COMBINE_BLOCKS < nu_ref[0])
    def _():
        lax.fori_loop(0, COMBINE_BLOCKS * MOE_M // SCATTER_M, sub_block, 0)

    @pl.when(s == pl.num_programs(0) - 1)
    def _():
        cp = pltpu.make_async_copy(acc.at[pl.ds(0, T * ROW_VREGS)], o_ref, sem)
        cp.start()
        cp.wait()


def _combine(n_used, row_tok3, row_w3, yr):
    rows = COMBINE_BLOCKS * MOE_M
    steps = N_BLOCKS // COMBINE_BLOCKS
    grid_spec = pltpu.PrefetchScalarGridSpec(
        num_scalar_prefetch=1,
        grid=(steps,),
        in_specs=[pl.BlockSpec((1, 1, rows), lambda s, nu: (s, 0, 0), memory_space=pltpu.SMEM),
                  pl.BlockSpec((1, 1, rows), lambda s, nu: (s, 0, 0), memory_space=pltpu.SMEM),
                  pl.BlockSpec((rows, D_MODEL), lambda s, nu: (s, 0))],
        out_specs=pl.BlockSpec(memory_space=pl.ANY),
        scratch_shapes=[pltpu.VMEM(((T + 1) * ROW_VREGS, LANES), F32),
                        pltpu.VMEM((ROW_VREGS * SCATTER_STRIDE, LANES), F32),
                        pltpu.SemaphoreType.DMA(())],
    )
    return pl.pallas_call(
        _combine_kernel,
        grid_spec=grid_spec,
        out_shape=jax.ShapeDtypeStruct((T * ROW_VREGS, LANES), F32),
        compiler_params=pltpu.CompilerParams(vmem_limit_bytes=48 * MIB),
        name="moe_combine",
    )(n_used, row_tok3.reshape(steps, 1, rows), row_w3.reshape(steps, 1, rows), yr)


def _route(top_keys, top_w, counts):
    experts = jnp.arange(N_EXPERTS, dtype=jnp.int32)
    padded = (counts + MOE_M - 1) // MOE_M * MOE_M
    pad_end = jnp.cumsum(padded)
    pad_start = pad_end - padded
    n_used = (pad_end[-1] // MOE_M).astype(jnp.int32)
    assert N_ROWS - N_ASSIGN == N_EXPERTS * MOE_M
    used = jnp.arange(MOE_M, dtype=jnp.int32)[None, :] < (padded - counts)[:, None]
    pad_keys = jnp.where(used, 2 * experts[:, None] + 1, 2 * N_EXPERTS + 1) << TOK_BITS | T
    keys = jnp.concatenate([top_keys.reshape(N_ASSIGN), pad_keys.reshape(N_EXPERTS * MOE_M)])
    wts = jnp.concatenate([top_w.reshape(N_ASSIGN), jnp.zeros((N_EXPERTS * MOE_M,), F32)])
    keys, row_w = lax.sort((keys, wts), num_keys=1)
    row_tok = keys & ((1 << TOK_BITS) - 1)
    blk0 = jnp.arange(N_BLOCKS, dtype=jnp.int32) * MOE_M
    blk_e = jnp.minimum(jnp.sum((pad_end[:, None] <= blk0[None, :]).astype(jnp.int32), axis=0), N_EXPERTS - 1)
    mine = experts[:, None] == blk_e[None, :]
    first = (blk0 == jnp.sum(jnp.where(mine, pad_start[:, None], 0), axis=0)).astype(jnp.int32)
    later = (experts[None, :] > experts[:, None]) & (counts[None, :] > 0)
    nxt = jnp.min(jnp.where(later, experts[None, :], N_EXPERTS), axis=1)
    nxt = jnp.where(nxt == N_EXPERTS, -1, nxt)
    next_e = jnp.sum(jnp.where(mine, nxt[:, None], 0), axis=0).astype(jnp.int32)
    return (row_tok.reshape(N_BLOCKS, 1, MOE_M), row_w.reshape(N_BLOCKS, 1, MOE_M), blk_e.astype(jnp.int32),
            first, next_e, n_used.reshape(1))


def _final_tile(x_ref, moe_ref, mod_ref, g_ref, y_o):
    x = x_ref[...] + mod_ref[0, 5:6, :] * _tiles_to_rows(moe_ref)
    ms = jnp.mean(x * x, axis=-1, keepdims=True)
    y_o[...] = x * lax.rsqrt(ms + EPS) * g_ref[...]


def _final_kernel(x_hbm, moe_hbm, mod_hbm, g_hbm, yc_hbm, yl_hbm):
    per_lat = DEC_SEQ // TM
    groups = ((0, N_CTX_TILES, lambda i: 0, yc_hbm),
              (N_CTX_TILES, T_LAT // TM, lambda i: 1 + i // per_lat, yl_hbm))
    for first, n_tiles, mod_row, out in groups:
        pltpu.emit_pipeline(
            _final_tile,
            grid=(n_tiles,),
            in_specs=[pl.BlockSpec((TM, D_MODEL), lambda i, f=first: (i + f, 0), pipeline_mode=pl.Buffered(3)),
                      pl.BlockSpec((TM * ROW_VREGS, LANES), lambda i, f=first: (i + f, 0),
                                   pipeline_mode=pl.Buffered(3)),
                      pl.BlockSpec((1, 6, D_MODEL), lambda i, r=mod_row: (r(i), 0, 0)),
                      pl.BlockSpec((1, D_MODEL), lambda i: (0, 0))],
            out_specs=[pl.BlockSpec((TM, D_MODEL), lambda i: (i, 0))],
        )(x_hbm, moe_hbm, mod_hbm, g_hbm, out)


def _final(x1, moe_tiles, mod_l, g):
    any_spec = pl.BlockSpec(memory_space=pl.ANY)
    return pl.pallas_call(
        _final_kernel,
        in_specs=[any_spec] * 4,
        out_specs=[any_spec] * 2,
        out_shape=[jax.ShapeDtypeStruct((T_CTX, D_MODEL), F32), jax.ShapeDtypeStruct((T_LAT, D_MODEL), F32)],
        compiler_params=pltpu.CompilerParams(vmem_limit_bytes=32 * MIB),
        name="residual_final",
    )(x1, moe_tiles, mod_l, g)


def _rope_tables():
    t = np.arange(DEC_SEQ)
    pos = np.stack([t // GRID_W, t % GRID_W], axis=1).astype(np.float32)
    half = HEAD_DIM // 2
    inv = np.float32(ROPE_BASE) ** (-np.arange(0, half, 2, dtype=np.float32) / np.float32(half))
    d = np.arange(HEAD_DIM)
    which = d // half
    freq = d % (half // 2)
    sign = np.where((d % half) < half // 2, -1.0, 1.0).astype(np.float32)
    ang = (pos[:, which] * inv[freq][None, :]).astype(np.float32)
    reps = LANES // HEAD_DIM
    cos = np.tile(np.cos(ang).astype(np.float32), (1, reps))
    sin = np.tile(np.sin(ang).astype(np.float32) * sign[None, :], (1, reps))
    return jnp.asarray(cos), jnp.asarray(sin)


def kernel(x_prompt, x_sample, cache_k, cache_v, state_ret, c, c_ctx, norm1_g, norm2_g, w_mod, b_mod, w_in, conv_w, attn_sink, ret_decay, w_a, w_b, w_c, w_o, w_router, b_router, w_gu, b_gu, w_down, b_down, final_g):
    xs = (x_prompt.reshape(T_CTX, D_MODEL), x_sample.reshape(T_LAT, D_MODEL))
    cond8 = jnp.zeros((8, D_MODEL), F32).at[0].set(c_ctx).at[1:1 + DEC_BATCH].set(c)
    mod = _modulation(cond8, w_mod, b_mod)
    cos, sin = _rope_tables()
    ck = cache_k.reshape(DEC_BATCH, DEPTH, PAST_LEN, D_KV)
    cv = cache_v.reshape(DEC_BATCH, DEPTH, PAST_LEN, D_KV)

    ks, vs, rs = [], [], []
    prev = None
    for l in range(DEPTH):
        mod_l = mod[l, 0:1 + DEC_BATCH].reshape(1 + DEC_BATCH, 6, D_MODEL)
        if prev is None:
            conv_in, q, k, v, ret, gates = _inproj(xs, mod_l, norm1_g[l][None, :], w_in, l)
        else:
            x, conv_in, q, k, v, ret, gates = _inproj(None, mod_l, norm1_g[l][None, :], w_in, l, prev=prev)
            xs = (x,)
        rd8 = jnp.broadcast_to(ret_decay[l].reshape(2 * RET_HEADS, 1), (2 * RET_HEADS, LANES))
        *ys_ctx, rfin = _mixers(conv_in, q, k, v, ret, conv_w[l], attn_sink[l], rd8, latent=False)
        ys_lat = _mixers(conv_in, q, k, v, ret, conv_w[l], attn_sink[l], rd8, latent=True,
                         cos=cos, sin=sin, cache_k=ck, cache_v=cv, state=state_ret, layer=l)
        wr = jnp.pad(w_router[l], ((0, 0), (0, LANES - N_EXPERTS)))
        wrh = wr.astype(BF16)
        wrl = (wr - wrh.astype(F32)).astype(BF16)
        br = jnp.pad(b_router[l], (0, LANES - N_EXPERTS))[None, :]
        x1, xp, top_keys, top_w, cnt = _merge(xs, ys_ctx, ys_lat, gates, mod_l, norm2_g[l][None, :], w_a, w_b, w_c,
                                           w_o, wrh, wrl, br, l)
        row_tok, row_w, blk_e, first, next_e, n_used = _route(top_keys[0:TOP_K], top_w[0:TOP_K],
                                                              cnt[0, 0:N_EXPERTS])
        yrows = _moe_experts(blk_e, first, next_e, n_used, row_tok, xp, w_gu, b_gu, w_down, b_down, l)
        moe = _combine(n_used, row_tok, row_w, yrows)
        prev = (x1, moe, mod_l)
        ks.append(k[0:T_CTX].reshape(BATCH, SEQ, N_KV, HEAD_DIM))
        vs.append(v[0:T_CTX].reshape(BATCH, SEQ, N_KV, HEAD_DIM))
        rs.append(rfin)

    yc, yl = _final(*prev, final_g[None, :])
    y_prompt = yc.reshape(BATCH, SEQ, D_MODEL)
    y_sample = yl.reshape(DEC_BATCH, DEC_SEQ, D_MODEL)
    return (y_prompt, y_sample, jnp.stack(ks, axis=1), jnp.stack(vs, axis=1), jnp.stack(rs, axis=1))
```
